```python
import jax, jax.numpy as jnp
from jax import lax
import numpy as np

D_MODEL = 1024
BATCH = 16
SEQ = 256
DEPTH = 1
DEC_BATCH = 4
DEC_SEQ = 1024
PAST_LEN = 512

GRID_W = 64
D_CONV = 512
CONV_W = 3
N_GLA_HEADS = 4
HEAD_DK = 128
HEAD_DV = 128
D_GLA_K = N_GLA_HEADS * HEAD_DK
D_GLA_V = N_GLA_HEADS * HEAD_DV
GLA_RANK = 16
GLA_GATE_NORM = 16.0
GLA_CHUNK = 64
N_EXPERTS = 64
TOP_K = 8
D_EXPERT = 256
D_SHARED = 256
ROUTED_SCALE = 2.5
EPS = 1e-6
SPLIT_SIZES = (D_CONV, D_CONV, D_CONV, D_GLA_K, D_GLA_K, D_GLA_V, D_GLA_V, GLA_RANK, GLA_RANK, D_MODEL, D_MODEL)
D_IN_PROJ = sum(SPLIT_SIZES)

kernel_name = "hybrid_diffusion_conv_gla_moe_step"


def rmsnorm(x, w):
    x32 = x.astype(jnp.float32)
    y = x32 * lax.rsqrt(jnp.mean(x32 * x32, axis=-1, keepdims=True) + EPS)
    return y.astype(x.dtype) * w


def conv3_rows(u, conv_w, conv_b, rows, row_len):
    bsz, l, ch = u.shape
    r = u.reshape(bsz, rows, row_len, ch)
    p = jnp.pad(r, ((0, 0), (0, 0), (1, 1), (0, 0)))
    y = (conv_w[0] * p[:, :, 0:row_len] + conv_w[1] * p[:, :, 1:row_len + 1]
         + conv_w[2] * p[:, :, 2:row_len + 2] + conv_b)
    return y.reshape(bsz, l, ch)


def gla_chunked(q, k, v, log_a, s0):
    bsz, nh, l, dk = q.shape
    dv = v.shape[-1]
    c = GLA_CHUNK
    n = l // c
    f32 = jnp.float32
    q = q.astype(f32).reshape(bsz, nh, n, c, dk)
    k = k.astype(f32).reshape(bsz, nh, n, c, dk)
    v = v.astype(f32).reshape(bsz, nh, n, c, dv)
    bcum = jnp.cumsum(log_a.astype(f32).reshape(bsz, nh, n, c, dk), axis=3)
    mask = jnp.tril(jnp.ones((c, c), dtype=bool))
    diff = bcum[:, :, :, :, None, :] - bcum[:, :, :, None, :, :]
    decay = jnp.exp(jnp.where(mask[:, :, None], diff, -jnp.inf))
    scores = jnp.sum(q[:, :, :, :, None, :] * k[:, :, :, None, :, :] * decay, axis=-1)
    o_intra = jnp.einsum('bhnij,bhnje->bhnie', scores, v)
    b_last = bcum[:, :, :, -1, :]
    k_tail = k * jnp.exp(b_last[:, :, :, None, :] - bcum)
    u = jnp.einsum('bhncd,bhnce->bhnde', k_tail, v)
    a_chunk = jnp.exp(b_last)

    def step(s, inp):
        a, uu = inp
        return a[..., None] * s + uu, s

    s_final, s_starts = lax.scan(step, s0.astype(f32),
                                 (jnp.moveaxis(a_chunk, 2, 0), jnp.moveaxis(u, 2, 0)))
    s_starts = jnp.moveaxis(s_starts, 0, 2)
    o_inter = jnp.einsum('bhncd,bhnde->bhnce', q * jnp.exp(bcum), s_starts)
    o = (o_intra + o_inter).reshape(bsz, nh, l, dv)
    return o, s_final


def to_heads(t, hd):
    bsz, l, _ = t.shape
    return t.reshape(bsz, l, N_GLA_HEADS, hd).transpose(0, 2, 1, 3)


def gla_mixer(q, k, v, g_out, lr_f, lr_b, w_decay, b_decay, gla_norm_w, s0_f, s0_b):
    bsz, l, _ = q.shape
    qh = to_heads(q, HEAD_DK) * (HEAD_DK ** -0.5)
    kh = to_heads(k, HEAD_DK)
    vh = to_heads(v, HEAD_DV)
    la_f = to_heads(jax.nn.log_sigmoid((lr_f @ w_decay[0] + b_decay[0]).astype(jnp.float32)) / GLA_GATE_NORM, HEAD_DK)
    la_b = to_heads(jax.nn.log_sigmoid((lr_b @ w_decay[1] + b_decay[1]).astype(jnp.float32)) / GLA_GATE_NORM, HEAD_DK)
    o_f, s_f = gla_chunked(qh, kh, vh, la_f, s0_f)
    flip = lambda t: jnp.flip(t, axis=2)
    o_b, s_b = gla_chunked(flip(qh), flip(kh), flip(vh), flip(la_b), s0_b)
    o = o_f + flip(o_b)
    o = o * lax.rsqrt(jnp.mean(o * o, axis=-1, keepdims=True) + EPS)
    o = o.astype(q.dtype) * gla_norm_w[None, :, None, :]
    o = o.transpose(0, 2, 1, 3).reshape(bsz, l, D_GLA_V)
    return o * jax.nn.silu(g_out), s_f.astype(q.dtype), s_b.astype(q.dtype)


def moe(h, w_router, b_router, w_gate_e, w_up_e, w_down_e, w_gate_s, w_up_s, w_down_s):
    bsz, l, d = h.shape
    t = h.reshape(bsz * l, d)
    scores = jax.nn.sigmoid((t @ w_router).astype(jnp.float32))
    _, idx = lax.top_k(scores + b_router.astype(jnp.float32), TOP_K)
    sel = jnp.take_along_axis(scores, idx, axis=-1)
    wts = sel / jnp.sum(sel, axis=-1, keepdims=True) * ROUTED_SCALE
    combine = jnp.sum(jax.nn.one_hot(idx, N_EXPERTS, dtype=jnp.float32) * wts[..., None], axis=1)
    hg = jnp.einsum('td,edf->tef', t, w_gate_e)
    hu = jnp.einsum('td,edf->tef', t, w_up_e)
    act = jax.nn.silu(hg) * hu * combine[..., None].astype(t.dtype)
    routed = jnp.einsum('tef,efd->td', act, w_down_e)
    shared = (jax.nn.silu(t @ w_gate_s) * (t @ w_up_s)) @ w_down_s
    return (routed + shared).reshape(bsz, l, d)


def trunk_layer(x, cond, rows, row_len, s0_f, s0_b, p):
    (w_mod, b_mod, norm1_w, w_in, conv_w, conv_b, w_decay, b_decay, gla_norm_w,
     w_br_conv, w_br_gla, w_out, norm2_w, w_router, b_router,
     w_gate_e, w_up_e, w_down_e, w_gate_s, w_up_s, w_down_s) = p
    mod = (jax.nn.silu(cond) @ w_mod + b_mod)[:, None, :]
    sh1, sc1, g1, sh2, sc2, g2 = jnp.split(mod, 6, axis=-1)
    h = rmsnorm(x, norm1_w) * (1 + sc1) + sh1
    points = np.cumsum(SPLIT_SIZES)[:-1].tolist()
    (u_in, gate_b, gate_c, q, k, v, g_out, lr_f, lr_b,
     gate_br_conv, gate_br_gla) = jnp.split(h @ w_in, points, axis=-1)
    y_conv = gate_b * conv3_rows(gate_c * u_in, conv_w, conv_b, rows, row_len)
    y_gla, s_f, s_b = gla_mixer(q, k, v, g_out, lr_f, lr_b, w_decay, b_decay, gla_norm_w, s0_f, s0_b)
    merged = (jax.nn.sigmoid(gate_br_conv) * (y_conv @ w_br_conv)
              + jax.nn.sigmoid(gate_br_gla) * (y_gla @ w_br_gla))
    x = x + g1 * (merged @ w_out)
    h2 = rmsnorm(x, norm2_w) * (1 + sc2) + sh2
    x = x + g2 * moe(h2, w_router, b_router, w_gate_e, w_up_e, w_down_e, w_gate_s, w_up_s, w_down_s)
    return x, s_f, s_b


def setup_inputs(seed: int = 0) -> dict:
    key = jax.random.key(seed)
    ks = jax.random.split(key, 32)
    nrm = lambda i, shape, s: jax.random.normal(ks[i], shape, jnp.float32) * s
    L = DEPTH
    return {
        "x_prompt": nrm(0, (BATCH, SEQ, D_MODEL), 1.0),
        "x_sample": nrm(1, (DEC_BATCH, DEC_SEQ, D_MODEL), 1.0),
        "state_gla": nrm(2, (DEC_BATCH, DEPTH, 2, N_GLA_HEADS, HEAD_DK, HEAD_DV), 0.5),
        "c": nrm(3, (DEC_BATCH, D_MODEL), 1.0),
        "c_ctx": nrm(4, (D_MODEL,), 1.0),
        "w_mod": nrm(5, (L, D_MODEL, 6 * D_MODEL), 0.5 * D_MODEL ** -0.5),
        "b_mod": nrm(6, (L, 6 * D_MODEL), 0.02),
        "norm1_w": 1.0 + nrm(7, (L, D_MODEL), 0.02),
        "w_in": nrm(8, (L, D_MODEL, D_IN_PROJ), D_MODEL ** -0.5),
        "conv_w": nrm(9, (L, CONV_W, D_CONV), CONV_W ** -0.5),
        "conv_b": nrm(10, (L, D_CONV), 0.02),
        "w_decay": nrm(11, (L, 2, GLA_RANK, D_GLA_K), GLA_RANK ** -0.5),
        "b_decay": nrm(12, (L, 2, D_GLA_K), 0.1),
        "gla_norm_w": 1.0 + nrm(13, (L, N_GLA_HEADS, HEAD_DV), 0.02),
        "w_br_conv": nrm(14, (L, D_CONV, D_MODEL), D_CONV ** -0.5),
        "w_br_gla": nrm(15, (L, D_GLA_V, D_MODEL), D_GLA_V ** -0.5),
        "w_out": nrm(16, (L, D_MODEL, D_MODEL), D_MODEL ** -0.5),
        "norm2_w": 1.0 + nrm(17, (L, D_MODEL), 0.02),
        "w_router": nrm(18, (L, D_MODEL, N_EXPERTS), D_MODEL ** -0.5),
        "b_router": nrm(19, (L, N_EXPERTS), 0.01),
        "w_gate_e": nrm(20, (L, N_EXPERTS, D_MODEL, D_EXPERT), D_MODEL ** -0.5),
        "w_up_e": nrm(21, (L, N_EXPERTS, D_MODEL, D_EXPERT), D_MODEL ** -0.5),
        "w_down_e": nrm(22, (L, N_EXPERTS, D_EXPERT, D_MODEL), D_EXPERT ** -0.5),
        "w_gate_s": nrm(23, (L, D_MODEL, D_SHARED), D_MODEL ** -0.5),
        "w_up_s": nrm(24, (L, D_MODEL, D_SHARED), D_MODEL ** -0.5),
        "w_down_s": nrm(25, (L, D_SHARED, D_MODEL), D_SHARED ** -0.5),
        "final_norm_w": 1.0 + nrm(26, (D_MODEL,), 0.02),
    }


def reference(x_prompt, x_sample, state_gla, c, c_ctx, w_mod, b_mod, norm1_w, w_in, conv_w, conv_b,
              w_decay, b_decay, gla_norm_w, w_br_conv, w_br_gla, w_out, norm2_w, w_router, b_router,
              w_gate_e, w_up_e, w_down_e, w_gate_s, w_up_s, w_down_s, final_norm_w):
    y_p = x_prompt
    y_s = x_sample
    ctx_len = x_prompt.shape[1]
    rows = x_sample.shape[1] // GRID_W
    s_zero = jnp.zeros((x_prompt.shape[0], N_GLA_HEADS, HEAD_DK, HEAD_DV), x_prompt.dtype)
    ctx_cond = c_ctx[None, :]
    new_states = []
    for l in range(DEPTH):
        p = (w_mod[l], b_mod[l], norm1_w[l], w_in[l], conv_w[l], conv_b[l], w_decay[l], b_decay[l],
             gla_norm_w[l], w_br_conv[l], w_br_gla[l], w_out[l], norm2_w[l], w_router[l], b_router[l],
             w_gate_e[l], w_up_e[l], w_down_e[l], w_gate_s[l], w_up_s[l], w_down_s[l])
        y_p, s_f, s_b = trunk_layer(y_p, ctx_cond, 1, ctx_len, s_zero, s_zero, p)
        new_states.append(jnp.stack([s_f, s_b], axis=1))
        y_s, _, _ = trunk_layer(y_s, c, rows, GRID_W, state_gla[:, l, 0], state_gla[:, l, 1], p)
    new_state_gla = jnp.stack(new_states, axis=1)
    y_prompt = rmsnorm(y_p, final_norm_w)
    y_sample = rmsnorm(y_s, final_norm_w)
    return (y_prompt, y_sample, new_state_gla)
```

```python
import functools

import jax
import jax.numpy as jnp
from jax import lax
from jax.experimental import pallas as pl
from jax.experimental.pallas import tpu as pltpu

F32 = jnp.float32
BF16 = jnp.bfloat16

D_MODEL = 1024
GRID_W = 64
D_CONV = 512
N_HEADS = 4
HEAD_D = 128
D_GLA = N_HEADS * HEAD_D
GLA_RANK = 16
GLA_GATE_NORM = 16.0
CHUNK = 64
SUB = 8
N_SUB = CHUNK // SUB
N_EXPERTS = 64
TOP_K = 8
D_EXPERT = 256
ROUTED_SCALE = 2.5
EPS = 1e-6

C_U, C_GB, C_GC, C_Q, C_K, C_V, C_GO = 0, 512, 1024, 1536, 2048, 2560, 3072
C_BRC, C_BRG, C_LR = 3584, 4608, 5632
D_PROJ = 5760
LR_PAD = 128

VMEM_LIMIT = 56 * 1024 * 1024


def _dot(a, b):
    return jnp.dot(a, b, preferred_element_type=F32)


def _dot_nt(a, b):
    return lax.dot_general(a, b, (((1,), (1,)), ((), ())), preferred_element_type=F32)


def _dot_tn(a, b):
    return lax.dot_general(a, b, (((0,), (0,)), ((), ())), preferred_element_type=F32)


def _dot_hi(a, b):
    return jnp.dot(a, b, preferred_element_type=F32, precision=lax.Precision.HIGHEST)


def _split_bf16(x):
    hi = x.astype(BF16)
    lo = (x - hi.astype(F32)).astype(BF16)
    return hi, lo


def _rms(x):
    return x * lax.rsqrt(jnp.mean(x * x, axis=-1, keepdims=True) + EPS)


def _mod_kernel(cond_ref, w_ref, b_ref, o_ref):
    c = cond_ref[...]
    o_ref[...] = _dot_hi(c * jax.nn.sigmoid(c), w_ref[...]) + b_ref[...]


def _modulation(cond, w_mod, b_mod):
    n_rows = cond.shape[0]
    tn = 1536
    return pl.pallas_call(
        _mod_kernel,
        grid=(6 * D_MODEL // tn,),
        in_specs=[pl.BlockSpec((n_rows, D_MODEL), lambda j: (0, 0)),
                  pl.BlockSpec((D_MODEL, tn), lambda j: (0, j)),
                  pl.BlockSpec((1, tn), lambda j: (0, j))],
        out_specs=pl.BlockSpec((n_rows, tn), lambda j: (0, j)),
        out_shape=jax.ShapeDtypeStruct((n_rows, 6 * D_MODEL), F32),
        compiler_params=pltpu.CompilerParams(dimension_semantics=("arbitrary",),
                                             vmem_limit_bytes=VMEM_LIMIT),
        name="modulation",
    )(cond, w_mod, b_mod)


def _inproj_kernel(x_ref, mod_ref, nw_ref, w_ref, o_ref, *, rows_per_mod, tm):
    i = pl.program_id(1)
    row = (i * tm) // rows_per_mod
    sh = mod_ref[pl.ds(row, 1), 0:D_MODEL]
    sc = mod_ref[pl.ds(row, 1), D_MODEL:2 * D_MODEL]
    h = _rms(x_ref[...]) * nw_ref[...] * (1.0 + sc) + sh
    o_ref[...] = _dot(h.astype(BF16), w_ref[...]).astype(BF16)


def _in_proj(x2d, mod, norm_w, w_in_r, rows_per_mod):
    t = x2d.shape[0]
    tm, tn = 512, 1920
    kern = functools.partial(_inproj_kernel, rows_per_mod=rows_per_mod, tm=tm)
    return pl.pallas_call(
        kern,
        grid=(D_PROJ // tn, t // tm),
        in_specs=[pl.BlockSpec((tm, D_MODEL), lambda j, i: (i, 0)),
                  pl.BlockSpec(mod.shape, lambda j, i: (0, 0)),
                  pl.BlockSpec((1, D_MODEL), lambda j, i: (0, 0)),
                  pl.BlockSpec((D_MODEL, tn), lambda j, i: (0, j))],
        out_specs=pl.BlockSpec((tm, tn), lambda j, i: (i, j)),
        out_shape=jax.ShapeDtypeStruct((t, D_PROJ), BF16),
        compiler_params=pltpu.CompilerParams(dimension_semantics=("arbitrary", "arbitrary"),
                                             vmem_limit_bytes=VMEM_LIMIT),
        name="in_proj",
    )(x2d, mod, norm_w, w_in_r)


def _log_sigmoid(x):
    return jnp.minimum(x, 0.0) - jnp.log1p(jnp.exp(-jnp.abs(x)))


def _gla_chunk_head(qc, kc, vc, bc, st, rev):
    lane = lax.broadcasted_iota(jnp.int32, (SUB, CHUNK), 1)
    sub = lax.broadcasted_iota(jnp.int32, (SUB, CHUNK), 0)
    tot = bc[0:1] if rev else bc[CHUNK - 1:CHUNK]

    o = _dot_nt((qc * jnp.exp(bc)).astype(BF16), st.astype(BF16))
    k_tail = kc * jnp.exp(tot - bc)
    st_new = st * jnp.exp(tot) + _dot_tn(vc.astype(BF16), k_tail.astype(BF16))

    lhs_segs, rhs_segs = [], []

    def rows(before, mid, after):
        parts = ([jnp.zeros((before, HEAD_D), F32)] if before else []) + [mid]
        parts += [jnp.zeros((after, HEAD_D), F32)] if after else []
        return jnp.concatenate(parts, axis=0) if len(parts) > 1 else mid

    key_blocks = range(1, N_SUB) if rev else range(0, N_SUB - 1)
    for jb in key_blocks:
        r0 = jb * SUB
        ref_row = bc[r0:r0 + 1] if rev else bc[r0 + SUB - 1:r0 + SUB]
        ke = kc[r0:r0 + SUB] * jnp.exp(jnp.minimum(ref_row - bc[r0:r0 + SUB], 0.0))
        rhs_segs.append(rows(r0, ke, CHUNK - r0 - SUB))
        if rev:
            ql = qc[:r0] * jnp.exp(jnp.minimum(bc[:r0] - ref_row, 0.0))
            lhs_segs.append(rows(0, ql, CHUNK - r0))
        else:
            ql = qc[r0 + SUB:] * jnp.exp(jnp.minimum(bc[r0 + SUB:] - ref_row, 0.0))
            lhs_segs.append(rows(r0 + SUB, ql, 0))
    far = _dot_nt(jnp.concatenate(lhs_segs, axis=1).astype(BF16),
                  jnp.concatenate(rhs_segs, axis=1).astype(BF16))

    blocks = []
    for ib in range(N_SUB):
        r0 = ib * SUB
        qi, bi = qc[r0:r0 + SUB], bc[r0:r0 + SUB]
        acc = jnp.zeros((SUB, CHUNK), F32)
        for jj in range(SUB):
            j = r0 + jj
            e = jnp.exp(jnp.minimum(bi - bc[j:j + 1], 0.0))
            col = jnp.sum(qi * (kc[j:j + 1] * e), axis=-1, keepdims=True)
            acc = jnp.where(lane == j, col, acc)
        keep = (lane - r0 >= sub) if rev else (lane - r0 <= sub)
        blocks.append(jnp.where(keep, acc, 0.0))
    scores = far + jnp.concatenate(blocks, axis=0)
    o = o + _dot(scores.astype(BF16), vc.astype(BF16))
    return o, st_new


def _mixer_kernel(*refs, seq_len, row_len, has_s0, emit_state):
    it = iter(refs)
    proj_ref, x_ref, mod_ref, cw_ref, cb_ref, wdec_ref, bdec_ref, gnw_ref = (next(it) for _ in range(8))
    wbc_ref, wbg_ref, wout_ref = (next(it) for _ in range(3))
    s0_ref = next(it) if has_s0 else None
    out_ref = next(it)
    st_out_ref = next(it) if emit_state else None
    la_f_ref, la_b_ref, o_ref, st_ref = (next(it) for _ in range(4))

    L = seq_len
    n_chunks = L // CHUNK
    tr = 256
    assert tr % row_len == 0 and L % tr == 0

    def decay_body(t, carry):
        r0 = pl.multiple_of(t * tr, tr)
        lr = proj_ref[pl.ds(r0, tr), C_LR:C_LR + LR_PAD]
        for d, ref in ((0, la_f_ref), (1, la_b_ref)):
            whi, wlo = _split_bf16(wdec_ref[d])
            z = _dot(lr, whi) + _dot(lr, wlo) + bdec_ref[d:d + 1]
            ref[pl.ds(r0, tr), :] = _log_sigmoid(z) * (1.0 / GLA_GATE_NORM)
        return carry
    lax.fori_loop(0, L // tr, decay_body, 0)

    ri = lax.broadcasted_iota(jnp.int32, (CHUNK, CHUNK), 0)
    ci = lax.broadcasted_iota(jnp.int32, (CHUNK, CHUNK), 1)

    for rev, la_ref in ((False, la_f_ref), (True, la_b_ref)):
        d = 1 if rev else 0
        tri = jnp.where((ci >= ri) if rev else (ci <= ri), 1.0, 0.0).astype(BF16)
        for h in range(N_HEADS):
            if has_s0:
                st_ref[h] = s0_ref[0, d, h].T
            else:
                st_ref[h] = jnp.zeros((HEAD_D, HEAD_D), F32)

        def chunk_body(c, carry, rev=rev, la_ref=la_ref, tri=tri):
            cc = (n_chunks - 1 - c) if rev else c
            r0 = pl.multiple_of(cc * CHUNK, CHUNK)
            la_hi, la_lo = _split_bf16(la_ref[pl.ds(r0, CHUNK), :])
            bcum = _dot(tri, la_hi) + _dot(tri, la_lo)
            for h in range(N_HEADS):
                lo, hi = h * HEAD_D, (h + 1) * HEAD_D
                qc = proj_ref[pl.ds(r0, CHUNK), C_Q + lo:C_Q + hi].astype(F32) * (HEAD_D ** -0.5)
                kc = proj_ref[pl.ds(r0, CHUNK), C_K + lo:C_K + hi].astype(F32)
                vc = proj_ref[pl.ds(r0, CHUNK), C_V + lo:C_V + hi].astype(F32)
                oc, st_new = _gla_chunk_head(qc, kc, vc, bcum[:, lo:hi], st_ref[h], rev)
                st_ref[h] = st_new
                if rev:
                    o_ref[pl.ds(r0, CHUNK), lo:hi] += oc
                else:
                    o_ref[pl.ds(r0, CHUNK), lo:hi] = oc
            return carry
        lax.fori_loop(0, n_chunks, chunk_body, 0)

        if emit_state:
            for h in range(N_HEADS):
                st_out_ref[0, d, h] = st_ref[h].T

    mod_row = pl.program_id(0) if mod_ref.shape[0] > 1 else 0
    g1 = mod_ref[pl.ds(mod_row, 1), 2 * D_MODEL:3 * D_MODEL]
    hsel_r = lax.broadcasted_iota(jnp.int32, (D_GLA, D_GLA), 0) // HEAD_D
    hsel_c = lax.broadcasted_iota(jnp.int32, (D_GLA, D_GLA), 1) // HEAD_D
    head_avg = jnp.where(hsel_r == hsel_c, 1.0 / HEAD_D, 0.0).astype(BF16)
    pos = lax.broadcasted_iota(jnp.int32, (tr, 1), 0)

    def dense_body(t, carry):
        r0 = pl.multiple_of(t * tr, tr)
        rows = pl.ds(r0, tr)
        cu = (proj_ref[rows, C_GC:C_GC + D_CONV].astype(F32) * proj_ref[rows, C_U:C_U + D_CONV].astype(F32))
        in_row = pos % row_len
        left = jnp.where(in_row == 0, 0.0, pltpu.roll(cu, 1, axis=0))
        right = jnp.where(in_row == row_len - 1, 0.0, pltpu.roll(cu, tr - 1, axis=0))
        conv = cw_ref[0:1] * left + cw_ref[1:2] * cu + cw_ref[2:3] * right + cb_ref[...]
        y_conv = proj_ref[rows, C_GB:C_GB + D_CONV].astype(F32) * conv

        o = o_ref[rows, :]
        osq_hi, osq_lo = _split_bf16(o * o)
        ms = _dot(osq_hi, head_avg) + _dot(osq_lo, head_avg)
        g_out = proj_ref[rows, C_GO:C_GO + D_GLA].astype(F32)
        y_gla = o * lax.rsqrt(ms + EPS) * gnw_ref[...] * (g_out * jax.nn.sigmoid(g_out))

        merged = (jax.nn.sigmoid(proj_ref[rows, C_BRC:C_BRC + D_MODEL].astype(F32)) * _dot(y_conv.astype(BF16), wbc_ref[...])
                  + jax.nn.sigmoid(proj_ref[rows, C_BRG:C_BRG + D_MODEL].astype(F32)) * _dot(y_gla.astype(BF16), wbg_ref[...]))
        out_ref[rows, :] = x_ref[rows, :] + g1 * _dot(merged.astype(BF16), wout_ref[...])
        return carry
    lax.fori_loop(0, L // tr, dense_body, 0)


def _mixer(proj, x2d, mod, conv_w, conv_b, wdec, bdec, gnw, wbc, wbg, wout, s0, *, seq_len, row_len, emit_state):
    t = x2d.shape[0]
    nb = t // seq_len
    has_s0 = s0 is not None
    one = pl.Buffered(1)
    const = lambda shape: pl.BlockSpec(shape, lambda b: (0,) * len(shape), pipeline_mode=one)
    in_specs = [pl.BlockSpec((seq_len, D_PROJ), lambda b: (b, 0), pipeline_mode=one),
                pl.BlockSpec((seq_len, D_MODEL), lambda b: (b, 0)),
                const(mod.shape),
                const((3, D_CONV)), const((1, D_CONV)), const((2, LR_PAD, D_GLA)), const((2, D_GLA)),
                const((1, D_GLA)), const((D_CONV, D_MODEL)), const((D_GLA, D_MODEL)), const((D_MODEL, D_MODEL))]
    args = [proj, x2d, mod, conv_w, conv_b, wdec, bdec, gnw, wbc, wbg, wout]
    if has_s0:
        in_specs.append(pl.BlockSpec((1, 2, N_HEADS, HEAD_D, HEAD_D), lambda b: (b, 0, 0, 0, 0)))
        args.append(s0)
    out_specs = [pl.BlockSpec((seq_len, D_MODEL), lambda b: (b, 0))]
    out_shape = [jax.ShapeDtypeStruct((t, D_MODEL), F32)]
    if emit_state:
        out_specs.append(pl.BlockSpec((1, 2, N_HEADS, HEAD_D, HEAD_D), lambda b: (b, 0, 0, 0, 0)))
        out_shape.append(jax.ShapeDtypeStruct((nb, 2, N_HEADS, HEAD_D, HEAD_D), F32))
    kern = functools.partial(_mixer_kernel, seq_len=seq_len, row_len=row_len, has_s0=has_s0, emit_state=emit_state)
    return pl.pallas_call(
        kern,
        grid=(nb,),
        in_specs=in_specs,
        out_specs=out_specs,
        out_shape=out_shape,
        scratch_shapes=[pltpu.VMEM((seq_len, D_GLA), F32), pltpu.VMEM((seq_len, D_GLA), F32),
                        pltpu.VMEM((seq_len, D_GLA), F32), pltpu.VMEM((N_HEADS, HEAD_D, HEAD_D), F32)],
        compiler_params=pltpu.CompilerParams(dimension_semantics=("arbitrary",),
                                             vmem_limit_bytes=VMEM_LIMIT),
        name="mixer",
    )(*args)


E_BLK = 2


def _moe_kernel(x_ref, mod_ref, n2w_ref, wr_ref, br_ref, wg_ref, wu_ref, wd_ref,
                wgs_ref, wus_ref, wds_ref, fnw_ref, out_ref, h2_ref, comb_ref, acc_ref, *, rows_per_mod, tm, final):
    i = pl.program_id(0)
    e = pl.program_id(1)
    row = (i * tm) // rows_per_mod

    @pl.when(e == 0)
    def _():
        sh = mod_ref[pl.ds(row, 1), 3 * D_MODEL:4 * D_MODEL]
        sc = mod_ref[pl.ds(row, 1), 4 * D_MODEL:5 * D_MODEL]
        h2 = _rms(x_ref[...]) * n2w_ref[...] * (1.0 + sc) + sh
        hb = h2.astype(BF16)
        h2_ref[...] = hb
        scores = jax.nn.sigmoid(_dot_hi(h2, wr_ref[...]))
        biased = scores + br_ref[...]
        lane = lax.broadcasted_iota(jnp.int32, scores.shape, 1)
        sel = jnp.zeros(scores.shape, F32)
        for _k in range(TOP_K):
            m = jnp.max(biased, axis=-1, keepdims=True)
            first = jnp.min(jnp.where(biased == m, lane, N_EXPERTS), axis=-1, keepdims=True)
            pick = lane == first
            sel = jnp.where(pick, scores, sel)
            biased = jnp.where(pick, -jnp.inf, biased)
        comb_ref[...] = sel / jnp.sum(sel, axis=-1, keepdims=True) * ROUTED_SCALE
        hg = _dot(hb, wgs_ref[...].astype(BF16))
        hu = _dot(hb, wus_ref[...].astype(BF16))
        acc_ref[...] = _dot((hg * jax.nn.sigmoid(hg) * hu).astype(BF16), wds_ref[...].astype(BF16))

    hb = h2_ref[...]
    comb = comb_ref[...]
    lane = lax.broadcasted_iota(jnp.int32, comb.shape, 1)
    for ee in range(E_BLK):
        w = jnp.sum(jnp.where(lane == e * E_BLK + ee, comb, 0.0), axis=-1, keepdims=True)
        hg = _dot(hb, wg_ref[ee].astype(BF16))
        hu = _dot(hb, wu_ref[ee].astype(BF16))
        act = hg * jax.nn.sigmoid(hg) * hu * w
        acc_ref[...] += _dot(act.astype(BF16), wd_ref[ee].astype(BF16))

    @pl.when(e == pl.num_programs(1) - 1)
    def _():
        g2 = mod_ref[pl.ds(row, 1), 5 * D_MODEL:6 * D_MODEL]
        x2 = x_ref[...] + g2 * acc_ref[...]
        out_ref[...] = _rms(x2) * fnw_ref[...] if final else x2


def _moe(x2d, mod, n2w, w_router, b_router, wg, wu, wd, wgs, wus, wds, fnw, rows_per_mod, final):
    t = x2d.shape[0]
    tm = 1024
    kern = functools.partial(_moe_kernel, rows_per_mod=rows_per_mod, tm=tm, final=final)
    const = lambda shape: pl.BlockSpec(shape, lambda i, e: (0,) * len(shape), pipeline_mode=pl.Buffered(1))
    return pl.pallas_call(
        kern,
        grid=(t // tm, N_EXPERTS // E_BLK),
        in_specs=[pl.BlockSpec((tm, D_MODEL), lambda i, e: (i, 0)),
                  const(mod.shape), const((1, D_MODEL)), const((D_MODEL, N_EXPERTS)), const((1, N_EXPERTS)),
                  pl.BlockSpec((E_BLK, D_MODEL, D_EXPERT), lambda i, e: (e, 0, 0)),
                  pl.BlockSpec((E_BLK, D_MODEL, D_EXPERT), lambda i, e: (e, 0, 0)),
                  pl.BlockSpec((E_BLK, D_EXPERT, D_MODEL), lambda i, e: (e, 0, 0)),
                  const((D_MODEL, D_EXPERT)), const((D_MODEL, D_EXPERT)), const((D_EXPERT, D_MODEL)),
                  const((1, D_MODEL))],
        out_specs=pl.BlockSpec((tm, D_MODEL), lambda i, e: (i, 0)),
        out_shape=jax.ShapeDtypeStruct((t, D_MODEL), F32),
        scratch_shapes=[pltpu.VMEM((tm, D_MODEL), BF16), pltpu.VMEM((tm, N_EXPERTS), F32),
                        pltpu.VMEM((tm, D_MODEL), F32)],
        compiler_params=pltpu.CompilerParams(dimension_semantics=("arbitrary", "arbitrary"),
                                             vmem_limit_bytes=VMEM_LIMIT),
        name="moe",
    )(x2d, mod, n2w, w_router, b_router, wg, wu, wd, wgs, wus, wds, fnw)


def kernel(x_prompt, x_sample, state_gla, c, c_ctx, w_mod, b_mod, norm1_w, w_in, conv_w, conv_b, w_decay, b_decay,
           gla_norm_w, w_br_conv, w_br_gla, w_out, norm2_w, w_router, b_router, w_gate_e, w_up_e, w_down_e,
           w_gate_s, w_up_s, w_down_s, final_norm_w):
    depth = w_mod.shape[0]
    nb_p, len_p, _ = x_prompt.shape
    nb_s, len_s, _ = x_sample.shape
    yp = x_prompt.reshape(nb_p * len_p, D_MODEL)
    ys = x_sample.reshape(nb_s * len_s, D_MODEL)
    fnw = final_norm_w.reshape(1, D_MODEL)

    cond = jnp.concatenate([c_ctx[None, :], c, jnp.zeros((8 - 1 - nb_s, D_MODEL), F32)], axis=0)
    states = []
    for l in range(depth):
        mod = _modulation(cond, w_mod[l], b_mod[l].reshape(1, -1))
        mod_p, mod_s = mod[0:1], mod[1:1 + nb_s]

        wl = w_in[l]
        w_in_r = jnp.concatenate([wl[:, :3584], wl[:, 3616:5664], wl[:, 3584:3616],
                                  jnp.zeros((D_MODEL, D_PROJ - 5664), F32)], axis=1).astype(BF16)
        wdec = jnp.zeros((2, LR_PAD, D_GLA), F32)
        wdec = wdec.at[0, 0:GLA_RANK].set(w_decay[l, 0]).at[1, GLA_RANK:2 * GLA_RANK].set(w_decay[l, 1])
        n1w = norm1_w[l].reshape(1, D_MODEL)
        mix_w = (conv_w[l], conv_b[l].reshape(1, D_CONV), wdec, b_decay[l], gla_norm_w[l].reshape(1, D_GLA),
                 w_br_conv[l].astype(BF16), w_br_gla[l].astype(BF16), w_out[l].astype(BF16))
        moe_w = (norm2_w[l].reshape(1, D_MODEL), w_router[l], b_router[l].reshape(1, N_EXPERTS),
                 w_gate_e[l], w_up_e[l], w_down_e[l], w_gate_s[l], w_up_s[l], w_down_s[l])

        proj_p = _in_proj(yp, mod_p, n1w, w_in_r, rows_per_mod=nb_p * len_p)
        yp, st = _mixer(proj_p, yp, mod_p, *mix_w, None, seq_len=len_p, row_len=len_p, emit_state=True)
        states.append(st)
        proj_s = _in_proj(ys, mod_s, n1w, w_in_r, rows_per_mod=len_s)
        (ys,) = _mixer(proj_s, ys, mod_s, *mix_w, state_gla[:, l], seq_len=len_s, row_len=GRID_W, emit_state=False)

        final = l == depth - 1
        yp = _moe(yp, mod_p, *moe_w, fnw, rows_per_mod=nb_p * len_p, final=final)
        ys = _moe(ys, mod_s, *moe_w, fnw, rows_per_mod=len_s, final=final)
    new_state = jnp.stack(states, axis=1)
    return (yp.reshape(nb_p, len_p, D_MODEL), ys.reshape(nb_s, len_s, D_MODEL), new_state)
```

```python
import functools

import jax
import jax.numpy as jnp
from jax import lax
from jax.experimental import pallas as pl
from jax.experimental.pallas import tpu as pltpu

F32 = jnp.float32
BF16 = jnp.bfloat16

D_MODEL = 1024
GRID_W = 64
D_CONV = 512
N_HEADS = 4
HEAD_D = 128
D_GLA = N_HEADS * HEAD_D
GLA_RANK = 16
GLA_GATE_NORM = 16.0
CHUNK = 64
SUB = 8
N_SUB = CHUNK // SUB
N_EXPERTS = 64
TOP_K = 8
D_EXPERT = 256
ROUTED_SCALE = 2.5
EPS = 1e-6

C_U, C_GB, C_GC, C_Q, C_K, C_V, C_GO = 0, 512, 1024, 1536, 2048, 2560, 3072
C_BRC, C_BRG, C_LR = 3584, 4608, 5632
D_PROJ = 5760
LR_PAD = 128

VMEM_LIMIT = 56 * 1024 * 1024


def _dot(a, b):
    return jnp.dot(a, b, preferred_element_type=F32)


def _dot_nt(a, b):
    return lax.dot_general(a, b, (((1,), (1,)), ((), ())), preferred_element_type=F32)


def _dot_tn(a, b):
    return lax.dot_general(a, b, (((0,), (0,)), ((), ())), preferred_element_type=F32)


def _dot_hi(a, b):
    return jnp.dot(a, b, preferred_element_type=F32, precision=lax.Precision.HIGHEST)


def _split_bf16(x):
    hi = x.astype(BF16)
    lo = (x - hi.astype(F32)).astype(BF16)
    return hi, lo


def _rms(x):
    return x * lax.rsqrt(jnp.mean(x * x, axis=-1, keepdims=True) + EPS)


def _mod_kernel(cond_ref, w_ref, b_ref, o_ref):
    c = cond_ref[...]
    o_ref[...] = _dot_hi(c * jax.nn.sigmoid(c), w_ref[...]) + b_ref[...]


def _modulation(cond, w_mod, b_mod):
    n_rows = cond.shape[0]
    tn = 1536
    return pl.pallas_call(
        _mod_kernel,
        grid=(6 * D_MODEL // tn,),
        in_specs=[pl.BlockSpec((n_rows, D_MODEL), lambda j: (0, 0)),
                  pl.BlockSpec((D_MODEL, tn), lambda j: (0, j)),
                  pl.BlockSpec((1, tn), lambda j: (0, j))],
        out_specs=pl.BlockSpec((n_rows, tn), lambda j: (0, j)),
        out_shape=jax.ShapeDtypeStruct((n_rows, 6 * D_MODEL), F32),
        compiler_params=pltpu.CompilerParams(dimension_semantics=("arbitrary",),
                                             vmem_limit_bytes=VMEM_LIMIT),
        name="modulation",
    )(cond, w_mod, b_mod)


def _inproj_kernel(x_ref, mod_ref, nw_ref, w_ref, o_ref, *, rows_per_mod, tm):
    i = pl.program_id(1)
    row = (i * tm) // rows_per_mod
    sh = mod_ref[pl.ds(row, 1), 0:D_MODEL]
    sc = mod_ref[pl.ds(row, 1), D_MODEL:2 * D_MODEL]
    h = _rms(x_ref[...]) * nw_ref[...] * (1.0 + sc) + sh
    o_ref[...] = _dot(h.astype(BF16), w_ref[...]).astype(BF16)


def _in_proj(x2d, mod, norm_w, w_in_r, rows_per_mod):
    t = x2d.shape[0]
    tm, tn = 512, 1920
    kern = functools.partial(_inproj_kernel, rows_per_mod=rows_per_mod, tm=tm)
    return pl.pallas_call(
        kern,
        grid=(D_PROJ // tn, t // tm),
        in_specs=[pl.BlockSpec((tm, D_MODEL), lambda j, i: (i, 0)),
                  pl.BlockSpec(mod.shape, lambda j, i: (0, 0)),
                  pl.BlockSpec((1, D_MODEL), lambda j, i: (0, 0)),
                  pl.BlockSpec((D_MODEL, tn), lambda j, i: (0, j))],
        out_specs=pl.BlockSpec((tm, tn), lambda j, i: (i, j)),
        out_shape=jax.ShapeDtypeStruct((t, D_PROJ), BF16),
        compiler_params=pltpu.CompilerParams(dimension_semantics=("arbitrary", "arbitrary"),
                                             vmem_limit_bytes=VMEM_LIMIT),
        name="in_proj",
    )(x2d, mod, norm_w, w_in_r)


def _log_sigmoid(x):
    return jnp.minimum(x, 0.0) - jnp.log1p(jnp.exp(-jnp.abs(x)))


def _gla_chunk_head(qc, kc, vc, bc, st, rev):
    lane = lax.broadcasted_iota(jnp.int32, (SUB, CHUNK), 1)
    sub = lax.broadcasted_iota(jnp.int32, (SUB, CHUNK), 0)
    tot = bc[0:1] if rev else bc[CHUNK - 1:CHUNK]

    o = _dot_nt((qc * jnp.exp(bc)).astype(BF16), st.astype(BF16))
    k_tail = kc * jnp.exp(tot - bc)
    st_new = st * jnp.exp(tot) + _dot_tn(vc.astype(BF16), k_tail.astype(BF16))

    lhs_segs, rhs_segs = [], []

    def rows(before, mid, after):
        parts = ([jnp.zeros((before, HEAD_D), F32)] if before else []) + [mid]
        parts += [jnp.zeros((after, HEAD_D), F32)] if after else []
        return jnp.concatenate(parts, axis=0) if len(parts) > 1 else mid

    key_blocks = range(1, N_SUB) if rev else range(0, N_SUB - 1)
    for jb in key_blocks:
        r0 = jb * SUB
        ref_row = bc[r0:r0 + 1] if rev else bc[r0 + SUB - 1:r0 + SUB]
        ke = kc[r0:r0 + SUB] * jnp.exp(jnp.minimum(ref_row - bc[r0:r0 + SUB], 0.0))
        rhs_segs.append(rows(r0, ke, CHUNK - r0 - SUB))
        if rev:
            ql = qc[:r0] * jnp.exp(jnp.minimum(bc[:r0] - ref_row, 0.0))
            lhs_segs.append(rows(0, ql, CHUNK - r0))
        else:
            ql = qc[r0 + SUB:] * jnp.exp(jnp.minimum(bc[r0 + SUB:] - ref_row, 0.0))
            lhs_segs.append(rows(r0 + SUB, ql, 0))
    far = _dot_nt(jnp.concatenate(lhs_segs, axis=1).astype(BF16),
                  jnp.concatenate(rhs_segs, axis=1).astype(BF16))

    blocks = []
    for ib in range(N_SUB):
        r0 = ib * SUB
        qi, bi = qc[r0:r0 + SUB], bc[r0:r0 + SUB]
        acc = jnp.zeros((SUB, CHUNK), F32)
        for jj in range(SUB):
            j = r0 + jj
            e = jnp.exp(jnp.minimum(bi - bc[j:j + 1], 0.0))
            col = jnp.sum(qi * (kc[j:j + 1] * e), axis=-1, keepdims=True)
            acc = jnp.where(lane == j, col, acc)
        keep = (lane - r0 >= sub) if rev else (lane - r0 <= sub)
        blocks.append(jnp.where(keep, acc, 0.0))
    scores = far + jnp.concatenate(blocks, axis=0)
    o = o + _dot(scores.astype(BF16), vc.astype(BF16))
    return o, st_new


def _mixer_kernel(*refs, seq_len, row_len, has_s0, emit_state):
    it = iter(refs)
    proj_ref, x_ref, mod_ref, cw_ref, cb_ref, wdec_ref, bdec_ref, gnw_ref = (next(it) for _ in range(8))
    wbc_ref, wbg_ref, wout_ref = (next(it) for _ in range(3))
    s0_ref = next(it) if has_s0 else None
    out_ref = next(it)
    st_out_ref = next(it) if emit_state else None
    la_f_ref, la_b_ref, o_ref, st_ref = (next(it) for _ in range(4))

    L = seq_len
    n_chunks = L // CHUNK
    tr = 256
    assert tr % row_len == 0 and L % tr == 0

    def decay_body(t, carry):
        r0 = pl.multiple_of(t * tr, tr)
        lr = proj_ref[pl.ds(r0, tr), C_LR:C_LR + LR_PAD]
        for d, ref in ((0, la_f_ref), (1, la_b_ref)):
            whi, wlo = _split_bf16(wdec_ref[d])
            z = _dot(lr, whi) + _dot(lr, wlo) + bdec_ref[d:d + 1]
            ref[pl.ds(r0, tr), :] = _log_sigmoid(z) * (1.0 / GLA_GATE_NORM)
        return carry
    lax.fori_loop(0, L // tr, decay_body, 0)

    ri = lax.broadcasted_iota(jnp.int32, (CHUNK, CHUNK), 0)
    ci = lax.broadcasted_iota(jnp.int32, (CHUNK, CHUNK), 1)

    for rev, la_ref in ((False, la_f_ref), (True, la_b_ref)):
        d = 1 if rev else 0
        tri = jnp.where((ci >= ri) if rev else (ci <= ri), 1.0, 0.0).astype(BF16)
        for h in range(N_HEADS):
            if has_s0:
                st_ref[h] = s0_ref[0, d, h].T
            else:
                st_ref[h] = jnp.zeros((HEAD_D, HEAD_D), F32)

        def chunk_body(c, carry, rev=rev, la_ref=la_ref, tri=tri):
            cc = (n_chunks - 1 - c) if rev else c
            r0 = pl.multiple_of(cc * CHUNK, CHUNK)
            la_hi, la_lo = _split_bf16(la_ref[pl.ds(r0, CHUNK), :])
            bcum = _dot(tri, la_hi) + _dot(tri, la_lo)
            for h in range(N_HEADS):
                lo, hi = h * HEAD_D, (h + 1) * HEAD_D
                qc = proj_ref[pl.ds(r0, CHUNK), C_Q + lo:C_Q + hi].astype(F32) * (HEAD_D ** -0.5)
                kc = proj_ref[pl.ds(r0, CHUNK), C_K + lo:C_K + hi].astype(F32)
                vc = proj_ref[pl.ds(r0, CHUNK), C_V + lo:C_V + hi].astype(F32)
                oc, st_new = _gla_chunk_head(qc, kc, vc, bcum[:, lo:hi], st_ref[h], rev)
                st_ref[h] = st_new
                if rev:
                    o_ref[pl.ds(r0, CHUNK), lo:hi] += oc
                else:
                    o_ref[pl.ds(r0, CHUNK), lo:hi] = oc
            return carry
        lax.fori_loop(0, n_chunks, chunk_body, 0)

        if emit_state:
            for h in range(N_HEADS):
                st_out_ref[0, d, h] = st_ref[h].T

    mod_row = pl.program_id(0) if mod_ref.shape[0] > 1 else 0
    g1 = mod_ref[pl.ds(mod_row, 1), 2 * D_MODEL:3 * D_MODEL]
    hsel_r = lax.broadcasted_iota(jnp.int32, (D_GLA, D_GLA), 0) // HEAD_D
    hsel_c = lax.broadcasted_iota(jnp.int32, (D_GLA, D_GLA), 1) // HEAD_D
    head_avg = jnp.where(hsel_r == hsel_c, 1.0 / HEAD_D, 0.0).astype(BF16)
    pos = lax.broadcasted_iota(jnp.int32, (tr, 1), 0)

    def dense_body(t, carry):
        r0 = pl.multiple_of(t * tr, tr)
        rows = pl.ds(r0, tr)
        cu = (proj_ref[rows, C_GC:C_GC + D_CONV].astype(F32) * proj_ref[rows, C_U:C_U + D_CONV].astype(F32))
        in_row = pos % row_len
        left = jnp.where(in_row == 0, 0.0, pltpu.roll(cu, 1, axis=0))
        right = jnp.where(in_row == row_len - 1, 0.0, pltpu.roll(cu, tr - 1, axis=0))
        conv = cw_ref[0:1] * left + cw_ref[1:2] * cu + cw_ref[2:3] * right + cb_ref[...]
        y_conv = proj_ref[rows, C_GB:C_GB + D_CONV].astype(F32) * conv

        o = o_ref[rows, :]
        osq_hi, osq_lo = _split_bf16(o * o)
        ms = _dot(osq_hi, head_avg) + _dot(osq_lo, head_avg)
        g_out = proj_ref[rows, C_GO:C_GO + D_GLA].astype(F32)
        y_gla = o * lax.rsqrt(ms + EPS) * gnw_ref[...] * (g_out * jax.nn.sigmoid(g_out))

        merged = (jax.nn.sigmoid(proj_ref[rows, C_BRC:C_BRC + D_MODEL].astype(F32)) * _dot(y_conv.astype(BF16), wbc_ref[...])
                  + jax.nn.sigmoid(proj_ref[rows, C_BRG:C_BRG + D_MODEL].astype(F32)) * _dot(y_gla.astype(BF16), wbg_ref[...]))
        out_ref[rows, :] = x_ref[rows, :] + g1 * _dot(merged.astype(BF16), wout_ref[...])
        return carry
    lax.fori_loop(0, L // tr, dense_body, 0)


def _mixer(proj, x2d, mod, conv_w, conv_b, wdec, bdec, gnw, wbc, wbg, wout, s0, *, seq_len, row_len, emit_state):
    t = x2d.shape[0]
    nb = t // seq_len
    has_s0 = s0 is not None
    one = pl.Buffered(1)
    const = lambda shape: pl.BlockSpec(shape, lambda b: (0,) * len(shape), pipeline_mode=one)
    in_specs = [pl.BlockSpec((seq_len, D_PROJ), lambda b: (b, 0), pipeline_mode=one),
                pl.BlockSpec((seq_len, D_MODEL), lambda b: (b, 0)),
                const(mod.shape),
                const((3, D_CONV)), const((1, D_CONV)), const((2, LR_PAD, D_GLA)), const((2, D_GLA)),
                const((1, D_GLA)), const((D_CONV, D_MODEL)), const((D_GLA, D_MODEL)), const((D_MODEL, D_MODEL))]
    args = [proj, x2d, mod, conv_w, conv_b, wdec, bdec, gnw, wbc, wbg, wout]
    if has_s0:
        in_specs.append(pl.BlockSpec((1, 2, N_HEADS, HEAD_D, HEAD_D), lambda b: (b, 0, 0, 0, 0)))
        args.append(s0)
    out_specs = [pl.BlockSpec((seq_len, D_MODEL), lambda b: (b, 0))]
    out_shape = [jax.ShapeDtypeStruct((t, D_MODEL), F32)]
    if emit_state:
        out_specs.append(pl.BlockSpec((1, 2, N_HEADS, HEAD_D, HEAD_D), lambda b: (b, 0, 0, 0, 0)))
        out_shape.append(jax.ShapeDtypeStruct((nb, 2, N_HEADS, HEAD_D, HEAD_D), F32))
    kern = functools.partial(_mixer_kernel, seq_len=seq_len, row_len=row_len, has_s0=has_s0, emit_state=emit_state)
    return pl.pallas_call(
        kern,
        grid=(nb,),
        in_specs=in_specs,
        out_specs=out_specs,
        out_shape=out_shape,
        scratch_shapes=[pltpu.VMEM((seq_len, D_GLA), F32), pltpu.VMEM((seq_len, D_GLA), F32),
                        pltpu.VMEM((seq_len, D_GLA), F32), pltpu.VMEM((N_HEADS, HEAD_D, HEAD_D), F32)],
        compiler_params=pltpu.CompilerParams(dimension_semantics=("arbitrary",),
                                             vmem_limit_bytes=VMEM_LIMIT),
        name="mixer",
    )(*args)


I32 = jnp.int32
TB = 256
ROW_ALIGN = 16
TR = 256
R_LOC = 3072
H2W = 1152


def _dot_nt_hi(a, b):
    return lax.dot_general(a, b, (((1,), (1,)), ((), ())), preferred_element_type=F32,
                           precision=lax.Precision.HIGHEST)


def _select_x(i, n_p_tiles, xp_ref, xs_ref):
    return jnp.where(i < n_p_tiles, xp_ref[...], xs_ref[...])


def _mod_row(i, n_p_tiles, tiles_per_mod):
    return jnp.where(i < n_p_tiles, 0, 1 + (i - n_p_tiles) // tiles_per_mod)


def _route_kernel(xp_ref, xs_ref, mod_ref, n2w_ref, wrt_ref, brb_ref, h2_ref, lpos_ref, cnt_ref, *,
                  n_p_tiles, tiles_per_mod):
    i = pl.program_id(0)
    row = _mod_row(i, n_p_tiles, tiles_per_mod)
    x = _select_x(i, n_p_tiles, xp_ref, xs_ref)
    sh = mod_ref[pl.ds(row, 1), 3 * D_MODEL:4 * D_MODEL]
    sc = mod_ref[pl.ds(row, 1), 4 * D_MODEL:5 * D_MODEL]
    h2 = _rms(x) * n2w_ref[...] * (1.0 + sc) + sh

    scores = jax.nn.sigmoid(_dot_nt_hi(wrt_ref[...], h2))
    biased = scores + brb_ref[...]
    eidx = lax.broadcasted_iota(I32, scores.shape, 0)
    picks = []
    for _k in range(TOP_K):
        m = jnp.max(biased, axis=0, keepdims=True)
        first = jnp.min(jnp.where(biased == m, eidx, N_EXPERTS), axis=0, keepdims=True)
        pick = eidx == first
        picks.append(pick)
        biased = jnp.where(pick, -jnp.inf, biased)
    sel = jnp.zeros(scores.shape, F32)
    for pick in picks:
        sel = jnp.where(pick, 1.0, sel)
    selsc = sel * scores
    comb = selsc / jnp.sum(selsc, axis=0, keepdims=True) * ROUTED_SCALE

    selb = sel.astype(BF16)
    tr_ = lax.broadcasted_iota(I32, (TB, TB), 0)
    tc_ = lax.broadcasted_iota(I32, (TB, TB), 1)
    rank = _dot(selb, jnp.where(tr_ < tc_, 1.0, 0.0).astype(BF16))
    n_b = _dot(selb, jnp.ones((TB, 128), BF16))
    m_b = jnp.floor((n_b + (ROW_ALIGN - 1)) * (1.0 / ROW_ALIGN)) * ROW_ALIGN
    er_ = lax.broadcasted_iota(I32, (N_EXPERTS, N_EXPERTS), 0)
    ec_ = lax.broadcasted_iota(I32, (N_EXPERTS, N_EXPERTS), 1)
    loff_b = _dot(jnp.where(ec_ < er_, 1.0, 0.0).astype(BF16), m_b.astype(BF16))
    lposf = jnp.concatenate([loff_b] * (TB // 128), axis=1) + rank
    rows = [jnp.sum(jnp.where(pick, lposf, 0.0), axis=0, keepdims=True) for pick in picks]
    lpos_ref[0] = jnp.concatenate(rows, axis=0).astype(I32)
    cnt_ref[0] = m_b

    combt = comb.T
    chi = combt.astype(BF16).astype(F32)
    h2_ref[:, 0:D_MODEL] = h2.astype(BF16)
    h2_ref[:, D_MODEL:H2W] = jnp.concatenate([chi, combt - chi], axis=1).astype(BF16)


def _route(x1p, x1s, mod, n2w, w_router_t, b_router_b, *, tiles_per_mod):
    n_p, n_s = x1p.shape[0] // TB, x1s.shape[0] // TB
    nt = n_p + n_s
    kern = functools.partial(_route_kernel, n_p_tiles=n_p, tiles_per_mod=tiles_per_mod)
    const = lambda shape: pl.BlockSpec(shape, lambda i: (0,) * len(shape))
    return pl.pallas_call(
        kern,
        grid=(nt,),
        in_specs=[pl.BlockSpec((TB, D_MODEL), lambda i: (jnp.minimum(i, n_p - 1), 0)),
                  pl.BlockSpec((TB, D_MODEL), lambda i: (jnp.maximum(i - n_p, 0), 0)),
                  const(mod.shape), const((1, D_MODEL)), const((N_EXPERTS, D_MODEL)), const((N_EXPERTS, TB))],
        out_specs=[pl.BlockSpec((TB, H2W), lambda i: (i, 0)),
                   pl.BlockSpec((1, TOP_K, TB), lambda i: (i, 0, 0)),
                   pl.BlockSpec((1, N_EXPERTS, 128), lambda i: (i, 0, 0))],
        out_shape=[jax.ShapeDtypeStruct((nt * TB, H2W), BF16),
                   jax.ShapeDtypeStruct((nt, TOP_K, TB), I32),
                   jax.ShapeDtypeStruct((nt, N_EXPERTS, 128), F32)],
        compiler_params=pltpu.CompilerParams(dimension_semantics=("arbitrary",), vmem_limit_bytes=VMEM_LIMIT),
        name="moe_route",
    )(x1p, x1s, mod, n2w, w_router_t, b_router_b)


def _plan_kernel(cnt_ref, off_ref, loff_ref, msz_ref, tail_ref, te_ref, *, nt, n_row_tiles):
    lane = lax.broadcasted_iota(I32, (N_EXPERTS, 128), 1)
    m = jnp.zeros((N_EXPERTS, 128), F32)
    for i in range(nt):
        m = jnp.where(lane == i, cnt_ref[i], m)
    total = jnp.broadcast_to(jnp.sum(m, axis=1, keepdims=True), (N_EXPERTS, 128))
    gsz = jnp.floor((total + (TR - 1)) * (1.0 / TR)) * TR
    er_ = lax.broadcasted_iota(I32, (N_EXPERTS, N_EXPERTS), 0)
    ec_ = lax.broadcasted_iota(I32, (N_EXPERTS, N_EXPERTS), 1)
    lstrict = jnp.where(ec_ < er_, 1.0, 0.0)
    ir_ = lax.broadcasted_iota(I32, (128, 128), 0)
    ic_ = lax.broadcasted_iota(I32, (128, 128), 1)
    ustrict = jnp.where(ir_ < ic_, 1.0, 0.0)
    gstart = _dot_hi(lstrict, gsz)
    off_ref[...] = (gstart + _dot_hi(m, ustrict)).astype(I32)
    loff_ref[...] = _dot_hi(lstrict, m).astype(I32)
    msz_ref[...] = m.astype(I32)
    tail_ref[...] = jnp.where(lane == 0, gstart + total, jnp.where(lane == 1, gsz - total, 0.0)).astype(I32)
    gend = gstart + gsz
    te_lanes = te_ref.shape[1]
    gend_w = jnp.concatenate([gend] * (te_lanes // 128), axis=1)
    tile_start = lax.broadcasted_iota(I32, (N_EXPERTS, te_lanes), 1).astype(F32) * TR
    te = jnp.sum(jnp.where(gend_w <= tile_start, 1.0, 0.0), axis=0, keepdims=True)
    used = gend[N_EXPERTS - 1:N_EXPERTS, 0:1] * (1.0 / TR)
    te_lane = lax.broadcasted_iota(I32, (1, te_lanes), 1)
    te = jnp.where(te_lane == n_row_tiles, used, jnp.minimum(te, N_EXPERTS - 1.0))
    te_ref[...] = jnp.broadcast_to(te, te_ref.shape).astype(I32)


def _plan(cnt, n_row_tiles):
    nt = cnt.shape[0]
    assert nt <= 128
    te_lanes = -(-(n_row_tiles + 1) // 128) * 128
    tab = jax.ShapeDtypeStruct((N_EXPERTS, 128), I32)
    return pl.pallas_call(
        functools.partial(_plan_kernel, nt=nt, n_row_tiles=n_row_tiles),
        out_shape=[tab, tab, tab, tab, jax.ShapeDtypeStruct((8, te_lanes), I32)],
        compiler_params=pltpu.CompilerParams(vmem_limit_bytes=VMEM_LIMIT),
        name="moe_plan",
    )(cnt)


def _run_copies(msz_ref, tile, make_copy, start):
    def body(e, carry):
        m = msz_ref[e, tile]

        @pl.when(m > 0)
        def _():
            cp = make_copy(e, pl.multiple_of(m, ROW_ALIGN))
            if start:
                cp.start()
            else:
                cp.wait()
        return carry
    lax.fori_loop(0, N_EXPERTS, body, 0)


def _dispatch_kernel(off_ref, loff_ref, msz_ref, tail_ref, h2_ref, lpos_ref, xs_hbm, xloc_ref, zero_ref, sem,
                     tail_sem, *, nt):
    i = pl.program_id(0)
    slot = i % 2

    def copy_for(tile, slot_):
        def make(e, m):
            lo = pl.multiple_of(loff_ref[e, tile], ROW_ALIGN)
            of = pl.multiple_of(off_ref[e, tile], ROW_ALIGN)
            return pltpu.make_async_copy(xloc_ref.at[slot_, pl.ds(lo, m)], xs_hbm.at[pl.ds(of, m)], sem.at[slot_])
        return make

    used_rows = loff_ref[N_EXPERTS - 1, i] + msz_ref[N_EXPERTS - 1, i]
    lpos = lpos_ref[0]
    h2 = h2_ref[...]
    ck = 256
    for c in range(R_LOC // ck):
        @pl.when(c * ck < used_rows)
        def _(c=c):
            r = lax.broadcasted_iota(I32, (ck, TB), 0) + c * ck
            d = jnp.zeros((ck, TB), F32)
            for k in range(TOP_K):
                d = jnp.where(r == lpos[k:k + 1, :], 1.0, d)
            res = _dot(d.astype(BF16), h2)
            xloc_ref[slot, c * ck:(c + 1) * ck, :] = res.astype(BF16)

    _run_copies(msz_ref, i, copy_for(i, slot), True)

    @pl.when(i > 0)
    def _():
        _run_copies(msz_ref, i - 1, copy_for(i - 1, 1 - slot), False)

    @pl.when(i == nt - 1)
    def _():
        zero_ref[...] = jnp.zeros(zero_ref.shape, BF16)

        def tail_copies(start):
            def body(e, carry):
                n = tail_ref[e, 1]

                @pl.when(n > 0)
                def _():
                    st = pl.multiple_of(tail_ref[e, 0], ROW_ALIGN)
                    nn = pl.multiple_of(n, ROW_ALIGN)
                    cp = pltpu.make_async_copy(zero_ref.at[pl.ds(0, nn)], xs_hbm.at[pl.ds(st, nn)], tail_sem)
                    if start:
                        cp.start()
                    else:
                        cp.wait()
                return carry
            lax.fori_loop(0, N_EXPERTS, body, 0)
        tail_copies(True)
        _run_copies(msz_ref, i, copy_for(i, slot), False)
        tail_copies(False)


def _dispatch(off, loff, msz, tail, h2ext, lpos, n_rows):
    nt = lpos.shape[0]
    grid_spec = pltpu.PrefetchScalarGridSpec(
        num_scalar_prefetch=4,
        grid=(nt,),
        in_specs=[pl.BlockSpec((TB, H2W), lambda i, *_: (i, 0)),
                  pl.BlockSpec((1, TOP_K, TB), lambda i, *_: (i, 0, 0))],
        out_specs=pl.BlockSpec(memory_space=pl.ANY),
        scratch_shapes=[pltpu.VMEM((2, R_LOC, H2W), BF16), pltpu.VMEM((TR, H2W), BF16),
                        pltpu.SemaphoreType.DMA((2,)), pltpu.SemaphoreType.DMA],
    )
    return pl.pallas_call(
        functools.partial(_dispatch_kernel, nt=nt),
        grid_spec=grid_spec,
        out_shape=jax.ShapeDtypeStruct((n_rows, H2W), BF16),
        compiler_params=pltpu.CompilerParams(dimension_semantics=("arbitrary",), vmem_limit_bytes=VMEM_LIMIT),
        name="moe_dispatch",
    )(off, loff, msz, tail, h2ext, lpos)


def _expert_kernel(te_ref, xs_ref, wg_ref, wu_ref, wd_ref, ys_ref, *, n_row_tiles):
    j = pl.program_id(0)

    @pl.when(j < te_ref[n_row_tiles])
    def _():
        x = xs_ref[:, 0:D_MODEL]
        ext = xs_ref[:, D_MODEL:H2W].astype(F32)
        wts = ext[:, :N_EXPERTS] + ext[:, N_EXPERTS:]
        lane = lax.broadcasted_iota(I32, wts.shape, 1)
        w = jnp.sum(jnp.where(lane == te_ref[j], wts, 0.0), axis=-1, keepdims=True)
        hg = _dot(x, wg_ref[0].astype(BF16))
        hu = _dot(x, wu_ref[0].astype(BF16))
        act = hg * jax.nn.sigmoid(hg) * hu * w
        ys_ref[...] = _dot(act.astype(BF16), wd_ref[0].astype(BF16)).astype(BF16)


def _experts(te, xs, wg, wu, wd, n_row_tiles):
    def row_tile(j, te_ref):
        return jnp.minimum(j, te_ref[n_row_tiles] - 1)
    grid_spec = pltpu.PrefetchScalarGridSpec(
        num_scalar_prefetch=1,
        grid=(n_row_tiles,),
        in_specs=[pl.BlockSpec((TR, H2W), lambda j, te_ref: (row_tile(j, te_ref), 0)),
                  pl.BlockSpec((1, D_MODEL, D_EXPERT), lambda j, te_ref: (te_ref[row_tile(j, te_ref)], 0, 0)),
                  pl.BlockSpec((1, D_MODEL, D_EXPERT), lambda j, te_ref: (te_ref[row_tile(j, te_ref)], 0, 0)),
                  pl.BlockSpec((1, D_EXPERT, D_MODEL), lambda j, te_ref: (te_ref[row_tile(j, te_ref)], 0, 0))],
        out_specs=pl.BlockSpec((TR, D_MODEL), lambda j, te_ref: (row_tile(j, te_ref), 0)),
    )
    return pl.pallas_call(
        functools.partial(_expert_kernel, n_row_tiles=n_row_tiles),
        grid_spec=grid_spec,
        out_shape=jax.ShapeDtypeStruct((n_row_tiles * TR, D_MODEL), BF16),
        compiler_params=pltpu.CompilerParams(dimension_semantics=("arbitrary",), vmem_limit_bytes=VMEM_LIMIT),
        name="moe_experts",
    )(te, xs, wg, wu, wd)


def _combine_kernel(off_ref, loff_ref, msz_ref, lpos_ref, h2_ref, xp_ref, xs_ref, mod_ref, wgs_ref, wus_ref, wds_ref,
                    fnw_ref, ysrt_hbm, yp_ref, ys_ref, yloc_ref, acc_ref, sem, *, nt, n_p_tiles, tiles_per_mod,
                    final):
    i = pl.program_id(0)
    slot = i % 2

    def copy_for(tile, slot_):
        def make(e, m):
            lo = pl.multiple_of(loff_ref[e, tile], ROW_ALIGN)
            of = pl.multiple_of(off_ref[e, tile], ROW_ALIGN)
            return pltpu.make_async_copy(ysrt_hbm.at[pl.ds(of, m)], yloc_ref.at[slot_, pl.ds(lo, m)], sem.at[slot_])
        return make

    @pl.when(i == 0)
    def _():
        yloc_ref[...] = jnp.zeros(yloc_ref.shape, BF16)
        _run_copies(msz_ref, 0, copy_for(0, 0), True)

    @pl.when(i + 1 < nt)
    def _():
        _run_copies(msz_ref, i + 1, copy_for(i + 1, 1 - slot), True)

    hb = h2_ref[...]
    hg = _dot(hb, wgs_ref[...].astype(BF16))
    hu = _dot(hb, wus_ref[...].astype(BF16))
    acc_ref[...] = _dot((hg * jax.nn.sigmoid(hg) * hu).astype(BF16), wds_ref[...].astype(BF16))

    _run_copies(msz_ref, i, copy_for(i, slot), False)

    used_rows = loff_ref[N_EXPERTS - 1, i] + msz_ref[N_EXPERTS - 1, i]
    lpos_pad = jnp.concatenate([lpos_ref[0].astype(F32), jnp.zeros((128 - TOP_K, TB), F32)], axis=0)
    lposc = lpos_pad.T.astype(I32)
    ck = 512
    for c in range(R_LOC // ck):
        @pl.when(c * ck < used_rows)
        def _(c=c):
            r = lax.broadcasted_iota(I32, (TB, ck), 1) + c * ck
            cm = jnp.zeros((TB, ck), F32)
            for k in range(TOP_K):
                cm = jnp.where(r == lposc[:, k:k + 1], 1.0, cm)
            acc_ref[...] += _dot(cm.astype(BF16), yloc_ref[slot, c * ck:(c + 1) * ck, :])

    row = _mod_row(i, n_p_tiles, tiles_per_mod)
    g2 = mod_ref[pl.ds(row, 1), 5 * D_MODEL:6 * D_MODEL]
    x2 = _select_x(i, n_p_tiles, xp_ref, xs_ref) + g2 * acc_ref[...]
    y = _rms(x2) * fnw_ref[...] if final else x2

    @pl.when(i < n_p_tiles)
    def _():
        yp_ref[...] = y

    @pl.when(i >= n_p_tiles)
    def _():
        ys_ref[...] = y


def _combine(off, loff, msz, lpos, h2ext, x1p, x1s, mod, wgs, wus, wds, fnw, ysorted, *, tiles_per_mod, final):
    n_p, n_s = x1p.shape[0] // TB, x1s.shape[0] // TB
    nt = n_p + n_s
    const = lambda shape: pl.BlockSpec(shape, lambda i, *_: (0,) * len(shape), pipeline_mode=pl.Buffered(1))
    p_idx = lambda i, *_: (jnp.minimum(i, n_p - 1), 0)
    s_idx = lambda i, *_: (jnp.maximum(i - n_p, 0), 0)
    grid_spec = pltpu.PrefetchScalarGridSpec(
        num_scalar_prefetch=3,
        grid=(nt,),
        in_specs=[pl.BlockSpec((1, TOP_K, TB), lambda i, *_: (i, 0, 0)),
                  pl.BlockSpec((TB, D_MODEL), lambda i, *_: (i, 0)),
                  pl.BlockSpec((TB, D_MODEL), p_idx), pl.BlockSpec((TB, D_MODEL), s_idx),
                  const(mod.shape), const((D_MODEL, D_EXPERT)), const((D_MODEL, D_EXPERT)), const((D_EXPERT, D_MODEL)),
                  const((1, D_MODEL)), pl.BlockSpec(memory_space=pl.ANY)],
        out_specs=[pl.BlockSpec((TB, D_MODEL), p_idx), pl.BlockSpec((TB, D_MODEL), s_idx)],
        scratch_shapes=[pltpu.VMEM((2, R_LOC, D_MODEL), BF16), pltpu.VMEM((TB, D_MODEL), F32),
                        pltpu.SemaphoreType.DMA((2,))],
    )
    kern = functools.partial(_combine_kernel, nt=nt, n_p_tiles=n_p, tiles_per_mod=tiles_per_mod, final=final)
    return pl.pallas_call(
        kern,
        grid_spec=grid_spec,
        out_shape=[jax.ShapeDtypeStruct(x1p.shape, F32), jax.ShapeDtypeStruct(x1s.shape, F32)],
        compiler_params=pltpu.CompilerParams(dimension_semantics=("arbitrary",), vmem_limit_bytes=VMEM_LIMIT),
        name="moe_combine",
    )(off, loff, msz, lpos, h2ext, x1p, x1s, mod, wgs, wus, wds, fnw, ysorted)


def _moe(x1p, x1s, mod, n2w, w_router, b_router, wg, wu, wd, wgs, wus, wds, fnw, *, tokens_per_mod, final):
    assert R_LOC >= TB * TOP_K + N_EXPERTS * (ROW_ALIGN - 1) and tokens_per_mod % TB == 0
    nt = (x1p.shape[0] + x1s.shape[0]) // TB
    n_rows_max = nt * TB * TOP_K + nt * N_EXPERTS * (ROW_ALIGN - 1) + N_EXPERTS * (TR - ROW_ALIGN)
    n_row_tiles = -(-n_rows_max // TR)
    tiles_per_mod = tokens_per_mod // TB
    brb = jnp.broadcast_to(b_router.reshape(N_EXPERTS, 1), (N_EXPERTS, TB))
    h2ext, lpos, cnt = _route(x1p, x1s, mod, n2w, w_router.T, brb, tiles_per_mod=tiles_per_mod)
    off, loff, msz, tail, te = _plan(cnt, n_row_tiles)
    xs = _dispatch(off, loff, msz, tail, h2ext, lpos, n_row_tiles * TR)
    ysorted = _experts(te[0], xs, wg, wu, wd, n_row_tiles)
    return _combine(off, loff, msz, lpos, h2ext, x1p, x1s, mod, wgs, wus, wds, fnw, ysorted,
                    tiles_per_mod=tiles_per_mod, final=final)


def kernel(x_prompt, x_sample, state_gla, c, c_ctx, w_mod, b_mod, norm1_w, w_in, conv_w, conv_b, w_decay, b_decay,
           gla_norm_w, w_br_conv, w_br_gla, w_out, norm2_w, w_router, b_router, w_gate_e, w_up_e, w_down_e,
           w_gate_s, w_up_s, w_down_s, final_norm_w):
    depth = w_mod.shape[0]
    nb_p, len_p, _ = x_prompt.shape
    nb_s, len_s, _ = x_sample.shape
    yp = x_prompt.reshape(nb_p * len_p, D_MODEL)
    ys = x_sample.reshape(nb_s * len_s, D_MODEL)
    fnw = final_norm_w.reshape(1, D_MODEL)

    cond = jnp.concatenate([c_ctx[None, :], c, jnp.zeros((8 - 1 - nb_s, D_MODEL), F32)], axis=0)
    states = []
    for l in range(depth):
        mod = _modulation(cond, w_mod[l], b_mod[l].reshape(1, -1))
        mod_p, mod_s = mod[0:1], mod[1:1 + nb_s]

        wl = w_in[l]
        w_in_r = jnp.concatenate([wl[:, :3584], wl[:, 3616:5664], wl[:, 3584:3616],
                                  jnp.zeros((D_MODEL, D_PROJ - 5664), F32)], axis=1).astype(BF16)
        wdec = jnp.zeros((2, LR_PAD, D_GLA), F32)
        wdec = wdec.at[0, 0:GLA_RANK].set(w_decay[l, 0]).at[1, GLA_RANK:2 * GLA_RANK].set(w_decay[l, 1])
        n1w = norm1_w[l].reshape(1, D_MODEL)
        mix_w = (conv_w[l], conv_b[l].reshape(1, D_CONV), wdec, b_decay[l], gla_norm_w[l].reshape(1, D_GLA),
                 w_br_conv[l].astype(BF16), w_br_gla[l].astype(BF16), w_out[l].astype(BF16))
        moe_w = (norm2_w[l].reshape(1, D_MODEL), w_router[l], b_router[l],
                 w_gate_e[l], w_up_e[l], w_down_e[l], w_gate_s[l], w_up_s[l], w_down_s[l])

        proj_p = _in_proj(yp, mod_p, n1w, w_in_r, rows_per_mod=nb_p * len_p)
        yp, st = _mixer(proj_p, yp, mod_p, *mix_w, None, seq_len=len_p, row_len=len_p, emit_state=True)
        states.append(st)
        proj_s = _in_proj(ys, mod_s, n1w, w_in_r, rows_per_mod=len_s)
        (ys,) = _mixer(proj_s, ys, mod_s, *mix_w, state_gla[:, l], seq_len=len_s, row_len=GRID_W, emit_state=False)

        yp, ys = _moe(yp, ys, mod, *moe_w, fnw, tokens_per_mod=len_s, final=l == depth - 1)
    new_state = jnp.stack(states, axis=1)
    return (yp.reshape(nb_p, len_p, D_MODEL), ys.reshape(nb_s, len_s, D_MODEL), new_state)
```

```python
import functools

import jax
import jax.numpy as jnp
from jax import lax
from jax.experimental import pallas as pl
from jax.experimental.pallas import tpu as pltpu

F32 = jnp.float32
BF16 = jnp.bfloat16

D_MODEL = 1024
GRID_W = 64
D_CONV = 512
N_HEADS = 4
HEAD_D = 128
D_GLA = N_HEADS * HEAD_D
GLA_RANK = 16
GLA_GATE_NORM = 16.0
CHUNK = 64
SUB = 8
N_SUB = CHUNK // SUB
N_EXPERTS = 64
TOP_K = 8
D_EXPERT = 256
ROUTED_SCALE = 2.5
EPS = 1e-6

C_U, C_GB, C_GC, C_Q, C_K, C_V, C_GO = 0, 512, 1024, 1536, 2048, 2560, 3072
C_BRC, C_BRG, C_LR = 3584, 4608, 5632
D_PROJ = 5760
LR_PAD = 128

VMEM_LIMIT = 56 * 1024 * 1024


def _dot(a, b):
    return jnp.dot(a, b, preferred_element_type=F32)


def _dot_nt(a, b):
    return lax.dot_general(a, b, (((1,), (1,)), ((), ())), preferred_element_type=F32)


def _dot_tn(a, b):
    return lax.dot_general(a, b, (((0,), (0,)), ((), ())), preferred_element_type=F32)


def _dot_hi(a, b):
    return jnp.dot(a, b, preferred_element_type=F32, precision=lax.Precision.HIGHEST)


def _split_bf16(x):
    hi = x.astype(BF16)
    lo = (x - hi.astype(F32)).astype(BF16)
    return hi, lo


def _rms(x):
    return x * lax.rsqrt(jnp.mean(x * x, axis=-1, keepdims=True) + EPS)


def _mod_kernel(cond_ref, w_ref, b_ref, o_ref):
    c = cond_ref[...]
    o_ref[...] = _dot_hi(c * jax.nn.sigmoid(c), w_ref[...]) + b_ref[...]


def _modulation(cond, w_mod, b_mod):
    n_rows = cond.shape[0]
    tn = 1536
    return pl.pallas_call(
        _mod_kernel,
        grid=(6 * D_MODEL // tn,),
        in_specs=[pl.BlockSpec((n_rows, D_MODEL), lambda j: (0, 0)),
                  pl.BlockSpec((D_MODEL, tn), lambda j: (0, j)),
                  pl.BlockSpec((1, tn), lambda j: (0, j))],
        out_specs=pl.BlockSpec((n_rows, tn), lambda j: (0, j)),
        out_shape=jax.ShapeDtypeStruct((n_rows, 6 * D_MODEL), F32),
        compiler_params=pltpu.CompilerParams(dimension_semantics=("arbitrary",),
                                             vmem_limit_bytes=VMEM_LIMIT),
        name="modulation",
    )(cond, w_mod, b_mod)


def _inproj_kernel(x_ref, mod_ref, nw_ref, w_ref, o_ref, *, rows_per_mod, tm):
    i = pl.program_id(1)
    row = (i * tm) // rows_per_mod
    sh = mod_ref[pl.ds(row, 1), 0:D_MODEL]
    sc = mod_ref[pl.ds(row, 1), D_MODEL:2 * D_MODEL]
    h = _rms(x_ref[...]) * nw_ref[...] * (1.0 + sc) + sh
    o_ref[...] = _dot(h.astype(BF16), w_ref[...]).astype(BF16)


def _in_proj(x2d, mod, norm_w, w_in_r, rows_per_mod):
    t = x2d.shape[0]
    tm, tn = 512, 1920
    kern = functools.partial(_inproj_kernel, rows_per_mod=rows_per_mod, tm=tm)
    return pl.pallas_call(
        kern,
        grid=(D_PROJ // tn, t // tm),
        in_specs=[pl.BlockSpec((tm, D_MODEL), lambda j, i: (i, 0)),
                  pl.BlockSpec(mod.shape, lambda j, i: (0, 0)),
                  pl.BlockSpec((1, D_MODEL), lambda j, i: (0, 0)),
                  pl.BlockSpec((D_MODEL, tn), lambda j, i: (0, j))],
        out_specs=pl.BlockSpec((tm, tn), lambda j, i: (i, j)),
        out_shape=jax.ShapeDtypeStruct((t, D_PROJ), BF16),
        compiler_params=pltpu.CompilerParams(dimension_semantics=("arbitrary", "arbitrary"),
                                             vmem_limit_bytes=VMEM_LIMIT),
        name="in_proj",
    )(x2d, mod, norm_w, w_in_r)


def _log_sigmoid(x):
    return jnp.minimum(x, 0.0) - jnp.log1p(jnp.exp(-jnp.abs(x)))


def _gla_chunk_head(qc, kc, vc, bc, st, rev):
    lane = lax.broadcasted_iota(jnp.int32, (SUB, CHUNK), 1)
    sub = lax.broadcasted_iota(jnp.int32, (SUB, CHUNK), 0)
    tot = bc[0:1] if rev else bc[CHUNK - 1:CHUNK]

    o = _dot_nt((qc * jnp.exp(bc)).astype(BF16), st.astype(BF16))
    k_tail = kc * jnp.exp(tot - bc)
    st_new = st * jnp.exp(tot) + _dot_tn(vc.astype(BF16), k_tail.astype(BF16))

    lhs_segs, rhs_segs = [], []

    def rows(before, mid, after):
        parts = ([jnp.zeros((before, HEAD_D), F32)] if before else []) + [mid]
        parts += [jnp.zeros((after, HEAD_D), F32)] if after else []
        return jnp.concatenate(parts, axis=0) if len(parts) > 1 else mid

    key_blocks = range(1, N_SUB) if rev else range(0, N_SUB - 1)
    for jb in key_blocks:
        r0 = jb * SUB
        ref_row = bc[r0:r0 + 1] if rev else bc[r0 + SUB - 1:r0 + SUB]
        ke = kc[r0:r0 + SUB] * jnp.exp(jnp.minimum(ref_row - bc[r0:r0 + SUB], 0.0))
        rhs_segs.append(rows(r0, ke, CHUNK - r0 - SUB))
        if rev:
            ql = qc[:r0] * jnp.exp(jnp.minimum(bc[:r0] - ref_row, 0.0))
            lhs_segs.append(rows(0, ql, CHUNK - r0))
        else:
            ql = qc[r0 + SUB:] * jnp.exp(jnp.minimum(bc[r0 + SUB:] - ref_row, 0.0))
            lhs_segs.append(rows(r0 + SUB, ql, 0))
    far = _dot_nt(jnp.concatenate(lhs_segs, axis=1).astype(BF16),
                  jnp.concatenate(rhs_segs, axis=1).astype(BF16))

    blocks = []
    for ib in range(N_SUB):
        r0 = ib * SUB
        qi, bi = qc[r0:r0 + SUB], bc[r0:r0 + SUB]
        acc = jnp.zeros((SUB, CHUNK), F32)
        for jj in range(SUB):
            j = r0 + jj
            e = jnp.exp(jnp.minimum(bi - bc[j:j + 1], 0.0))
            col = jnp.sum(qi * (kc[j:j + 1] * e), axis=-1, keepdims=True)
            acc = jnp.where(lane == j, col, acc)
        keep = (lane - r0 >= sub) if rev else (lane - r0 <= sub)
        blocks.append(jnp.where(keep, acc, 0.0))
    scores = far + jnp.concatenate(blocks, axis=0)
    o = o + _dot(scores.astype(BF16), vc.astype(BF16))
    return o, st_new


def _mixer_kernel(*refs, seq_len, row_len, has_s0, emit_state):
    it = iter(refs)
    proj_ref, x_ref, mod_ref, cw_ref, cb_ref, wdec_ref, bdec_ref, gnw_ref = (next(it) for _ in range(8))
    wbc_ref, wbg_ref, wout_ref = (next(it) for _ in range(3))
    s0_ref = next(it) if has_s0 else None
    out_ref = next(it)
    st_out_ref = next(it) if emit_state else None
    la_f_ref, la_b_ref, o_ref, st_ref = (next(it) for _ in range(4))

    L = seq_len
    n_chunks = L // CHUNK
    tr = 256
    assert tr % row_len == 0 and L % tr == 0

    def decay_body(t, carry):
        r0 = pl.multiple_of(t * tr, tr)
        lr = proj_ref[pl.ds(r0, tr), C_LR:C_LR + LR_PAD]
        for d, ref in ((0, la_f_ref), (1, la_b_ref)):
            whi, wlo = _split_bf16(wdec_ref[d])
            z = _dot(lr, whi) + _dot(lr, wlo) + bdec_ref[d:d + 1]
            ref[pl.ds(r0, tr), :] = _log_sigmoid(z) * (1.0 / GLA_GATE_NORM)
        return carry
    lax.fori_loop(0, L // tr, decay_body, 0)

    ri = lax.broadcasted_iota(jnp.int32, (CHUNK, CHUNK), 0)
    ci = lax.broadcasted_iota(jnp.int32, (CHUNK, CHUNK), 1)

    for rev, la_ref in ((False, la_f_ref), (True, la_b_ref)):
        d = 1 if rev else 0
        tri = jnp.where((ci >= ri) if rev else (ci <= ri), 1.0, 0.0).astype(BF16)
        for h in range(N_HEADS):
            if has_s0:
                st_ref[h] = s0_ref[0, d, h].T
            else:
                st_ref[h] = jnp.zeros((HEAD_D, HEAD_D), F32)

        def chunk_body(c, carry, rev=rev, la_ref=la_ref, tri=tri):
            cc = (n_chunks - 1 - c) if rev else c
            r0 = pl.multiple_of(cc * CHUNK, CHUNK)
            la_hi, la_lo = _split_bf16(la_ref[pl.ds(r0, CHUNK), :])
            bcum = _dot(tri, la_hi) + _dot(tri, la_lo)
            for h in range(N_HEADS):
                lo, hi = h * HEAD_D, (h + 1) * HEAD_D
                qc = proj_ref[pl.ds(r0, CHUNK), C_Q + lo:C_Q + hi].astype(F32) * (HEAD_D ** -0.5)
                kc = proj_ref[pl.ds(r0, CHUNK), C_K + lo:C_K + hi].astype(F32)
                vc = proj_ref[pl.ds(r0, CHUNK), C_V + lo:C_V + hi].astype(F32)
                oc, st_new = _gla_chunk_head(qc, kc, vc, bcum[:, lo:hi], st_ref[h], rev)
                st_ref[h] = st_new
                if rev:
                    o_ref[pl.ds(r0, CHUNK), lo:hi] += oc
                else:
                    o_ref[pl.ds(r0, CHUNK), lo:hi] = oc
            return carry
        lax.fori_loop(0, n_chunks, chunk_body, 0)

        if emit_state:
            for h in range(N_HEADS):
                st_out_ref[0, d, h] = st_ref[h].T

    mod_row = pl.program_id(0) if mod_ref.shape[0] > 1 else 0
    g1 = mod_ref[pl.ds(mod_row, 1), 2 * D_MODEL:3 * D_MODEL]
    hsel_r = lax.broadcasted_iota(jnp.int32, (D_GLA, D_GLA), 0) // HEAD_D
    hsel_c = lax.broadcasted_iota(jnp.int32, (D_GLA, D_GLA), 1) // HEAD_D
    head_avg = jnp.where(hsel_r == hsel_c, 1.0 / HEAD_D, 0.0).astype(BF16)
    pos = lax.broadcasted_iota(jnp.int32, (tr, 1), 0)

    def dense_body(t, carry):
        r0 = pl.multiple_of(t * tr, tr)
        rows = pl.ds(r0, tr)
        cu = (proj_ref[rows, C_GC:C_GC + D_CONV].astype(F32) * proj_ref[rows, C_U:C_U + D_CONV].astype(F32))
        in_row = pos % row_len
        left = jnp.where(in_row == 0, 0.0, pltpu.roll(cu, 1, axis=0))
        right = jnp.where(in_row == row_len - 1, 0.0, pltpu.roll(cu, tr - 1, axis=0))
        conv = cw_ref[0:1] * left + cw_ref[1:2] * cu + cw_ref[2:3] * right + cb_ref[...]
        y_conv = proj_ref[rows, C_GB:C_GB + D_CONV].astype(F32) * conv

        o = o_ref[rows, :]
        osq_hi, osq_lo = _split_bf16(o * o)
        ms = _dot(osq_hi, head_avg) + _dot(osq_lo, head_avg)
        g_out = proj_ref[rows, C_GO:C_GO + D_GLA].astype(F32)
        y_gla = o * lax.rsqrt(ms + EPS) * gnw_ref[...] * (g_out * jax.nn.sigmoid(g_out))

        merged = (jax.nn.sigmoid(proj_ref[rows, C_BRC:C_BRC + D_MODEL].astype(F32)) * _dot(y_conv.astype(BF16), wbc_ref[...])
                  + jax.nn.sigmoid(proj_ref[rows, C_BRG:C_BRG + D_MODEL].astype(F32)) * _dot(y_gla.astype(BF16), wbg_ref[...]))
        out_ref[rows, :] = x_ref[rows, :] + g1 * _dot(merged.astype(BF16), wout_ref[...])
        return carry
    lax.fori_loop(0, L // tr, dense_body, 0)


def _mixer(proj, x2d, mod, conv_w, conv_b, wdec, bdec, gnw, wbc, wbg, wout, s0, *, seq_len, row_len, emit_state):
    t = x2d.shape[0]
    nb = t // seq_len
    has_s0 = s0 is not None
    one = pl.Buffered(1)
    const = lambda shape: pl.BlockSpec(shape, lambda b: (0,) * len(shape), pipeline_mode=one)
    in_specs = [pl.BlockSpec((seq_len, D_PROJ), lambda b: (b, 0), pipeline_mode=one),
                pl.BlockSpec((seq_len, D_MODEL), lambda b: (b, 0)),
                const(mod.shape),
                const((3, D_CONV)), const((1, D_CONV)), const((2, LR_PAD, D_GLA)), const((2, D_GLA)),
                const((1, D_GLA)), const((D_CONV, D_MODEL)), const((D_GLA, D_MODEL)), const((D_MODEL, D_MODEL))]
    args = [proj, x2d, mod, conv_w, conv_b, wdec, bdec, gnw, wbc, wbg, wout]
    if has_s0:
        in_specs.append(pl.BlockSpec((1, 2, N_HEADS, HEAD_D, HEAD_D), lambda b: (b, 0, 0, 0, 0)))
        args.append(s0)
    out_specs = [pl.BlockSpec((seq_len, D_MODEL), lambda b: (b, 0))]
    out_shape = [jax.ShapeDtypeStruct((t, D_MODEL), F32)]
    if emit_state:
        out_specs.append(pl.BlockSpec((1, 2, N_HEADS, HEAD_D, HEAD_D), lambda b: (b, 0, 0, 0, 0)))
        out_shape.append(jax.ShapeDtypeStruct((nb, 2, N_HEADS, HEAD_D, HEAD_D), F32))
    kern = functools.partial(_mixer_kernel, seq_len=seq_len, row_len=row_len, has_s0=has_s0, emit_state=emit_state)
    return pl.pallas_call(
        kern,
        grid=(nb,),
        in_specs=in_specs,
        out_specs=out_specs,
        out_shape=out_shape,
        scratch_shapes=[pltpu.VMEM((seq_len, D_GLA), F32), pltpu.VMEM((seq_len, D_GLA), F32),
                        pltpu.VMEM((seq_len, D_GLA), F32), pltpu.VMEM((N_HEADS, HEAD_D, HEAD_D), F32)],
        compiler_params=pltpu.CompilerParams(dimension_semantics=("arbitrary",),
                                             vmem_limit_bytes=VMEM_LIMIT),
        name="mixer",
    )(*args)


I32 = jnp.int32
TB = 256
ROW_ALIGN = 16
TR = 512
R_LOC = 3072
H2W = 1152


def _dot_nt_hi(a, b):
    return lax.dot_general(a, b, (((1,), (1,)), ((), ())), preferred_element_type=F32,
                           precision=lax.Precision.HIGHEST)


def _select_x(i, n_p_tiles, xp_ref, xs_ref):
    return jnp.where(i < n_p_tiles, xp_ref[...], xs_ref[...])


def _mod_row(i, n_p_tiles, tiles_per_mod):
    return jnp.where(i < n_p_tiles, 0, 1 + (i - n_p_tiles) // tiles_per_mod)


def _route_kernel(xp_ref, xs_ref, mod_ref, n2w_ref, wrt_ref, brb_ref, h2_ref, lpos_ref, cnt_ref, *,
                  n_p_tiles, tiles_per_mod):
    i = pl.program_id(0)
    row = _mod_row(i, n_p_tiles, tiles_per_mod)
    x = _select_x(i, n_p_tiles, xp_ref, xs_ref)
    sh = mod_ref[pl.ds(row, 1), 3 * D_MODEL:4 * D_MODEL]
    sc = mod_ref[pl.ds(row, 1), 4 * D_MODEL:5 * D_MODEL]
    h2 = _rms(x) * n2w_ref[...] * (1.0 + sc) + sh

    scores = jax.nn.sigmoid(_dot_nt_hi(wrt_ref[...], h2))
    biased = scores + brb_ref[...]
    eidx = lax.broadcasted_iota(I32, scores.shape, 0)
    picks = []
    for _k in range(TOP_K):
        m = jnp.max(biased, axis=0, keepdims=True)
        first = jnp.min(jnp.where(biased == m, eidx, N_EXPERTS), axis=0, keepdims=True)
        pick = eidx == first
        picks.append(pick)
        biased = jnp.where(pick, -jnp.inf, biased)
    sel = jnp.zeros(scores.shape, F32)
    for pick in picks:
        sel = jnp.where(pick, 1.0, sel)
    selsc = sel * scores
    comb = selsc / jnp.sum(selsc, axis=0, keepdims=True) * ROUTED_SCALE

    selb = sel.astype(BF16)
    tr_ = lax.broadcasted_iota(I32, (TB, TB), 0)
    tc_ = lax.broadcasted_iota(I32, (TB, TB), 1)
    rank = _dot(selb, jnp.where(tr_ < tc_, 1.0, 0.0).astype(BF16))
    n_b = _dot(selb, jnp.ones((TB, 128), BF16))
    m_b = jnp.floor((n_b + (ROW_ALIGN - 1)) * (1.0 / ROW_ALIGN)) * ROW_ALIGN
    er_ = lax.broadcasted_iota(I32, (N_EXPERTS, N_EXPERTS), 0)
    ec_ = lax.broadcasted_iota(I32, (N_EXPERTS, N_EXPERTS), 1)
    loff_b = _dot(jnp.where(ec_ < er_, 1.0, 0.0).astype(BF16), m_b.astype(BF16))
    lposf = jnp.concatenate([loff_b] * (TB // 128), axis=1) + rank
    rows = [jnp.sum(jnp.where(pick, lposf, 0.0), axis=0, keepdims=True) for pick in picks]
    lpos_ref[0] = jnp.concatenate(rows, axis=0).astype(I32)
    cnt_ref[0] = m_b

    combt = comb.T
    chi = combt.astype(BF16).astype(F32)
    h2_ref[:, 0:D_MODEL] = h2.astype(BF16)
    h2_ref[:, D_MODEL:H2W] = jnp.concatenate([chi, combt - chi], axis=1).astype(BF16)


def _route(x1p, x1s, mod, n2w, w_router_t, b_router_b, *, tiles_per_mod):
    n_p, n_s = x1p.shape[0] // TB, x1s.shape[0] // TB
    nt = n_p + n_s
    kern = functools.partial(_route_kernel, n_p_tiles=n_p, tiles_per_mod=tiles_per_mod)
    const = lambda shape: pl.BlockSpec(shape, lambda i: (0,) * len(shape))
    return pl.pallas_call(
        kern,
        grid=(nt,),
        in_specs=[pl.BlockSpec((TB, D_MODEL), lambda i: (jnp.minimum(i, n_p - 1), 0)),
                  pl.BlockSpec((TB, D_MODEL), lambda i: (jnp.maximum(i - n_p, 0), 0)),
                  const(mod.shape), const((1, D_MODEL)), const((N_EXPERTS, D_MODEL)), const((N_EXPERTS, TB))],
        out_specs=[pl.BlockSpec((TB, H2W), lambda i: (i, 0)),
                   pl.BlockSpec((1, TOP_K, TB), lambda i: (i, 0, 0)),
                   pl.BlockSpec((1, N_EXPERTS, 128), lambda i: (i, 0, 0))],
        out_shape=[jax.ShapeDtypeStruct((nt * TB, H2W), BF16),
                   jax.ShapeDtypeStruct((nt, TOP_K, TB), I32),
                   jax.ShapeDtypeStruct((nt, N_EXPERTS, 128), F32)],
        compiler_params=pltpu.CompilerParams(dimension_semantics=("arbitrary",), vmem_limit_bytes=VMEM_LIMIT),
        name="moe_route",
    )(x1p, x1s, mod, n2w, w_router_t, b_router_b)


def _plan_kernel(cnt_ref, off_ref, loff_ref, msz_ref, tail_ref, te_ref, *, nt, n_row_tiles):
    lane = lax.broadcasted_iota(I32, (N_EXPERTS, 128), 1)
    m = jnp.zeros((N_EXPERTS, 128), F32)
    for i in range(nt):
        m = jnp.where(lane == i, cnt_ref[i], m)
    total = jnp.broadcast_to(jnp.sum(m, axis=1, keepdims=True), (N_EXPERTS, 128))
    gsz = jnp.floor((total + (TR - 1)) * (1.0 / TR)) * TR
    er_ = lax.broadcasted_iota(I32, (N_EXPERTS, N_EXPERTS), 0)
    ec_ = lax.broadcasted_iota(I32, (N_EXPERTS, N_EXPERTS), 1)
    lstrict = jnp.where(ec_ < er_, 1.0, 0.0)
    ir_ = lax.broadcasted_iota(I32, (128, 128), 0)
    ic_ = lax.broadcasted_iota(I32, (128, 128), 1)
    ustrict = jnp.where(ir_ < ic_, 1.0, 0.0)
    gstart = _dot_hi(lstrict, gsz)
    off_ref[...] = (gstart + _dot_hi(m, ustrict)).astype(I32)
    loff_ref[...] = _dot_hi(lstrict, m).astype(I32)
    msz_ref[...] = m.astype(I32)
    tail_ref[...] = jnp.where(lane == 0, gstart + total, jnp.where(lane == 1, gsz - total, 0.0)).astype(I32)
    gend = gstart + gsz
    te_lanes = te_ref.shape[1]
    gend_w = jnp.concatenate([gend] * (te_lanes // 128), axis=1)
    tile_start = lax.broadcasted_iota(I32, (N_EXPERTS, te_lanes), 1).astype(F32) * TR
    te = jnp.sum(jnp.where(gend_w <= tile_start, 1.0, 0.0), axis=0, keepdims=True)
    used = gend[N_EXPERTS - 1:N_EXPERTS, 0:1] * (1.0 / TR)
    te_lane = lax.broadcasted_iota(I32, (1, te_lanes), 1)
    te = jnp.where(te_lane == n_row_tiles, used, jnp.minimum(te, N_EXPERTS - 1.0))
    te_ref[...] = jnp.broadcast_to(te, te_ref.shape).astype(I32)


def _plan(cnt, n_row_tiles):
    nt = cnt.shape[0]
    assert nt <= 128
    te_lanes = -(-(n_row_tiles + 1) // 128) * 128
    tab = jax.ShapeDtypeStruct((N_EXPERTS, 128), I32)
    return pl.pallas_call(
        functools.partial(_plan_kernel, nt=nt, n_row_tiles=n_row_tiles),
        out_shape=[tab, tab, tab, tab, jax.ShapeDtypeStruct((8, te_lanes), I32)],
        compiler_params=pltpu.CompilerParams(vmem_limit_bytes=VMEM_LIMIT),
        name="moe_plan",
    )(cnt)


def _start_copies(msz_ref, tile, make_copy):
    def body(e, carry):
        m = msz_ref[e, tile]

        @pl.when(m > 0)
        def _():
            make_copy(e, pl.multiple_of(m, ROW_ALIGN)).start()
        return carry
    lax.fori_loop(0, N_EXPERTS, body, 0, unroll=8)


def _tile_rows(loff_ref, msz_ref, tile):
    return pl.multiple_of(loff_ref[N_EXPERTS - 1, tile] + msz_ref[N_EXPERTS - 1, tile], ROW_ALIGN)


def _dispatch_kernel(off_ref, loff_ref, msz_ref, tail_ref, h2_ref, lpos_ref, xs_hbm, xloc_ref, zero_ref, sem,
                     tail_sem, *, nt):
    i = pl.program_id(0)
    slot = i % 2

    def copy_for(tile, slot_):
        def make(e, m):
            lo = pl.multiple_of(loff_ref[e, tile], ROW_ALIGN)
            of = pl.multiple_of(off_ref[e, tile], ROW_ALIGN)
            return pltpu.make_async_copy(xloc_ref.at[slot_, pl.ds(lo, m)], xs_hbm.at[pl.ds(of, m)], sem.at[slot_])
        return make

    def wait_tile(tile, slot_):
        n = _tile_rows(loff_ref, msz_ref, tile)
        pltpu.make_async_copy(xloc_ref.at[slot_, pl.ds(0, n)], xs_hbm.at[pl.ds(0, n)], sem.at[slot_]).wait()

    used_rows = _tile_rows(loff_ref, msz_ref, i)
    lpos = lpos_ref[0]
    h2 = h2_ref[...]
    ck = 256
    for c in range(R_LOC // ck):
        @pl.when(c * ck < used_rows)
        def _(c=c):
            r = lax.broadcasted_iota(I32, (ck, TB), 0) + c * ck
            d = jnp.zeros((ck, TB), F32)
            for k in range(TOP_K):
                d = jnp.where(r == lpos[k:k + 1, :], 1.0, d)
            res = _dot(d.astype(BF16), h2)
            xloc_ref[slot, c * ck:(c + 1) * ck, :] = res.astype(BF16)

    _start_copies(msz_ref, i, copy_for(i, slot))

    @pl.when(i > 0)
    def _():
        wait_tile(i - 1, 1 - slot)

    @pl.when(i == nt - 1)
    def _():
        zero_ref[...] = jnp.zeros(zero_ref.shape, BF16)

        def tail_copies(start):
            def body(e, carry):
                n = tail_ref[e, 1]

                @pl.when(n > 0)
                def _():
                    st = pl.multiple_of(tail_ref[e, 0], ROW_ALIGN)
                    nn = pl.multiple_of(n, ROW_ALIGN)
                    cp = pltpu.make_async_copy(zero_ref.at[pl.ds(0, nn)], xs_hbm.at[pl.ds(st, nn)], tail_sem)
                    if start:
                        cp.start()
                    else:
                        cp.wait()
                return carry
            lax.fori_loop(0, N_EXPERTS, body, 0)
        tail_copies(True)
        wait_tile(i, slot)
        tail_copies(False)


def _dispatch(off, loff, msz, tail, h2ext, lpos, n_rows):
    nt = lpos.shape[0]
    grid_spec = pltpu.PrefetchScalarGridSpec(
        num_scalar_prefetch=4,
        grid=(nt,),
        in_specs=[pl.BlockSpec((TB, H2W), lambda i, *_: (i, 0)),
                  pl.BlockSpec((1, TOP_K, TB), lambda i, *_: (i, 0, 0))],
        out_specs=pl.BlockSpec(memory_space=pl.ANY),
        scratch_shapes=[pltpu.VMEM((2, R_LOC, H2W), BF16), pltpu.VMEM((TR, H2W), BF16),
                        pltpu.SemaphoreType.DMA((2,)), pltpu.SemaphoreType.DMA],
    )
    return pl.pallas_call(
        functools.partial(_dispatch_kernel, nt=nt),
        grid_spec=grid_spec,
        out_shape=jax.ShapeDtypeStruct((n_rows, H2W), BF16),
        compiler_params=pltpu.CompilerParams(dimension_semantics=("arbitrary",), vmem_limit_bytes=VMEM_LIMIT),
        name="moe_dispatch",
    )(off, loff, msz, tail, h2ext, lpos)


def _expert_kernel(te_ref, xs_ref, wg_ref, wu_ref, wd_ref, ys_ref, wgu_ref, wdb_ref):
    j = pl.program_id(0)
    e = te_ref[j]

    @pl.when((j == 0) | (e != te_ref[jnp.maximum(j - 1, 0)]))
    def _():
        wgu_ref[:, :D_EXPERT] = wg_ref[0].astype(BF16)
        wgu_ref[:, D_EXPERT:] = wu_ref[0].astype(BF16)
        wdb_ref[...] = wd_ref[0].astype(BF16)

    x = xs_ref[:, 0:D_MODEL]
    ext = xs_ref[:, D_MODEL:H2W].astype(F32)
    wts = ext[:, :N_EXPERTS] + ext[:, N_EXPERTS:]
    lane = lax.broadcasted_iota(I32, wts.shape, 1)
    w = jnp.sum(jnp.where(lane == e, wts, 0.0), axis=-1, keepdims=True)
    h = _dot(x, wgu_ref[...])
    hg, hu = h[:, :D_EXPERT], h[:, D_EXPERT:]
    act = hg * jax.nn.sigmoid(hg) * hu * w
    ys_ref[...] = _dot(act.astype(BF16), wdb_ref[...]).astype(BF16)


def _experts(te, n_used, xs, wg, wu, wd, n_row_tiles):
    grid_spec = pltpu.PrefetchScalarGridSpec(
        num_scalar_prefetch=1,
        grid=(n_used,),
        in_specs=[pl.BlockSpec((TR, H2W), lambda j, te_ref: (j, 0)),
                  pl.BlockSpec((1, D_MODEL, D_EXPERT), lambda j, te_ref: (te_ref[j], 0, 0)),
                  pl.BlockSpec((1, D_MODEL, D_EXPERT), lambda j, te_ref: (te_ref[j], 0, 0)),
                  pl.BlockSpec((1, D_EXPERT, D_MODEL), lambda j, te_ref: (te_ref[j], 0, 0))],
        out_specs=pl.BlockSpec((TR, D_MODEL), lambda j, te_ref: (j, 0)),
        scratch_shapes=[pltpu.VMEM((D_MODEL, 2 * D_EXPERT), BF16), pltpu.VMEM((D_EXPERT, D_MODEL), BF16)],
    )
    return pl.pallas_call(
        _expert_kernel,
        grid_spec=grid_spec,
        out_shape=jax.ShapeDtypeStruct((n_row_tiles * TR, D_MODEL), BF16),
        compiler_params=pltpu.CompilerParams(dimension_semantics=("arbitrary",), vmem_limit_bytes=VMEM_LIMIT),
        name="moe_experts",
    )(te, xs, wg, wu, wd)


def _combine_kernel(off_ref, loff_ref, msz_ref, lpos_ref, h2_ref, xp_ref, xs_ref, mod_ref, wgs_ref, wus_ref, wds_ref,
                    fnw_ref, ysrt_hbm, yp_ref, ys_ref, yloc_ref, acc_ref, sem, *, nt, n_p_tiles, tiles_per_mod,
                    final):
    i = pl.program_id(0)
    slot = i % 2

    def copy_for(tile, slot_):
        def make(e, m):
            lo = pl.multiple_of(loff_ref[e, tile], ROW_ALIGN)
            of = pl.multiple_of(off_ref[e, tile], ROW_ALIGN)
            return pltpu.make_async_copy(ysrt_hbm.at[pl.ds(of, m)], yloc_ref.at[slot_, pl.ds(lo, m)], sem.at[slot_])
        return make

    @pl.when(i == 0)
    def _():
        yloc_ref[...] = jnp.zeros(yloc_ref.shape, BF16)
        _start_copies(msz_ref, 0, copy_for(0, 0))

    @pl.when(i + 1 < nt)
    def _():
        _start_copies(msz_ref, i + 1, copy_for(i + 1, 1 - slot))

    hb = h2_ref[...]
    hg = _dot(hb, wgs_ref[...].astype(BF16))
    hu = _dot(hb, wus_ref[...].astype(BF16))
    acc_ref[...] = _dot((hg * jax.nn.sigmoid(hg) * hu).astype(BF16), wds_ref[...].astype(BF16))

    used_rows = _tile_rows(loff_ref, msz_ref, i)
    pltpu.make_async_copy(ysrt_hbm.at[pl.ds(0, used_rows)], yloc_ref.at[slot, pl.ds(0, used_rows)],
                          sem.at[slot]).wait()

    lpos_pad = jnp.concatenate([lpos_ref[0].astype(F32), jnp.zeros((128 - TOP_K, TB), F32)], axis=0)
    lposc = lpos_pad.T.astype(I32)
    ck = 512
    for c in range(R_LOC // ck):
        @pl.when(c * ck < used_rows)
        def _(c=c):
            r = lax.broadcasted_iota(I32, (TB, ck), 1) + c * ck
            cm = jnp.zeros((TB, ck), F32)
            for k in range(TOP_K):
                cm = jnp.where(r == lposc[:, k:k + 1], 1.0, cm)
            acc_ref[...] += _dot(cm.astype(BF16), yloc_ref[slot, c * ck:(c + 1) * ck, :])

    row = _mod_row(i, n_p_tiles, tiles_per_mod)
    g2 = mod_ref[pl.ds(row, 1), 5 * D_MODEL:6 * D_MODEL]
    x2 = _select_x(i, n_p_tiles, xp_ref, xs_ref) + g2 * acc_ref[...]
    y = _rms(x2) * fnw_ref[...] if final else x2

    @pl.when(i < n_p_tiles)
    def _():
        yp_ref[...] = y

    @pl.when(i >= n_p_tiles)
    def _():
        ys_ref[...] = y


def _combine(off, loff, msz, lpos, h2ext, x1p, x1s, mod, wgs, wus, wds, fnw, ysorted, *, tiles_per_mod, final):
    n_p, n_s = x1p.shape[0] // TB, x1s.shape[0] // TB
    nt = n_p + n_s
    const = lambda shape: pl.BlockSpec(shape, lambda i, *_: (0,) * len(shape), pipeline_mode=pl.Buffered(1))
    p_idx = lambda i, *_: (jnp.minimum(i, n_p - 1), 0)
    s_idx = lambda i, *_: (jnp.maximum(i - n_p, 0), 0)
    grid_spec = pltpu.PrefetchScalarGridSpec(
        num_scalar_prefetch=3,
        grid=(nt,),
        in_specs=[pl.BlockSpec((1, TOP_K, TB), lambda i, *_: (i, 0, 0)),
                  pl.BlockSpec((TB, D_MODEL), lambda i, *_: (i, 0)),
                  pl.BlockSpec((TB, D_MODEL), p_idx), pl.BlockSpec((TB, D_MODEL), s_idx),
                  const(mod.shape), const((D_MODEL, D_EXPERT)), const((D_MODEL, D_EXPERT)), const((D_EXPERT, D_MODEL)),
                  const((1, D_MODEL)), pl.BlockSpec(memory_space=pl.ANY)],
        out_specs=[pl.BlockSpec((TB, D_MODEL), p_idx), pl.BlockSpec((TB, D_MODEL), s_idx)],
        scratch_shapes=[pltpu.VMEM((2, R_LOC, D_MODEL), BF16), pltpu.VMEM((TB, D_MODEL), F32),
                        pltpu.SemaphoreType.DMA((2,))],
    )
    kern = functools.partial(_combine_kernel, nt=nt, n_p_tiles=n_p, tiles_per_mod=tiles_per_mod, final=final)
    return pl.pallas_call(
        kern,
        grid_spec=grid_spec,
        out_shape=[jax.ShapeDtypeStruct(x1p.shape, F32), jax.ShapeDtypeStruct(x1s.shape, F32)],
        compiler_params=pltpu.CompilerParams(dimension_semantics=("arbitrary",), vmem_limit_bytes=VMEM_LIMIT),
        name="moe_combine",
    )(off, loff, msz, lpos, h2ext, x1p, x1s, mod, wgs, wus, wds, fnw, ysorted)


def _moe(x1p, x1s, mod, n2w, w_router, b_router, wg, wu, wd, wgs, wus, wds, fnw, *, tokens_per_mod, final):
    assert R_LOC >= TB * TOP_K + N_EXPERTS * (ROW_ALIGN - 1) and tokens_per_mod % TB == 0
    nt = (x1p.shape[0] + x1s.shape[0]) // TB
    n_rows_max = nt * TB * TOP_K + nt * N_EXPERTS * (ROW_ALIGN - 1) + N_EXPERTS * (TR - ROW_ALIGN)
    n_row_tiles = -(-n_rows_max // TR)
    tiles_per_mod = tokens_per_mod // TB
    brb = jnp.broadcast_to(b_router.reshape(N_EXPERTS, 1), (N_EXPERTS, TB))
    h2ext, lpos, cnt = _route(x1p, x1s, mod, n2w, w_router.T, brb, tiles_per_mod=tiles_per_mod)
    off, loff, msz, tail, te = _plan(cnt, n_row_tiles)
    xs = _dispatch(off, loff, msz, tail, h2ext, lpos, n_row_tiles * TR)
    ysorted = _experts(te[0], te[0, n_row_tiles], xs, wg, wu, wd, n_row_tiles)
    return _combine(off, loff, msz, lpos, h2ext, x1p, x1s, mod, wgs, wus, wds, fnw, ysorted,
                    tiles_per_mod=tiles_per_mod, final=final)


def kernel(x_prompt, x_sample, state_gla, c, c_ctx, w_mod, b_mod, norm1_w, w_in, conv_w, conv_b, w_decay, b_decay,
           gla_norm_w, w_br_conv, w_br_gla, w_out, norm2_w, w_router, b_router, w_gate_e, w_up_e, w_down_e,
           w_gate_s, w_up_s, w_down_s, final_norm_w):
    depth = w_mod.shape[0]
    nb_p, len_p, _ = x_prompt.shape
    nb_s, len_s, _ = x_sample.shape
    yp = x_prompt.reshape(nb_p * len_p, D_MODEL)
    ys = x_sample.reshape(nb_s * len_s, D_MODEL)
    fnw = final_norm_w.reshape(1, D_MODEL)

    cond = jnp.concatenate([c_ctx[None, :], c, jnp.zeros((8 - 1 - nb_s, D_MODEL), F32)], axis=0)
    states = []
    for l in range(depth):
        mod = _modulation(cond, w_mod[l], b_mod[l].reshape(1, -1))
        mod_p, mod_s = mod[0:1], mod[1:1 + nb_s]

        wl = w_in[l]
        w_in_r = jnp.concatenate([wl[:, :3584], wl[:, 3616:5664], wl[:, 3584:3616],
                                  jnp.zeros((D_MODEL, D_PROJ - 5664), F32)], axis=1).astype(BF16)
        wdec = jnp.zeros((2, LR_PAD, D_GLA), F32)
        wdec = wdec.at[0, 0:GLA_RANK].set(w_decay[l, 0]).at[1, GLA_RANK:2 * GLA_RANK].set(w_decay[l, 1])
        n1w = norm1_w[l].reshape(1, D_MODEL)
        mix_w = (conv_w[l], conv_b[l].reshape(1, D_CONV), wdec, b_decay[l], gla_norm_w[l].reshape(1, D_GLA),
                 w_br_conv[l].astype(BF16), w_br_gla[l].astype(BF16), w_out[l].astype(BF16))
        moe_w = (norm2_w[l].reshape(1, D_MODEL), w_router[l], b_router[l],
                 w_gate_e[l], w_up_e[l], w_down_e[l], w_gate_s[l], w_up_s[l], w_down_s[l])

        proj_p = _in_proj(yp, mod_p, n1w, w_in_r, rows_per_mod=nb_p * len_p)
        yp, st = _mixer(proj_p, yp, mod_p, *mix_w, None, seq_len=len_p, row_len=len_p, emit_state=True)
        states.append(st)
        proj_s = _in_proj(ys, mod_s, n1w, w_in_r, rows_per_mod=len_s)
        (ys,) = _mixer(proj_s, ys, mod_s, *mix_w, state_gla[:, l], seq_len=len_s, row_len=GRID_W, emit_state=False)

        yp, ys = _moe(yp, ys, mod, *moe_w, fnw, tokens_per_mod=len_s, final=l == depth - 1)
    new_state = jnp.stack(states, axis=1)
    return (yp.reshape(nb_p, len_p, D_MODEL), ys.reshape(nb_s, len_s, D_MODEL), new_state)
```

```python
import functools

import jax
import jax.numpy as jnp
from jax import lax
from jax.experimental import pallas as pl
from jax.experimental.pallas import tpu as pltpu

F32 = jnp.float32
BF16 = jnp.bfloat16

D_MODEL = 1024
GRID_W = 64
D_CONV = 512
N_HEADS = 4
HEAD_D = 128
D_GLA = N_HEADS * HEAD_D
GLA_RANK = 16
GLA_GATE_NORM = 16.0
LOG2_E = 1.4426950408889634
CHUNK = 64
SUB = 8
N_SUB = CHUNK // SUB
N_EXPERTS = 64
TOP_K = 8
D_EXPERT = 256
ROUTED_SCALE = 2.5
EPS = 1e-6

C_U, C_GB, C_GC, C_Q, C_K, C_V, C_GO = 0, 512, 1024, 1536, 2048, 2560, 3072
C_BRC, C_BRG, C_LR = 3584, 4608, 5632
D_PROJ = 5760
LR_PAD = 128

VMEM_LIMIT = 56 * 1024 * 1024


def _dot(a, b):
    return jnp.dot(a, b, preferred_element_type=F32)


def _dot_nt(a, b):
    return lax.dot_general(a, b, (((1,), (1,)), ((), ())), preferred_element_type=F32)


def _dot_tn(a, b):
    return lax.dot_general(a, b, (((0,), (0,)), ((), ())), preferred_element_type=F32)


def _dot_hi(a, b):
    return jnp.dot(a, b, preferred_element_type=F32, precision=lax.Precision.HIGHEST)


def _split_bf16(x):
    hi = x.astype(BF16)
    lo = (x - hi.astype(F32)).astype(BF16)
    return hi, lo


def _rms(x):
    return x * lax.rsqrt(jnp.mean(x * x, axis=-1, keepdims=True) + EPS)


def _sigmoid(x):
    return 0.5 * jnp.tanh(0.5 * x) + 0.5


def _mod_kernel(cond_ref, w_ref, b_ref, o_ref):
    c = cond_ref[...]
    o_ref[...] = _dot_hi(c * jax.nn.sigmoid(c), w_ref[...]) + b_ref[...]


def _modulation(cond, w_mod, b_mod):
    n_rows = cond.shape[0]
    tn = 1536
    return pl.pallas_call(
        _mod_kernel,
        grid=(6 * D_MODEL // tn,),
        in_specs=[pl.BlockSpec((n_rows, D_MODEL), lambda j: (0, 0)),
                  pl.BlockSpec((D_MODEL, tn), lambda j: (0, j)),
                  pl.BlockSpec((1, tn), lambda j: (0, j))],
        out_specs=pl.BlockSpec((n_rows, tn), lambda j: (0, j)),
        out_shape=jax.ShapeDtypeStruct((n_rows, 6 * D_MODEL), F32),
        compiler_params=pltpu.CompilerParams(dimension_semantics=("arbitrary",),
                                             vmem_limit_bytes=VMEM_LIMIT),
        name="modulation",
    )(cond, w_mod, b_mod)


def _inproj_kernel(x_ref, mod_ref, nw_ref, w_ref, o_ref, *, rows_per_mod, tm):
    i = pl.program_id(1)
    row = (i * tm) // rows_per_mod
    sh = mod_ref[pl.ds(row, 1), 0:D_MODEL]
    sc = mod_ref[pl.ds(row, 1), D_MODEL:2 * D_MODEL]
    h = _rms(x_ref[...]) * nw_ref[...] * (1.0 + sc) + sh
    o_ref[...] = _dot(h.astype(BF16), w_ref[...]).astype(BF16)


def _in_proj(x2d, mod, norm_w, w_in_r, rows_per_mod):
    t = x2d.shape[0]
    tm, tn = 512, 1920
    kern = functools.partial(_inproj_kernel, rows_per_mod=rows_per_mod, tm=tm)
    return pl.pallas_call(
        kern,
        grid=(D_PROJ // tn, t // tm),
        in_specs=[pl.BlockSpec((tm, D_MODEL), lambda j, i: (i, 0)),
                  pl.BlockSpec(mod.shape, lambda j, i: (0, 0)),
                  pl.BlockSpec((1, D_MODEL), lambda j, i: (0, 0)),
                  pl.BlockSpec((D_MODEL, tn), lambda j, i: (0, j))],
        out_specs=pl.BlockSpec((tm, tn), lambda j, i: (i, j)),
        out_shape=jax.ShapeDtypeStruct((t, D_PROJ), BF16),
        compiler_params=pltpu.CompilerParams(dimension_semantics=("arbitrary", "arbitrary"),
                                             vmem_limit_bytes=VMEM_LIMIT),
        name="in_proj",
    )(x2d, mod, norm_w, w_in_r)


def _log2_sigmoid(x):
    return jnp.minimum(x, 0.0) * LOG2_E - jnp.log2(1.0 + jnp.exp2(jnp.abs(x) * (-LOG2_E)))


def _gla_chunk_head(qc, kc, vc, bc, st, rev):
    lane = lax.broadcasted_iota(jnp.int32, (SUB, CHUNK), 1)
    sub = lax.broadcasted_iota(jnp.int32, (SUB, CHUNK), 0)
    tot = bc[0:1] if rev else bc[CHUNK - 1:CHUNK]

    o = _dot_nt((qc * jnp.exp2(bc)).astype(BF16), st.astype(BF16))
    k_tail = kc * jnp.exp2(tot - bc)
    st_new = st * jnp.exp2(tot) + _dot_tn(vc.astype(BF16), k_tail.astype(BF16))

    lhs_segs, rhs_segs = [], []

    def rows(before, mid, after):
        parts = ([jnp.zeros((before, HEAD_D), F32)] if before else []) + [mid]
        parts += [jnp.zeros((after, HEAD_D), F32)] if after else []
        return jnp.concatenate(parts, axis=0) if len(parts) > 1 else mid

    key_blocks = range(1, N_SUB) if rev else range(0, N_SUB - 1)
    for jb in key_blocks:
        r0 = jb * SUB
        ref_row = bc[r0:r0 + 1] if rev else bc[r0 + SUB - 1:r0 + SUB]
        ke = kc[r0:r0 + SUB] * jnp.exp2(ref_row - bc[r0:r0 + SUB])
        rhs_segs.append(rows(r0, ke, CHUNK - r0 - SUB))
        if rev:
            ql = qc[:r0] * jnp.exp2(bc[:r0] - ref_row)
            lhs_segs.append(rows(0, ql, CHUNK - r0))
        else:
            ql = qc[r0 + SUB:] * jnp.exp2(bc[r0 + SUB:] - ref_row)
            lhs_segs.append(rows(r0 + SUB, ql, 0))
    far = _dot_nt(jnp.concatenate(lhs_segs, axis=1).astype(BF16),
                  jnp.concatenate(rhs_segs, axis=1).astype(BF16))

    blocks = []
    for ib in range(N_SUB):
        r0 = ib * SUB
        qi, bi = qc[r0:r0 + SUB], bc[r0:r0 + SUB]
        acc = jnp.zeros((SUB, CHUNK), F32)
        for jj in range(SUB):
            j = r0 + jj
            e = jnp.exp2(bi - bc[j:j + 1])
            col = jnp.sum(qi * (kc[j:j + 1] * e), axis=-1, keepdims=True)
            acc = jnp.where(lane == j, col, acc)
        keep = (lane - r0 >= sub) if rev else (lane - r0 <= sub)
        blocks.append(jnp.where(keep, acc, 0.0))
    scores = far + jnp.concatenate(blocks, axis=0)
    o = o + _dot(scores.astype(BF16), vc.astype(BF16))
    return o, st_new


def _mixer_kernel(*refs, seq_len, row_len, has_s0, emit_state):
    it = iter(refs)
    proj_ref, x_ref, mod_ref, cw_ref, cb_ref, wdec_ref, bdec_ref, gnw_ref = (next(it) for _ in range(8))
    wbc_ref, wbg_ref, wout_ref = (next(it) for _ in range(3))
    s0_ref = next(it) if has_s0 else None
    out_ref = next(it)
    st_out_ref = next(it) if emit_state else None
    la_f_ref, la_b_ref, o_ref, st_ref = (next(it) for _ in range(4))

    L = seq_len
    n_chunks = L // CHUNK
    tr = 256
    assert tr % row_len == 0 and L % tr == 0

    ri = lax.broadcasted_iota(jnp.int32, (tr, tr), 0)
    ci = lax.broadcasted_iota(jnp.int32, (tr, tr), 1)
    same_chunk = (ri // CHUNK) == (ci // CHUNK)

    def decay_body(t, carry):
        r0 = pl.multiple_of(t * tr, tr)
        lr = proj_ref[pl.ds(r0, tr), C_LR:C_LR + LR_PAD]
        for d, ref in ((0, la_f_ref), (1, la_b_ref)):
            whi, wlo = _split_bf16(wdec_ref[d])
            z = _dot(lr, whi) + _dot(lr, wlo) + bdec_ref[d:d + 1]
            la_hi, la_lo = _split_bf16(_log2_sigmoid(z) * (1.0 / GLA_GATE_NORM))
            tri = jnp.where(same_chunk & ((ci >= ri) if d else (ci <= ri)), 1.0, 0.0).astype(BF16)
            ref[pl.ds(r0, tr), :] = _dot(tri, la_hi) + _dot(tri, la_lo)
        return carry
    lax.fori_loop(0, L // tr, decay_body, 0)

    for rev, la_ref in ((False, la_f_ref), (True, la_b_ref)):
        d = 1 if rev else 0
        for h in range(N_HEADS):
            if has_s0:
                st_ref[h] = s0_ref[0, d, h].T
            else:
                st_ref[h] = jnp.zeros((HEAD_D, HEAD_D), F32)

        def chunk_body(c, carry, rev=rev, la_ref=la_ref):
            cc = (n_chunks - 1 - c) if rev else c
            r0 = pl.multiple_of(cc * CHUNK, CHUNK)
            for h in range(N_HEADS):
                lo, hi = h * HEAD_D, (h + 1) * HEAD_D
                qc = proj_ref[pl.ds(r0, CHUNK), C_Q + lo:C_Q + hi].astype(F32) * (HEAD_D ** -0.5)
                kc = proj_ref[pl.ds(r0, CHUNK), C_K + lo:C_K + hi].astype(F32)
                vc = proj_ref[pl.ds(r0, CHUNK), C_V + lo:C_V + hi].astype(F32)
                oc, st_new = _gla_chunk_head(qc, kc, vc, la_ref[pl.ds(r0, CHUNK), lo:hi], st_ref[h], rev)
                st_ref[h] = st_new
                if rev:
                    o_ref[pl.ds(r0, CHUNK), lo:hi] += oc
                else:
                    o_ref[pl.ds(r0, CHUNK), lo:hi] = oc
            return carry
        lax.fori_loop(0, n_chunks, chunk_body, 0, unroll=2)

        if emit_state:
            for h in range(N_HEADS):
                st_out_ref[0, d, h] = st_ref[h].T

    mod_row = pl.program_id(0) if mod_ref.shape[0] > 1 else 0
    g1 = mod_ref[pl.ds(mod_row, 1), 2 * D_MODEL:3 * D_MODEL]
    hsel_r = lax.broadcasted_iota(jnp.int32, (D_GLA, D_GLA), 0) // HEAD_D
    hsel_c = lax.broadcasted_iota(jnp.int32, (D_GLA, D_GLA), 1) // HEAD_D
    head_avg = jnp.where(hsel_r == hsel_c, 1.0 / HEAD_D, 0.0).astype(BF16)
    pos = lax.broadcasted_iota(jnp.int32, (tr, 1), 0)

    def dense_body(t, carry):
        r0 = pl.multiple_of(t * tr, tr)
        rows = pl.ds(r0, tr)
        cu = (proj_ref[rows, C_GC:C_GC + D_CONV].astype(F32) * proj_ref[rows, C_U:C_U + D_CONV].astype(F32))
        in_row = pos % row_len
        left = jnp.where(in_row == 0, 0.0, pltpu.roll(cu, 1, axis=0))
        right = jnp.where(in_row == row_len - 1, 0.0, pltpu.roll(cu, tr - 1, axis=0))
        conv = cw_ref[0:1] * left + cw_ref[1:2] * cu + cw_ref[2:3] * right + cb_ref[...]
        y_conv = proj_ref[rows, C_GB:C_GB + D_CONV].astype(F32) * conv

        o = o_ref[rows, :]
        osq_hi, osq_lo = _split_bf16(o * o)
        ms = _dot(osq_hi, head_avg) + _dot(osq_lo, head_avg)
        g_out = proj_ref[rows, C_GO:C_GO + D_GLA].astype(F32)
        y_gla = o * lax.rsqrt(ms + EPS) * gnw_ref[...] * (g_out * _sigmoid(g_out))

        merged = (_sigmoid(proj_ref[rows, C_BRC:C_BRC + D_MODEL].astype(F32)) * _dot(y_conv.astype(BF16), wbc_ref[...])
                  + _sigmoid(proj_ref[rows, C_BRG:C_BRG + D_MODEL].astype(F32)) * _dot(y_gla.astype(BF16), wbg_ref[...]))
        out_ref[rows, :] = x_ref[rows, :] + g1 * _dot(merged.astype(BF16), wout_ref[...])
        return carry
    lax.fori_loop(0, L // tr, dense_body, 0)


def _mixer(proj, x2d, mod, conv_w, conv_b, wdec, bdec, gnw, wbc, wbg, wout, s0, *, seq_len, row_len, emit_state):
    t = x2d.shape[0]
    nb = t // seq_len
    has_s0 = s0 is not None
    one = pl.Buffered(1)
    const = lambda shape: pl.BlockSpec(shape, lambda b: (0,) * len(shape), pipeline_mode=one)
    in_specs = [pl.BlockSpec((seq_len, D_PROJ), lambda b: (b, 0), pipeline_mode=one),
                pl.BlockSpec((seq_len, D_MODEL), lambda b: (b, 0)),
                const(mod.shape),
                const((3, D_CONV)), const((1, D_CONV)), const((2, LR_PAD, D_GLA)), const((2, D_GLA)),
                const((1, D_GLA)), const((D_CONV, D_MODEL)), const((D_GLA, D_MODEL)), const((D_MODEL, D_MODEL))]
    args = [proj, x2d, mod, conv_w, conv_b, wdec, bdec, gnw, wbc, wbg, wout]
    if has_s0:
        in_specs.append(pl.BlockSpec((1, 2, N_HEADS, HEAD_D, HEAD_D), lambda b: (b, 0, 0, 0, 0)))
        args.append(s0)
    out_specs = [pl.BlockSpec((seq_len, D_MODEL), lambda b: (b, 0))]
    out_shape = [jax.ShapeDtypeStruct((t, D_MODEL), F32)]
    if emit_state:
        out_specs.append(pl.BlockSpec((1, 2, N_HEADS, HEAD_D, HEAD_D), lambda b: (b, 0, 0, 0, 0)))
        out_shape.append(jax.ShapeDtypeStruct((nb, 2, N_HEADS, HEAD_D, HEAD_D), F32))
    kern = functools.partial(_mixer_kernel, seq_len=seq_len, row_len=row_len, has_s0=has_s0, emit_state=emit_state)
    return pl.pallas_call(
        kern,
        grid=(nb,),
        in_specs=in_specs,
        out_specs=out_specs,
        out_shape=out_shape,
        scratch_shapes=[pltpu.VMEM((seq_len, D_GLA), F32), pltpu.VMEM((seq_len, D_GLA), F32),
                        pltpu.VMEM((seq_len, D_GLA), F32), pltpu.VMEM((N_HEADS, HEAD_D, HEAD_D), F32)],
        compiler_params=pltpu.CompilerParams(dimension_semantics=("arbitrary",),
                                             vmem_limit_bytes=VMEM_LIMIT),
        name="mixer",
    )(*args)


I32 = jnp.int32
TB = 256
ROW_ALIGN = 16
TR = 512
R_LOC = 3072
H2W = 1152


def _dot_nt_hi(a, b):
    return lax.dot_general(a, b, (((1,), (1,)), ((), ())), preferred_element_type=F32,
                           precision=lax.Precision.HIGHEST)


def _select_x(i, n_p_tiles, xp_ref, xs_ref):
    return jnp.where(i < n_p_tiles, xp_ref[...], xs_ref[...])


def _mod_row(i, n_p_tiles, tiles_per_mod):
    return jnp.where(i < n_p_tiles, 0, 1 + (i - n_p_tiles) // tiles_per_mod)


def _route_kernel(xp_ref, xs_ref, mod_ref, n2w_ref, wrt_ref, brb_ref, h2_ref, lpos_ref, cnt_ref, *,
                  n_p_tiles, tiles_per_mod):
    i = pl.program_id(0)
    row = _mod_row(i, n_p_tiles, tiles_per_mod)
    x = _select_x(i, n_p_tiles, xp_ref, xs_ref)
    sh = mod_ref[pl.ds(row, 1), 3 * D_MODEL:4 * D_MODEL]
    sc = mod_ref[pl.ds(row, 1), 4 * D_MODEL:5 * D_MODEL]
    h2 = _rms(x) * n2w_ref[...] * (1.0 + sc) + sh

    scores = jax.nn.sigmoid(_dot_nt_hi(wrt_ref[...], h2))
    biased = scores + brb_ref[...]
    eidx = lax.broadcasted_iota(I32, scores.shape, 0)
    picks = []
    for _k in range(TOP_K):
        m = jnp.max(biased, axis=0, keepdims=True)
        first = jnp.min(jnp.where(biased == m, eidx, N_EXPERTS), axis=0, keepdims=True)
        pick = eidx == first
        picks.append(pick)
        biased = jnp.where(pick, -jnp.inf, biased)
    sel = jnp.zeros(scores.shape, F32)
    for pick in picks:
        sel = jnp.where(pick, 1.0, sel)
    selsc = sel * scores
    comb = selsc / jnp.sum(selsc, axis=0, keepdims=True) * ROUTED_SCALE

    selb = sel.astype(BF16)
    tr_ = lax.broadcasted_iota(I32, (TB, TB), 0)
    tc_ = lax.broadcasted_iota(I32, (TB, TB), 1)
    rank = _dot(selb, jnp.where(tr_ < tc_, 1.0, 0.0).astype(BF16))
    n_b = _dot(selb, jnp.ones((TB, 128), BF16))
    m_b = jnp.floor((n_b + (ROW_ALIGN - 1)) * (1.0 / ROW_ALIGN)) * ROW_ALIGN
    er_ = lax.broadcasted_iota(I32, (N_EXPERTS, N_EXPERTS), 0)
    ec_ = lax.broadcasted_iota(I32, (N_EXPERTS, N_EXPERTS), 1)
    loff_b = _dot(jnp.where(ec_ < er_, 1.0, 0.0).astype(BF16), m_b.astype(BF16))
    lposf = jnp.concatenate([loff_b] * (TB // 128), axis=1) + rank
    rows = [jnp.sum(jnp.where(pick, lposf, 0.0), axis=0, keepdims=True) for pick in picks]
    lpos_ref[0] = jnp.concatenate(rows, axis=0).astype(I32)
    cnt_ref[0] = m_b

    combt = comb.T
    chi = combt.astype(BF16).astype(F32)
    h2_ref[:, 0:D_MODEL] = h2.astype(BF16)
    h2_ref[:, D_MODEL:H2W] = jnp.concatenate([chi, combt - chi], axis=1).astype(BF16)


def _route(x1p, x1s, mod, n2w, w_router_t, b_router_b, *, tiles_per_mod):
    n_p, n_s = x1p.shape[0] // TB, x1s.shape[0] // TB
    nt = n_p + n_s
    kern = functools.partial(_route_kernel, n_p_tiles=n_p, tiles_per_mod=tiles_per_mod)
    const = lambda shape: pl.BlockSpec(shape, lambda i: (0,) * len(shape))
    return pl.pallas_call(
        kern,
        grid=(nt,),
        in_specs=[pl.BlockSpec((TB, D_MODEL), lambda i: (jnp.minimum(i, n_p - 1), 0)),
                  pl.BlockSpec((TB, D_MODEL), lambda i: (jnp.maximum(i - n_p, 0), 0)),
                  const(mod.shape), const((1, D_MODEL)), const((N_EXPERTS, D_MODEL)), const((N_EXPERTS, TB))],
        out_specs=[pl.BlockSpec((TB, H2W), lambda i: (i, 0)),
                   pl.BlockSpec((1, TOP_K, TB), lambda i: (i, 0, 0)),
                   pl.BlockSpec((1, N_EXPERTS, 128), lambda i: (i, 0, 0))],
        out_shape=[jax.ShapeDtypeStruct((nt * TB, H2W), BF16),
                   jax.ShapeDtypeStruct((nt, TOP_K, TB), I32),
                   jax.ShapeDtypeStruct((nt, N_EXPERTS, 128), F32)],
        compiler_params=pltpu.CompilerParams(dimension_semantics=("arbitrary",), vmem_limit_bytes=VMEM_LIMIT),
        name="moe_route",
    )(x1p, x1s, mod, n2w, w_router_t, b_router_b)


def _plan_kernel(cnt_ref, off_ref, loff_ref, msz_ref, tail_ref, te_ref, *, nt, n_row_tiles):
    lane = lax.broadcasted_iota(I32, (N_EXPERTS, 128), 1)
    m = jnp.zeros((N_EXPERTS, 128), F32)
    for i in range(nt):
        m = jnp.where(lane == i, cnt_ref[i], m)
    total = jnp.broadcast_to(jnp.sum(m, axis=1, keepdims=True), (N_EXPERTS, 128))
    gsz = jnp.floor((total + (TR - 1)) * (1.0 / TR)) * TR
    er_ = lax.broadcasted_iota(I32, (N_EXPERTS, N_EXPERTS), 0)
    ec_ = lax.broadcasted_iota(I32, (N_EXPERTS, N_EXPERTS), 1)
    lstrict = jnp.where(ec_ < er_, 1.0, 0.0)
    ir_ = lax.broadcasted_iota(I32, (128, 128), 0)
    ic_ = lax.broadcasted_iota(I32, (128, 128), 1)
    ustrict = jnp.where(ir_ < ic_, 1.0, 0.0)
    gstart = _dot_hi(lstrict, gsz)
    off_ref[...] = (gstart + _dot_hi(m, ustrict)).astype(I32)
    loff_ref[...] = _dot_hi(lstrict, m).astype(I32)
    msz_ref[...] = m.astype(I32)
    tail_ref[...] = jnp.where(lane == 0, gstart + total, jnp.where(lane == 1, gsz - total, 0.0)).astype(I32)
    gend = gstart + gsz
    te_lanes = te_ref.shape[1]
    gend_w = jnp.concatenate([gend] * (te_lanes // 128), axis=1)
    tile_start = lax.broadcasted_iota(I32, (N_EXPERTS, te_lanes), 1).astype(F32) * TR
    te = jnp.sum(jnp.where(gend_w <= tile_start, 1.0, 0.0), axis=0, keepdims=True)
    used = gend[N_EXPERTS - 1:N_EXPERTS, 0:1] * (1.0 / TR)
    te_lane = lax.broadcasted_iota(I32, (1, te_lanes), 1)
    te = jnp.where(te_lane == n_row_tiles, used, jnp.minimum(te, N_EXPERTS - 1.0))
    te_ref[...] = jnp.broadcast_to(te, te_ref.shape).astype(I32)


def _plan(cnt, n_row_tiles):
    nt = cnt.shape[0]
    assert nt <= 128
    te_lanes = -(-(n_row_tiles + 1) // 128) * 128
    tab = jax.ShapeDtypeStruct((N_EXPERTS, 128), I32)
    return pl.pallas_call(
        functools.partial(_plan_kernel, nt=nt, n_row_tiles=n_row_tiles),
        out_shape=[tab, tab, tab, tab, jax.ShapeDtypeStruct((8, te_lanes), I32)],
        compiler_params=pltpu.CompilerParams(vmem_limit_bytes=VMEM_LIMIT),
        name="moe_plan",
    )(cnt)


def _start_copies(msz_ref, tile, make_copy):
    def body(e, carry):
        m = msz_ref[e, tile]

        @pl.when(m > 0)
        def _():
            make_copy(e, pl.multiple_of(m, ROW_ALIGN)).start()
        return carry
    lax.fori_loop(0, N_EXPERTS, body, 0, unroll=8)


def _tile_rows(loff_ref, msz_ref, tile):
    return pl.multiple_of(loff_ref[N_EXPERTS - 1, tile] + msz_ref[N_EXPERTS - 1, tile], ROW_ALIGN)


def _dispatch_kernel(off_ref, loff_ref, msz_ref, tail_ref, h2_ref, lpos_ref, xs_hbm, xloc_ref, zero_ref, sem,
                     tail_sem, *, nt):
    i = pl.program_id(0)
    slot = i % 2

    def copy_for(tile, slot_):
        def make(e, m):
            lo = pl.multiple_of(loff_ref[e, tile], ROW_ALIGN)
            of = pl.multiple_of(off_ref[e, tile], ROW_ALIGN)
            return pltpu.make_async_copy(xloc_ref.at[slot_, pl.ds(lo, m)], xs_hbm.at[pl.ds(of, m)], sem.at[slot_])
        return make

    def wait_tile(tile, slot_):
        n = _tile_rows(loff_ref, msz_ref, tile)
        pltpu.make_async_copy(xloc_ref.at[slot_, pl.ds(0, n)], xs_hbm.at[pl.ds(0, n)], sem.at[slot_]).wait()

    used_rows = _tile_rows(loff_ref, msz_ref, i)
    lpos = lpos_ref[0]
    h2 = h2_ref[...]
    ck = 256
    for c in range(R_LOC // ck):
        @pl.when(c * ck < used_rows)
        def _(c=c):
            r = lax.broadcasted_iota(I32, (ck, TB), 0) + c * ck
            d = jnp.zeros((ck, TB), F32)
            for k in range(TOP_K):
                d = jnp.where(r == lpos[k:k + 1, :], 1.0, d)
            res = _dot(d.astype(BF16), h2)
            xloc_ref[slot, c * ck:(c + 1) * ck, :] = res.astype(BF16)

    _start_copies(msz_ref, i, copy_for(i, slot))

    @pl.when(i > 0)
    def _():
        wait_tile(i - 1, 1 - slot)

    @pl.when(i == nt - 1)
    def _():
        zero_ref[...] = jnp.zeros(zero_ref.shape, BF16)

        def tail_copies(start):
            def body(e, carry):
                n = tail_ref[e, 1]

                @pl.when(n > 0)
                def _():
                    st = pl.multiple_of(tail_ref[e, 0], ROW_ALIGN)
                    nn = pl.multiple_of(n, ROW_ALIGN)
                    cp = pltpu.make_async_copy(zero_ref.at[pl.ds(0, nn)], xs_hbm.at[pl.ds(st, nn)], tail_sem)
                    if start:
                        cp.start()
                    else:
                        cp.wait()
                return carry
            lax.fori_loop(0, N_EXPERTS, body, 0)
        tail_copies(True)
        wait_tile(i, slot)
        tail_copies(False)


def _dispatch(off, loff, msz, tail, h2ext, lpos, n_rows):
    nt = lpos.shape[0]
    grid_spec = pltpu.PrefetchScalarGridSpec(
        num_scalar_prefetch=4,
        grid=(nt,),
        in_specs=[pl.BlockSpec((TB, H2W), lambda i, *_: (i, 0)),
                  pl.BlockSpec((1, TOP_K, TB), lambda i, *_: (i, 0, 0))],
        out_specs=pl.BlockSpec(memory_space=pl.ANY),
        scratch_shapes=[pltpu.VMEM((2, R_LOC, H2W), BF16), pltpu.VMEM((TR, H2W), BF16),
                        pltpu.SemaphoreType.DMA((2,)), pltpu.SemaphoreType.DMA],
    )
    return pl.pallas_call(
        functools.partial(_dispatch_kernel, nt=nt),
        grid_spec=grid_spec,
        out_shape=jax.ShapeDtypeStruct((n_rows, H2W), BF16),
        compiler_params=pltpu.CompilerParams(dimension_semantics=("arbitrary",), vmem_limit_bytes=VMEM_LIMIT),
        name="moe_dispatch",
    )(off, loff, msz, tail, h2ext, lpos)


def _expert_kernel(te_ref, xs_ref, wg_ref, wu_ref, wd_ref, ys_ref, wgu_ref, wdb_ref, *, n_row_tiles):
    j = pl.program_id(0)
    e = te_ref[j]

    @pl.when(j < te_ref[n_row_tiles])
    def _():
        @pl.when((j == 0) | (e != te_ref[jnp.maximum(j - 1, 0)]))
        def _():
            wgu_ref[:, :D_EXPERT] = wg_ref[0].astype(BF16)
            wgu_ref[:, D_EXPERT:] = wu_ref[0].astype(BF16)
            wdb_ref[...] = wd_ref[0].astype(BF16)

        x = xs_ref[:, 0:D_MODEL]
        ext = xs_ref[:, D_MODEL:H2W].astype(F32)
        wts = ext[:, :N_EXPERTS] + ext[:, N_EXPERTS:]
        lane = lax.broadcasted_iota(I32, wts.shape, 1)
        w = jnp.sum(jnp.where(lane == e, wts, 0.0), axis=-1, keepdims=True)
        h = _dot(x, wgu_ref[...])
        hg, hu = h[:, :D_EXPERT], h[:, D_EXPERT:]
        act = hg * _sigmoid(hg) * hu * w
        ys_ref[...] = _dot(act.astype(BF16), wdb_ref[...]).astype(BF16)


def _experts(te, xs, wg, wu, wd, n_row_tiles):
    def row_tile(j, te_ref):
        return jnp.maximum(jnp.minimum(j, te_ref[n_row_tiles] - 1), 0)
    grid_spec = pltpu.PrefetchScalarGridSpec(
        num_scalar_prefetch=1,
        grid=(n_row_tiles,),
        in_specs=[pl.BlockSpec((TR, H2W), lambda j, te_ref: (row_tile(j, te_ref), 0)),
                  pl.BlockSpec((1, D_MODEL, D_EXPERT), lambda j, te_ref: (te_ref[row_tile(j, te_ref)], 0, 0)),
                  pl.BlockSpec((1, D_MODEL, D_EXPERT), lambda j, te_ref: (te_ref[row_tile(j, te_ref)], 0, 0)),
                  pl.BlockSpec((1, D_EXPERT, D_MODEL), lambda j, te_ref: (te_ref[row_tile(j, te_ref)], 0, 0))],
        out_specs=pl.BlockSpec((TR, D_MODEL), lambda j, te_ref: (row_tile(j, te_ref), 0)),
        scratch_shapes=[pltpu.VMEM((D_MODEL, 2 * D_EXPERT), BF16), pltpu.VMEM((D_EXPERT, D_MODEL), BF16)],
    )
    return pl.pallas_call(
        functools.partial(_expert_kernel, n_row_tiles=n_row_tiles),
        grid_spec=grid_spec,
        out_shape=jax.ShapeDtypeStruct((n_row_tiles * TR, D_MODEL), BF16),
        compiler_params=pltpu.CompilerParams(dimension_semantics=("arbitrary",), vmem_limit_bytes=VMEM_LIMIT),
        name="moe_experts",
    )(te, xs, wg, wu, wd)


def _combine_kernel(off_ref, loff_ref, msz_ref, lpos_ref, h2_ref, xp_ref, xs_ref, mod_ref, wgs_ref, wus_ref, wds_ref,
                    fnw_ref, ysrt_hbm, yp_ref, ys_ref, yloc_ref, acc_ref, sem, *, nt, n_p_tiles, tiles_per_mod,
                    final):
    i = pl.program_id(0)
    slot = i % 2

    def copy_for(tile, slot_):
        def make(e, m):
            lo = pl.multiple_of(loff_ref[e, tile], ROW_ALIGN)
            of = pl.multiple_of(off_ref[e, tile], ROW_ALIGN)
            return pltpu.make_async_copy(ysrt_hbm.at[pl.ds(of, m)], yloc_ref.at[slot_, pl.ds(lo, m)], sem.at[slot_])
        return make

    @pl.when(i == 0)
    def _():
        yloc_ref[...] = jnp.zeros(yloc_ref.shape, BF16)
        _start_copies(msz_ref, 0, copy_for(0, 0))

    @pl.when(i + 1 < nt)
    def _():
        _start_copies(msz_ref, i + 1, copy_for(i + 1, 1 - slot))

    hb = h2_ref[...]
    hg = _dot(hb, wgs_ref[...].astype(BF16))
    hu = _dot(hb, wus_ref[...].astype(BF16))
    acc_ref[...] = _dot((hg * _sigmoid(hg) * hu).astype(BF16), wds_ref[...].astype(BF16))

    used_rows = _tile_rows(loff_ref, msz_ref, i)
    pltpu.make_async_copy(ysrt_hbm.at[pl.ds(0, used_rows)], yloc_ref.at[slot, pl.ds(0, used_rows)],
                          sem.at[slot]).wait()

    lpos_pad = jnp.concatenate([lpos_ref[0].astype(F32), jnp.zeros((128 - TOP_K, TB), F32)], axis=0)
    lposc = lpos_pad.T.astype(I32)
    ck = 512
    for c in range(R_LOC // ck):
        @pl.when(c * ck < used_rows)
        def _(c=c):
            r = lax.broadcasted_iota(I32, (TB, ck), 1) + c * ck
            cm = jnp.zeros((TB, ck), F32)
            for k in range(TOP_K):
                cm = jnp.where(r == lposc[:, k:k + 1], 1.0, cm)
            acc_ref[...] += _dot(cm.astype(BF16), yloc_ref[slot, c * ck:(c + 1) * ck, :])

    row = _mod_row(i, n_p_tiles, tiles_per_mod)
    g2 = mod_ref[pl.ds(row, 1), 5 * D_MODEL:6 * D_MODEL]
    x2 = _select_x(i, n_p_tiles, xp_ref, xs_ref) + g2 * acc_ref[...]
    y = _rms(x2) * fnw_ref[...] if final else x2

    @pl.when(i < n_p_tiles)
    def _():
        yp_ref[...] = y

    @pl.when(i >= n_p_tiles)
    def _():
        ys_ref[...] = y


def _combine(off, loff, msz, lpos, h2ext, x1p, x1s, mod, wgs, wus, wds, fnw, ysorted, *, tiles_per_mod, final):
    n_p, n_s = x1p.shape[0] // TB, x1s.shape[0] // TB
    nt = n_p + n_s
    const = lambda shape: pl.BlockSpec(shape, lambda i, *_: (0,) * len(shape), pipeline_mode=pl.Buffered(1))
    p_idx = lambda i, *_: (jnp.minimum(i, n_p - 1), 0)
    s_idx = lambda i, *_: (jnp.maximum(i - n_p, 0), 0)
    grid_spec = pltpu.PrefetchScalarGridSpec(
        num_scalar_prefetch=3,
        grid=(nt,),
        in_specs=[pl.BlockSpec((1, TOP_K, TB), lambda i, *_: (i, 0, 0)),
                  pl.BlockSpec((TB, D_MODEL), lambda i, *_: (i, 0)),
                  pl.BlockSpec((TB, D_MODEL), p_idx), pl.BlockSpec((TB, D_MODEL), s_idx),
                  const(mod.shape), const((D_MODEL, D_EXPERT)), const((D_MODEL, D_EXPERT)), const((D_EXPERT, D_MODEL)),
                  const((1, D_MODEL)), pl.BlockSpec(memory_space=pl.ANY)],
        out_specs=[pl.BlockSpec((TB, D_MODEL), p_idx), pl.BlockSpec((TB, D_MODEL), s_idx)],
        scratch_shapes=[pltpu.VMEM((2, R_LOC, D_MODEL), BF16), pltpu.VMEM((TB, D_MODEL), F32),
                        pltpu.SemaphoreType.DMA((2,))],
    )
    kern = functools.partial(_combine_kernel, nt=nt, n_p_tiles=n_p, tiles_per_mod=tiles_per_mod, final=final)
    return pl.pallas_call(
        kern,
        grid_spec=grid_spec,
        out_shape=[jax.ShapeDtypeStruct(x1p.shape, F32), jax.ShapeDtypeStruct(x1s.shape, F32)],
        compiler_params=pltpu.CompilerParams(dimension_semantics=("arbitrary",), vmem_limit_bytes=VMEM_LIMIT),
        name="moe_combine",
    )(off, loff, msz, lpos, h2ext, x1p, x1s, mod, wgs, wus, wds, fnw, ysorted)


def _moe(x1p, x1s, mod, n2w, w_router, b_router, wg, wu, wd, wgs, wus, wds, fnw, *, tokens_per_mod, final):
    assert R_LOC >= TB * TOP_K + N_EXPERTS * (ROW_ALIGN - 1) and tokens_per_mod % TB == 0
    nt = (x1p.shape[0] + x1s.shape[0]) // TB
    n_rows_max = nt * TB * TOP_K + nt * N_EXPERTS * (ROW_ALIGN - 1) + N_EXPERTS * (TR - ROW_ALIGN)
    n_row_tiles = -(-n_rows_max // TR)
    tiles_per_mod = tokens_per_mod // TB
    brb = jnp.broadcast_to(b_router.reshape(N_EXPERTS, 1), (N_EXPERTS, TB))
    h2ext, lpos, cnt = _route(x1p, x1s, mod, n2w, w_router.T, brb, tiles_per_mod=tiles_per_mod)
    off, loff, msz, tail, te = _plan(cnt, n_row_tiles)
    xs = _dispatch(off, loff, msz, tail, h2ext, lpos, n_row_tiles * TR)
    ysorted = _experts(te[0], xs, wg, wu, wd, n_row_tiles)
    return _combine(off, loff, msz, lpos, h2ext, x1p, x1s, mod, wgs, wus, wds, fnw, ysorted,
                    tiles_per_mod=tiles_per_mod, final=final)


def kernel(x_prompt, x_sample, state_gla, c, c_ctx, w_mod, b_mod, norm1_w, w_in, conv_w, conv_b, w_decay, b_decay,
           gla_norm_w, w_br_conv, w_br_gla, w_out, norm2_w, w_router, b_router, w_gate_e, w_up_e, w_down_e,
           w_gate_s, w_up_s, w_down_s, final_norm_w):
    depth = w_mod.shape[0]
    nb_p, len_p, _ = x_prompt.shape
    nb_s, len_s, _ = x_sample.shape
    yp = x_prompt.reshape(nb_p * len_p, D_MODEL)
    ys = x_sample.reshape(nb_s * len_s, D_MODEL)
    fnw = final_norm_w.reshape(1, D_MODEL)

    cond = jnp.concatenate([c_ctx[None, :], c, jnp.zeros((8 - 1 - nb_s, D_MODEL), F32)], axis=0)
    states = []
    for l in range(depth):
        mod = _modulation(cond, w_mod[l], b_mod[l].reshape(1, -1))
        mod_p, mod_s = mod[0:1], mod[1:1 + nb_s]

        wl = w_in[l]
        w_in_r = jnp.concatenate([wl[:, :3584], wl[:, 3616:5664], wl[:, 3584:3616],
                                  jnp.zeros((D_MODEL, D_PROJ - 5664), F32)], axis=1).astype(BF16)
        wdec = jnp.zeros((2, LR_PAD, D_GLA), F32)
        wdec = wdec.at[0, 0:GLA_RANK].set(w_decay[l, 0]).at[1, GLA_RANK:2 * GLA_RANK].set(w_decay[l, 1])
        n1w = norm1_w[l].reshape(1, D_MODEL)
        mix_w = (conv_w[l], conv_b[l].reshape(1, D_CONV), wdec, b_decay[l], gla_norm_w[l].reshape(1, D_GLA),
                 w_br_conv[l].astype(BF16), w_br_gla[l].astype(BF16), w_out[l].astype(BF16))
        moe_w = (norm2_w[l].reshape(1, D_MODEL), w_router[l], b_router[l],
                 w_gate_e[l], w_up_e[l], w_down_e[l], w_gate_s[l], w_up_s[l], w_down_s[l])

        proj_p = _in_proj(yp, mod_p, n1w, w_in_r, rows_per_mod=nb_p * len_p)
        yp, st = _mixer(proj_p, yp, mod_p, *mix_w, None, seq_len=len_p, row_len=len_p, emit_state=True)
        states.append(st)
        proj_s = _in_proj(ys, mod_s, n1w, w_in_r, rows_per_mod=len_s)
        (ys,) = _mixer(proj_s, ys, mod_s, *mix_w, state_gla[:, l], seq_len=len_s, row_len=GRID_W, emit_state=False)

        yp, ys = _moe(yp, ys, mod, *moe_w, fnw, tokens_per_mod=len_s, final=l == depth - 1)
    new_state = jnp.stack(states, axis=1)
    return (yp.reshape(nb_p, len_p, D_MODEL), ys.reshape(nb_s, len_s, D_MODEL), new_state)
```

```python
import functools

import jax
import jax.numpy as jnp
from jax import lax
from jax.experimental import pallas as pl
from jax.experimental.pallas import tpu as pltpu

F32 = jnp.float32
BF16 = jnp.bfloat16

D_MODEL = 1024
GRID_W = 64
D_CONV = 512
N_HEADS = 4
HEAD_D = 128
D_GLA = N_HEADS * HEAD_D
GLA_RANK = 16
GLA_GATE_NORM = 16.0
LOG2_E = 1.4426950408889634
CHUNK = 64
SUB = 8
N_SUB = CHUNK // SUB
N_EXPERTS = 64
TOP_K = 8
D_EXPERT = 256
ROUTED_SCALE = 2.5
EPS = 1e-6

C_U, C_GB, C_GC, C_Q, C_K, C_V, C_GO = 0, 512, 1024, 1536, 2048, 2560, 3072
C_BRC, C_BRG, C_LR = 3584, 4608, 5632
D_PROJ = 5760
LR_PAD = 128

VMEM_LIMIT = 56 * 1024 * 1024


def _dot(a, b):
    return jnp.dot(a, b, preferred_element_type=F32)


def _dot_nt(a, b):
    return lax.dot_general(a, b, (((1,), (1,)), ((), ())), preferred_element_type=F32)


def _dot_tn(a, b):
    return lax.dot_general(a, b, (((0,), (0,)), ((), ())), preferred_element_type=F32)


def _dot_hi(a, b):
    return jnp.dot(a, b, preferred_element_type=F32, precision=lax.Precision.HIGHEST)


def _split_bf16(x):
    hi = x.astype(BF16)
    lo = (x - hi.astype(F32)).astype(BF16)
    return hi, lo


def _rms(x):
    return x * lax.rsqrt(jnp.mean(x * x, axis=-1, keepdims=True) + EPS)


def _sigmoid(x):
    return 0.5 * jnp.tanh(0.5 * x) + 0.5


def _mod_kernel(cond_ref, w_ref, b_ref, o_ref):
    c = cond_ref[...]
    o_ref[...] = _dot_hi(c * jax.nn.sigmoid(c), w_ref[...]) + b_ref[...]


def _modulation(cond, w_mod, b_mod):
    n_rows = cond.shape[0]
    tn = 1536
    return pl.pallas_call(
        _mod_kernel,
        grid=(6 * D_MODEL // tn,),
        in_specs=[pl.BlockSpec((n_rows, D_MODEL), lambda j: (0, 0)),
                  pl.BlockSpec((D_MODEL, tn), lambda j: (0, j)),
                  pl.BlockSpec((1, tn), lambda j: (0, j))],
        out_specs=pl.BlockSpec((n_rows, tn), lambda j: (0, j)),
        out_shape=jax.ShapeDtypeStruct((n_rows, 6 * D_MODEL), F32),
        compiler_params=pltpu.CompilerParams(dimension_semantics=("arbitrary",),
                                             vmem_limit_bytes=VMEM_LIMIT),
        name="modulation",
    )(cond, w_mod, b_mod)


def _inproj_kernel(x_ref, mod_ref, nw_ref, w_ref, o_ref, *, rows_per_mod, tm):
    i = pl.program_id(1)
    row = (i * tm) // rows_per_mod
    sh = mod_ref[pl.ds(row, 1), 0:D_MODEL]
    sc = mod_ref[pl.ds(row, 1), D_MODEL:2 * D_MODEL]
    h = _rms(x_ref[...]) * nw_ref[...] * (1.0 + sc) + sh
    o_ref[...] = _dot(h.astype(BF16), w_ref[...]).astype(BF16)


def _in_proj(x2d, mod, norm_w, w_in_r, rows_per_mod):
    t = x2d.shape[0]
    tm, tn = 1024, 1920
    kern = functools.partial(_inproj_kernel, rows_per_mod=rows_per_mod, tm=tm)
    return pl.pallas_call(
        kern,
        grid=(D_PROJ // tn, t // tm),
        in_specs=[pl.BlockSpec((tm, D_MODEL), lambda j, i: (i, 0)),
                  pl.BlockSpec(mod.shape, lambda j, i: (0, 0)),
                  pl.BlockSpec((1, D_MODEL), lambda j, i: (0, 0)),
                  pl.BlockSpec((D_MODEL, tn), lambda j, i: (0, j))],
        out_specs=pl.BlockSpec((tm, tn), lambda j, i: (i, j)),
        out_shape=jax.ShapeDtypeStruct((t, D_PROJ), BF16),
        compiler_params=pltpu.CompilerParams(dimension_semantics=("arbitrary", "arbitrary"),
                                             vmem_limit_bytes=VMEM_LIMIT),
        name="in_proj",
    )(x2d, mod, norm_w, w_in_r)


def _log2_sigmoid(x):
    return jnp.minimum(x, 0.0) * LOG2_E - jnp.log2(1.0 + jnp.exp2(jnp.abs(x) * (-LOG2_E)))


def _gla_chunk_head(qc, kc, vc, bc, st, rev):
    lane = lax.broadcasted_iota(jnp.int32, (SUB, CHUNK), 1)
    sub = lax.broadcasted_iota(jnp.int32, (SUB, CHUNK), 0)
    tot = bc[0:1] if rev else bc[CHUNK - 1:CHUNK]

    o = _dot_nt((qc * jnp.exp2(bc)).astype(BF16), st.astype(BF16))
    k_tail = kc * jnp.exp2(tot - bc)
    st_new = st * jnp.exp2(tot) + _dot_tn(vc.astype(BF16), k_tail.astype(BF16))

    lhs_segs, rhs_segs = [], []

    def rows(before, mid, after):
        parts = ([jnp.zeros((before, HEAD_D), F32)] if before else []) + [mid]
        parts += [jnp.zeros((after, HEAD_D), F32)] if after else []
        return jnp.concatenate(parts, axis=0) if len(parts) > 1 else mid

    key_blocks = range(1, N_SUB) if rev else range(0, N_SUB - 1)
    for jb in key_blocks:
        r0 = jb * SUB
        ref_row = bc[r0:r0 + 1] if rev else bc[r0 + SUB - 1:r0 + SUB]
        ke = kc[r0:r0 + SUB] * jnp.exp2(ref_row - bc[r0:r0 + SUB])
        rhs_segs.append(rows(r0, ke, CHUNK - r0 - SUB))
        if rev:
            ql = qc[:r0] * jnp.exp2(bc[:r0] - ref_row)
            lhs_segs.append(rows(0, ql, CHUNK - r0))
        else:
            ql = qc[r0 + SUB:] * jnp.exp2(bc[r0 + SUB:] - ref_row)
            lhs_segs.append(rows(r0 + SUB, ql, 0))
    far = _dot_nt(jnp.concatenate(lhs_segs, axis=1).astype(BF16),
                  jnp.concatenate(rhs_segs, axis=1).astype(BF16))

    blocks = []
    for ib in range(N_SUB):
        r0 = ib * SUB
        qi, bi = qc[r0:r0 + SUB], bc[r0:r0 + SUB]
        acc = jnp.zeros((SUB, CHUNK), F32)
        for jj in range(SUB):
            j = r0 + jj
            e = jnp.exp2(bi - bc[j:j + 1])
            col = jnp.sum(qi * (kc[j:j + 1] * e), axis=-1, keepdims=True)
            acc = jnp.where(lane == j, col, acc)
        keep = (lane - r0 >= sub) if rev else (lane - r0 <= sub)
        blocks.append(jnp.where(keep, acc, 0.0))
    scores = far + jnp.concatenate(blocks, axis=0)
    o = o + _dot(scores.astype(BF16), vc.astype(BF16))
    return o, st_new


def _mixer_kernel(*refs, seq_len, row_len, has_s0, emit_state):
    it = iter(refs)
    proj_ref, x_ref, mod_ref, cw_ref, cb_ref, wdec_ref, bdec_ref, gnw_ref = (next(it) for _ in range(8))
    wbc_ref, wbg_ref, wout_ref = (next(it) for _ in range(3))
    s0_ref = next(it) if has_s0 else None
    out_ref = next(it)
    st_out_ref = next(it) if emit_state else None
    la_f_ref, la_b_ref, o_ref, st_ref = (next(it) for _ in range(4))

    L = seq_len
    n_chunks = L // CHUNK
    tr = 256
    assert tr % row_len == 0 and L % tr == 0

    ri = lax.broadcasted_iota(jnp.int32, (tr, tr), 0)
    ci = lax.broadcasted_iota(jnp.int32, (tr, tr), 1)
    same_chunk = (ri // CHUNK) == (ci // CHUNK)

    def decay_body(t, carry):
        r0 = pl.multiple_of(t * tr, tr)
        lr = proj_ref[pl.ds(r0, tr), C_LR:C_LR + LR_PAD]
        for d, ref in ((0, la_f_ref), (1, la_b_ref)):
            whi, wlo = _split_bf16(wdec_ref[d])
            z = _dot(lr, whi) + _dot(lr, wlo) + bdec_ref[d:d + 1]
            la_hi, la_lo = _split_bf16(_log2_sigmoid(z) * (1.0 / GLA_GATE_NORM))
            tri = jnp.where(same_chunk & ((ci >= ri) if d else (ci <= ri)), 1.0, 0.0).astype(BF16)
            ref[pl.ds(r0, tr), :] = _dot(tri, la_hi) + _dot(tri, la_lo)
        return carry
    lax.fori_loop(0, L // tr, decay_body, 0)

    for rev, la_ref in ((False, la_f_ref), (True, la_b_ref)):
        d = 1 if rev else 0
        for h in range(N_HEADS):
            if has_s0:
                st_ref[h] = s0_ref[0, d, h].T
            else:
                st_ref[h] = jnp.zeros((HEAD_D, HEAD_D), F32)

        def chunk_body(c, carry, rev=rev, la_ref=la_ref):
            cc = (n_chunks - 1 - c) if rev else c
            r0 = pl.multiple_of(cc * CHUNK, CHUNK)
            for h in range(N_HEADS):
                lo, hi = h * HEAD_D, (h + 1) * HEAD_D
                qc = proj_ref[pl.ds(r0, CHUNK), C_Q + lo:C_Q + hi].astype(F32) * (HEAD_D ** -0.5)
                kc = proj_ref[pl.ds(r0, CHUNK), C_K + lo:C_K + hi].astype(F32)
                vc = proj_ref[pl.ds(r0, CHUNK), C_V + lo:C_V + hi].astype(F32)
                oc, st_new = _gla_chunk_head(qc, kc, vc, la_ref[pl.ds(r0, CHUNK), lo:hi], st_ref[h], rev)
                st_ref[h] = st_new
                if rev:
                    o_ref[pl.ds(r0, CHUNK), lo:hi] += oc
                else:
                    o_ref[pl.ds(r0, CHUNK), lo:hi] = oc
            return carry
        lax.fori_loop(0, n_chunks, chunk_body, 0, unroll=2)

        if emit_state:
            for h in range(N_HEADS):
                st_out_ref[0, d, h] = st_ref[h].T

    mod_row = pl.program_id(0) if mod_ref.shape[0] > 1 else 0
    g1 = mod_ref[pl.ds(mod_row, 1), 2 * D_MODEL:3 * D_MODEL]
    hsel_r = lax.broadcasted_iota(jnp.int32, (D_GLA, D_GLA), 0) // HEAD_D
    hsel_c = lax.broadcasted_iota(jnp.int32, (D_GLA, D_GLA), 1) // HEAD_D
    head_avg = jnp.where(hsel_r == hsel_c, 1.0 / HEAD_D, 0.0).astype(BF16)
    pos = lax.broadcasted_iota(jnp.int32, (tr, 1), 0)

    def dense_body(t, carry):
        r0 = pl.multiple_of(t * tr, tr)
        rows = pl.ds(r0, tr)
        cu = (proj_ref[rows, C_GC:C_GC + D_CONV].astype(F32) * proj_ref[rows, C_U:C_U + D_CONV].astype(F32))
        in_row = pos % row_len
        left = jnp.where(in_row == 0, 0.0, pltpu.roll(cu, 1, axis=0))
        right = jnp.where(in_row == row_len - 1, 0.0, pltpu.roll(cu, tr - 1, axis=0))
        conv = cw_ref[0:1] * left + cw_ref[1:2] * cu + cw_ref[2:3] * right + cb_ref[...]
        y_conv = proj_ref[rows, C_GB:C_GB + D_CONV].astype(F32) * conv

        o = o_ref[rows, :]
        osq_hi, osq_lo = _split_bf16(o * o)
        ms = _dot(osq_hi, head_avg) + _dot(osq_lo, head_avg)
        g_out = proj_ref[rows, C_GO:C_GO + D_GLA].astype(F32)
        y_gla = o * lax.rsqrt(ms + EPS) * gnw_ref[...] * (g_out * _sigmoid(g_out))

        merged = (_sigmoid(proj_ref[rows, C_BRC:C_BRC + D_MODEL].astype(F32)) * _dot(y_conv.astype(BF16), wbc_ref[...])
                  + _sigmoid(proj_ref[rows, C_BRG:C_BRG + D_MODEL].astype(F32)) * _dot(y_gla.astype(BF16), wbg_ref[...]))
        out_ref[rows, :] = x_ref[rows, :] + g1 * _dot(merged.astype(BF16), wout_ref[...])
        return carry
    lax.fori_loop(0, L // tr, dense_body, 0)


def _mixer(proj, x2d, mod, conv_w, conv_b, wdec, bdec, gnw, wbc, wbg, wout, s0, *, seq_len, row_len, emit_state):
    t = x2d.shape[0]
    nb = t // seq_len
    has_s0 = s0 is not None
    one = pl.Buffered(1)
    const = lambda shape: pl.BlockSpec(shape, lambda b: (0,) * len(shape), pipeline_mode=one)
    in_specs = [pl.BlockSpec((seq_len, D_PROJ), lambda b: (b, 0), pipeline_mode=one),
                pl.BlockSpec((seq_len, D_MODEL), lambda b: (b, 0)),
                const(mod.shape),
                const((3, D_CONV)), const((1, D_CONV)), const((2, LR_PAD, D_GLA)), const((2, D_GLA)),
                const((1, D_GLA)), const((D_CONV, D_MODEL)), const((D_GLA, D_MODEL)), const((D_MODEL, D_MODEL))]
    args = [proj, x2d, mod, conv_w, conv_b, wdec, bdec, gnw, wbc, wbg, wout]
    if has_s0:
        in_specs.append(pl.BlockSpec((1, 2, N_HEADS, HEAD_D, HEAD_D), lambda b: (b, 0, 0, 0, 0)))
        args.append(s0)
    out_specs = [pl.BlockSpec((seq_len, D_MODEL), lambda b: (b, 0))]
    out_shape = [jax.ShapeDtypeStruct((t, D_MODEL), F32)]
    if emit_state:
        out_specs.append(pl.BlockSpec((1, 2, N_HEADS, HEAD_D, HEAD_D), lambda b: (b, 0, 0, 0, 0)))
        out_shape.append(jax.ShapeDtypeStruct((nb, 2, N_HEADS, HEAD_D, HEAD_D), F32))
    kern = functools.partial(_mixer_kernel, seq_len=seq_len, row_len=row_len, has_s0=has_s0, emit_state=emit_state)
    return pl.pallas_call(
        kern,
        grid=(nb,),
        in_specs=in_specs,
        out_specs=out_specs,
        out_shape=out_shape,
        scratch_shapes=[pltpu.VMEM((seq_len, D_GLA), F32), pltpu.VMEM((seq_len, D_GLA), F32),
                        pltpu.VMEM((seq_len, D_GLA), F32), pltpu.VMEM((N_HEADS, HEAD_D, HEAD_D), F32)],
        compiler_params=pltpu.CompilerParams(dimension_semantics=("arbitrary",),
                                             vmem_limit_bytes=VMEM_LIMIT),
        name="mixer",
    )(*args)


I32 = jnp.int32
TB = 256
ROW_ALIGN = 16
TR = 512
R_LOC = 3072
H2W = 1152


def _dot_nt_hi(a, b):
    return lax.dot_general(a, b, (((1,), (1,)), ((), ())), preferred_element_type=F32,
                           precision=lax.Precision.HIGHEST)


def _select_x(i, n_p_tiles, xp_ref, xs_ref):
    return jnp.where(i < n_p_tiles, xp_ref[...], xs_ref[...])


def _mod_row(i, n_p_tiles, tiles_per_mod):
    return jnp.where(i < n_p_tiles, 0, 1 + (i - n_p_tiles) // tiles_per_mod)


def _route_kernel(xp_ref, xs_ref, mod_ref, n2w_ref, wrt_ref, brb_ref, h2_ref, lpos_ref, cnt_ref, *,
                  n_p_tiles, tiles_per_mod):
    i = pl.program_id(0)
    row = _mod_row(i, n_p_tiles, tiles_per_mod)
    x = _select_x(i, n_p_tiles, xp_ref, xs_ref)
    sh = mod_ref[pl.ds(row, 1), 3 * D_MODEL:4 * D_MODEL]
    sc = mod_ref[pl.ds(row, 1), 4 * D_MODEL:5 * D_MODEL]
    h2 = _rms(x) * n2w_ref[...] * (1.0 + sc) + sh

    scores = jax.nn.sigmoid(_dot_nt_hi(wrt_ref[...], h2))
    biased = scores + brb_ref[...]
    eidx = lax.broadcasted_iota(I32, scores.shape, 0)
    picks = []
    for _k in range(TOP_K):
        m = jnp.max(biased, axis=0, keepdims=True)
        first = jnp.min(jnp.where(biased == m, eidx, N_EXPERTS), axis=0, keepdims=True)
        pick = eidx == first
        picks.append(pick)
        biased = jnp.where(pick, -jnp.inf, biased)
    sel = jnp.zeros(scores.shape, F32)
    for pick in picks:
        sel = jnp.where(pick, 1.0, sel)
    selsc = sel * scores
    comb = selsc / jnp.sum(selsc, axis=0, keepdims=True) * ROUTED_SCALE

    selb = sel.astype(BF16)
    tr_ = lax.broadcasted_iota(I32, (TB, TB), 0)
    tc_ = lax.broadcasted_iota(I32, (TB, TB), 1)
    rank = _dot(selb, jnp.where(tr_ < tc_, 1.0, 0.0).astype(BF16))
    n_b = _dot(selb, jnp.ones((TB, 128), BF16))
    m_b = jnp.floor((n_b + (ROW_ALIGN - 1)) * (1.0 / ROW_ALIGN)) * ROW_ALIGN
    er_ = lax.broadcasted_iota(I32, (N_EXPERTS, N_EXPERTS), 0)
    ec_ = lax.broadcasted_iota(I32, (N_EXPERTS, N_EXPERTS), 1)
    loff_b = _dot(jnp.where(ec_ < er_, 1.0, 0.0).astype(BF16), m_b.astype(BF16))
    lposf = jnp.concatenate([loff_b] * (TB // 128), axis=1) + rank
    rows = [jnp.sum(jnp.where(pick, lposf, 0.0), axis=0, keepdims=True) for pick in picks]
    lpos_ref[0] = jnp.concatenate(rows, axis=0).astype(I32)
    cnt_ref[0] = m_b

    combt = comb.T
    chi = combt.astype(BF16).astype(F32)
    h2_ref[:, 0:D_MODEL] = h2.astype(BF16)
    h2_ref[:, D_MODEL:H2W] = jnp.concatenate([chi, combt - chi], axis=1).astype(BF16)


def _route(x1p, x1s, mod, n2w, w_router_t, b_router_b, *, tiles_per_mod):
    n_p, n_s = x1p.shape[0] // TB, x1s.shape[0] // TB
    nt = n_p + n_s
    kern = functools.partial(_route_kernel, n_p_tiles=n_p, tiles_per_mod=tiles_per_mod)
    const = lambda shape: pl.BlockSpec(shape, lambda i: (0,) * len(shape))
    return pl.pallas_call(
        kern,
        grid=(nt,),
        in_specs=[pl.BlockSpec((TB, D_MODEL), lambda i: (jnp.minimum(i, n_p - 1), 0)),
                  pl.BlockSpec((TB, D_MODEL), lambda i: (jnp.maximum(i - n_p, 0), 0)),
                  const(mod.shape), const((1, D_MODEL)), const((N_EXPERTS, D_MODEL)), const((N_EXPERTS, TB))],
        out_specs=[pl.BlockSpec((TB, H2W), lambda i: (i, 0)),
                   pl.BlockSpec((1, TOP_K, TB), lambda i: (i, 0, 0)),
                   pl.BlockSpec((1, N_EXPERTS, 128), lambda i: (i, 0, 0))],
        out_shape=[jax.ShapeDtypeStruct((nt * TB, H2W), BF16),
                   jax.ShapeDtypeStruct((nt, TOP_K, TB), I32),
                   jax.ShapeDtypeStruct((nt, N_EXPERTS, 128), F32)],
        compiler_params=pltpu.CompilerParams(dimension_semantics=("arbitrary",), vmem_limit_bytes=VMEM_LIMIT),
        name="moe_route",
    )(x1p, x1s, mod, n2w, w_router_t, b_router_b)


def _plan_kernel(cnt_ref, off_ref, loff_ref, msz_ref, tail_ref, te_ref, *, nt, n_row_tiles):
    lane = lax.broadcasted_iota(I32, (N_EXPERTS, 128), 1)
    m = jnp.zeros((N_EXPERTS, 128), F32)
    for i in range(nt):
        m = jnp.where(lane == i, cnt_ref[i], m)
    total = jnp.broadcast_to(jnp.sum(m, axis=1, keepdims=True), (N_EXPERTS, 128))
    gsz = jnp.floor((total + (TR - 1)) * (1.0 / TR)) * TR
    er_ = lax.broadcasted_iota(I32, (N_EXPERTS, N_EXPERTS), 0)
    ec_ = lax.broadcasted_iota(I32, (N_EXPERTS, N_EXPERTS), 1)
    lstrict = jnp.where(ec_ < er_, 1.0, 0.0)
    ir_ = lax.broadcasted_iota(I32, (128, 128), 0)
    ic_ = lax.broadcasted_iota(I32, (128, 128), 1)
    ustrict = jnp.where(ir_ < ic_, 1.0, 0.0)
    gstart = _dot_hi(lstrict, gsz)
    off_ref[...] = (gstart + _dot_hi(m, ustrict)).astype(I32)
    loff_ref[...] = _dot_hi(lstrict, m).astype(I32)
    msz_ref[...] = m.astype(I32)
    tail_ref[...] = jnp.where(lane == 0, gstart + total, jnp.where(lane == 1, gsz - total, 0.0)).astype(I32)
    gend = gstart + gsz
    te_lanes = te_ref.shape[1]
    gend_w = jnp.concatenate([gend] * (te_lanes // 128), axis=1)
    tile_start = lax.broadcasted_iota(I32, (N_EXPERTS, te_lanes), 1).astype(F32) * TR
    te = jnp.sum(jnp.where(gend_w <= tile_start, 1.0, 0.0), axis=0, keepdims=True)
    used = gend[N_EXPERTS - 1:N_EXPERTS, 0:1] * (1.0 / TR)
    te_lane = lax.broadcasted_iota(I32, (1, te_lanes), 1)
    te = jnp.where(te_lane == n_row_tiles, used, jnp.minimum(te, N_EXPERTS - 1.0))
    te_ref[...] = jnp.broadcast_to(te, te_ref.shape).astype(I32)


def _plan(cnt, n_row_tiles):
    nt = cnt.shape[0]
    assert nt <= 128
    te_lanes = -(-(n_row_tiles + 1) // 128) * 128
    tab = jax.ShapeDtypeStruct((N_EXPERTS, 128), I32)
    return pl.pallas_call(
        functools.partial(_plan_kernel, nt=nt, n_row_tiles=n_row_tiles),
        out_shape=[tab, tab, tab, tab, jax.ShapeDtypeStruct((8, te_lanes), I32)],
        compiler_params=pltpu.CompilerParams(vmem_limit_bytes=VMEM_LIMIT),
        name="moe_plan",
    )(cnt)


def _start_copies(msz_ref, tile, make_copy):
    def body(e, carry):
        m = msz_ref[e, tile]

        @pl.when(m > 0)
        def _():
            make_copy(e, pl.multiple_of(m, ROW_ALIGN)).start()
        return carry
    lax.fori_loop(0, N_EXPERTS, body, 0, unroll=8)


def _tile_rows(loff_ref, msz_ref, tile):
    return pl.multiple_of(loff_ref[N_EXPERTS - 1, tile] + msz_ref[N_EXPERTS - 1, tile], ROW_ALIGN)


def _dispatch_kernel(off_ref, loff_ref, msz_ref, tail_ref, h2_ref, lpos_ref, xs_hbm, xloc_ref, zero_ref, sem,
                     tail_sem, *, nt):
    i = pl.program_id(0)
    slot = i % 2

    def copy_for(tile, slot_):
        def make(e, m):
            lo = pl.multiple_of(loff_ref[e, tile], ROW_ALIGN)
            of = pl.multiple_of(off_ref[e, tile], ROW_ALIGN)
            return pltpu.make_async_copy(xloc_ref.at[slot_, pl.ds(lo, m)], xs_hbm.at[pl.ds(of, m)], sem.at[slot_])
        return make

    def wait_tile(tile, slot_):
        n = _tile_rows(loff_ref, msz_ref, tile)
        pltpu.make_async_copy(xloc_ref.at[slot_, pl.ds(0, n)], xs_hbm.at[pl.ds(0, n)], sem.at[slot_]).wait()

    used_rows = _tile_rows(loff_ref, msz_ref, i)
    lpos = lpos_ref[0].astype(jnp.int16)
    h2 = h2_ref[...]
    ck = 1024
    one, zero = jnp.ones((ck, TB), BF16), jnp.zeros((ck, TB), BF16)
    for c in range(R_LOC // ck):
        @pl.when(c * ck < used_rows)
        def _(c=c):
            r = (lax.broadcasted_iota(I32, (ck, TB), 0) + c * ck).astype(jnp.int16)
            d = zero
            for k in range(TOP_K):
                d = jnp.where(r == lpos[k:k + 1, :], one, d)
            res = _dot(d, h2)
            xloc_ref[slot, c * ck:(c + 1) * ck, :] = res.astype(BF16)

    _start_copies(msz_ref, i, copy_for(i, slot))

    @pl.when(i > 0)
    def _():
        wait_tile(i - 1, 1 - slot)

    @pl.when(i == nt - 1)
    def _():
        zero_ref[...] = jnp.zeros(zero_ref.shape, BF16)

        def tail_copies(start):
            def body(e, carry):
                n = tail_ref[e, 1]

                @pl.when(n > 0)
                def _():
                    st = pl.multiple_of(tail_ref[e, 0], ROW_ALIGN)
                    nn = pl.multiple_of(n, ROW_ALIGN)
                    cp = pltpu.make_async_copy(zero_ref.at[pl.ds(0, nn)], xs_hbm.at[pl.ds(st, nn)], tail_sem)
                    if start:
                        cp.start()
                    else:
                        cp.wait()
                return carry
            lax.fori_loop(0, N_EXPERTS, body, 0)
        tail_copies(True)
        wait_tile(i, slot)
        tail_copies(False)


def _dispatch(off, loff, msz, tail, h2ext, lpos, n_rows):
    nt = lpos.shape[0]
    grid_spec = pltpu.PrefetchScalarGridSpec(
        num_scalar_prefetch=4,
        grid=(nt,),
        in_specs=[pl.BlockSpec((TB, H2W), lambda i, *_: (i, 0)),
                  pl.BlockSpec((1, TOP_K, TB), lambda i, *_: (i, 0, 0))],
        out_specs=pl.BlockSpec(memory_space=pl.ANY),
        scratch_shapes=[pltpu.VMEM((2, R_LOC, H2W), BF16), pltpu.VMEM((TR, H2W), BF16),
                        pltpu.SemaphoreType.DMA((2,)), pltpu.SemaphoreType.DMA],
    )
    return pl.pallas_call(
        functools.partial(_dispatch_kernel, nt=nt),
        grid_spec=grid_spec,
        out_shape=jax.ShapeDtypeStruct((n_rows, H2W), BF16),
        compiler_params=pltpu.CompilerParams(dimension_semantics=("arbitrary",), vmem_limit_bytes=VMEM_LIMIT),
        name="moe_dispatch",
    )(off, loff, msz, tail, h2ext, lpos)


def _expert_kernel(te_ref, xs_ref, wg_ref, wu_ref, wd_ref, ys_ref, wgu_ref, wdb_ref, *, n_row_tiles):
    j = pl.program_id(0)
    e = te_ref[j]

    @pl.when(j < te_ref[n_row_tiles])
    def _():
        @pl.when((j == 0) | (e != te_ref[jnp.maximum(j - 1, 0)]))
        def _():
            wgu_ref[:, :D_EXPERT] = wg_ref[0].astype(BF16)
            wgu_ref[:, D_EXPERT:] = wu_ref[0].astype(BF16)
            wdb_ref[...] = wd_ref[0].astype(BF16)

        x = xs_ref[:, 0:D_MODEL]
        ext = xs_ref[:, D_MODEL:H2W].astype(F32)
        wts = ext[:, :N_EXPERTS] + ext[:, N_EXPERTS:]
        lane = lax.broadcasted_iota(I32, wts.shape, 1)
        w = jnp.sum(jnp.where(lane == e, wts, 0.0), axis=-1, keepdims=True)
        h = _dot(x, wgu_ref[...])
        hg, hu = h[:, :D_EXPERT], h[:, D_EXPERT:]
        act = hg * _sigmoid(hg) * hu * w
        ys_ref[...] = _dot(act.astype(BF16), wdb_ref[...]).astype(BF16)


def _experts(te, xs, wg, wu, wd, n_row_tiles):
    def row_tile(j, te_ref):
        return jnp.maximum(jnp.minimum(j, te_ref[n_row_tiles] - 1), 0)
    grid_spec = pltpu.PrefetchScalarGridSpec(
        num_scalar_prefetch=1,
        grid=(n_row_tiles,),
        in_specs=[pl.BlockSpec((TR, H2W), lambda j, te_ref: (row_tile(j, te_ref), 0)),
                  pl.BlockSpec((1, D_MODEL, D_EXPERT), lambda j, te_ref: (te_ref[row_tile(j, te_ref)], 0, 0)),
                  pl.BlockSpec((1, D_MODEL, D_EXPERT), lambda j, te_ref: (te_ref[row_tile(j, te_ref)], 0, 0)),
                  pl.BlockSpec((1, D_EXPERT, D_MODEL), lambda j, te_ref: (te_ref[row_tile(j, te_ref)], 0, 0))],
        out_specs=pl.BlockSpec((TR, D_MODEL), lambda j, te_ref: (row_tile(j, te_ref), 0)),
        scratch_shapes=[pltpu.VMEM((D_MODEL, 2 * D_EXPERT), BF16), pltpu.VMEM((D_EXPERT, D_MODEL), BF16)],
    )
    return pl.pallas_call(
        functools.partial(_expert_kernel, n_row_tiles=n_row_tiles),
        grid_spec=grid_spec,
        out_shape=jax.ShapeDtypeStruct((n_row_tiles * TR, D_MODEL), BF16),
        compiler_params=pltpu.CompilerParams(dimension_semantics=("arbitrary",), vmem_limit_bytes=VMEM_LIMIT),
        name="moe_experts",
    )(te, xs, wg, wu, wd)


def _combine_kernel(off_ref, loff_ref, msz_ref, lpos_ref, h2_ref, xp_ref, xs_ref, mod_ref, wgs_ref, wus_ref, wds_ref,
                    fnw_ref, ysrt_hbm, yp_ref, ys_ref, yloc_ref, acc_ref, sem, *, nt, n_p_tiles, tiles_per_mod,
                    final):
    i = pl.program_id(0)
    slot = i % 2

    def copy_for(tile, slot_):
        def make(e, m):
            lo = pl.multiple_of(loff_ref[e, tile], ROW_ALIGN)
            of = pl.multiple_of(off_ref[e, tile], ROW_ALIGN)
            return pltpu.make_async_copy(ysrt_hbm.at[pl.ds(of, m)], yloc_ref.at[slot_, pl.ds(lo, m)], sem.at[slot_])
        return make

    @pl.when(i == 0)
    def _():
        yloc_ref[...] = jnp.zeros(yloc_ref.shape, BF16)
        _start_copies(msz_ref, 0, copy_for(0, 0))

    @pl.when(i + 1 < nt)
    def _():
        _start_copies(msz_ref, i + 1, copy_for(i + 1, 1 - slot))

    hb = h2_ref[...]
    hg = _dot(hb, wgs_ref[...].astype(BF16))
    hu = _dot(hb, wus_ref[...].astype(BF16))
    acc_ref[...] = _dot((hg * _sigmoid(hg) * hu).astype(BF16), wds_ref[...].astype(BF16))

    used_rows = _tile_rows(loff_ref, msz_ref, i)
    pltpu.make_async_copy(ysrt_hbm.at[pl.ds(0, used_rows)], yloc_ref.at[slot, pl.ds(0, used_rows)],
                          sem.at[slot]).wait()

    lpos_pad = jnp.concatenate([lpos_ref[0].astype(F32), jnp.zeros((128 - TOP_K, TB), F32)], axis=0)
    lposc = lpos_pad.T.astype(I32)
    ck = 512
    cols = [jnp.broadcast_to(lposc[:, k:k + 1], (TB, ck)).astype(jnp.int16) for k in range(TOP_K)]
    one, zero = jnp.ones((TB, ck), BF16), jnp.zeros((TB, ck), BF16)
    for c in range(R_LOC // ck):
        @pl.when(c * ck < used_rows)
        def _(c=c):
            r = (lax.broadcasted_iota(I32, (TB, ck), 1) + c * ck).astype(jnp.int16)
            cm = zero
            for k in range(TOP_K):
                cm = jnp.where(r == cols[k], one, cm)
            acc_ref[...] += _dot(cm, yloc_ref[slot, c * ck:(c + 1) * ck, :])

    row = _mod_row(i, n_p_tiles, tiles_per_mod)
    g2 = mod_ref[pl.ds(row, 1), 5 * D_MODEL:6 * D_MODEL]
    x2 = _select_x(i, n_p_tiles, xp_ref, xs_ref) + g2 * acc_ref[...]
    y = _rms(x2) * fnw_ref[...] if final else x2

    @pl.when(i < n_p_tiles)
    def _():
        yp_ref[...] = y

    @pl.when(i >= n_p_tiles)
    def _():
        ys_ref[...] = y


def _combine(off, loff, msz, lpos, h2ext, x1p, x1s, mod, wgs, wus, wds, fnw, ysorted, *, tiles_per_mod, final):
    n_p, n_s = x1p.shape[0] // TB, x1s.shape[0] // TB
    nt = n_p + n_s
    const = lambda shape: pl.BlockSpec(shape, lambda i, *_: (0,) * len(shape), pipeline_mode=pl.Buffered(1))
    p_idx = lambda i, *_: (jnp.minimum(i, n_p - 1), 0)
    s_idx = lambda i, *_: (jnp.maximum(i - n_p, 0), 0)
    grid_spec = pltpu.PrefetchScalarGridSpec(
        num_scalar_prefetch=3,
        grid=(nt,),
        in_specs=[pl.BlockSpec((1, TOP_K, TB), lambda i, *_: (i, 0, 0)),
                  pl.BlockSpec((TB, D_MODEL), lambda i, *_: (i, 0)),
                  pl.BlockSpec((TB, D_MODEL), p_idx), pl.BlockSpec((TB, D_MODEL), s_idx),
                  const(mod.shape), const((D_MODEL, D_EXPERT)), const((D_MODEL, D_EXPERT)), const((D_EXPERT, D_MODEL)),
                  const((1, D_MODEL)), pl.BlockSpec(memory_space=pl.ANY)],
        out_specs=[pl.BlockSpec((TB, D_MODEL), p_idx), pl.BlockSpec((TB, D_MODEL), s_idx)],
        scratch_shapes=[pltpu.VMEM((2, R_LOC, D_MODEL), BF16), pltpu.VMEM((TB, D_MODEL), F32),
                        pltpu.SemaphoreType.DMA((2,))],
    )
    kern = functools.partial(_combine_kernel, nt=nt, n_p_tiles=n_p, tiles_per_mod=tiles_per_mod, final=final)
    return pl.pallas_call(
        kern,
        grid_spec=grid_spec,
        out_shape=[jax.ShapeDtypeStruct(x1p.shape, F32), jax.ShapeDtypeStruct(x1s.shape, F32)],
        compiler_params=pltpu.CompilerParams(dimension_semantics=("arbitrary",), vmem_limit_bytes=VMEM_LIMIT),
        name="moe_combine",
    )(off, loff, msz, lpos, h2ext, x1p, x1s, mod, wgs, wus, wds, fnw, ysorted)


def _moe(x1p, x1s, mod, n2w, w_router, b_router, wg, wu, wd, wgs, wus, wds, fnw, *, tokens_per_mod, final):
    assert R_LOC >= TB * TOP_K + N_EXPERTS * (ROW_ALIGN - 1) and tokens_per_mod % TB == 0
    nt = (x1p.shape[0] + x1s.shape[0]) // TB
    n_rows_max = nt * TB * TOP_K + nt * N_EXPERTS * (ROW_ALIGN - 1) + N_EXPERTS * (TR - ROW_ALIGN)
    n_row_tiles = -(-n_rows_max // TR)
    tiles_per_mod = tokens_per_mod // TB
    brb = jnp.broadcast_to(b_router.reshape(N_EXPERTS, 1), (N_EXPERTS, TB))
    h2ext, lpos, cnt = _route(x1p, x1s, mod, n2w, w_router.T, brb, tiles_per_mod=tiles_per_mod)
    off, loff, msz, tail, te = _plan(cnt, n_row_tiles)
    xs = _dispatch(off, loff, msz, tail, h2ext, lpos, n_row_tiles * TR)
    ysorted = _experts(te[0], xs, wg, wu, wd, n_row_tiles)
    return _combine(off, loff, msz, lpos, h2ext, x1p, x1s, mod, wgs, wus, wds, fnw, ysorted,
                    tiles_per_mod=tiles_per_mod, final=final)


def kernel(x_prompt, x_sample, state_gla, c, c_ctx, w_mod, b_mod, norm1_w, w_in, conv_w, conv_b, w_decay, b_decay,
           gla_norm_w, w_br_conv, w_br_gla, w_out, norm2_w, w_router, b_router, w_gate_e, w_up_e, w_down_e,
           w_gate_s, w_up_s, w_down_s, final_norm_w):
    depth = w_mod.shape[0]
    nb_p, len_p, _ = x_prompt.shape
    nb_s, len_s, _ = x_sample.shape
    yp = x_prompt.reshape(nb_p * len_p, D_MODEL)
    ys = x_sample.reshape(nb_s * len_s, D_MODEL)
    fnw = final_norm_w.reshape(1, D_MODEL)

    cond = jnp.concatenate([c_ctx[None, :], c, jnp.zeros((8 - 1 - nb_s, D_MODEL), F32)], axis=0)
    states = []
    for l in range(depth):
        mod = _modulation(cond, w_mod[l], b_mod[l].reshape(1, -1))
        mod_p, mod_s = mod[0:1], mod[1:1 + nb_s]

        wl = w_in[l]
        w_in_r = jnp.concatenate([wl[:, :3584], wl[:, 3616:5664], wl[:, 3584:3616],
                                  jnp.zeros((D_MODEL, D_PROJ - 5664), F32)], axis=1).astype(BF16)
        wdec = jnp.zeros((2, LR_PAD, D_GLA), F32)
        wdec = wdec.at[0, 0:GLA_RANK].set(w_decay[l, 0]).at[1, GLA_RANK:2 * GLA_RANK].set(w_decay[l, 1])
        n1w = norm1_w[l].reshape(1, D_MODEL)
        mix_w = (conv_w[l], conv_b[l].reshape(1, D_CONV), wdec, b_decay[l], gla_norm_w[l].reshape(1, D_GLA),
                 w_br_conv[l].astype(BF16), w_br_gla[l].astype(BF16), w_out[l].astype(BF16))
        moe_w = (norm2_w[l].reshape(1, D_MODEL), w_router[l], b_router[l],
                 w_gate_e[l], w_up_e[l], w_down_e[l], w_gate_s[l], w_up_s[l], w_down_s[l])

        proj_p = _in_proj(yp, mod_p, n1w, w_in_r, rows_per_mod=nb_p * len_p)
        yp, st = _mixer(proj_p, yp, mod_p, *mix_w, None, seq_len=len_p, row_len=len_p, emit_state=True)
        states.append(st)
        proj_s = _in_proj(ys, mod_s, n1w, w_in_r, rows_per_mod=len_s)
        (ys,) = _mixer(proj_s, ys, mod_s, *mix_w, state_gla[:, l], seq_len=len_s, row_len=GRID_W, emit_state=False)

        yp, ys = _moe(yp, ys, mod, *moe_w, fnw, tokens_per_mod=len_s, final=l == depth - 1)
    new_state = jnp.stack(states, axis=1)
    return (yp.reshape(nb_p, len_p, D_MODEL), ys.reshape(nb_s, len_s, D_MODEL), new_state)
```

```python
import functools

import jax
import jax.numpy as jnp
from jax import lax
from jax.experimental import pallas as pl
from jax.experimental.pallas import tpu as pltpu

F32 = jnp.float32
BF16 = jnp.bfloat16

D_MODEL = 1024
GRID_W = 64
D_CONV = 512
N_HEADS = 4
HEAD_D = 128
D_GLA = N_HEADS * HEAD_D
GLA_RANK = 16
GLA_GATE_NORM = 16.0
LOG2_E = 1.4426950408889634
CHUNK = 64
SUB = 8
N_SUB = CHUNK // SUB
N_EXPERTS = 64
TOP_K = 8
D_EXPERT = 256
ROUTED_SCALE = 2.5
EPS = 1e-6

C_U, C_GB, C_GC, C_Q, C_K, C_V, C_GO = 0, 512, 1024, 1536, 2048, 2560, 3072
C_BRC, C_BRG, C_LR = 3584, 4608, 5632
D_PROJ = 5760
LR_PAD = 128

VMEM_LIMIT = 56 * 1024 * 1024


def _dot(a, b):
    return jnp.dot(a, b, preferred_element_type=F32)


def _dot_nt(a, b):
    return lax.dot_general(a, b, (((1,), (1,)), ((), ())), preferred_element_type=F32)


def _dot_tn(a, b):
    return lax.dot_general(a, b, (((0,), (0,)), ((), ())), preferred_element_type=F32)


def _dot_hi(a, b):
    return jnp.dot(a, b, preferred_element_type=F32, precision=lax.Precision.HIGHEST)


def _split_bf16(x):
    hi = x.astype(BF16)
    lo = (x - hi.astype(F32)).astype(BF16)
    return hi, lo


def _rms(x):
    return x * lax.rsqrt(jnp.mean(x * x, axis=-1, keepdims=True) + EPS)


def _sigmoid(x):
    return 0.5 * jnp.tanh(0.5 * x) + 0.5


def _mod_kernel(cond_ref, w_ref, b_ref, o_ref):
    c = cond_ref[...]
    o_ref[...] = _dot_hi(c * jax.nn.sigmoid(c), w_ref[...]) + b_ref[...]


def _modulation(cond, w_mod, b_mod):
    n_rows = cond.shape[0]
    tn = 1536
    return pl.pallas_call(
        _mod_kernel,
        grid=(6 * D_MODEL // tn,),
        in_specs=[pl.BlockSpec((n_rows, D_MODEL), lambda j: (0, 0)),
                  pl.BlockSpec((D_MODEL, tn), lambda j: (0, j)),
                  pl.BlockSpec((1, tn), lambda j: (0, j))],
        out_specs=pl.BlockSpec((n_rows, tn), lambda j: (0, j)),
        out_shape=jax.ShapeDtypeStruct((n_rows, 6 * D_MODEL), F32),
        compiler_params=pltpu.CompilerParams(dimension_semantics=("arbitrary",),
                                             vmem_limit_bytes=VMEM_LIMIT),
        name="modulation",
    )(cond, w_mod, b_mod)


def _inproj_kernel(x_ref, mod_ref, nw_ref, w_ref, o_ref, *, rows_per_mod, tm):
    i = pl.program_id(1)
    row = (i * tm) // rows_per_mod
    sh = mod_ref[pl.ds(row, 1), 0:D_MODEL]
    sc = mod_ref[pl.ds(row, 1), D_MODEL:2 * D_MODEL]
    h = _rms(x_ref[...]) * nw_ref[...] * (1.0 + sc) + sh
    o_ref[...] = _dot(h.astype(BF16), w_ref[...]).astype(BF16)


def _in_proj(x2d, mod, norm_w, w_in_r, rows_per_mod):
    t = x2d.shape[0]
    tm, tn = 1024, 1920
    kern = functools.partial(_inproj_kernel, rows_per_mod=rows_per_mod, tm=tm)
    return pl.pallas_call(
        kern,
        grid=(D_PROJ // tn, t // tm),
        in_specs=[pl.BlockSpec((tm, D_MODEL), lambda j, i: (i, 0)),
                  pl.BlockSpec(mod.shape, lambda j, i: (0, 0)),
                  pl.BlockSpec((1, D_MODEL), lambda j, i: (0, 0)),
                  pl.BlockSpec((D_MODEL, tn), lambda j, i: (0, j))],
        out_specs=pl.BlockSpec((tm, tn), lambda j, i: (i, j)),
        out_shape=jax.ShapeDtypeStruct((t, D_PROJ), BF16),
        compiler_params=pltpu.CompilerParams(dimension_semantics=("arbitrary", "arbitrary"),
                                             vmem_limit_bytes=VMEM_LIMIT),
        name="in_proj",
    )(x2d, mod, norm_w, w_in_r)


def _log2_sigmoid(x):
    return jnp.minimum(x, 0.0) * LOG2_E - jnp.log2(1.0 + jnp.exp2(jnp.abs(x) * (-LOG2_E)))


def _gla_chunk_head(qc, kc, vc, bc, st, rev):
    lane = lax.broadcasted_iota(jnp.int32, (SUB, CHUNK), 1)
    sub = lax.broadcasted_iota(jnp.int32, (SUB, CHUNK), 0)
    tot = bc[0:1] if rev else bc[CHUNK - 1:CHUNK]

    o = _dot_nt((qc * jnp.exp2(bc)).astype(BF16), st.astype(BF16))
    k_tail = kc * jnp.exp2(tot - bc)
    st_new = st * jnp.exp2(tot) + _dot_tn(vc.astype(BF16), k_tail.astype(BF16))

    lhs_segs, rhs_segs = [], []

    def rows(before, mid, after):
        parts = ([jnp.zeros((before, HEAD_D), F32)] if before else []) + [mid]
        parts += [jnp.zeros((after, HEAD_D), F32)] if after else []
        return jnp.concatenate(parts, axis=0) if len(parts) > 1 else mid

    key_blocks = range(1, N_SUB) if rev else range(0, N_SUB - 1)
    for jb in key_blocks:
        r0 = jb * SUB
        ref_row = bc[r0:r0 + 1] if rev else bc[r0 + SUB - 1:r0 + SUB]
        ke = kc[r0:r0 + SUB] * jnp.exp2(ref_row - bc[r0:r0 + SUB])
        rhs_segs.append(rows(r0, ke, CHUNK - r0 - SUB))
        if rev:
            ql = qc[:r0] * jnp.exp2(bc[:r0] - ref_row)
            lhs_segs.append(rows(0, ql, CHUNK - r0))
        else:
            ql = qc[r0 + SUB:] * jnp.exp2(bc[r0 + SUB:] - ref_row)
            lhs_segs.append(rows(r0 + SUB, ql, 0))
    far = _dot_nt(jnp.concatenate(lhs_segs, axis=1).astype(BF16),
                  jnp.concatenate(rhs_segs, axis=1).astype(BF16))

    blocks = []
    for ib in range(N_SUB):
        r0 = ib * SUB
        qi, bi = qc[r0:r0 + SUB], bc[r0:r0 + SUB]
        acc = jnp.zeros((SUB, CHUNK), F32)
        for jj in range(SUB):
            j = r0 + jj
            e = jnp.exp2(bi - bc[j:j + 1])
            col = jnp.sum(qi * (kc[j:j + 1] * e), axis=-1, keepdims=True)
            acc = jnp.where(lane == j, col, acc)
        keep = (lane - r0 >= sub) if rev else (lane - r0 <= sub)
        blocks.append(jnp.where(keep, acc, 0.0))
    scores = far + jnp.concatenate(blocks, axis=0)
    o = o + _dot(scores.astype(BF16), vc.astype(BF16))
    return o, st_new


def _mixer_kernel(*refs, seq_len, row_len, has_s0, emit_state):
    it = iter(refs)
    proj_ref, x_ref, mod_ref, cw_ref, cb_ref, wdec_ref, bdec_ref, gnw_ref = (next(it) for _ in range(8))
    wbc_ref, wbg_ref, wout_ref = (next(it) for _ in range(3))
    s0_ref = next(it) if has_s0 else None
    out_ref = next(it)
    st_out_ref = next(it) if emit_state else None
    la_f_ref, la_b_ref, o_ref, st_ref = (next(it) for _ in range(4))

    L = seq_len
    n_chunks = L // CHUNK
    tr = 256
    assert tr % row_len == 0 and L % tr == 0

    ri = lax.broadcasted_iota(jnp.int32, (tr, tr), 0)
    ci = lax.broadcasted_iota(jnp.int32, (tr, tr), 1)
    same_chunk = (ri // CHUNK) == (ci // CHUNK)

    def decay_body(t, carry):
        r0 = pl.multiple_of(t * tr, tr)
        lr = proj_ref[pl.ds(r0, tr), C_LR:C_LR + LR_PAD]
        for d, ref in ((0, la_f_ref), (1, la_b_ref)):
            whi, wlo = _split_bf16(wdec_ref[d])
            z = _dot(lr, whi) + _dot(lr, wlo) + bdec_ref[d:d + 1]
            la_hi, la_lo = _split_bf16(_log2_sigmoid(z) * (1.0 / GLA_GATE_NORM))
            tri = jnp.where(same_chunk & ((ci >= ri) if d else (ci <= ri)), 1.0, 0.0).astype(BF16)
            ref[pl.ds(r0, tr), :] = _dot(tri, la_hi) + _dot(tri, la_lo)
        return carry
    lax.fori_loop(0, L // tr, decay_body, 0)

    for rev, la_ref in ((False, la_f_ref), (True, la_b_ref)):
        d = 1 if rev else 0
        for h in range(N_HEADS):
            if has_s0:
                st_ref[h] = s0_ref[0, d, h].T
            else:
                st_ref[h] = jnp.zeros((HEAD_D, HEAD_D), F32)

        def chunk_body(c, carry, rev=rev, la_ref=la_ref):
            cc = (n_chunks - 1 - c) if rev else c
            r0 = pl.multiple_of(cc * CHUNK, CHUNK)
            for h in range(N_HEADS):
                lo, hi = h * HEAD_D, (h + 1) * HEAD_D
                qc = proj_ref[pl.ds(r0, CHUNK), C_Q + lo:C_Q + hi].astype(F32) * (HEAD_D ** -0.5)
                kc = proj_ref[pl.ds(r0, CHUNK), C_K + lo:C_K + hi].astype(F32)
                vc = proj_ref[pl.ds(r0, CHUNK), C_V + lo:C_V + hi].astype(F32)
                oc, st_new = _gla_chunk_head(qc, kc, vc, la_ref[pl.ds(r0, CHUNK), lo:hi], st_ref[h], rev)
                st_ref[h] = st_new
                if rev:
                    o_ref[pl.ds(r0, CHUNK), lo:hi] += oc
                else:
                    o_ref[pl.ds(r0, CHUNK), lo:hi] = oc
            return carry
        lax.fori_loop(0, n_chunks, chunk_body, 0, unroll=2)

        if emit_state:
            for h in range(N_HEADS):
                st_out_ref[0, d, h] = st_ref[h].T

    mod_row = pl.program_id(0) if mod_ref.shape[0] > 1 else 0
    g1 = mod_ref[pl.ds(mod_row, 1), 2 * D_MODEL:3 * D_MODEL]
    hsel_r = lax.broadcasted_iota(jnp.int32, (D_GLA, D_GLA), 0) // HEAD_D
    hsel_c = lax.broadcasted_iota(jnp.int32, (D_GLA, D_GLA), 1) // HEAD_D
    head_avg = jnp.where(hsel_r == hsel_c, 1.0 / HEAD_D, 0.0).astype(BF16)
    pos = lax.broadcasted_iota(jnp.int32, (tr, 1), 0)

    def dense_body(t, carry):
        r0 = pl.multiple_of(t * tr, tr)
        rows = pl.ds(r0, tr)
        cu = (proj_ref[rows, C_GC:C_GC + D_CONV].astype(F32) * proj_ref[rows, C_U:C_U + D_CONV].astype(F32))
        in_row = pos % row_len
        left = jnp.where(in_row == 0, 0.0, pltpu.roll(cu, 1, axis=0))
        right = jnp.where(in_row == row_len - 1, 0.0, pltpu.roll(cu, tr - 1, axis=0))
        conv = cw_ref[0:1] * left + cw_ref[1:2] * cu + cw_ref[2:3] * right + cb_ref[...]
        y_conv = proj_ref[rows, C_GB:C_GB + D_CONV].astype(F32) * conv

        o = o_ref[rows, :]
        osq_hi, osq_lo = _split_bf16(o * o)
        ms = _dot(osq_hi, head_avg) + _dot(osq_lo, head_avg)
        g_out = proj_ref[rows, C_GO:C_GO + D_GLA].astype(F32)
        y_gla = o * lax.rsqrt(ms + EPS) * gnw_ref[...] * (g_out * _sigmoid(g_out))

        merged = (_sigmoid(proj_ref[rows, C_BRC:C_BRC + D_MODEL].astype(F32)) * _dot(y_conv.astype(BF16), wbc_ref[...])
                  + _sigmoid(proj_ref[rows, C_BRG:C_BRG + D_MODEL].astype(F32)) * _dot(y_gla.astype(BF16), wbg_ref[...]))
        out_ref[rows, :] = x_ref[rows, :] + g1 * _dot(merged.astype(BF16), wout_ref[...])
        return carry
    lax.fori_loop(0, L // tr, dense_body, 0)


def _mixer(proj, x2d, mod, conv_w, conv_b, wdec, bdec, gnw, wbc, wbg, wout, s0, *, seq_len, row_len, emit_state):
    t = x2d.shape[0]
    nb = t // seq_len
    has_s0 = s0 is not None
    one = pl.Buffered(1)
    const = lambda shape: pl.BlockSpec(shape, lambda b: (0,) * len(shape), pipeline_mode=one)
    in_specs = [pl.BlockSpec((seq_len, D_PROJ), lambda b: (b, 0), pipeline_mode=one),
                pl.BlockSpec((seq_len, D_MODEL), lambda b: (b, 0)),
                const(mod.shape),
                const((3, D_CONV)), const((1, D_CONV)), const((2, LR_PAD, D_GLA)), const((2, D_GLA)),
                const((1, D_GLA)), const((D_CONV, D_MODEL)), const((D_GLA, D_MODEL)), const((D_MODEL, D_MODEL))]
    args = [proj, x2d, mod, conv_w, conv_b, wdec, bdec, gnw, wbc, wbg, wout]
    if has_s0:
        in_specs.append(pl.BlockSpec((1, 2, N_HEADS, HEAD_D, HEAD_D), lambda b: (b, 0, 0, 0, 0)))
        args.append(s0)
    out_specs = [pl.BlockSpec((seq_len, D_MODEL), lambda b: (b, 0))]
    out_shape = [jax.ShapeDtypeStruct((t, D_MODEL), F32)]
    if emit_state:
        out_specs.append(pl.BlockSpec((1, 2, N_HEADS, HEAD_D, HEAD_D), lambda b: (b, 0, 0, 0, 0)))
        out_shape.append(jax.ShapeDtypeStruct((nb, 2, N_HEADS, HEAD_D, HEAD_D), F32))
    kern = functools.partial(_mixer_kernel, seq_len=seq_len, row_len=row_len, has_s0=has_s0, emit_state=emit_state)
    return pl.pallas_call(
        kern,
        grid=(nb,),
        in_specs=in_specs,
        out_specs=out_specs,
        out_shape=out_shape,
        scratch_shapes=[pltpu.VMEM((seq_len, D_GLA), F32), pltpu.VMEM((seq_len, D_GLA), F32),
                        pltpu.VMEM((seq_len, D_GLA), F32), pltpu.VMEM((N_HEADS, HEAD_D, HEAD_D), F32)],
        compiler_params=pltpu.CompilerParams(dimension_semantics=("arbitrary",),
                                             vmem_limit_bytes=VMEM_LIMIT),
        name="mixer",
    )(*args)


I32 = jnp.int32
TB = 256
ROW_ALIGN = 16
TR = 1024
R_LOC = 3072
H2W = 1152


def _dot_nt_hi(a, b):
    return lax.dot_general(a, b, (((1,), (1,)), ((), ())), preferred_element_type=F32,
                           precision=lax.Precision.HIGHEST)


def _select_x(i, n_p_tiles, xp_ref, xs_ref):
    return jnp.where(i < n_p_tiles, xp_ref[...], xs_ref[...])


def _mod_row(i, n_p_tiles, tiles_per_mod):
    return jnp.where(i < n_p_tiles, 0, 1 + (i - n_p_tiles) // tiles_per_mod)


def _route_kernel(xp_ref, xs_ref, mod_ref, n2w_ref, wrt_ref, brb_ref, h2_ref, lpos_ref, cnt_ref, *,
                  n_p_tiles, tiles_per_mod):
    i = pl.program_id(0)
    row = _mod_row(i, n_p_tiles, tiles_per_mod)
    x = _select_x(i, n_p_tiles, xp_ref, xs_ref)
    sh = mod_ref[pl.ds(row, 1), 3 * D_MODEL:4 * D_MODEL]
    sc = mod_ref[pl.ds(row, 1), 4 * D_MODEL:5 * D_MODEL]
    h2 = _rms(x) * n2w_ref[...] * (1.0 + sc) + sh

    scores = jax.nn.sigmoid(_dot_nt_hi(wrt_ref[...], h2))
    biased = scores + brb_ref[...]
    eidx = lax.broadcasted_iota(I32, scores.shape, 0)
    picks = []
    for _k in range(TOP_K):
        m = jnp.max(biased, axis=0, keepdims=True)
        first = jnp.min(jnp.where(biased == m, eidx, N_EXPERTS), axis=0, keepdims=True)
        pick = eidx == first
        picks.append(pick)
        biased = jnp.where(pick, -jnp.inf, biased)
    sel = jnp.zeros(scores.shape, F32)
    for pick in picks:
        sel = jnp.where(pick, 1.0, sel)
    selsc = sel * scores
    comb = selsc / jnp.sum(selsc, axis=0, keepdims=True) * ROUTED_SCALE

    selb = sel.astype(BF16)
    tr_ = lax.broadcasted_iota(I32, (TB, TB), 0)
    tc_ = lax.broadcasted_iota(I32, (TB, TB), 1)
    rank = _dot(selb, jnp.where(tr_ < tc_, 1.0, 0.0).astype(BF16))
    n_b = _dot(selb, jnp.ones((TB, 128), BF16))
    m_b = jnp.floor((n_b + (ROW_ALIGN - 1)) * (1.0 / ROW_ALIGN)) * ROW_ALIGN
    er_ = lax.broadcasted_iota(I32, (N_EXPERTS, N_EXPERTS), 0)
    ec_ = lax.broadcasted_iota(I32, (N_EXPERTS, N_EXPERTS), 1)
    loff_b = _dot(jnp.where(ec_ < er_, 1.0, 0.0).astype(BF16), m_b.astype(BF16))
    lposf = jnp.concatenate([loff_b] * (TB // 128), axis=1) + rank
    rows = [jnp.sum(jnp.where(pick, lposf, 0.0), axis=0, keepdims=True) for pick in picks]
    lpos_ref[0] = jnp.concatenate(rows, axis=0).astype(I32)
    cnt_ref[0] = m_b

    combt = comb.T
    chi = combt.astype(BF16).astype(F32)
    h2_ref[:, 0:D_MODEL] = h2.astype(BF16)
    h2_ref[:, D_MODEL:H2W] = jnp.concatenate([chi, combt - chi], axis=1).astype(BF16)


def _route(x1p, x1s, mod, n2w, w_router_t, b_router_b, *, tiles_per_mod):
    n_p, n_s = x1p.shape[0] // TB, x1s.shape[0] // TB
    nt = n_p + n_s
    kern = functools.partial(_route_kernel, n_p_tiles=n_p, tiles_per_mod=tiles_per_mod)
    const = lambda shape: pl.BlockSpec(shape, lambda i: (0,) * len(shape))
    return pl.pallas_call(
        kern,
        grid=(nt,),
        in_specs=[pl.BlockSpec((TB, D_MODEL), lambda i: (jnp.minimum(i, n_p - 1), 0)),
                  pl.BlockSpec((TB, D_MODEL), lambda i: (jnp.maximum(i - n_p, 0), 0)),
                  const(mod.shape), const((1, D_MODEL)), const((N_EXPERTS, D_MODEL)), const((N_EXPERTS, TB))],
        out_specs=[pl.BlockSpec((TB, H2W), lambda i: (i, 0)),
                   pl.BlockSpec((1, TOP_K, TB), lambda i: (i, 0, 0)),
                   pl.BlockSpec((1, N_EXPERTS, 128), lambda i: (i, 0, 0))],
        out_shape=[jax.ShapeDtypeStruct((nt * TB, H2W), BF16),
                   jax.ShapeDtypeStruct((nt, TOP_K, TB), I32),
                   jax.ShapeDtypeStruct((nt, N_EXPERTS, 128), F32)],
        compiler_params=pltpu.CompilerParams(dimension_semantics=("arbitrary",), vmem_limit_bytes=VMEM_LIMIT),
        name="moe_route",
    )(x1p, x1s, mod, n2w, w_router_t, b_router_b)


def _plan_kernel(cnt_ref, off_ref, loff_ref, msz_ref, tail_ref, te_ref, *, nt, n_row_tiles):
    lane = lax.broadcasted_iota(I32, (N_EXPERTS, 128), 1)
    m = jnp.zeros((N_EXPERTS, 128), F32)
    for i in range(nt):
        m = jnp.where(lane == i, cnt_ref[i], m)
    total = jnp.broadcast_to(jnp.sum(m, axis=1, keepdims=True), (N_EXPERTS, 128))
    gsz = jnp.floor((total + (TR - 1)) * (1.0 / TR)) * TR
    er_ = lax.broadcasted_iota(I32, (N_EXPERTS, N_EXPERTS), 0)
    ec_ = lax.broadcasted_iota(I32, (N_EXPERTS, N_EXPERTS), 1)
    lstrict = jnp.where(ec_ < er_, 1.0, 0.0)
    ir_ = lax.broadcasted_iota(I32, (128, 128), 0)
    ic_ = lax.broadcasted_iota(I32, (128, 128), 1)
    ustrict = jnp.where(ir_ < ic_, 1.0, 0.0)
    gstart = _dot_hi(lstrict, gsz)
    off_ref[...] = (gstart + _dot_hi(m, ustrict)).astype(I32)
    loff_ref[...] = _dot_hi(lstrict, m).astype(I32)
    msz_ref[...] = m.astype(I32)
    tail_ref[...] = jnp.where(lane == 0, gstart + total, jnp.where(lane == 1, gsz - total, 0.0)).astype(I32)
    gend = gstart + gsz
    te_lanes = te_ref.shape[1]
    gend_w = jnp.concatenate([gend] * (te_lanes // 128), axis=1)
    tile_start = lax.broadcasted_iota(I32, (N_EXPERTS, te_lanes), 1).astype(F32) * TR
    te = jnp.sum(jnp.where(gend_w <= tile_start, 1.0, 0.0), axis=0, keepdims=True)
    used = gend[N_EXPERTS - 1:N_EXPERTS, 0:1] * (1.0 / TR)
    te_lane = lax.broadcasted_iota(I32, (1, te_lanes), 1)
    te = jnp.where(te_lane == n_row_tiles, used, jnp.minimum(te, N_EXPERTS - 1.0))
    te_ref[...] = jnp.broadcast_to(te, te_ref.shape).astype(I32)


def _plan(cnt, n_row_tiles):
    nt = cnt.shape[0]
    assert nt <= 128
    te_lanes = -(-(n_row_tiles + 1) // 128) * 128
    tab = jax.ShapeDtypeStruct((N_EXPERTS, 128), I32)
    return pl.pallas_call(
        functools.partial(_plan_kernel, nt=nt, n_row_tiles=n_row_tiles),
        out_shape=[tab, tab, tab, tab, jax.ShapeDtypeStruct((8, te_lanes), I32)],
        compiler_params=pltpu.CompilerParams(vmem_limit_bytes=VMEM_LIMIT),
        name="moe_plan",
    )(cnt)


def _start_copies(msz_ref, tile, make_copy):
    def body(e, carry):
        m = msz_ref[e, tile]

        @pl.when(m > 0)
        def _():
            make_copy(e, pl.multiple_of(m, ROW_ALIGN)).start()
        return carry
    lax.fori_loop(0, N_EXPERTS, body, 0, unroll=8)


def _tile_rows(loff_ref, msz_ref, tile):
    return pl.multiple_of(loff_ref[N_EXPERTS - 1, tile] + msz_ref[N_EXPERTS - 1, tile], ROW_ALIGN)


def _dispatch_kernel(off_ref, loff_ref, msz_ref, tail_ref, h2_ref, lpos_ref, xs_hbm, xloc_ref, zero_ref, sem,
                     tail_sem, *, nt):
    i = pl.program_id(0)
    slot = i % 2

    def copy_for(tile, slot_):
        def make(e, m):
            lo = pl.multiple_of(loff_ref[e, tile], ROW_ALIGN)
            of = pl.multiple_of(off_ref[e, tile], ROW_ALIGN)
            return pltpu.make_async_copy(xloc_ref.at[slot_, pl.ds(lo, m)], xs_hbm.at[pl.ds(of, m)], sem.at[slot_])
        return make

    def wait_tile(tile, slot_):
        n = _tile_rows(loff_ref, msz_ref, tile)
        pltpu.make_async_copy(xloc_ref.at[slot_, pl.ds(0, n)], xs_hbm.at[pl.ds(0, n)], sem.at[slot_]).wait()

    used_rows = _tile_rows(loff_ref, msz_ref, i)
    lpos = lpos_ref[0].astype(jnp.int16)
    h2 = h2_ref[...]
    ck = 1024
    one, zero = jnp.ones((ck, TB), BF16), jnp.zeros((ck, TB), BF16)
    for c in range(R_LOC // ck):
        @pl.when(c * ck < used_rows)
        def _(c=c):
            r = (lax.broadcasted_iota(I32, (ck, TB), 0) + c * ck).astype(jnp.int16)
            d = zero
            for k in range(TOP_K):
                d = jnp.where(r == lpos[k:k + 1, :], one, d)
            res = _dot(d, h2)
            xloc_ref[slot, c * ck:(c + 1) * ck, :] = res.astype(BF16)

    _start_copies(msz_ref, i, copy_for(i, slot))

    @pl.when(i > 0)
    def _():
        wait_tile(i - 1, 1 - slot)

    @pl.when(i == nt - 1)
    def _():
        zero_ref[...] = jnp.zeros(zero_ref.shape, BF16)

        def tail_copies(start):
            def body(e, carry):
                n = tail_ref[e, 1]

                @pl.when(n > 0)
                def _():
                    st = pl.multiple_of(tail_ref[e, 0], ROW_ALIGN)
                    nn = pl.multiple_of(n, ROW_ALIGN)
                    cp = pltpu.make_async_copy(zero_ref.at[pl.ds(0, nn)], xs_hbm.at[pl.ds(st, nn)], tail_sem)
                    if start:
                        cp.start()
                    else:
                        cp.wait()
                return carry
            lax.fori_loop(0, N_EXPERTS, body, 0)
        tail_copies(True)
        wait_tile(i, slot)
        tail_copies(False)


def _dispatch(off, loff, msz, tail, h2ext, lpos, n_rows):
    nt = lpos.shape[0]
    grid_spec = pltpu.PrefetchScalarGridSpec(
        num_scalar_prefetch=4,
        grid=(nt,),
        in_specs=[pl.BlockSpec((TB, H2W), lambda i, *_: (i, 0)),
                  pl.BlockSpec((1, TOP_K, TB), lambda i, *_: (i, 0, 0))],
        out_specs=pl.BlockSpec(memory_space=pl.ANY),
        scratch_shapes=[pltpu.VMEM((2, R_LOC, H2W), BF16), pltpu.VMEM((TR, H2W), BF16),
                        pltpu.SemaphoreType.DMA((2,)), pltpu.SemaphoreType.DMA],
    )
    return pl.pallas_call(
        functools.partial(_dispatch_kernel, nt=nt),
        grid_spec=grid_spec,
        out_shape=jax.ShapeDtypeStruct((n_rows, H2W), BF16),
        compiler_params=pltpu.CompilerParams(dimension_semantics=("arbitrary",), vmem_limit_bytes=VMEM_LIMIT),
        name="moe_dispatch",
    )(off, loff, msz, tail, h2ext, lpos)


def _expert_kernel(te_ref, xs_ref, wg_ref, wu_ref, wd_ref, ys_ref, wgu_ref, wdb_ref, *, n_row_tiles):
    j = pl.program_id(0)
    e = te_ref[j]

    @pl.when(j < te_ref[n_row_tiles])
    def _():
        @pl.when((j == 0) | (e != te_ref[jnp.maximum(j - 1, 0)]))
        def _():
            wgu_ref[:, :D_EXPERT] = wg_ref[0].astype(BF16)
            wgu_ref[:, D_EXPERT:] = wu_ref[0].astype(BF16)
            wdb_ref[...] = wd_ref[0].astype(BF16)

        x = xs_ref[:, 0:D_MODEL]
        ext = xs_ref[:, D_MODEL:H2W].astype(F32)
        wts = ext[:, :N_EXPERTS] + ext[:, N_EXPERTS:]
        lane = lax.broadcasted_iota(I32, wts.shape, 1)
        w = jnp.sum(jnp.where(lane == e, wts, 0.0), axis=-1, keepdims=True)
        h = _dot(x, wgu_ref[...])
        hg, hu = h[:, :D_EXPERT], h[:, D_EXPERT:]
        act = hg * _sigmoid(hg) * hu * w
        ys_ref[...] = _dot(act.astype(BF16), wdb_ref[...]).astype(BF16)


def _experts(te, xs, wg, wu, wd, n_row_tiles):
    def row_tile(j, te_ref):
        return jnp.maximum(jnp.minimum(j, te_ref[n_row_tiles] - 1), 0)
    grid_spec = pltpu.PrefetchScalarGridSpec(
        num_scalar_prefetch=1,
        grid=(n_row_tiles,),
        in_specs=[pl.BlockSpec((TR, H2W), lambda j, te_ref: (row_tile(j, te_ref), 0)),
                  pl.BlockSpec((1, D_MODEL, D_EXPERT), lambda j, te_ref: (te_ref[row_tile(j, te_ref)], 0, 0)),
                  pl.BlockSpec((1, D_MODEL, D_EXPERT), lambda j, te_ref: (te_ref[row_tile(j, te_ref)], 0, 0)),
                  pl.BlockSpec((1, D_EXPERT, D_MODEL), lambda j, te_ref: (te_ref[row_tile(j, te_ref)], 0, 0))],
        out_specs=pl.BlockSpec((TR, D_MODEL), lambda j, te_ref: (row_tile(j, te_ref), 0)),
        scratch_shapes=[pltpu.VMEM((D_MODEL, 2 * D_EXPERT), BF16), pltpu.VMEM((D_EXPERT, D_MODEL), BF16)],
    )
    return pl.pallas_call(
        functools.partial(_expert_kernel, n_row_tiles=n_row_tiles),
        grid_spec=grid_spec,
        out_shape=jax.ShapeDtypeStruct((n_row_tiles * TR, D_MODEL), BF16),
        compiler_params=pltpu.CompilerParams(dimension_semantics=("arbitrary",), vmem_limit_bytes=VMEM_LIMIT),
        name="moe_experts",
    )(te, xs, wg, wu, wd)


def _combine_kernel(off_ref, loff_ref, msz_ref, lpos_ref, h2_ref, xp_ref, xs_ref, mod_ref, wgs_ref, wus_ref, wds_ref,
                    fnw_ref, ysrt_hbm, yp_ref, ys_ref, yloc_ref, acc_ref, sem, *, nt, n_p_tiles, tiles_per_mod,
                    final):
    i = pl.program_id(0)
    slot = i % 2

    def copy_for(tile, slot_):
        def make(e, m):
            lo = pl.multiple_of(loff_ref[e, tile], ROW_ALIGN)
            of = pl.multiple_of(off_ref[e, tile], ROW_ALIGN)
            return pltpu.make_async_copy(ysrt_hbm.at[pl.ds(of, m)], yloc_ref.at[slot_, pl.ds(lo, m)], sem.at[slot_])
        return make

    @pl.when(i == 0)
    def _():
        yloc_ref[...] = jnp.zeros(yloc_ref.shape, BF16)
        _start_copies(msz_ref, 0, copy_for(0, 0))

    @pl.when(i + 1 < nt)
    def _():
        _start_copies(msz_ref, i + 1, copy_for(i + 1, 1 - slot))

    hb = h2_ref[...]
    hg = _dot(hb, wgs_ref[...].astype(BF16))
    hu = _dot(hb, wus_ref[...].astype(BF16))
    acc_ref[...] = _dot((hg * _sigmoid(hg) * hu).astype(BF16), wds_ref[...].astype(BF16))

    used_rows = _tile_rows(loff_ref, msz_ref, i)
    pltpu.make_async_copy(ysrt_hbm.at[pl.ds(0, used_rows)], yloc_ref.at[slot, pl.ds(0, used_rows)],
                          sem.at[slot]).wait()

    lpos_pad = jnp.concatenate([lpos_ref[0].astype(F32), jnp.zeros((128 - TOP_K, TB), F32)], axis=0)
    lposc = lpos_pad.T.astype(I32)
    ck = 512
    cols = [jnp.broadcast_to(lposc[:, k:k + 1], (TB, ck)).astype(jnp.int16) for k in range(TOP_K)]
    one, zero = jnp.ones((TB, ck), BF16), jnp.zeros((TB, ck), BF16)
    for c in range(R_LOC // ck):
        @pl.when(c * ck < used_rows)
        def _(c=c):
            r = (lax.broadcasted_iota(I32, (TB, ck), 1) + c * ck).astype(jnp.int16)
            cm = zero
            for k in range(TOP_K):
                cm = jnp.where(r == cols[k], one, cm)
            acc_ref[...] += _dot(cm, yloc_ref[slot, c * ck:(c + 1) * ck, :])

    row = _mod_row(i, n_p_tiles, tiles_per_mod)
    g2 = mod_ref[pl.ds(row, 1), 5 * D_MODEL:6 * D_MODEL]
    x2 = _select_x(i, n_p_tiles, xp_ref, xs_ref) + g2 * acc_ref[...]
    y = _rms(x2) * fnw_ref[...] if final else x2

    @pl.when(i < n_p_tiles)
    def _():
        yp_ref[...] = y

    @pl.when(i >= n_p_tiles)
    def _():
        ys_ref[...] = y


def _combine(off, loff, msz, lpos, h2ext, x1p, x1s, mod, wgs, wus, wds, fnw, ysorted, *, tiles_per_mod, final):
    n_p, n_s = x1p.shape[0] // TB, x1s.shape[0] // TB
    nt = n_p + n_s
    const = lambda shape: pl.BlockSpec(shape, lambda i, *_: (0,) * len(shape), pipeline_mode=pl.Buffered(1))
    p_idx = lambda i, *_: (jnp.minimum(i, n_p - 1), 0)
    s_idx = lambda i, *_: (jnp.maximum(i - n_p, 0), 0)
    grid_spec = pltpu.PrefetchScalarGridSpec(
        num_scalar_prefetch=3,
        grid=(nt,),
        in_specs=[pl.BlockSpec((1, TOP_K, TB), lambda i, *_: (i, 0, 0)),
                  pl.BlockSpec((TB, D_MODEL), lambda i, *_: (i, 0)),
                  pl.BlockSpec((TB, D_MODEL), p_idx), pl.BlockSpec((TB, D_MODEL), s_idx),
                  const(mod.shape), const((D_MODEL, D_EXPERT)), const((D_MODEL, D_EXPERT)), const((D_EXPERT, D_MODEL)),
                  const((1, D_MODEL)), pl.BlockSpec(memory_space=pl.ANY)],
        out_specs=[pl.BlockSpec((TB, D_MODEL), p_idx), pl.BlockSpec((TB, D_MODEL), s_idx)],
        scratch_shapes=[pltpu.VMEM((2, R_LOC, D_MODEL), BF16), pltpu.VMEM((TB, D_MODEL), F32),
                        pltpu.SemaphoreType.DMA((2,))],
    )
    kern = functools.partial(_combine_kernel, nt=nt, n_p_tiles=n_p, tiles_per_mod=tiles_per_mod, final=final)
    return pl.pallas_call(
        kern,
        grid_spec=grid_spec,
        out_shape=[jax.ShapeDtypeStruct(x1p.shape, F32), jax.ShapeDtypeStruct(x1s.shape, F32)],
        compiler_params=pltpu.CompilerParams(dimension_semantics=("arbitrary",), vmem_limit_bytes=VMEM_LIMIT),
        name="moe_combine",
    )(off, loff, msz, lpos, h2ext, x1p, x1s, mod, wgs, wus, wds, fnw, ysorted)


def _moe(x1p, x1s, mod, n2w, w_router, b_router, wg, wu, wd, wgs, wus, wds, fnw, *, tokens_per_mod, final):
    assert R_LOC >= TB * TOP_K + N_EXPERTS * (ROW_ALIGN - 1) and tokens_per_mod % TB == 0
    nt = (x1p.shape[0] + x1s.shape[0]) // TB
    n_rows_max = nt * TB * TOP_K + nt * N_EXPERTS * (ROW_ALIGN - 1) + N_EXPERTS * (TR - ROW_ALIGN)
    n_row_tiles = -(-n_rows_max // TR)
    tiles_per_mod = tokens_per_mod // TB
    brb = jnp.broadcast_to(b_router.reshape(N_EXPERTS, 1), (N_EXPERTS, TB))
    h2ext, lpos, cnt = _route(x1p, x1s, mod, n2w, w_router.T, brb, tiles_per_mod=tiles_per_mod)
    off, loff, msz, tail, te = _plan(cnt, n_row_tiles)
    xs = _dispatch(off, loff, msz, tail, h2ext, lpos, n_row_tiles * TR)
    ysorted = _experts(te[0], xs, wg, wu, wd, n_row_tiles)
    return _combine(off, loff, msz, lpos, h2ext, x1p, x1s, mod, wgs, wus, wds, fnw, ysorted,
                    tiles_per_mod=tiles_per_mod, final=final)


def kernel(x_prompt, x_sample, state_gla, c, c_ctx, w_mod, b_mod, norm1_w, w_in, conv_w, conv_b, w_decay, b_decay,
           gla_norm_w, w_br_conv, w_br_gla, w_out, norm2_w, w_router, b_router, w_gate_e, w_up_e, w_down_e,
           w_gate_s, w_up_s, w_down_s, final_norm_w):
    depth = w_mod.shape[0]
    nb_p, len_p, _ = x_prompt.shape
    nb_s, len_s, _ = x_sample.shape
    yp = x_prompt.reshape(nb_p * len_p, D_MODEL)
    ys = x_sample.reshape(nb_s * len_s, D_MODEL)
    fnw = final_norm_w.reshape(1, D_MODEL)

    cond = jnp.concatenate([c_ctx[None, :], c, jnp.zeros((8 - 1 - nb_s, D_MODEL), F32)], axis=0)
    states = []
    for l in range(depth):
        mod = _modulation(cond, w_mod[l], b_mod[l].reshape(1, -1))
        mod_p, mod_s = mod[0:1], mod[1:1 + nb_s]

        wl = w_in[l]
        w_in_r = jnp.concatenate([wl[:, :3584], wl[:, 3616:5664], wl[:, 3584:3616],
                                  jnp.zeros((D_MODEL, D_PROJ - 5664), F32)], axis=1).astype(BF16)
        wdec = jnp.zeros((2, LR_PAD, D_GLA), F32)
        wdec = wdec.at[0, 0:GLA_RANK].set(w_decay[l, 0]).at[1, GLA_RANK:2 * GLA_RANK].set(w_decay[l, 1])
        n1w = norm1_w[l].reshape(1, D_MODEL)
        mix_w = (conv_w[l], conv_b[l].reshape(1, D_CONV), wdec, b_decay[l], gla_norm_w[l].reshape(1, D_GLA),
                 w_br_conv[l].astype(BF16), w_br_gla[l].astype(BF16), w_out[l].astype(BF16))
        moe_w = (norm2_w[l].reshape(1, D_MODEL), w_router[l], b_router[l],
                 w_gate_e[l], w_up_e[l], w_down_e[l], w_gate_s[l], w_up_s[l], w_down_s[l])

        proj_p = _in_proj(yp, mod_p, n1w, w_in_r, rows_per_mod=nb_p * len_p)
        yp, st = _mixer(proj_p, yp, mod_p, *mix_w, None, seq_len=len_p, row_len=len_p, emit_state=True)
        states.append(st)
        proj_s = _in_proj(ys, mod_s, n1w, w_in_r, rows_per_mod=len_s)
        (ys,) = _mixer(proj_s, ys, mod_s, *mix_w, state_gla[:, l], seq_len=len_s, row_len=GRID_W, emit_state=False)

        yp, ys = _moe(yp, ys, mod, *moe_w, fnw, tokens_per_mod=len_s, final=l == depth - 1)
    new_state = jnp.stack(states, axis=1)
    return (yp.reshape(nb_p, len_p, D_MODEL), ys.reshape(nb_s, len_s, D_MODEL), new_state)
```

```python
import functools

import jax
import jax.numpy as jnp
from jax import lax
from jax.experimental import pallas as pl
from jax.experimental.pallas import tpu as pltpu

F32 = jnp.float32
BF16 = jnp.bfloat16

D_MODEL = 1024
GRID_W = 64
D_CONV = 512
N_HEADS = 4
HEAD_D = 128
D_GLA = N_HEADS * HEAD_D
GLA_RANK = 16
GLA_GATE_NORM = 16.0
LOG2_E = 1.4426950408889634
CHUNK = 64
SUB = 8
N_SUB = CHUNK // SUB
N_EXPERTS = 64
TOP_K = 8
D_EXPERT = 256
ROUTED_SCALE = 2.5
EPS = 1e-6

C_U, C_GB, C_GC, C_Q, C_K, C_V, C_GO = 0, 512, 1024, 1536, 2048, 2560, 3072
C_BRC, C_BRG, C_LR = 3584, 4608, 5632
D_PROJ = 5760
LR_PAD = 128

VMEM_LIMIT = 56 * 1024 * 1024


def _dot(a, b):
    return jnp.dot(a, b, preferred_element_type=F32)


def _dot_nt(a, b):
    return lax.dot_general(a, b, (((1,), (1,)), ((), ())), preferred_element_type=F32)


def _dot_tn(a, b):
    return lax.dot_general(a, b, (((0,), (0,)), ((), ())), preferred_element_type=F32)


def _dot_hi(a, b):
    return jnp.dot(a, b, preferred_element_type=F32, precision=lax.Precision.HIGHEST)


def _split_bf16(x):
    hi = x.astype(BF16)
    lo = (x - hi.astype(F32)).astype(BF16)
    return hi, lo


def _rms(x):
    return x * lax.rsqrt(jnp.mean(x * x, axis=-1, keepdims=True) + EPS)


def _sigmoid(x):
    return 0.5 * jnp.tanh(0.5 * x) + 0.5


def _mod_kernel(cond_ref, w_ref, b_ref, o_ref):
    c = cond_ref[...]
    o_ref[...] = _dot_hi(c * jax.nn.sigmoid(c), w_ref[...]) + b_ref[...]


def _modulation(cond, w_mod, b_mod):
    n_rows = cond.shape[0]
    tn = 1536
    return pl.pallas_call(
        _mod_kernel,
        grid=(6 * D_MODEL // tn,),
        in_specs=[pl.BlockSpec((n_rows, D_MODEL), lambda j: (0, 0)),
                  pl.BlockSpec((D_MODEL, tn), lambda j: (0, j)),
                  pl.BlockSpec((1, tn), lambda j: (0, j))],
        out_specs=pl.BlockSpec((n_rows, tn), lambda j: (0, j)),
        out_shape=jax.ShapeDtypeStruct((n_rows, 6 * D_MODEL), F32),
        compiler_params=pltpu.CompilerParams(dimension_semantics=("arbitrary",),
                                             vmem_limit_bytes=VMEM_LIMIT),
        name="modulation",
    )(cond, w_mod, b_mod)


D_IN_PROJ = 5664
C_LR_SRC, C_GATES_SRC = 3584, 3616


def _w_in_prep_kernel(w_ref, o_ref):
    rc = 64
    n_gate = 2 * D_MODEL

    def body(t, carry):
        r = pl.ds(pl.multiple_of(t * rc, rc), rc)
        o_ref[r, 0:C_BRC] = w_ref[r, 0:C_LR_SRC].astype(BF16)
        o_ref[r, C_BRC:C_LR] = w_ref[r, C_GATES_SRC:C_GATES_SRC + n_gate].astype(BF16)
        lr = jnp.concatenate([w_ref[r, C_LR_SRC:C_GATES_SRC], jnp.zeros((rc, LR_PAD - 2 * GLA_RANK), F32)], axis=1)
        o_ref[r, C_LR:D_PROJ] = lr.astype(BF16)
        return carry
    lax.fori_loop(0, o_ref.shape[0] // rc, body, 0)


def _w_in_prep(w_in):
    tr = 256
    return pl.pallas_call(
        _w_in_prep_kernel,
        grid=(D_MODEL // tr,),
        in_specs=[pl.BlockSpec((tr, D_IN_PROJ), lambda i: (i, 0))],
        out_specs=pl.BlockSpec((tr, D_PROJ), lambda i: (i, 0)),
        out_shape=jax.ShapeDtypeStruct((D_MODEL, D_PROJ), BF16),
        compiler_params=pltpu.CompilerParams(dimension_semantics=("arbitrary",), vmem_limit_bytes=VMEM_LIMIT),
        name="w_in_prep",
    )(w_in)


def _inproj_kernel(x_ref, mod_ref, nw_ref, w_ref, o_ref, *, rows_per_mod, tm):
    i = pl.program_id(1)
    row = (i * tm) // rows_per_mod
    sh = mod_ref[pl.ds(row, 1), 0:D_MODEL]
    sc = mod_ref[pl.ds(row, 1), D_MODEL:2 * D_MODEL]
    h = _rms(x_ref[...]) * nw_ref[...] * (1.0 + sc) + sh
    o_ref[...] = _dot(h.astype(BF16), w_ref[...]).astype(BF16)


def _in_proj(x2d, mod, norm_w, w_in_r, rows_per_mod):
    t = x2d.shape[0]
    tm, tn = 1024, 1920
    kern = functools.partial(_inproj_kernel, rows_per_mod=rows_per_mod, tm=tm)
    return pl.pallas_call(
        kern,
        grid=(D_PROJ // tn, t // tm),
        in_specs=[pl.BlockSpec((tm, D_MODEL), lambda j, i: (i, 0)),
                  pl.BlockSpec(mod.shape, lambda j, i: (0, 0)),
                  pl.BlockSpec((1, D_MODEL), lambda j, i: (0, 0)),
                  pl.BlockSpec((D_MODEL, tn), lambda j, i: (0, j))],
        out_specs=pl.BlockSpec((tm, tn), lambda j, i: (i, j)),
        out_shape=jax.ShapeDtypeStruct((t, D_PROJ), BF16),
        compiler_params=pltpu.CompilerParams(dimension_semantics=("arbitrary", "arbitrary"),
                                             vmem_limit_bytes=VMEM_LIMIT),
        name="in_proj",
    )(x2d, mod, norm_w, w_in_r)


def _log2_sigmoid(x):
    return jnp.minimum(x, 0.0) * LOG2_E - jnp.log2(1.0 + jnp.exp2(jnp.abs(x) * (-LOG2_E)))


def _gla_chunk_head(qc, kc, vc, bc, st, rev):
    lane = lax.broadcasted_iota(jnp.int32, (SUB, CHUNK), 1)
    sub = lax.broadcasted_iota(jnp.int32, (SUB, CHUNK), 0)
    tot = bc[0:1] if rev else bc[CHUNK - 1:CHUNK]

    o = _dot_nt((qc * jnp.exp2(bc)).astype(BF16), st.astype(BF16))
    k_tail = kc * jnp.exp2(tot - bc)
    st_new = st * jnp.exp2(tot) + _dot_tn(vc.astype(BF16), k_tail.astype(BF16))

    lhs_segs, rhs_segs = [], []

    def rows(before, mid, after):
        parts = ([jnp.zeros((before, HEAD_D), F32)] if before else []) + [mid]
        parts += [jnp.zeros((after, HEAD_D), F32)] if after else []
        return jnp.concatenate(parts, axis=0) if len(parts) > 1 else mid

    key_blocks = range(1, N_SUB) if rev else range(0, N_SUB - 1)
    for jb in key_blocks:
        r0 = jb * SUB
        ref_row = bc[r0:r0 + 1] if rev else bc[r0 + SUB - 1:r0 + SUB]
        ke = kc[r0:r0 + SUB] * jnp.exp2(ref_row - bc[r0:r0 + SUB])
        rhs_segs.append(rows(r0, ke, CHUNK - r0 - SUB))
        if rev:
            ql = qc[:r0] * jnp.exp2(bc[:r0] - ref_row)
            lhs_segs.append(rows(0, ql, CHUNK - r0))
        else:
            ql = qc[r0 + SUB:] * jnp.exp2(bc[r0 + SUB:] - ref_row)
            lhs_segs.append(rows(r0 + SUB, ql, 0))
    far = _dot_nt(jnp.concatenate(lhs_segs, axis=1).astype(BF16),
                  jnp.concatenate(rhs_segs, axis=1).astype(BF16))

    blocks = []
    for ib in range(N_SUB):
        r0 = ib * SUB
        qi, bi = qc[r0:r0 + SUB], bc[r0:r0 + SUB]
        acc = jnp.zeros((SUB, CHUNK), F32)
        for jj in range(SUB):
            j = r0 + jj
            e = jnp.exp2(bi - bc[j:j + 1])
            col = jnp.sum(qi * (kc[j:j + 1] * e), axis=-1, keepdims=True)
            acc = jnp.where(lane == j, col, acc)
        keep = (lane - r0 >= sub) if rev else (lane - r0 <= sub)
        blocks.append(jnp.where(keep, acc, 0.0))
    scores = far + jnp.concatenate(blocks, axis=0)
    o = o + _dot(scores.astype(BF16), vc.astype(BF16))
    return o, st_new


def _mixer_kernel(*refs, seq_len, row_len, has_s0, emit_state):
    it = iter(refs)
    proj_ref, x_ref, mod_ref, cw_ref, cb_ref, wdec_ref, bdec_ref, gnw_ref = (next(it) for _ in range(8))
    wbc_ref, wbg_ref, wout_ref = (next(it) for _ in range(3))
    s0_ref = next(it) if has_s0 else None
    out_ref = next(it)
    st_out_ref = next(it) if emit_state else None
    la_f_ref, la_b_ref, o_ref, st_ref = (next(it) for _ in range(4))

    L = seq_len
    n_chunks = L // CHUNK
    tr = 256
    assert tr % row_len == 0 and L % tr == 0

    ri = lax.broadcasted_iota(jnp.int32, (tr, tr), 0)
    ci = lax.broadcasted_iota(jnp.int32, (tr, tr), 1)
    same_chunk = (ri // CHUNK) == (ci // CHUNK)

    def decay_body(t, carry):
        r0 = pl.multiple_of(t * tr, tr)
        lr = proj_ref[pl.ds(r0, tr), C_LR:C_LR + LR_PAD]
        for d, ref in ((0, la_f_ref), (1, la_b_ref)):
            whi, wlo = _split_bf16(wdec_ref[d])
            z = _dot(lr, whi) + _dot(lr, wlo) + bdec_ref[d:d + 1]
            la_hi, la_lo = _split_bf16(_log2_sigmoid(z) * (1.0 / GLA_GATE_NORM))
            tri = jnp.where(same_chunk & ((ci >= ri) if d else (ci <= ri)), 1.0, 0.0).astype(BF16)
            ref[pl.ds(r0, tr), :] = _dot(tri, la_hi) + _dot(tri, la_lo)
        return carry
    lax.fori_loop(0, L // tr, decay_body, 0)

    for rev, la_ref in ((False, la_f_ref), (True, la_b_ref)):
        d = 1 if rev else 0
        for h in range(N_HEADS):
            if has_s0:
                st_ref[h] = s0_ref[0, d, h].T
            else:
                st_ref[h] = jnp.zeros((HEAD_D, HEAD_D), F32)

        def chunk_body(c, carry, rev=rev, la_ref=la_ref):
            cc = (n_chunks - 1 - c) if rev else c
            r0 = pl.multiple_of(cc * CHUNK, CHUNK)
            for h in range(N_HEADS):
                lo, hi = h * HEAD_D, (h + 1) * HEAD_D
                qc = proj_ref[pl.ds(r0, CHUNK), C_Q + lo:C_Q + hi].astype(F32) * (HEAD_D ** -0.5)
                kc = proj_ref[pl.ds(r0, CHUNK), C_K + lo:C_K + hi].astype(F32)
                vc = proj_ref[pl.ds(r0, CHUNK), C_V + lo:C_V + hi].astype(F32)
                oc, st_new = _gla_chunk_head(qc, kc, vc, la_ref[pl.ds(r0, CHUNK), lo:hi], st_ref[h], rev)
                st_ref[h] = st_new
                if rev:
                    o_ref[pl.ds(r0, CHUNK), lo:hi] += oc
                else:
                    o_ref[pl.ds(r0, CHUNK), lo:hi] = oc
            return carry
        lax.fori_loop(0, n_chunks, chunk_body, 0, unroll=2)

        if emit_state:
            for h in range(N_HEADS):
                st_out_ref[0, d, h] = st_ref[h].T

    mod_row = pl.program_id(0) if mod_ref.shape[0] > 1 else 0
    g1 = mod_ref[pl.ds(mod_row, 1), 2 * D_MODEL:3 * D_MODEL]
    hsel_r = lax.broadcasted_iota(jnp.int32, (D_GLA, D_GLA), 0) // HEAD_D
    hsel_c = lax.broadcasted_iota(jnp.int32, (D_GLA, D_GLA), 1) // HEAD_D
    head_avg = jnp.where(hsel_r == hsel_c, 1.0 / HEAD_D, 0.0).astype(BF16)
    pos = lax.broadcasted_iota(jnp.int32, (tr, 1), 0)

    def dense_body(t, carry):
        r0 = pl.multiple_of(t * tr, tr)
        rows = pl.ds(r0, tr)
        cu = (proj_ref[rows, C_GC:C_GC + D_CONV].astype(F32) * proj_ref[rows, C_U:C_U + D_CONV].astype(F32))
        in_row = pos % row_len
        left = jnp.where(in_row == 0, 0.0, pltpu.roll(cu, 1, axis=0))
        right = jnp.where(in_row == row_len - 1, 0.0, pltpu.roll(cu, tr - 1, axis=0))
        conv = cw_ref[0:1] * left + cw_ref[1:2] * cu + cw_ref[2:3] * right + cb_ref[...]
        y_conv = proj_ref[rows, C_GB:C_GB + D_CONV].astype(F32) * conv

        o = o_ref[rows, :]
        osq_hi, osq_lo = _split_bf16(o * o)
        ms = _dot(osq_hi, head_avg) + _dot(osq_lo, head_avg)
        g_out = proj_ref[rows, C_GO:C_GO + D_GLA].astype(F32)
        y_gla = o * lax.rsqrt(ms + EPS) * gnw_ref[...] * (g_out * _sigmoid(g_out))

        merged = (_sigmoid(proj_ref[rows, C_BRC:C_BRC + D_MODEL].astype(F32)) * _dot(y_conv.astype(BF16), wbc_ref[...])
                  + _sigmoid(proj_ref[rows, C_BRG:C_BRG + D_MODEL].astype(F32)) * _dot(y_gla.astype(BF16), wbg_ref[...]))
        out_ref[rows, :] = x_ref[rows, :] + g1 * _dot(merged.astype(BF16), wout_ref[...])
        return carry
    lax.fori_loop(0, L // tr, dense_body, 0)


def _mixer(proj, x2d, mod, conv_w, conv_b, wdec, bdec, gnw, wbc, wbg, wout, s0, *, seq_len, row_len, emit_state):
    t = x2d.shape[0]
    nb = t // seq_len
    has_s0 = s0 is not None
    one = pl.Buffered(1)
    const = lambda shape: pl.BlockSpec(shape, lambda b: (0,) * len(shape), pipeline_mode=one)
    in_specs = [pl.BlockSpec((seq_len, D_PROJ), lambda b: (b, 0), pipeline_mode=one),
                pl.BlockSpec((seq_len, D_MODEL), lambda b: (b, 0)),
                const(mod.shape),
                const((3, D_CONV)), const((1, D_CONV)), const((2, LR_PAD, D_GLA)), const((2, D_GLA)),
                const((1, D_GLA)), const((D_CONV, D_MODEL)), const((D_GLA, D_MODEL)), const((D_MODEL, D_MODEL))]
    args = [proj, x2d, mod, conv_w, conv_b, wdec, bdec, gnw, wbc, wbg, wout]
    if has_s0:
        in_specs.append(pl.BlockSpec((1, 2, N_HEADS, HEAD_D, HEAD_D), lambda b: (b, 0, 0, 0, 0)))
        args.append(s0)
    out_specs = [pl.BlockSpec((seq_len, D_MODEL), lambda b: (b, 0))]
    out_shape = [jax.ShapeDtypeStruct((t, D_MODEL), F32)]
    if emit_state:
        out_specs.append(pl.BlockSpec((1, 2, N_HEADS, HEAD_D, HEAD_D), lambda b: (b, 0, 0, 0, 0)))
        out_shape.append(jax.ShapeDtypeStruct((nb, 2, N_HEADS, HEAD_D, HEAD_D), F32))
    kern = functools.partial(_mixer_kernel, seq_len=seq_len, row_len=row_len, has_s0=has_s0, emit_state=emit_state)
    return pl.pallas_call(
        kern,
        grid=(nb,),
        in_specs=in_specs,
        out_specs=out_specs,
        out_shape=out_shape,
        scratch_shapes=[pltpu.VMEM((seq_len, D_GLA), F32), pltpu.VMEM((seq_len, D_GLA), F32),
                        pltpu.VMEM((seq_len, D_GLA), F32), pltpu.VMEM((N_HEADS, HEAD_D, HEAD_D), F32)],
        compiler_params=pltpu.CompilerParams(dimension_semantics=("arbitrary",),
                                             vmem_limit_bytes=VMEM_LIMIT),
        name="mixer",
    )(*args)


I32 = jnp.int32
TB = 256
ROW_ALIGN = 16
TR = 1024
R_LOC = 3072
H2W = 1152


def _select_x(i, n_p_tiles, xp_ref, xs_ref):
    return jnp.where(i < n_p_tiles, xp_ref[...], xs_ref[...])


def _mod_row(i, n_p_tiles, tiles_per_mod):
    return jnp.where(i < n_p_tiles, 0, 1 + (i - n_p_tiles) // tiles_per_mod)


def _route_kernel(xp_ref, xs_ref, mod_ref, n2w_ref, wrt_ref, brb_ref, h2_ref, lpos_ref, cnt_ref, *,
                  n_p_tiles, tiles_per_mod):
    i = pl.program_id(0)
    row = _mod_row(i, n_p_tiles, tiles_per_mod)
    x = _select_x(i, n_p_tiles, xp_ref, xs_ref)
    sh = mod_ref[pl.ds(row, 1), 3 * D_MODEL:4 * D_MODEL]
    sc = mod_ref[pl.ds(row, 1), 4 * D_MODEL:5 * D_MODEL]
    h2 = _rms(x) * n2w_ref[...] * (1.0 + sc) + sh

    h_hi, h_lo = _split_bf16(h2)
    w_hi, w_lo = _split_bf16(wrt_ref[...])
    scores = jax.nn.sigmoid(_dot_nt(w_hi, h_hi) + (_dot_nt(w_hi, h_lo) + _dot_nt(w_lo, h_hi)))
    biased = scores + brb_ref[...]
    eidx = lax.broadcasted_iota(I32, scores.shape, 0)
    picks = []
    for _k in range(TOP_K):
        m = jnp.max(biased, axis=0, keepdims=True)
        first = jnp.min(jnp.where(biased == m, eidx, N_EXPERTS), axis=0, keepdims=True)
        pick = eidx == first
        picks.append(pick)
        biased = jnp.where(pick, -jnp.inf, biased)
    sel = jnp.zeros(scores.shape, F32)
    for pick in picks:
        sel = jnp.where(pick, 1.0, sel)
    selsc = sel * scores
    comb = selsc / jnp.sum(selsc, axis=0, keepdims=True) * ROUTED_SCALE

    selb = sel.astype(BF16)
    tr_ = lax.broadcasted_iota(I32, (TB, TB), 0)
    tc_ = lax.broadcasted_iota(I32, (TB, TB), 1)
    rank = _dot(selb, jnp.where(tr_ < tc_, 1.0, 0.0).astype(BF16))
    n_b = _dot(selb, jnp.ones((TB, 128), BF16))
    m_b = jnp.floor((n_b + (ROW_ALIGN - 1)) * (1.0 / ROW_ALIGN)) * ROW_ALIGN
    er_ = lax.broadcasted_iota(I32, (N_EXPERTS, N_EXPERTS), 0)
    ec_ = lax.broadcasted_iota(I32, (N_EXPERTS, N_EXPERTS), 1)
    loff_b = _dot(jnp.where(ec_ < er_, 1.0, 0.0).astype(BF16), m_b.astype(BF16))
    lposf = jnp.concatenate([loff_b] * (TB // 128), axis=1) + rank
    rows = [jnp.sum(jnp.where(pick, lposf, 0.0), axis=0, keepdims=True) for pick in picks]
    lpos_ref[0] = jnp.concatenate(rows, axis=0).astype(I32)
    cnt_ref[0] = m_b

    combt = comb.T
    chi = combt.astype(BF16).astype(F32)
    h2_ref[:, 0:D_MODEL] = h_hi
    h2_ref[:, D_MODEL:H2W] = jnp.concatenate([chi, combt - chi], axis=1).astype(BF16)


def _route(x1p, x1s, mod, n2w, w_router_t, b_router_b, *, tiles_per_mod):
    n_p, n_s = x1p.shape[0] // TB, x1s.shape[0] // TB
    nt = n_p + n_s
    kern = functools.partial(_route_kernel, n_p_tiles=n_p, tiles_per_mod=tiles_per_mod)
    const = lambda shape: pl.BlockSpec(shape, lambda i: (0,) * len(shape))
    return pl.pallas_call(
        kern,
        grid=(nt,),
        in_specs=[pl.BlockSpec((TB, D_MODEL), lambda i: (jnp.minimum(i, n_p - 1), 0)),
                  pl.BlockSpec((TB, D_MODEL), lambda i: (jnp.maximum(i - n_p, 0), 0)),
                  const(mod.shape), const((1, D_MODEL)), const((N_EXPERTS, D_MODEL)), const((N_EXPERTS, TB))],
        out_specs=[pl.BlockSpec((TB, H2W), lambda i: (i, 0)),
                   pl.BlockSpec((1, TOP_K, TB), lambda i: (i, 0, 0)),
                   pl.BlockSpec((1, N_EXPERTS, 128), lambda i: (i, 0, 0))],
        out_shape=[jax.ShapeDtypeStruct((nt * TB, H2W), BF16),
                   jax.ShapeDtypeStruct((nt, TOP_K, TB), I32),
                   jax.ShapeDtypeStruct((nt, N_EXPERTS, 128), F32)],
        compiler_params=pltpu.CompilerParams(dimension_semantics=("arbitrary",), vmem_limit_bytes=VMEM_LIMIT),
        name="moe_route",
    )(x1p, x1s, mod, n2w, w_router_t, b_router_b)


def _plan_kernel(cnt_ref, off_ref, loff_ref, msz_ref, tail_ref, te_ref, *, nt, n_row_tiles):
    lane = lax.broadcasted_iota(I32, (N_EXPERTS, 128), 1)
    m = jnp.zeros((N_EXPERTS, 128), F32)
    for i in range(nt):
        m = jnp.where(lane == i, cnt_ref[i], m)
    total = jnp.broadcast_to(jnp.sum(m, axis=1, keepdims=True), (N_EXPERTS, 128))
    gsz = jnp.floor((total + (TR - 1)) * (1.0 / TR)) * TR
    er_ = lax.broadcasted_iota(I32, (N_EXPERTS, N_EXPERTS), 0)
    ec_ = lax.broadcasted_iota(I32, (N_EXPERTS, N_EXPERTS), 1)
    lstrict = jnp.where(ec_ < er_, 1.0, 0.0)
    ir_ = lax.broadcasted_iota(I32, (128, 128), 0)
    ic_ = lax.broadcasted_iota(I32, (128, 128), 1)
    ustrict = jnp.where(ir_ < ic_, 1.0, 0.0)
    gstart = _dot_hi(lstrict, gsz)
    off_ref[...] = (gstart + _dot_hi(m, ustrict)).astype(I32)
    loff_ref[...] = _dot_hi(lstrict, m).astype(I32)
    msz_ref[...] = m.astype(I32)
    tail_ref[...] = jnp.where(lane == 0, gstart + total, jnp.where(lane == 1, gsz - total, 0.0)).astype(I32)
    gend = gstart + gsz
    te_lanes = te_ref.shape[1]
    gend_w = jnp.concatenate([gend] * (te_lanes // 128), axis=1)
    tile_start = lax.broadcasted_iota(I32, (N_EXPERTS, te_lanes), 1).astype(F32) * TR
    te = jnp.sum(jnp.where(gend_w <= tile_start, 1.0, 0.0), axis=0, keepdims=True)
    used = gend[N_EXPERTS - 1:N_EXPERTS, 0:1] * (1.0 / TR)
    te_lane = lax.broadcasted_iota(I32, (1, te_lanes), 1)
    te = jnp.where(te_lane == n_row_tiles, used, jnp.minimum(te, N_EXPERTS - 1.0))
    te_ref[...] = jnp.broadcast_to(te, te_ref.shape).astype(I32)


def _plan(cnt, n_row_tiles):
    nt = cnt.shape[0]
    assert nt <= 128
    te_lanes = -(-(n_row_tiles + 1) // 128) * 128
    tab = jax.ShapeDtypeStruct((N_EXPERTS, 128), I32)
    return pl.pallas_call(
        functools.partial(_plan_kernel, nt=nt, n_row_tiles=n_row_tiles),
        out_shape=[tab, tab, tab, tab, jax.ShapeDtypeStruct((8, te_lanes), I32)],
        compiler_params=pltpu.CompilerParams(vmem_limit_bytes=VMEM_LIMIT),
        name="moe_plan",
    )(cnt)


def _start_copies(msz_ref, tile, make_copy):
    def body(e, carry):
        m = msz_ref[e, tile]

        @pl.when(m > 0)
        def _():
            make_copy(e, pl.multiple_of(m, ROW_ALIGN)).start()
        return carry
    lax.fori_loop(0, N_EXPERTS, body, 0, unroll=8)


def _tile_rows(loff_ref, msz_ref, tile):
    return pl.multiple_of(loff_ref[N_EXPERTS - 1, tile] + msz_ref[N_EXPERTS - 1, tile], ROW_ALIGN)


def _dispatch_kernel(off_ref, loff_ref, msz_ref, tail_ref, h2_ref, lpos_ref, xs_hbm, xloc_ref, zero_ref, sem,
                     tail_sem, *, nt):
    i = pl.program_id(0)
    slot = i % 2

    def copy_for(tile, slot_):
        def make(e, m):
            lo = pl.multiple_of(loff_ref[e, tile], ROW_ALIGN)
            of = pl.multiple_of(off_ref[e, tile], ROW_ALIGN)
            return pltpu.make_async_copy(xloc_ref.at[slot_, pl.ds(lo, m)], xs_hbm.at[pl.ds(of, m)], sem.at[slot_])
        return make

    def wait_tile(tile, slot_):
        n = _tile_rows(loff_ref, msz_ref, tile)
        pltpu.make_async_copy(xloc_ref.at[slot_, pl.ds(0, n)], xs_hbm.at[pl.ds(0, n)], sem.at[slot_]).wait()

    used_rows = _tile_rows(loff_ref, msz_ref, i)
    lpos = lpos_ref[0].astype(jnp.int16)
    h2 = h2_ref[...]
    ck = 1024
    one, zero = jnp.ones((ck, TB), BF16), jnp.zeros((ck, TB), BF16)
    for c in range(R_LOC // ck):
        @pl.when(c * ck < used_rows)
        def _(c=c):
            r = (lax.broadcasted_iota(I32, (ck, TB), 0) + c * ck).astype(jnp.int16)
            d = zero
            for k in range(TOP_K):
                d = jnp.where(r == lpos[k:k + 1, :], one, d)
            res = _dot(d, h2)
            xloc_ref[slot, c * ck:(c + 1) * ck, :] = res.astype(BF16)

    _start_copies(msz_ref, i, copy_for(i, slot))

    @pl.when(i > 0)
    def _():
        wait_tile(i - 1, 1 - slot)

    @pl.when(i == nt - 1)
    def _():
        zero_ref[...] = jnp.zeros(zero_ref.shape, BF16)

        def tail_copies(start):
            def body(e, carry):
                n = tail_ref[e, 1]

                @pl.when(n > 0)
                def _():
                    st = pl.multiple_of(tail_ref[e, 0], ROW_ALIGN)
                    nn = pl.multiple_of(n, ROW_ALIGN)
                    cp = pltpu.make_async_copy(zero_ref.at[pl.ds(0, nn)], xs_hbm.at[pl.ds(st, nn)], tail_sem)
                    if start:
                        cp.start()
                    else:
                        cp.wait()
                return carry
            lax.fori_loop(0, N_EXPERTS, body, 0)
        tail_copies(True)
        wait_tile(i, slot)
        tail_copies(False)


def _dispatch(off, loff, msz, tail, h2ext, lpos, n_rows):
    nt = lpos.shape[0]
    grid_spec = pltpu.PrefetchScalarGridSpec(
        num_scalar_prefetch=4,
        grid=(nt,),
        in_specs=[pl.BlockSpec((TB, H2W), lambda i, *_: (i, 0)),
                  pl.BlockSpec((1, TOP_K, TB), lambda i, *_: (i, 0, 0))],
        out_specs=pl.BlockSpec(memory_space=pl.ANY),
        scratch_shapes=[pltpu.VMEM((2, R_LOC, H2W), BF16), pltpu.VMEM((TR, H2W), BF16),
                        pltpu.SemaphoreType.DMA((2,)), pltpu.SemaphoreType.DMA],
    )
    return pl.pallas_call(
        functools.partial(_dispatch_kernel, nt=nt),
        grid_spec=grid_spec,
        out_shape=jax.ShapeDtypeStruct((n_rows, H2W), BF16),
        compiler_params=pltpu.CompilerParams(dimension_semantics=("arbitrary",), vmem_limit_bytes=VMEM_LIMIT),
        name="moe_dispatch",
    )(off, loff, msz, tail, h2ext, lpos)


def _expert_kernel(te_ref, xs_ref, wg_ref, wu_ref, wd_ref, ys_ref, wgu_ref, wdb_ref, *, n_row_tiles):
    j = pl.program_id(0)
    e = te_ref[j]

    @pl.when(j < te_ref[n_row_tiles])
    def _():
        @pl.when((j == 0) | (e != te_ref[jnp.maximum(j - 1, 0)]))
        def _():
            wgu_ref[:, :D_EXPERT] = wg_ref[0].astype(BF16)
            wgu_ref[:, D_EXPERT:] = wu_ref[0].astype(BF16)
            wdb_ref[...] = wd_ref[0].astype(BF16)

        x = xs_ref[:, 0:D_MODEL]
        ext = xs_ref[:, D_MODEL:H2W].astype(F32)
        wts = ext[:, :N_EXPERTS] + ext[:, N_EXPERTS:]
        lane = lax.broadcasted_iota(I32, wts.shape, 1)
        w = jnp.sum(jnp.where(lane == e, wts, 0.0), axis=-1, keepdims=True)
        h = _dot(x, wgu_ref[...])
        hg, hu = h[:, :D_EXPERT], h[:, D_EXPERT:]
        act = hg * _sigmoid(hg) * hu * w
        ys_ref[...] = _dot(act.astype(BF16), wdb_ref[...]).astype(BF16)


def _experts(te, xs, wg, wu, wd, n_row_tiles):
    def row_tile(j, te_ref):
        return jnp.maximum(jnp.minimum(j, te_ref[n_row_tiles] - 1), 0)
    grid_spec = pltpu.PrefetchScalarGridSpec(
        num_scalar_prefetch=1,
        grid=(n_row_tiles,),
        in_specs=[pl.BlockSpec((TR, H2W), lambda j, te_ref: (row_tile(j, te_ref), 0)),
                  pl.BlockSpec((1, D_MODEL, D_EXPERT), lambda j, te_ref: (te_ref[row_tile(j, te_ref)], 0, 0)),
                  pl.BlockSpec((1, D_MODEL, D_EXPERT), lambda j, te_ref: (te_ref[row_tile(j, te_ref)], 0, 0)),
                  pl.BlockSpec((1, D_EXPERT, D_MODEL), lambda j, te_ref: (te_ref[row_tile(j, te_ref)], 0, 0))],
        out_specs=pl.BlockSpec((TR, D_MODEL), lambda j, te_ref: (row_tile(j, te_ref), 0)),
        scratch_shapes=[pltpu.VMEM((D_MODEL, 2 * D_EXPERT), BF16), pltpu.VMEM((D_EXPERT, D_MODEL), BF16)],
    )
    return pl.pallas_call(
        functools.partial(_expert_kernel, n_row_tiles=n_row_tiles),
        grid_spec=grid_spec,
        out_shape=jax.ShapeDtypeStruct((n_row_tiles * TR, D_MODEL), BF16),
        compiler_params=pltpu.CompilerParams(dimension_semantics=("arbitrary",), vmem_limit_bytes=VMEM_LIMIT),
        name="moe_experts",
    )(te, xs, wg, wu, wd)


def _combine_kernel(off_ref, loff_ref, msz_ref, lpos_ref, h2_ref, xp_ref, xs_ref, mod_ref, wgs_ref, wus_ref, wds_ref,
                    fnw_ref, ysrt_hbm, yp_ref, ys_ref, yloc_ref, acc_ref, sem, *, nt, n_p_tiles, tiles_per_mod,
                    final):
    i = pl.program_id(0)
    slot = i % 2

    def copy_for(tile, slot_):
        def make(e, m):
            lo = pl.multiple_of(loff_ref[e, tile], ROW_ALIGN)
            of = pl.multiple_of(off_ref[e, tile], ROW_ALIGN)
            return pltpu.make_async_copy(ysrt_hbm.at[pl.ds(of, m)], yloc_ref.at[slot_, pl.ds(lo, m)], sem.at[slot_])
        return make

    @pl.when(i == 0)
    def _():
        yloc_ref[...] = jnp.zeros(yloc_ref.shape, BF16)
        _start_copies(msz_ref, 0, copy_for(0, 0))

    @pl.when(i + 1 < nt)
    def _():
        _start_copies(msz_ref, i + 1, copy_for(i + 1, 1 - slot))

    hb = h2_ref[...]
    hg = _dot(hb, wgs_ref[...].astype(BF16))
    hu = _dot(hb, wus_ref[...].astype(BF16))
    acc_ref[...] = _dot((hg * _sigmoid(hg) * hu).astype(BF16), wds_ref[...].astype(BF16))

    used_rows = _tile_rows(loff_ref, msz_ref, i)
    pltpu.make_async_copy(ysrt_hbm.at[pl.ds(0, used_rows)], yloc_ref.at[slot, pl.ds(0, used_rows)],
                          sem.at[slot]).wait()

    lpos_pad = jnp.concatenate([lpos_ref[0].astype(F32), jnp.zeros((128 - TOP_K, TB), F32)], axis=0)
    lposc = lpos_pad.T.astype(I32)
    ck = 512
    cols = [jnp.broadcast_to(lposc[:, k:k + 1], (TB, ck)).astype(jnp.int16) for k in range(TOP_K)]
    one, zero = jnp.ones((TB, ck), BF16), jnp.zeros((TB, ck), BF16)
    for c in range(R_LOC // ck):
        @pl.when(c * ck < used_rows)
        def _(c=c):
            r = (lax.broadcasted_iota(I32, (TB, ck), 1) + c * ck).astype(jnp.int16)
            cm = zero
            for k in range(TOP_K):
                cm = jnp.where(r == cols[k], one, cm)
            acc_ref[...] += _dot(cm, yloc_ref[slot, c * ck:(c + 1) * ck, :])

    row = _mod_row(i, n_p_tiles, tiles_per_mod)
    g2 = mod_ref[pl.ds(row, 1), 5 * D_MODEL:6 * D_MODEL]
    x2 = _select_x(i, n_p_tiles, xp_ref, xs_ref) + g2 * acc_ref[...]
    y = _rms(x2) * fnw_ref[...] if final else x2

    @pl.when(i < n_p_tiles)
    def _():
        yp_ref[...] = y

    @pl.when(i >= n_p_tiles)
    def _():
        ys_ref[...] = y


def _combine(off, loff, msz, lpos, h2ext, x1p, x1s, mod, wgs, wus, wds, fnw, ysorted, *, tiles_per_mod, final):
    n_p, n_s = x1p.shape[0] // TB, x1s.shape[0] // TB
    nt = n_p + n_s
    const = lambda shape: pl.BlockSpec(shape, lambda i, *_: (0,) * len(shape), pipeline_mode=pl.Buffered(1))
    p_idx = lambda i, *_: (jnp.minimum(i, n_p - 1), 0)
    s_idx = lambda i, *_: (jnp.maximum(i - n_p, 0), 0)
    grid_spec = pltpu.PrefetchScalarGridSpec(
        num_scalar_prefetch=3,
        grid=(nt,),
        in_specs=[pl.BlockSpec((1, TOP_K, TB), lambda i, *_: (i, 0, 0)),
                  pl.BlockSpec((TB, D_MODEL), lambda i, *_: (i, 0)),
                  pl.BlockSpec((TB, D_MODEL), p_idx), pl.BlockSpec((TB, D_MODEL), s_idx),
                  const(mod.shape), const((D_MODEL, D_EXPERT)), const((D_MODEL, D_EXPERT)), const((D_EXPERT, D_MODEL)),
                  const((1, D_MODEL)), pl.BlockSpec(memory_space=pl.ANY)],
        out_specs=[pl.BlockSpec((TB, D_MODEL), p_idx), pl.BlockSpec((TB, D_MODEL), s_idx)],
        scratch_shapes=[pltpu.VMEM((2, R_LOC, D_MODEL), BF16), pltpu.VMEM((TB, D_MODEL), F32),
                        pltpu.SemaphoreType.DMA((2,))],
    )
    kern = functools.partial(_combine_kernel, nt=nt, n_p_tiles=n_p, tiles_per_mod=tiles_per_mod, final=final)
    return pl.pallas_call(
        kern,
        grid_spec=grid_spec,
        out_shape=[jax.ShapeDtypeStruct(x1p.shape, F32), jax.ShapeDtypeStruct(x1s.shape, F32)],
        compiler_params=pltpu.CompilerParams(dimension_semantics=("arbitrary",), vmem_limit_bytes=VMEM_LIMIT),
        name="moe_combine",
    )(off, loff, msz, lpos, h2ext, x1p, x1s, mod, wgs, wus, wds, fnw, ysorted)


def _moe(x1p, x1s, mod, n2w, w_router, b_router, wg, wu, wd, wgs, wus, wds, fnw, *, tokens_per_mod, final):
    assert R_LOC >= TB * TOP_K + N_EXPERTS * (ROW_ALIGN - 1) and tokens_per_mod % TB == 0
    nt = (x1p.shape[0] + x1s.shape[0]) // TB
    n_rows_max = nt * TB * TOP_K + nt * N_EXPERTS * (ROW_ALIGN - 1) + N_EXPERTS * (TR - ROW_ALIGN)
    n_row_tiles = -(-n_rows_max // TR)
    tiles_per_mod = tokens_per_mod // TB
    brb = jnp.broadcast_to(b_router.reshape(N_EXPERTS, 1), (N_EXPERTS, TB))
    h2ext, lpos, cnt = _route(x1p, x1s, mod, n2w, w_router.T, brb, tiles_per_mod=tiles_per_mod)
    off, loff, msz, tail, te = _plan(cnt, n_row_tiles)
    xs = _dispatch(off, loff, msz, tail, h2ext, lpos, n_row_tiles * TR)
    ysorted = _experts(te[0], xs, wg, wu, wd, n_row_tiles)
    return _combine(off, loff, msz, lpos, h2ext, x1p, x1s, mod, wgs, wus, wds, fnw, ysorted,
                    tiles_per_mod=tiles_per_mod, final=final)


def kernel(x_prompt, x_sample, state_gla, c, c_ctx, w_mod, b_mod, norm1_w, w_in, conv_w, conv_b, w_decay, b_decay,
           gla_norm_w, w_br_conv, w_br_gla, w_out, norm2_w, w_router, b_router, w_gate_e, w_up_e, w_down_e,
           w_gate_s, w_up_s, w_down_s, final_norm_w):
    depth = w_mod.shape[0]
    nb_p, len_p, _ = x_prompt.shape
    nb_s, len_s, _ = x_sample.shape
    yp = x_prompt.reshape(nb_p * len_p, D_MODEL)
    ys = x_sample.reshape(nb_s * len_s, D_MODEL)
    fnw = final_norm_w.reshape(1, D_MODEL)

    cond = jnp.concatenate([c_ctx[None, :], c, jnp.zeros((8 - 1 - nb_s, D_MODEL), F32)], axis=0)
    states = []
    for l in range(depth):
        mod = _modulation(cond, w_mod[l], b_mod[l].reshape(1, -1))
        mod_p, mod_s = mod[0:1], mod[1:1 + nb_s]

        w_in_r = _w_in_prep(w_in[l])
        wdec = jnp.zeros((2, LR_PAD, D_GLA), F32)
        wdec = wdec.at[0, 0:GLA_RANK].set(w_decay[l, 0]).at[1, GLA_RANK:2 * GLA_RANK].set(w_decay[l, 1])
        n1w = norm1_w[l].reshape(1, D_MODEL)
        mix_w = (conv_w[l], conv_b[l].reshape(1, D_CONV), wdec, b_decay[l], gla_norm_w[l].reshape(1, D_GLA),
                 w_br_conv[l].astype(BF16), w_br_gla[l].astype(BF16), w_out[l].astype(BF16))
        moe_w = (norm2_w[l].reshape(1, D_MODEL), w_router[l], b_router[l],
                 w_gate_e[l], w_up_e[l], w_down_e[l], w_gate_s[l], w_up_s[l], w_down_s[l])

        proj_p = _in_proj(yp, mod_p, n1w, w_in_r, rows_per_mod=nb_p * len_p)
        yp, st = _mixer(proj_p, yp, mod_p, *mix_w, None, seq_len=len_p, row_len=len_p, emit_state=True)
        states.append(st)
        proj_s = _in_proj(ys, mod_s, n1w, w_in_r, rows_per_mod=len_s)
        (ys,) = _mixer(proj_s, ys, mod_s, *mix_w, state_gla[:, l], seq_len=len_s, row_len=GRID_W, emit_state=False)

        yp, ys = _moe(yp, ys, mod, *moe_w, fnw, tokens_per_mod=len_s, final=l == depth - 1)
    new_state = jnp.stack(states, axis=1)
    return (yp.reshape(nb_p, len_p, D_MODEL), ys.reshape(nb_s, len_s, D_MODEL), new_state)
```

```python
import functools

import jax
import jax.numpy as jnp
from jax import lax
from jax.experimental import pallas as pl
from jax.experimental.pallas import tpu as pltpu

F32 = jnp.float32
BF16 = jnp.bfloat16

D_MODEL = 1024
GRID_W = 64
D_CONV = 512
N_HEADS = 4
HEAD_D = 128
D_GLA = N_HEADS * HEAD_D
GLA_RANK = 16
GLA_GATE_NORM = 16.0
LOG2_E = 1.4426950408889634
CHUNK = 64
SUB = 8
N_SUB = CHUNK // SUB
N_EXPERTS = 64
TOP_K = 8
D_EXPERT = 256
ROUTED_SCALE = 2.5
EPS = 1e-6

C_U, C_GB, C_GC, C_Q, C_K, C_V, C_GO = 0, 512, 1024, 1536, 2048, 2560, 3072
C_BRC, C_BRG, C_LR = 3584, 4608, 5632
D_PROJ = 5760
LR_PAD = 128

VMEM_LIMIT = 56 * 1024 * 1024


def _dot(a, b):
    return jnp.dot(a, b, preferred_element_type=F32)


def _dot_nt(a, b):
    return lax.dot_general(a, b, (((1,), (1,)), ((), ())), preferred_element_type=F32)


def _dot_tn(a, b):
    return lax.dot_general(a, b, (((0,), (0,)), ((), ())), preferred_element_type=F32)


def _dot_hi(a, b):
    return jnp.dot(a, b, preferred_element_type=F32, precision=lax.Precision.HIGHEST)


def _split_bf16(x):
    hi = x.astype(BF16)
    lo = (x - hi.astype(F32)).astype(BF16)
    return hi, lo


def _rms(x):
    return x * lax.rsqrt(jnp.mean(x * x, axis=-1, keepdims=True) + EPS)


def _sigmoid(x):
    return 0.5 * jnp.tanh(0.5 * x) + 0.5


def _mod_kernel(cond_ref, w_ref, b_ref, o_ref):
    c = cond_ref[...]
    o_ref[...] = _dot_hi(c * jax.nn.sigmoid(c), w_ref[...]) + b_ref[...]


def _modulation(cond, w_mod, b_mod):
    n_rows = cond.shape[0]
    tn = 1536
    return pl.pallas_call(
        _mod_kernel,
        grid=(6 * D_MODEL // tn,),
        in_specs=[pl.BlockSpec((n_rows, D_MODEL), lambda j: (0, 0)),
                  pl.BlockSpec((D_MODEL, tn), lambda j: (0, j)),
                  pl.BlockSpec((1, tn), lambda j: (0, j))],
        out_specs=pl.BlockSpec((n_rows, tn), lambda j: (0, j)),
        out_shape=jax.ShapeDtypeStruct((n_rows, 6 * D_MODEL), F32),
        compiler_params=pltpu.CompilerParams(dimension_semantics=("arbitrary",),
                                             vmem_limit_bytes=VMEM_LIMIT),
        name="modulation",
    )(cond, w_mod, b_mod)


D_IN_PROJ = 5664
C_LR_SRC, C_GATES_SRC = 3584, 3616


def _w_in_prep_kernel(w_ref, o_ref):
    rc = 64
    n_gate = 2 * D_MODEL

    def body(t, carry):
        r = pl.ds(pl.multiple_of(t * rc, rc), rc)
        o_ref[r, 0:C_BRC] = w_ref[r, 0:C_LR_SRC].astype(BF16)
        o_ref[r, C_BRC:C_LR] = w_ref[r, C_GATES_SRC:C_GATES_SRC + n_gate].astype(BF16)
        lr = jnp.concatenate([w_ref[r, C_LR_SRC:C_GATES_SRC], jnp.zeros((rc, LR_PAD - 2 * GLA_RANK), F32)], axis=1)
        o_ref[r, C_LR:D_PROJ] = lr.astype(BF16)
        return carry
    lax.fori_loop(0, o_ref.shape[0] // rc, body, 0)


def _w_in_prep(w_in, layer):
    tr = 256
    return pl.pallas_call(
        _w_in_prep_kernel,
        grid=(D_MODEL // tr,),
        in_specs=[pl.BlockSpec((None, tr, D_IN_PROJ), lambda i: (layer, i, 0))],
        out_specs=pl.BlockSpec((tr, D_PROJ), lambda i: (i, 0)),
        out_shape=jax.ShapeDtypeStruct((D_MODEL, D_PROJ), BF16),
        compiler_params=pltpu.CompilerParams(dimension_semantics=("arbitrary",), vmem_limit_bytes=VMEM_LIMIT),
        name="w_in_prep",
    )(w_in)


def _inproj_kernel(x_ref, mod_ref, nw_ref, w_ref, o_ref, *, rows_per_mod, tm):
    i = pl.program_id(1)
    row = (i * tm) // rows_per_mod
    sh = mod_ref[pl.ds(row, 1), 0:D_MODEL]
    sc = mod_ref[pl.ds(row, 1), D_MODEL:2 * D_MODEL]
    h = _rms(x_ref[...]) * nw_ref[...] * (1.0 + sc) + sh
    o_ref[...] = _dot(h.astype(BF16), w_ref[...]).astype(BF16)


def _in_proj(x2d, mod, norm_w, w_in_r, rows_per_mod):
    t = x2d.shape[0]
    tm, tn = 1024, 1920
    kern = functools.partial(_inproj_kernel, rows_per_mod=rows_per_mod, tm=tm)
    return pl.pallas_call(
        kern,
        grid=(D_PROJ // tn, t // tm),
        in_specs=[pl.BlockSpec((tm, D_MODEL), lambda j, i: (i, 0)),
                  pl.BlockSpec(mod.shape, lambda j, i: (0, 0)),
                  pl.BlockSpec((1, D_MODEL), lambda j, i: (0, 0)),
                  pl.BlockSpec((D_MODEL, tn), lambda j, i: (0, j))],
        out_specs=pl.BlockSpec((tm, tn), lambda j, i: (i, j)),
        out_shape=jax.ShapeDtypeStruct((t, D_PROJ), BF16),
        compiler_params=pltpu.CompilerParams(dimension_semantics=("arbitrary", "arbitrary"),
                                             vmem_limit_bytes=VMEM_LIMIT),
        name="in_proj",
    )(x2d, mod, norm_w, w_in_r)


def _log2_sigmoid(x):
    return jnp.minimum(x, 0.0) * LOG2_E - jnp.log2(1.0 + jnp.exp2(jnp.abs(x) * (-LOG2_E)))


def _gla_chunk_head(qc, kc, vc, bc, st, rev):
    lane = lax.broadcasted_iota(jnp.int32, (SUB, CHUNK), 1)
    sub = lax.broadcasted_iota(jnp.int32, (SUB, CHUNK), 0)
    tot = bc[0:1] if rev else bc[CHUNK - 1:CHUNK]

    o = _dot_nt((qc * jnp.exp2(bc)).astype(BF16), st.astype(BF16))
    k_tail = kc * jnp.exp2(tot - bc)
    st_new = st * jnp.exp2(tot) + _dot_tn(vc.astype(BF16), k_tail.astype(BF16))

    lhs_segs, rhs_segs = [], []

    def rows(before, mid, after):
        parts = ([jnp.zeros((before, HEAD_D), F32)] if before else []) + [mid]
        parts += [jnp.zeros((after, HEAD_D), F32)] if after else []
        return jnp.concatenate(parts, axis=0) if len(parts) > 1 else mid

    key_blocks = range(1, N_SUB) if rev else range(0, N_SUB - 1)
    for jb in key_blocks:
        r0 = jb * SUB
        ref_row = bc[r0:r0 + 1] if rev else bc[r0 + SUB - 1:r0 + SUB]
        ke = kc[r0:r0 + SUB] * jnp.exp2(ref_row - bc[r0:r0 + SUB])
        rhs_segs.append(rows(r0, ke, CHUNK - r0 - SUB))
        if rev:
            ql = qc[:r0] * jnp.exp2(bc[:r0] - ref_row)
            lhs_segs.append(rows(0, ql, CHUNK - r0))
        else:
            ql = qc[r0 + SUB:] * jnp.exp2(bc[r0 + SUB:] - ref_row)
            lhs_segs.append(rows(r0 + SUB, ql, 0))
    far = _dot_nt(jnp.concatenate(lhs_segs, axis=1).astype(BF16),
                  jnp.concatenate(rhs_segs, axis=1).astype(BF16))

    blocks = []
    for ib in range(N_SUB):
        r0 = ib * SUB
        qi, bi = qc[r0:r0 + SUB], bc[r0:r0 + SUB]
        acc = jnp.zeros((SUB, CHUNK), F32)
        for jj in range(SUB):
            j = r0 + jj
            e = jnp.exp2(bi - bc[j:j + 1])
            col = jnp.sum(qi * (kc[j:j + 1] * e), axis=-1, keepdims=True)
            acc = jnp.where(lane == j, col, acc)
        keep = (lane - r0 >= sub) if rev else (lane - r0 <= sub)
        blocks.append(jnp.where(keep, acc, 0.0))
    scores = far + jnp.concatenate(blocks, axis=0)
    o = o + _dot(scores.astype(BF16), vc.astype(BF16))
    return o, st_new


def _mixer_kernel(*refs, seq_len, row_len, has_s0, emit_state):
    it = iter(refs)
    proj_ref, x_ref, mod_ref, cw_ref, cb_ref, wdec_ref, bdec_ref, gnw_ref = (next(it) for _ in range(8))
    wbc_ref, wbg_ref, wout_ref = (next(it) for _ in range(3))
    s0_ref = next(it) if has_s0 else None
    out_ref = next(it)
    st_out_ref = next(it) if emit_state else None
    la_f_ref, la_b_ref, o_ref, st_ref = (next(it) for _ in range(4))

    L = seq_len
    n_chunks = L // CHUNK
    tr = 256
    assert tr % row_len == 0 and L % tr == 0

    ri = lax.broadcasted_iota(jnp.int32, (tr, tr), 0)
    ci = lax.broadcasted_iota(jnp.int32, (tr, tr), 1)
    same_chunk = (ri // CHUNK) == (ci // CHUNK)

    def decay_body(t, carry):
        r0 = pl.multiple_of(t * tr, tr)
        lr = proj_ref[pl.ds(r0, tr), C_LR:C_LR + LR_PAD]
        for d, ref in ((0, la_f_ref), (1, la_b_ref)):
            whi, wlo = _split_bf16(wdec_ref[d])
            z = _dot(lr, whi) + _dot(lr, wlo) + bdec_ref[d:d + 1]
            la_hi, la_lo = _split_bf16(_log2_sigmoid(z) * (1.0 / GLA_GATE_NORM))
            tri = jnp.where(same_chunk & ((ci >= ri) if d else (ci <= ri)), 1.0, 0.0).astype(BF16)
            ref[pl.ds(r0, tr), :] = _dot(tri, la_hi) + _dot(tri, la_lo)
        return carry
    lax.fori_loop(0, L // tr, decay_body, 0)

    for rev, la_ref in ((False, la_f_ref), (True, la_b_ref)):
        d = 1 if rev else 0
        for h in range(N_HEADS):
            if has_s0:
                st_ref[h] = s0_ref[0, d, h].T
            else:
                st_ref[h] = jnp.zeros((HEAD_D, HEAD_D), F32)

        def chunk_body(c, carry, rev=rev, la_ref=la_ref):
            cc = (n_chunks - 1 - c) if rev else c
            r0 = pl.multiple_of(cc * CHUNK, CHUNK)
            for h in range(N_HEADS):
                lo, hi = h * HEAD_D, (h + 1) * HEAD_D
                qc = proj_ref[pl.ds(r0, CHUNK), C_Q + lo:C_Q + hi].astype(F32) * (HEAD_D ** -0.5)
                kc = proj_ref[pl.ds(r0, CHUNK), C_K + lo:C_K + hi].astype(F32)
                vc = proj_ref[pl.ds(r0, CHUNK), C_V + lo:C_V + hi].astype(F32)
                oc, st_new = _gla_chunk_head(qc, kc, vc, la_ref[pl.ds(r0, CHUNK), lo:hi], st_ref[h], rev)
                st_ref[h] = st_new
                if rev:
                    o_ref[pl.ds(r0, CHUNK), lo:hi] += oc
                else:
                    o_ref[pl.ds(r0, CHUNK), lo:hi] = oc
            return carry
        lax.fori_loop(0, n_chunks, chunk_body, 0, unroll=2)

        if emit_state:
            for h in range(N_HEADS):
                st_out_ref[0, d, h] = st_ref[h].T

    mod_row = pl.program_id(0) if mod_ref.shape[0] > 1 else 0
    g1 = mod_ref[pl.ds(mod_row, 1), 2 * D_MODEL:3 * D_MODEL]
    hsel_r = lax.broadcasted_iota(jnp.int32, (D_GLA, D_GLA), 0) // HEAD_D
    hsel_c = lax.broadcasted_iota(jnp.int32, (D_GLA, D_GLA), 1) // HEAD_D
    head_avg = jnp.where(hsel_r == hsel_c, 1.0 / HEAD_D, 0.0).astype(BF16)
    pos = lax.broadcasted_iota(jnp.int32, (tr, 1), 0)

    def dense_body(t, carry):
        r0 = pl.multiple_of(t * tr, tr)
        rows = pl.ds(r0, tr)
        cu = (proj_ref[rows, C_GC:C_GC + D_CONV].astype(F32) * proj_ref[rows, C_U:C_U + D_CONV].astype(F32))
        in_row = pos % row_len
        left = jnp.where(in_row == 0, 0.0, pltpu.roll(cu, 1, axis=0))
        right = jnp.where(in_row == row_len - 1, 0.0, pltpu.roll(cu, tr - 1, axis=0))
        conv = cw_ref[0:1] * left + cw_ref[1:2] * cu + cw_ref[2:3] * right + cb_ref[...]
        y_conv = proj_ref[rows, C_GB:C_GB + D_CONV].astype(F32) * conv

        o = o_ref[rows, :]
        osq_hi, osq_lo = _split_bf16(o * o)
        ms = _dot(osq_hi, head_avg) + _dot(osq_lo, head_avg)
        g_out = proj_ref[rows, C_GO:C_GO + D_GLA].astype(F32)
        y_gla = o * lax.rsqrt(ms + EPS) * gnw_ref[...] * (g_out * _sigmoid(g_out))

        merged = (_sigmoid(proj_ref[rows, C_BRC:C_BRC + D_MODEL].astype(F32)) * _dot(y_conv.astype(BF16), wbc_ref[...])
                  + _sigmoid(proj_ref[rows, C_BRG:C_BRG + D_MODEL].astype(F32)) * _dot(y_gla.astype(BF16), wbg_ref[...]))
        out_ref[rows, :] = x_ref[rows, :] + g1 * _dot(merged.astype(BF16), wout_ref[...])
        return carry
    lax.fori_loop(0, L // tr, dense_body, 0)


def _mixer(proj, x2d, mod, conv_w, conv_b, wdec, bdec, gnw, wbc, wbg, wout, s0, *, seq_len, row_len, emit_state):
    t = x2d.shape[0]
    nb = t // seq_len
    has_s0 = s0 is not None
    one = pl.Buffered(1)
    const = lambda shape: pl.BlockSpec(shape, lambda b: (0,) * len(shape), pipeline_mode=one)
    in_specs = [pl.BlockSpec((seq_len, D_PROJ), lambda b: (b, 0), pipeline_mode=one),
                pl.BlockSpec((seq_len, D_MODEL), lambda b: (b, 0)),
                const(mod.shape),
                const((3, D_CONV)), const((1, D_CONV)), const((2, LR_PAD, D_GLA)), const((2, D_GLA)),
                const((1, D_GLA)), const((D_CONV, D_MODEL)), const((D_GLA, D_MODEL)), const((D_MODEL, D_MODEL))]
    args = [proj, x2d, mod, conv_w, conv_b, wdec, bdec, gnw, wbc, wbg, wout]
    if has_s0:
        in_specs.append(pl.BlockSpec((1, 2, N_HEADS, HEAD_D, HEAD_D), lambda b: (b, 0, 0, 0, 0)))
        args.append(s0)
    out_specs = [pl.BlockSpec((seq_len, D_MODEL), lambda b: (b, 0))]
    out_shape = [jax.ShapeDtypeStruct((t, D_MODEL), F32)]
    if emit_state:
        out_specs.append(pl.BlockSpec((1, 2, N_HEADS, HEAD_D, HEAD_D), lambda b: (b, 0, 0, 0, 0)))
        out_shape.append(jax.ShapeDtypeStruct((nb, 2, N_HEADS, HEAD_D, HEAD_D), F32))
    kern = functools.partial(_mixer_kernel, seq_len=seq_len, row_len=row_len, has_s0=has_s0, emit_state=emit_state)
    return pl.pallas_call(
        kern,
        grid=(nb,),
        in_specs=in_specs,
        out_specs=out_specs,
        out_shape=out_shape,
        scratch_shapes=[pltpu.VMEM((seq_len, D_GLA), F32), pltpu.VMEM((seq_len, D_GLA), F32),
                        pltpu.VMEM((seq_len, D_GLA), F32), pltpu.VMEM((N_HEADS, HEAD_D, HEAD_D), F32)],
        compiler_params=pltpu.CompilerParams(dimension_semantics=("arbitrary",),
                                             vmem_limit_bytes=VMEM_LIMIT),
        name="mixer",
    )(*args)


I32 = jnp.int32
TB = 256
ROW_ALIGN = 16
TR = 1024
R_LOC = 3072
H2W = 1152


def _select_x(i, n_p_tiles, xp_ref, xs_ref):
    return jnp.where(i < n_p_tiles, xp_ref[...], xs_ref[...])


def _mod_row(i, n_p_tiles, tiles_per_mod):
    return jnp.where(i < n_p_tiles, 0, 1 + (i - n_p_tiles) // tiles_per_mod)


def _route_kernel(xp_ref, xs_ref, mod_ref, n2w_ref, wrt_ref, brb_ref, h2_ref, lpos_ref, cnt_ref, *,
                  n_p_tiles, tiles_per_mod):
    i = pl.program_id(0)
    row = _mod_row(i, n_p_tiles, tiles_per_mod)
    x = _select_x(i, n_p_tiles, xp_ref, xs_ref)
    sh = mod_ref[pl.ds(row, 1), 3 * D_MODEL:4 * D_MODEL]
    sc = mod_ref[pl.ds(row, 1), 4 * D_MODEL:5 * D_MODEL]
    h2 = _rms(x) * n2w_ref[...] * (1.0 + sc) + sh

    h_hi, h_lo = _split_bf16(h2)
    w_hi, w_lo = _split_bf16(wrt_ref[...])
    scores = jax.nn.sigmoid(_dot_nt(w_hi, h_hi) + (_dot_nt(w_hi, h_lo) + _dot_nt(w_lo, h_hi)))
    biased = scores + brb_ref[...]
    eidx = lax.broadcasted_iota(I32, scores.shape, 0)
    picks = []
    for _k in range(TOP_K):
        m = jnp.max(biased, axis=0, keepdims=True)
        first = jnp.min(jnp.where(biased == m, eidx, N_EXPERTS), axis=0, keepdims=True)
        pick = eidx == first
        picks.append(pick)
        biased = jnp.where(pick, -jnp.inf, biased)
    sel = jnp.zeros(scores.shape, F32)
    for pick in picks:
        sel = jnp.where(pick, 1.0, sel)
    selsc = sel * scores
    comb = selsc / jnp.sum(selsc, axis=0, keepdims=True) * ROUTED_SCALE

    selb = sel.astype(BF16)
    tr_ = lax.broadcasted_iota(I32, (TB, TB), 0)
    tc_ = lax.broadcasted_iota(I32, (TB, TB), 1)
    rank = _dot(selb, jnp.where(tr_ < tc_, 1.0, 0.0).astype(BF16))
    n_b = _dot(selb, jnp.ones((TB, 128), BF16))
    m_b = jnp.maximum(jnp.floor((n_b + (ROW_ALIGN - 1)) * (1.0 / ROW_ALIGN)), 1.0) * ROW_ALIGN
    er_ = lax.broadcasted_iota(I32, (N_EXPERTS, N_EXPERTS), 0)
    ec_ = lax.broadcasted_iota(I32, (N_EXPERTS, N_EXPERTS), 1)
    loff_b = _dot(jnp.where(ec_ < er_, 1.0, 0.0).astype(BF16), m_b.astype(BF16))
    lposf = jnp.concatenate([loff_b] * (TB // 128), axis=1) + rank
    rows = [jnp.sum(jnp.where(pick, lposf, 0.0), axis=0, keepdims=True) for pick in picks]
    lpos_ref[0] = jnp.concatenate(rows, axis=0).astype(I32)
    cnt_ref[0] = m_b

    combt = comb.T
    chi = combt.astype(BF16).astype(F32)
    h2_ref[:, 0:D_MODEL] = h_hi
    h2_ref[:, D_MODEL:H2W] = jnp.concatenate([chi, combt - chi], axis=1).astype(BF16)


def _route(x1p, x1s, mod, n2w, w_router_t, b_router_b, *, tiles_per_mod):
    n_p, n_s = x1p.shape[0] // TB, x1s.shape[0] // TB
    nt = n_p + n_s
    kern = functools.partial(_route_kernel, n_p_tiles=n_p, tiles_per_mod=tiles_per_mod)
    const = lambda shape: pl.BlockSpec(shape, lambda i: (0,) * len(shape))
    return pl.pallas_call(
        kern,
        grid=(nt,),
        in_specs=[pl.BlockSpec((TB, D_MODEL), lambda i: (jnp.minimum(i, n_p - 1), 0)),
                  pl.BlockSpec((TB, D_MODEL), lambda i: (jnp.maximum(i - n_p, 0), 0)),
                  const(mod.shape), const((1, D_MODEL)), const((N_EXPERTS, D_MODEL)), const((N_EXPERTS, TB))],
        out_specs=[pl.BlockSpec((TB, H2W), lambda i: (i, 0)),
                   pl.BlockSpec((1, TOP_K, TB), lambda i: (i, 0, 0)),
                   pl.BlockSpec((1, N_EXPERTS, 128), lambda i: (i, 0, 0))],
        out_shape=[jax.ShapeDtypeStruct((nt * TB, H2W), BF16),
                   jax.ShapeDtypeStruct((nt, TOP_K, TB), I32),
                   jax.ShapeDtypeStruct((nt, N_EXPERTS, 128), F32)],
        compiler_params=pltpu.CompilerParams(dimension_semantics=("arbitrary",), vmem_limit_bytes=VMEM_LIMIT),
        name="moe_route",
    )(x1p, x1s, mod, n2w, w_router_t, b_router_b)


def _plan_kernel(cnt_ref, off_ref, loff_ref, msz_ref, tail_ref, te_ref, *, nt, n_row_tiles):
    lane = lax.broadcasted_iota(I32, (N_EXPERTS, 128), 1)
    m = jnp.zeros((N_EXPERTS, 128), F32)
    for i in range(nt):
        m = jnp.where(lane == i, cnt_ref[i], m)
    total = jnp.broadcast_to(jnp.sum(m, axis=1, keepdims=True), (N_EXPERTS, 128))
    gsz = jnp.floor((total + (TR - 1)) * (1.0 / TR)) * TR
    er_ = lax.broadcasted_iota(I32, (N_EXPERTS, N_EXPERTS), 0)
    ec_ = lax.broadcasted_iota(I32, (N_EXPERTS, N_EXPERTS), 1)
    lstrict = jnp.where(ec_ < er_, 1.0, 0.0)
    ir_ = lax.broadcasted_iota(I32, (128, 128), 0)
    ic_ = lax.broadcasted_iota(I32, (128, 128), 1)
    ustrict = jnp.where(ir_ < ic_, 1.0, 0.0)
    gstart = _dot_hi(lstrict, gsz)
    off_ref[...] = (gstart + _dot_hi(m, ustrict)).astype(I32)
    loff_ref[...] = _dot_hi(lstrict, m).astype(I32)
    msz_ref[...] = m.astype(I32)
    tail_ref[...] = jnp.where(lane == 0, gstart + total, jnp.where(lane == 1, gsz - total, 0.0)).astype(I32)
    gend = gstart + gsz
    te_lanes = te_ref.shape[1]
    gend_w = jnp.concatenate([gend] * (te_lanes // 128), axis=1)
    tile_start = lax.broadcasted_iota(I32, (N_EXPERTS, te_lanes), 1).astype(F32) * TR
    te = jnp.sum(jnp.where(gend_w <= tile_start, 1.0, 0.0), axis=0, keepdims=True)
    used = gend[N_EXPERTS - 1:N_EXPERTS, 0:1] * (1.0 / TR)
    te_lane = lax.broadcasted_iota(I32, (1, te_lanes), 1)
    te = jnp.where(te_lane == n_row_tiles, used, jnp.minimum(te, N_EXPERTS - 1.0))
    te_ref[...] = jnp.broadcast_to(te, te_ref.shape).astype(I32)


def _plan(cnt, n_row_tiles):
    nt = cnt.shape[0]
    assert nt <= 128
    te_lanes = -(-(n_row_tiles + 1) // 128) * 128
    tab = jax.ShapeDtypeStruct((N_EXPERTS, 128), I32)
    return pl.pallas_call(
        functools.partial(_plan_kernel, nt=nt, n_row_tiles=n_row_tiles),
        out_shape=[tab, tab, tab, tab, jax.ShapeDtypeStruct((8, te_lanes), I32)],
        compiler_params=pltpu.CompilerParams(vmem_limit_bytes=VMEM_LIMIT),
        name="moe_plan",
    )(cnt)


def _start_copies(msz_ref, tile, make_copy):
    for e in range(N_EXPERTS):
        make_copy(e, pl.multiple_of(msz_ref[e, tile], ROW_ALIGN)).start()


def _tile_rows(loff_ref, msz_ref, tile):
    return pl.multiple_of(loff_ref[N_EXPERTS - 1, tile] + msz_ref[N_EXPERTS - 1, tile], ROW_ALIGN)


def _dispatch_kernel(off_ref, loff_ref, msz_ref, tail_ref, h2_ref, lpos_ref, xs_hbm, xloc_ref, zero_ref, sem,
                     tail_sem, *, nt):
    i = pl.program_id(0)
    slot = i % 2

    def copy_for(tile, slot_):
        def make(e, m):
            lo = pl.multiple_of(loff_ref[e, tile], ROW_ALIGN)
            of = pl.multiple_of(off_ref[e, tile], ROW_ALIGN)
            return pltpu.make_async_copy(xloc_ref.at[slot_, pl.ds(lo, m)], xs_hbm.at[pl.ds(of, m)], sem.at[slot_])
        return make

    def wait_tile(tile, slot_):
        n = _tile_rows(loff_ref, msz_ref, tile)
        pltpu.make_async_copy(xloc_ref.at[slot_, pl.ds(0, n)], xs_hbm.at[pl.ds(0, n)], sem.at[slot_]).wait()

    lpos = lpos_ref[0].astype(jnp.int16)
    h2 = h2_ref[...]
    ck = 1024
    one, zero = jnp.ones((ck, TB), BF16), jnp.zeros((ck, TB), BF16)
    for c in range(R_LOC // ck):
        r = (lax.broadcasted_iota(I32, (ck, TB), 0) + c * ck).astype(jnp.int16)
        d = zero
        for k in range(TOP_K):
            d = jnp.where(r == lpos[k:k + 1, :], one, d)
        res = _dot(d, h2)
        xloc_ref[slot, c * ck:(c + 1) * ck, :] = res.astype(BF16)

    _start_copies(msz_ref, i, copy_for(i, slot))

    @pl.when(i > 0)
    def _():
        wait_tile(i - 1, 1 - slot)

    @pl.when(i == nt - 1)
    def _():
        zero_ref[...] = jnp.zeros(zero_ref.shape, BF16)

        def tail_copies(start):
            def body(e, carry):
                n = tail_ref[e, 1]

                @pl.when(n > 0)
                def _():
                    st = pl.multiple_of(tail_ref[e, 0], ROW_ALIGN)
                    nn = pl.multiple_of(n, ROW_ALIGN)
                    cp = pltpu.make_async_copy(zero_ref.at[pl.ds(0, nn)], xs_hbm.at[pl.ds(st, nn)], tail_sem)
                    if start:
                        cp.start()
                    else:
                        cp.wait()
                return carry
            lax.fori_loop(0, N_EXPERTS, body, 0)
        tail_copies(True)
        wait_tile(i, slot)
        tail_copies(False)


def _dispatch(off, loff, msz, tail, h2ext, lpos, n_rows):
    nt = lpos.shape[0]
    grid_spec = pltpu.PrefetchScalarGridSpec(
        num_scalar_prefetch=4,
        grid=(nt,),
        in_specs=[pl.BlockSpec((TB, H2W), lambda i, *_: (i, 0)),
                  pl.BlockSpec((1, TOP_K, TB), lambda i, *_: (i, 0, 0))],
        out_specs=pl.BlockSpec(memory_space=pl.ANY),
        scratch_shapes=[pltpu.VMEM((2, R_LOC, H2W), BF16), pltpu.VMEM((TR, H2W), BF16),
                        pltpu.SemaphoreType.DMA((2,)), pltpu.SemaphoreType.DMA],
    )
    return pl.pallas_call(
        functools.partial(_dispatch_kernel, nt=nt),
        grid_spec=grid_spec,
        out_shape=jax.ShapeDtypeStruct((n_rows, H2W), BF16),
        compiler_params=pltpu.CompilerParams(dimension_semantics=("arbitrary",), vmem_limit_bytes=VMEM_LIMIT),
        name="moe_dispatch",
    )(off, loff, msz, tail, h2ext, lpos)


def _expert_kernel(te_ref, xs_ref, wg_ref, wu_ref, wd_ref, ys_ref, wgu_ref, wdb_ref, *, n_row_tiles):
    j = pl.program_id(0)
    e = te_ref[j]

    @pl.when(j < te_ref[n_row_tiles])
    def _():
        @pl.when((j == 0) | (e != te_ref[jnp.maximum(j - 1, 0)]))
        def _():
            wgu_ref[:, :D_EXPERT] = wg_ref[0].astype(BF16)
            wgu_ref[:, D_EXPERT:] = wu_ref[0].astype(BF16)
            wdb_ref[...] = wd_ref[0].astype(BF16)

        x = xs_ref[:, 0:D_MODEL]
        ext = xs_ref[:, D_MODEL:H2W].astype(F32)
        wts = ext[:, :N_EXPERTS] + ext[:, N_EXPERTS:]
        lane = lax.broadcasted_iota(I32, wts.shape, 1)
        w = jnp.sum(jnp.where(lane == e, wts, 0.0), axis=-1, keepdims=True)
        h = _dot(x, wgu_ref[...])
        hg, hu = h[:, :D_EXPERT], h[:, D_EXPERT:]
        act = hg * _sigmoid(hg) * hu * w
        ys_ref[...] = _dot(act.astype(BF16), wdb_ref[...]).astype(BF16)


def _experts(te, xs, wg, wu, wd, n_row_tiles):
    def row_tile(j, te_ref):
        return jnp.maximum(jnp.minimum(j, te_ref[n_row_tiles] - 1), 0)
    grid_spec = pltpu.PrefetchScalarGridSpec(
        num_scalar_prefetch=1,
        grid=(n_row_tiles,),
        in_specs=[pl.BlockSpec((TR, H2W), lambda j, te_ref: (row_tile(j, te_ref), 0)),
                  pl.BlockSpec((1, D_MODEL, D_EXPERT), lambda j, te_ref: (te_ref[row_tile(j, te_ref)], 0, 0)),
                  pl.BlockSpec((1, D_MODEL, D_EXPERT), lambda j, te_ref: (te_ref[row_tile(j, te_ref)], 0, 0)),
                  pl.BlockSpec((1, D_EXPERT, D_MODEL), lambda j, te_ref: (te_ref[row_tile(j, te_ref)], 0, 0))],
        out_specs=pl.BlockSpec((TR, D_MODEL), lambda j, te_ref: (row_tile(j, te_ref), 0)),
        scratch_shapes=[pltpu.VMEM((D_MODEL, 2 * D_EXPERT), BF16), pltpu.VMEM((D_EXPERT, D_MODEL), BF16)],
    )
    return pl.pallas_call(
        functools.partial(_expert_kernel, n_row_tiles=n_row_tiles),
        grid_spec=grid_spec,
        out_shape=jax.ShapeDtypeStruct((n_row_tiles * TR, D_MODEL), BF16),
        compiler_params=pltpu.CompilerParams(dimension_semantics=("arbitrary",), vmem_limit_bytes=VMEM_LIMIT),
        name="moe_experts",
    )(te, xs, wg, wu, wd)


def _combine_kernel(off_ref, loff_ref, msz_ref, lpos_ref, h2_ref, xp_ref, xs_ref, mod_ref, wgs_ref, wus_ref, wds_ref,
                    fnw_ref, ysrt_hbm, yp_ref, ys_ref, yloc_ref, acc_ref, sem, *, nt, n_p_tiles, tiles_per_mod,
                    final):
    i = pl.program_id(0)
    slot = i % 2

    def copy_for(tile, slot_):
        def make(e, m):
            lo = pl.multiple_of(loff_ref[e, tile], ROW_ALIGN)
            of = pl.multiple_of(off_ref[e, tile], ROW_ALIGN)
            return pltpu.make_async_copy(ysrt_hbm.at[pl.ds(of, m)], yloc_ref.at[slot_, pl.ds(lo, m)], sem.at[slot_])
        return make

    @pl.when(i == 0)
    def _():
        yloc_ref[...] = jnp.zeros(yloc_ref.shape, BF16)
        _start_copies(msz_ref, 0, copy_for(0, 0))

    nxt = jnp.minimum(i + 1, nt - 1)
    _start_copies(msz_ref, nxt, copy_for(nxt, 1 - slot))

    hb = h2_ref[...]
    hg = _dot(hb, wgs_ref[...].astype(BF16))
    hu = _dot(hb, wus_ref[...].astype(BF16))
    acc_ref[...] = _dot((hg * _sigmoid(hg) * hu).astype(BF16), wds_ref[...].astype(BF16))

    def wait_tile(tile, slot_):
        n = _tile_rows(loff_ref, msz_ref, tile)
        pltpu.make_async_copy(ysrt_hbm.at[pl.ds(0, n)], yloc_ref.at[slot_, pl.ds(0, n)], sem.at[slot_]).wait()

    wait_tile(i, slot)

    @pl.when(i == nt - 1)
    def _():
        wait_tile(i, 1 - slot)

    lpos_pad = jnp.concatenate([lpos_ref[0].astype(F32), jnp.zeros((128 - TOP_K, TB), F32)], axis=0)
    lposc = lpos_pad.T.astype(I32)
    ck = 512
    cols = [jnp.broadcast_to(lposc[:, k:k + 1], (TB, ck)).astype(jnp.int16) for k in range(TOP_K)]
    one, zero = jnp.ones((TB, ck), BF16), jnp.zeros((TB, ck), BF16)
    for c in range(R_LOC // ck):
        r = (lax.broadcasted_iota(I32, (TB, ck), 1) + c * ck).astype(jnp.int16)
        cm = zero
        for k in range(TOP_K):
            cm = jnp.where(r == cols[k], one, cm)
        acc_ref[...] += _dot(cm, yloc_ref[slot, c * ck:(c + 1) * ck, :])

    row = _mod_row(i, n_p_tiles, tiles_per_mod)
    g2 = mod_ref[pl.ds(row, 1), 5 * D_MODEL:6 * D_MODEL]
    x2 = _select_x(i, n_p_tiles, xp_ref, xs_ref) + g2 * acc_ref[...]
    y = _rms(x2) * fnw_ref[...] if final else x2

    @pl.when(i < n_p_tiles)
    def _():
        yp_ref[...] = y

    @pl.when(i >= n_p_tiles)
    def _():
        ys_ref[...] = y


def _combine(off, loff, msz, lpos, h2ext, x1p, x1s, mod, wgs, wus, wds, fnw, ysorted, *, tiles_per_mod, final):
    n_p, n_s = x1p.shape[0] // TB, x1s.shape[0] // TB
    nt = n_p + n_s
    const = lambda shape: pl.BlockSpec(shape, lambda i, *_: (0,) * len(shape), pipeline_mode=pl.Buffered(1))
    p_idx = lambda i, *_: (jnp.minimum(i, n_p - 1), 0)
    s_idx = lambda i, *_: (jnp.maximum(i - n_p, 0), 0)
    grid_spec = pltpu.PrefetchScalarGridSpec(
        num_scalar_prefetch=3,
        grid=(nt,),
        in_specs=[pl.BlockSpec((1, TOP_K, TB), lambda i, *_: (i, 0, 0)),
                  pl.BlockSpec((TB, D_MODEL), lambda i, *_: (i, 0)),
                  pl.BlockSpec((TB, D_MODEL), p_idx), pl.BlockSpec((TB, D_MODEL), s_idx),
                  const(mod.shape), const((D_MODEL, D_EXPERT)), const((D_MODEL, D_EXPERT)), const((D_EXPERT, D_MODEL)),
                  const((1, D_MODEL)), pl.BlockSpec(memory_space=pl.ANY)],
        out_specs=[pl.BlockSpec((TB, D_MODEL), p_idx), pl.BlockSpec((TB, D_MODEL), s_idx)],
        scratch_shapes=[pltpu.VMEM((2, R_LOC, D_MODEL), BF16), pltpu.VMEM((TB, D_MODEL), F32),
                        pltpu.SemaphoreType.DMA((2,))],
    )
    kern = functools.partial(_combine_kernel, nt=nt, n_p_tiles=n_p, tiles_per_mod=tiles_per_mod, final=final)
    return pl.pallas_call(
        kern,
        grid_spec=grid_spec,
        out_shape=[jax.ShapeDtypeStruct(x1p.shape, F32), jax.ShapeDtypeStruct(x1s.shape, F32)],
        compiler_params=pltpu.CompilerParams(dimension_semantics=("arbitrary",), vmem_limit_bytes=VMEM_LIMIT),
        name="moe_combine",
    )(off, loff, msz, lpos, h2ext, x1p, x1s, mod, wgs, wus, wds, fnw, ysorted)


def _moe(x1p, x1s, mod, n2w, w_router, b_router, wg, wu, wd, wgs, wus, wds, fnw, *, tokens_per_mod, final):
    assert R_LOC >= TB * TOP_K + N_EXPERTS * ROW_ALIGN and tokens_per_mod % TB == 0
    nt = (x1p.shape[0] + x1s.shape[0]) // TB
    n_rows_max = nt * TB * TOP_K + nt * N_EXPERTS * ROW_ALIGN + N_EXPERTS * (TR - ROW_ALIGN)
    n_row_tiles = -(-n_rows_max // TR)
    tiles_per_mod = tokens_per_mod // TB
    brb = jnp.broadcast_to(b_router.reshape(N_EXPERTS, 1), (N_EXPERTS, TB))
    h2ext, lpos, cnt = _route(x1p, x1s, mod, n2w, w_router.T, brb, tiles_per_mod=tiles_per_mod)
    off, loff, msz, tail, te = _plan(cnt, n_row_tiles)
    xs = _dispatch(off, loff, msz, tail, h2ext, lpos, n_row_tiles * TR)
    ysorted = _experts(te[0], xs, wg, wu, wd, n_row_tiles)
    return _combine(off, loff, msz, lpos, h2ext, x1p, x1s, mod, wgs, wus, wds, fnw, ysorted,
                    tiles_per_mod=tiles_per_mod, final=final)


def kernel(x_prompt, x_sample, state_gla, c, c_ctx, w_mod, b_mod, norm1_w, w_in, conv_w, conv_b, w_decay, b_decay,
           gla_norm_w, w_br_conv, w_br_gla, w_out, norm2_w, w_router, b_router, w_gate_e, w_up_e, w_down_e,
           w_gate_s, w_up_s, w_down_s, final_norm_w):
    depth = w_mod.shape[0]
    nb_p, len_p, _ = x_prompt.shape
    nb_s, len_s, _ = x_sample.shape
    yp = x_prompt.reshape(nb_p * len_p, D_MODEL)
    ys = x_sample.reshape(nb_s * len_s, D_MODEL)
    fnw = final_norm_w.reshape(1, D_MODEL)

    cond = jnp.concatenate([c_ctx[None, :], c, jnp.zeros((8 - 1 - nb_s, D_MODEL), F32)], axis=0)
    states = []
    for l in range(depth):
        mod = _modulation(cond, w_mod[l], b_mod[l].reshape(1, -1))
        mod_p, mod_s = mod[0:1], mod[1:1 + nb_s]

        w_in_r = _w_in_prep(w_in, l)
        wdec = jnp.zeros((2, LR_PAD, D_GLA), F32)
        wdec = wdec.at[0, 0:GLA_RANK].set(w_decay[l, 0]).at[1, GLA_RANK:2 * GLA_RANK].set(w_decay[l, 1])
        n1w = norm1_w[l].reshape(1, D_MODEL)
        mix_w = (conv_w[l], conv_b[l].reshape(1, D_CONV), wdec, b_decay[l], gla_norm_w[l].reshape(1, D_GLA),
                 w_br_conv[l].astype(BF16), w_br_gla[l].astype(BF16), w_out[l].astype(BF16))
        moe_w = (norm2_w[l].reshape(1, D_MODEL), w_router[l], b_router[l],
                 w_gate_e[l], w_up_e[l], w_down_e[l], w_gate_s[l], w_up_s[l], w_down_s[l])

        proj_p = _in_proj(yp, mod_p, n1w, w_in_r, rows_per_mod=nb_p * len_p)
        yp, st = _mixer(proj_p, yp, mod_p, *mix_w, None, seq_len=len_p, row_len=len_p, emit_state=True)
        states.append(st)
        proj_s = _in_proj(ys, mod_s, n1w, w_in_r, rows_per_mod=len_s)
        (ys,) = _mixer(proj_s, ys, mod_s, *mix_w, state_gla[:, l], seq_len=len_s, row_len=GRID_W, emit_state=False)

        yp, ys = _moe(yp, ys, mod, *moe_w, fnw, tokens_per_mod=len_s, final=l == depth - 1)
    new_state = jnp.stack(states, axis=1)
    return (yp.reshape(nb_p, len_p, D_MODEL), ys.reshape(nb_s, len_s, D_MODEL), new_state)
```

```python
import functools

import jax
import jax.numpy as jnp
from jax import lax
from jax.experimental import pallas as pl
from jax.experimental.pallas import tpu as pltpu

F32 = jnp.float32
BF16 = jnp.bfloat16

D_MODEL = 1024
GRID_W = 64
D_CONV = 512
N_HEADS = 4
HEAD_D = 128
D_GLA = N_HEADS * HEAD_D
GLA_RANK = 16
GLA_GATE_NORM = 16.0
LOG2_E = 1.4426950408889634
CHUNK = 64
SUB = 8
N_SUB = CHUNK // SUB
N_EXPERTS = 64
TOP_K = 8
D_EXPERT = 256
ROUTED_SCALE = 2.5
EPS = 1e-6

C_U, C_GB, C_GC, C_Q, C_K, C_V, C_GO = 0, 512, 1024, 1536, 2048, 2560, 3072
C_BRC, C_BRG, C_LR = 3584, 4608, 5632
D_PROJ = 5760
LR_PAD = 128

VMEM_LIMIT = 56 * 1024 * 1024


def _dot(a, b):
    return jnp.dot(a, b, preferred_element_type=F32)


def _dot_nt(a, b):
    return lax.dot_general(a, b, (((1,), (1,)), ((), ())), preferred_element_type=F32)


def _dot_tn(a, b):
    return lax.dot_general(a, b, (((0,), (0,)), ((), ())), preferred_element_type=F32)


def _dot_hi(a, b):
    return jnp.dot(a, b, preferred_element_type=F32, precision=lax.Precision.HIGHEST)


def _split_bf16(x):
    hi = x.astype(BF16)
    lo = (x - hi.astype(F32)).astype(BF16)
    return hi, lo


def _rms(x):
    return x * lax.rsqrt(jnp.mean(x * x, axis=-1, keepdims=True) + EPS)


def _sigmoid(x):
    return 0.5 * jnp.tanh(0.5 * x) + 0.5


def _mod_kernel(cond_ref, w_ref, b_ref, o_ref):
    c = cond_ref[...]
    o_ref[...] = _dot_hi(c * jax.nn.sigmoid(c), w_ref[...]) + b_ref[...]


def _modulation(cond, w_mod, b_mod):
    n_rows = cond.shape[0]
    tn = 1536
    return pl.pallas_call(
        _mod_kernel,
        grid=(6 * D_MODEL // tn,),
        in_specs=[pl.BlockSpec((n_rows, D_MODEL), lambda j: (0, 0)),
                  pl.BlockSpec((D_MODEL, tn), lambda j: (0, j)),
                  pl.BlockSpec((1, tn), lambda j: (0, j))],
        out_specs=pl.BlockSpec((n_rows, tn), lambda j: (0, j)),
        out_shape=jax.ShapeDtypeStruct((n_rows, 6 * D_MODEL), F32),
        compiler_params=pltpu.CompilerParams(dimension_semantics=("arbitrary",),
                                             vmem_limit_bytes=VMEM_LIMIT),
        name="modulation",
    )(cond, w_mod, b_mod)


D_IN_PROJ = 5664
C_LR_SRC, C_GATES_SRC = 3584, 3616


def _w_in_prep_kernel(w_ref, o_ref):
    rc = 64

    def copy_rows(dst0, src0, n):
        def body(t, carry):
            off = pl.multiple_of(t * rc, rc)
            o_ref[pl.ds(dst0 + off, rc), :] = w_ref[pl.ds(src0 + off, rc), :].astype(BF16)
            return carry
        lax.fori_loop(0, n // rc, body, 0)

    copy_rows(0, 0, C_LR_SRC)
    copy_rows(C_BRC, C_GATES_SRC, 2 * D_MODEL)
    o_ref[C_LR:C_LR + 2 * GLA_RANK, :] = w_ref[C_LR_SRC:C_GATES_SRC, :].astype(BF16)
    o_ref[C_LR + 2 * GLA_RANK:D_PROJ, :] = jnp.zeros((D_PROJ - C_LR - 2 * GLA_RANK, D_MODEL), BF16)


def _w_in_prep(w_in_t):
    return pl.pallas_call(
        _w_in_prep_kernel,
        out_shape=jax.ShapeDtypeStruct((D_PROJ, D_MODEL), BF16),
        compiler_params=pltpu.CompilerParams(vmem_limit_bytes=VMEM_LIMIT),
        name="w_in_prep",
    )(w_in_t)


def _inproj_kernel(x_ref, mod_ref, nw_ref, w_ref, o_ref, *, rows_per_mod, tm):
    i = pl.program_id(1)
    row = (i * tm) // rows_per_mod
    sh = mod_ref[pl.ds(row, 1), 0:D_MODEL]
    sc = mod_ref[pl.ds(row, 1), D_MODEL:2 * D_MODEL]
    h = _rms(x_ref[...]) * nw_ref[...] * (1.0 + sc) + sh
    o_ref[...] = _dot_nt(h.astype(BF16), w_ref[...]).astype(BF16)


def _in_proj(x2d, mod, norm_w, w_in_r, rows_per_mod):
    t = x2d.shape[0]
    tm, tn = 1024, 1920
    kern = functools.partial(_inproj_kernel, rows_per_mod=rows_per_mod, tm=tm)
    return pl.pallas_call(
        kern,
        grid=(D_PROJ // tn, t // tm),
        in_specs=[pl.BlockSpec((tm, D_MODEL), lambda j, i: (i, 0)),
                  pl.BlockSpec(mod.shape, lambda j, i: (0, 0)),
                  pl.BlockSpec((1, D_MODEL), lambda j, i: (0, 0)),
                  pl.BlockSpec((tn, D_MODEL), lambda j, i: (j, 0))],
        out_specs=pl.BlockSpec((tm, tn), lambda j, i: (i, j)),
        out_shape=jax.ShapeDtypeStruct((t, D_PROJ), BF16),
        compiler_params=pltpu.CompilerParams(dimension_semantics=("arbitrary", "arbitrary"),
                                             vmem_limit_bytes=VMEM_LIMIT),
        name="in_proj",
    )(x2d, mod, norm_w, w_in_r)


def _log2_sigmoid(x):
    return jnp.minimum(x, 0.0) * LOG2_E - jnp.log2(1.0 + jnp.exp2(jnp.abs(x) * (-LOG2_E)))


def _gla_chunk_head(qc, kc, vc, bc, st, rev):
    lane = lax.broadcasted_iota(jnp.int32, (SUB, CHUNK), 1)
    sub = lax.broadcasted_iota(jnp.int32, (SUB, CHUNK), 0)
    tot = bc[0:1] if rev else bc[CHUNK - 1:CHUNK]

    o = _dot_nt((qc * jnp.exp2(bc)).astype(BF16), st.astype(BF16))
    k_tail = kc * jnp.exp2(tot - bc)
    st_new = st * jnp.exp2(tot) + _dot_tn(vc.astype(BF16), k_tail.astype(BF16))

    lhs_segs, rhs_segs = [], []

    def rows(before, mid, after):
        parts = ([jnp.zeros((before, HEAD_D), F32)] if before else []) + [mid]
        parts += [jnp.zeros((after, HEAD_D), F32)] if after else []
        return jnp.concatenate(parts, axis=0) if len(parts) > 1 else mid

    key_blocks = range(1, N_SUB) if rev else range(0, N_SUB - 1)
    for jb in key_blocks:
        r0 = jb * SUB
        ref_row = bc[r0:r0 + 1] if rev else bc[r0 + SUB - 1:r0 + SUB]
        ke = kc[r0:r0 + SUB] * jnp.exp2(ref_row - bc[r0:r0 + SUB])
        rhs_segs.append(rows(r0, ke, CHUNK - r0 - SUB))
        if rev:
            ql = qc[:r0] * jnp.exp2(bc[:r0] - ref_row)
            lhs_segs.append(rows(0, ql, CHUNK - r0))
        else:
            ql = qc[r0 + SUB:] * jnp.exp2(bc[r0 + SUB:] - ref_row)
            lhs_segs.append(rows(r0 + SUB, ql, 0))
    far = _dot_nt(jnp.concatenate(lhs_segs, axis=1).astype(BF16),
                  jnp.concatenate(rhs_segs, axis=1).astype(BF16))

    blocks = []
    for ib in range(N_SUB):
        r0 = ib * SUB
        qi, bi = qc[r0:r0 + SUB], bc[r0:r0 + SUB]
        acc = jnp.zeros((SUB, CHUNK), F32)
        for jj in range(SUB):
            j = r0 + jj
            e = jnp.exp2(bi - bc[j:j + 1])
            col = jnp.sum(qi * (kc[j:j + 1] * e), axis=-1, keepdims=True)
            acc = jnp.where(lane == j, col, acc)
        keep = (lane - r0 >= sub) if rev else (lane - r0 <= sub)
        blocks.append(jnp.where(keep, acc, 0.0))
    scores = far + jnp.concatenate(blocks, axis=0)
    o = o + _dot(scores.astype(BF16), vc.astype(BF16))
    return o, st_new


def _mixer_kernel(*refs, seq_len, row_len, has_s0, emit_state):
    it = iter(refs)
    proj_ref, x_ref, mod_ref, cw_ref, cb_ref, wdec_ref, bdec_ref, gnw_ref = (next(it) for _ in range(8))
    wbc_ref, wbg_ref, wout_ref = (next(it) for _ in range(3))
    s0_ref = next(it) if has_s0 else None
    out_ref = next(it)
    st_out_ref = next(it) if emit_state else None
    la_f_ref, la_b_ref, o_ref, st_ref = (next(it) for _ in range(4))

    L = seq_len
    n_chunks = L // CHUNK
    tr = 256
    assert tr % row_len == 0 and L % tr == 0

    ri = lax.broadcasted_iota(jnp.int32, (tr, tr), 0)
    ci = lax.broadcasted_iota(jnp.int32, (tr, tr), 1)
    same_chunk = (ri // CHUNK) == (ci // CHUNK)

    def decay_body(t, carry):
        r0 = pl.multiple_of(t * tr, tr)
        lr = proj_ref[pl.ds(r0, tr), C_LR:C_LR + LR_PAD]
        for d, ref in ((0, la_f_ref), (1, la_b_ref)):
            whi, wlo = _split_bf16(wdec_ref[d])
            z = _dot(lr, whi) + _dot(lr, wlo) + bdec_ref[d:d + 1]
            la_hi, la_lo = _split_bf16(_log2_sigmoid(z) * (1.0 / GLA_GATE_NORM))
            tri = jnp.where(same_chunk & ((ci >= ri) if d else (ci <= ri)), 1.0, 0.0).astype(BF16)
            ref[pl.ds(r0, tr), :] = _dot(tri, la_hi) + _dot(tri, la_lo)
        return carry
    lax.fori_loop(0, L // tr, decay_body, 0)

    for rev, la_ref in ((False, la_f_ref), (True, la_b_ref)):
        d = 1 if rev else 0
        for h in range(N_HEADS):
            if has_s0:
                st_ref[h] = s0_ref[0, d, h].T
            else:
                st_ref[h] = jnp.zeros((HEAD_D, HEAD_D), F32)

        def chunk_body(c, carry, rev=rev, la_ref=la_ref):
            cc = (n_chunks - 1 - c) if rev else c
            r0 = pl.multiple_of(cc * CHUNK, CHUNK)
            for h in range(N_HEADS):
                lo, hi = h * HEAD_D, (h + 1) * HEAD_D
                qc = proj_ref[pl.ds(r0, CHUNK), C_Q + lo:C_Q + hi].astype(F32) * (HEAD_D ** -0.5)
                kc = proj_ref[pl.ds(r0, CHUNK), C_K + lo:C_K + hi].astype(F32)
                vc = proj_ref[pl.ds(r0, CHUNK), C_V + lo:C_V + hi].astype(F32)
                oc, st_new = _gla_chunk_head(qc, kc, vc, la_ref[pl.ds(r0, CHUNK), lo:hi], st_ref[h], rev)
                st_ref[h] = st_new
                if rev:
                    o_ref[pl.ds(r0, CHUNK), lo:hi] += oc
                else:
                    o_ref[pl.ds(r0, CHUNK), lo:hi] = oc
            return carry
        lax.fori_loop(0, n_chunks, chunk_body, 0, unroll=2)

        if emit_state:
            for h in range(N_HEADS):
                st_out_ref[0, d, h] = st_ref[h].T

    mod_row = pl.program_id(0) if mod_ref.shape[0] > 1 else 0
    g1 = mod_ref[pl.ds(mod_row, 1), 2 * D_MODEL:3 * D_MODEL]
    hsel_r = lax.broadcasted_iota(jnp.int32, (D_GLA, D_GLA), 0) // HEAD_D
    hsel_c = lax.broadcasted_iota(jnp.int32, (D_GLA, D_GLA), 1) // HEAD_D
    head_avg = jnp.where(hsel_r == hsel_c, 1.0 / HEAD_D, 0.0).astype(BF16)
    pos = lax.broadcasted_iota(jnp.int32, (tr, 1), 0)

    def dense_body(t, carry):
        r0 = pl.multiple_of(t * tr, tr)
        rows = pl.ds(r0, tr)
        cu = (proj_ref[rows, C_GC:C_GC + D_CONV].astype(F32) * proj_ref[rows, C_U:C_U + D_CONV].astype(F32))
        in_row = pos % row_len
        left = jnp.where(in_row == 0, 0.0, pltpu.roll(cu, 1, axis=0))
        right = jnp.where(in_row == row_len - 1, 0.0, pltpu.roll(cu, tr - 1, axis=0))
        conv = cw_ref[0:1] * left + cw_ref[1:2] * cu + cw_ref[2:3] * right + cb_ref[...]
        y_conv = proj_ref[rows, C_GB:C_GB + D_CONV].astype(F32) * conv

        o = o_ref[rows, :]
        osq_hi, osq_lo = _split_bf16(o * o)
        ms = _dot(osq_hi, head_avg) + _dot(osq_lo, head_avg)
        g_out = proj_ref[rows, C_GO:C_GO + D_GLA].astype(F32)
        y_gla = o * lax.rsqrt(ms + EPS) * gnw_ref[...] * (g_out * _sigmoid(g_out))

        merged = (_sigmoid(proj_ref[rows, C_BRC:C_BRC + D_MODEL].astype(F32)) * _dot(y_conv.astype(BF16), wbc_ref[...])
                  + _sigmoid(proj_ref[rows, C_BRG:C_BRG + D_MODEL].astype(F32)) * _dot(y_gla.astype(BF16), wbg_ref[...]))
        out_ref[rows, :] = x_ref[rows, :] + g1 * _dot(merged.astype(BF16), wout_ref[...])
        return carry
    lax.fori_loop(0, L // tr, dense_body, 0)


def _mixer(proj, x2d, mod, conv_w, conv_b, wdec, bdec, gnw, wbc, wbg, wout, s0, *, seq_len, row_len, emit_state):
    t = x2d.shape[0]
    nb = t // seq_len
    has_s0 = s0 is not None
    one = pl.Buffered(1)
    const = lambda shape: pl.BlockSpec(shape, lambda b: (0,) * len(shape), pipeline_mode=one)
    in_specs = [pl.BlockSpec((seq_len, D_PROJ), lambda b: (b, 0), pipeline_mode=one),
                pl.BlockSpec((seq_len, D_MODEL), lambda b: (b, 0)),
                const(mod.shape),
                const((3, D_CONV)), const((1, D_CONV)), const((2, LR_PAD, D_GLA)), const((2, D_GLA)),
                const((1, D_GLA)), const((D_CONV, D_MODEL)), const((D_GLA, D_MODEL)), const((D_MODEL, D_MODEL))]
    args = [proj, x2d, mod, conv_w, conv_b, wdec, bdec, gnw, wbc, wbg, wout]
    if has_s0:
        in_specs.append(pl.BlockSpec((1, 2, N_HEADS, HEAD_D, HEAD_D), lambda b: (b, 0, 0, 0, 0)))
        args.append(s0)
    out_specs = [pl.BlockSpec((seq_len, D_MODEL), lambda b: (b, 0))]
    out_shape = [jax.ShapeDtypeStruct((t, D_MODEL), F32)]
    if emit_state:
        out_specs.append(pl.BlockSpec((1, 2, N_HEADS, HEAD_D, HEAD_D), lambda b: (b, 0, 0, 0, 0)))
        out_shape.append(jax.ShapeDtypeStruct((nb, 2, N_HEADS, HEAD_D, HEAD_D), F32))
    kern = functools.partial(_mixer_kernel, seq_len=seq_len, row_len=row_len, has_s0=has_s0, emit_state=emit_state)
    return pl.pallas_call(
        kern,
        grid=(nb,),
        in_specs=in_specs,
        out_specs=out_specs,
        out_shape=out_shape,
        scratch_shapes=[pltpu.VMEM((seq_len, D_GLA), F32), pltpu.VMEM((seq_len, D_GLA), F32),
                        pltpu.VMEM((seq_len, D_GLA), F32), pltpu.VMEM((N_HEADS, HEAD_D, HEAD_D), F32)],
        compiler_params=pltpu.CompilerParams(dimension_semantics=("arbitrary",),
                                             vmem_limit_bytes=VMEM_LIMIT),
        name="mixer",
    )(*args)


I32 = jnp.int32
TB = 256
ROW_ALIGN = 16
TR = 1024
R_LOC = 3072
H2W = 1152


def _select_x(i, n_p_tiles, xp_ref, xs_ref):
    return jnp.where(i < n_p_tiles, xp_ref[...], xs_ref[...])


def _mod_row(i, n_p_tiles, tiles_per_mod):
    return jnp.where(i < n_p_tiles, 0, 1 + (i - n_p_tiles) // tiles_per_mod)


def _route_kernel(xp_ref, xs_ref, mod_ref, n2w_ref, wrt_ref, brb_ref, h2_ref, lpos_ref, cnt_ref, *,
                  n_p_tiles, tiles_per_mod):
    i = pl.program_id(0)
    row = _mod_row(i, n_p_tiles, tiles_per_mod)
    x = _select_x(i, n_p_tiles, xp_ref, xs_ref)
    sh = mod_ref[pl.ds(row, 1), 3 * D_MODEL:4 * D_MODEL]
    sc = mod_ref[pl.ds(row, 1), 4 * D_MODEL:5 * D_MODEL]
    h2 = _rms(x) * n2w_ref[...] * (1.0 + sc) + sh

    h_hi, h_lo = _split_bf16(h2)
    w_hi, w_lo = _split_bf16(wrt_ref[...])
    scores = jax.nn.sigmoid(_dot_nt(w_hi, h_hi) + (_dot_nt(w_hi, h_lo) + _dot_nt(w_lo, h_hi)))
    biased = scores + brb_ref[...]
    eidx = lax.broadcasted_iota(I32, scores.shape, 0)
    picks = []
    for _k in range(TOP_K):
        m = jnp.max(biased, axis=0, keepdims=True)
        first = jnp.min(jnp.where(biased == m, eidx, N_EXPERTS), axis=0, keepdims=True)
        pick = eidx == first
        picks.append(pick)
        biased = jnp.where(pick, -jnp.inf, biased)
    sel = jnp.zeros(scores.shape, F32)
    for pick in picks:
        sel = jnp.where(pick, 1.0, sel)
    selsc = sel * scores
    comb = selsc / jnp.sum(selsc, axis=0, keepdims=True) * ROUTED_SCALE

    selb = sel.astype(BF16)
    tr_ = lax.broadcasted_iota(I32, (TB, TB), 0)
    tc_ = lax.broadcasted_iota(I32, (TB, TB), 1)
    rank = _dot(selb, jnp.where(tr_ < tc_, 1.0, 0.0).astype(BF16))
    n_b = _dot(selb, jnp.ones((TB, 128), BF16))
    m_b = jnp.maximum(jnp.floor((n_b + (ROW_ALIGN - 1)) * (1.0 / ROW_ALIGN)), 1.0) * ROW_ALIGN
    er_ = lax.broadcasted_iota(I32, (N_EXPERTS, N_EXPERTS), 0)
    ec_ = lax.broadcasted_iota(I32, (N_EXPERTS, N_EXPERTS), 1)
    loff_b = _dot(jnp.where(ec_ < er_, 1.0, 0.0).astype(BF16), m_b.astype(BF16))
    lposf = jnp.concatenate([loff_b] * (TB // 128), axis=1) + rank
    rows = [jnp.sum(jnp.where(pick, lposf, 0.0), axis=0, keepdims=True) for pick in picks]
    lpos_ref[0] = jnp.concatenate(rows, axis=0).astype(I32)
    cnt_ref[0] = m_b

    combt = comb.T
    chi = combt.astype(BF16).astype(F32)
    h2_ref[:, 0:D_MODEL] = h_hi
    h2_ref[:, D_MODEL:H2W] = jnp.concatenate([chi, combt - chi], axis=1).astype(BF16)


def _route(x1p, x1s, mod, n2w, w_router_t, b_router_b, *, tiles_per_mod):
    n_p, n_s = x1p.shape[0] // TB, x1s.shape[0] // TB
    nt = n_p + n_s
    kern = functools.partial(_route_kernel, n_p_tiles=n_p, tiles_per_mod=tiles_per_mod)
    const = lambda shape: pl.BlockSpec(shape, lambda i: (0,) * len(shape))
    return pl.pallas_call(
        kern,
        grid=(nt,),
        in_specs=[pl.BlockSpec((TB, D_MODEL), lambda i: (jnp.minimum(i, n_p - 1), 0)),
                  pl.BlockSpec((TB, D_MODEL), lambda i: (jnp.maximum(i - n_p, 0), 0)),
                  const(mod.shape), const((1, D_MODEL)), const((N_EXPERTS, D_MODEL)), const((N_EXPERTS, TB))],
        out_specs=[pl.BlockSpec((TB, H2W), lambda i: (i, 0)),
                   pl.BlockSpec((1, TOP_K, TB), lambda i: (i, 0, 0)),
                   pl.BlockSpec((1, N_EXPERTS, 128), lambda i: (i, 0, 0))],
        out_shape=[jax.ShapeDtypeStruct((nt * TB, H2W), BF16),
                   jax.ShapeDtypeStruct((nt, TOP_K, TB), I32),
                   jax.ShapeDtypeStruct((nt, N_EXPERTS, 128), F32)],
        compiler_params=pltpu.CompilerParams(dimension_semantics=("arbitrary",), vmem_limit_bytes=VMEM_LIMIT),
        name="moe_route",
    )(x1p, x1s, mod, n2w, w_router_t, b_router_b)


def _plan_kernel(cnt_ref, off_ref, loff_ref, msz_ref, tail_ref, te_ref, *, nt, n_row_tiles):
    lane = lax.broadcasted_iota(I32, (N_EXPERTS, 128), 1)
    m = jnp.zeros((N_EXPERTS, 128), F32)
    for i in range(nt):
        m = jnp.where(lane == i, cnt_ref[i], m)
    total = jnp.broadcast_to(jnp.sum(m, axis=1, keepdims=True), (N_EXPERTS, 128))
    gsz = jnp.floor((total + (TR - 1)) * (1.0 / TR)) * TR
    er_ = lax.broadcasted_iota(I32, (N_EXPERTS, N_EXPERTS), 0)
    ec_ = lax.broadcasted_iota(I32, (N_EXPERTS, N_EXPERTS), 1)
    lstrict = jnp.where(ec_ < er_, 1.0, 0.0)
    ir_ = lax.broadcasted_iota(I32, (128, 128), 0)
    ic_ = lax.broadcasted_iota(I32, (128, 128), 1)
    ustrict = jnp.where(ir_ < ic_, 1.0, 0.0)
    gstart = _dot_hi(lstrict, gsz)
    off_ref[...] = (gstart + _dot_hi(m, ustrict)).astype(I32)
    loff_ref[...] = _dot_hi(lstrict, m).astype(I32)
    msz_ref[...] = m.astype(I32)
    tail_ref[...] = jnp.where(lane == 0, gstart + total, jnp.where(lane == 1, gsz - total, 0.0)).astype(I32)
    gend = gstart + gsz
    te_lanes = te_ref.shape[1]
    gend_w = jnp.concatenate([gend] * (te_lanes // 128), axis=1)
    tile_start = lax.broadcasted_iota(I32, (N_EXPERTS, te_lanes), 1).astype(F32) * TR
    te = jnp.sum(jnp.where(gend_w <= tile_start, 1.0, 0.0), axis=0, keepdims=True)
    used = gend[N_EXPERTS - 1:N_EXPERTS, 0:1] * (1.0 / TR)
    te_lane = lax.broadcasted_iota(I32, (1, te_lanes), 1)
    te = jnp.where(te_lane == n_row_tiles, used, jnp.minimum(te, N_EXPERTS - 1.0))
    te_ref[...] = jnp.broadcast_to(te, te_ref.shape).astype(I32)


def _plan(cnt, n_row_tiles):
    nt = cnt.shape[0]
    assert nt <= 128
    te_lanes = -(-(n_row_tiles + 1) // 128) * 128
    tab = jax.ShapeDtypeStruct((N_EXPERTS, 128), I32)
    return pl.pallas_call(
        functools.partial(_plan_kernel, nt=nt, n_row_tiles=n_row_tiles),
        out_shape=[tab, tab, tab, tab, jax.ShapeDtypeStruct((8, te_lanes), I32)],
        compiler_params=pltpu.CompilerParams(vmem_limit_bytes=VMEM_LIMIT),
        name="moe_plan",
    )(cnt)


def _start_copies(msz_ref, tile, make_copy):
    for e in range(N_EXPERTS):
        make_copy(e, pl.multiple_of(msz_ref[e, tile], ROW_ALIGN)).start()


def _tile_rows(loff_ref, msz_ref, tile):
    return pl.multiple_of(loff_ref[N_EXPERTS - 1, tile] + msz_ref[N_EXPERTS - 1, tile], ROW_ALIGN)


def _dispatch_kernel(off_ref, loff_ref, msz_ref, tail_ref, h2_ref, lpos_ref, xs_hbm, xloc_ref, zero_ref, sem,
                     tail_sem, *, nt):
    i = pl.program_id(0)
    slot = i % 2

    def copy_for(tile, slot_):
        def make(e, m):
            lo = pl.multiple_of(loff_ref[e, tile], ROW_ALIGN)
            of = pl.multiple_of(off_ref[e, tile], ROW_ALIGN)
            return pltpu.make_async_copy(xloc_ref.at[slot_, pl.ds(lo, m)], xs_hbm.at[pl.ds(of, m)], sem.at[slot_])
        return make

    def wait_tile(tile, slot_):
        n = _tile_rows(loff_ref, msz_ref, tile)
        pltpu.make_async_copy(xloc_ref.at[slot_, pl.ds(0, n)], xs_hbm.at[pl.ds(0, n)], sem.at[slot_]).wait()

    lpos = lpos_ref[0].astype(jnp.int16)
    h2 = h2_ref[...]
    ck = 1024
    one, zero = jnp.ones((ck, TB), BF16), jnp.zeros((ck, TB), BF16)
    for c in range(R_LOC // ck):
        r = (lax.broadcasted_iota(I32, (ck, TB), 0) + c * ck).astype(jnp.int16)
        d = zero
        for k in range(TOP_K):
            d = jnp.where(r == lpos[k:k + 1, :], one, d)
        res = _dot(d, h2)
        xloc_ref[slot, c * ck:(c + 1) * ck, :] = res.astype(BF16)

    _start_copies(msz_ref, i, copy_for(i, slot))

    @pl.when(i > 0)
    def _():
        wait_tile(i - 1, 1 - slot)

    @pl.when(i == nt - 1)
    def _():
        zero_ref[...] = jnp.zeros(zero_ref.shape, BF16)

        def tail_copies(start):
            def body(e, carry):
                n = tail_ref[e, 1]

                @pl.when(n > 0)
                def _():
                    st = pl.multiple_of(tail_ref[e, 0], ROW_ALIGN)
                    nn = pl.multiple_of(n, ROW_ALIGN)
                    cp = pltpu.make_async_copy(zero_ref.at[pl.ds(0, nn)], xs_hbm.at[pl.ds(st, nn)], tail_sem)
                    if start:
                        cp.start()
                    else:
                        cp.wait()
                return carry
            lax.fori_loop(0, N_EXPERTS, body, 0)
        tail_copies(True)
        wait_tile(i, slot)
        tail_copies(False)


def _dispatch(off, loff, msz, tail, h2ext, lpos, n_rows):
    nt = lpos.shape[0]
    grid_spec = pltpu.PrefetchScalarGridSpec(
        num_scalar_prefetch=4,
        grid=(nt,),
        in_specs=[pl.BlockSpec((TB, H2W), lambda i, *_: (i, 0)),
                  pl.BlockSpec((1, TOP_K, TB), lambda i, *_: (i, 0, 0))],
        out_specs=pl.BlockSpec(memory_space=pl.ANY),
        scratch_shapes=[pltpu.VMEM((2, R_LOC, H2W), BF16), pltpu.VMEM((TR, H2W), BF16),
                        pltpu.SemaphoreType.DMA((2,)), pltpu.SemaphoreType.DMA],
    )
    return pl.pallas_call(
        functools.partial(_dispatch_kernel, nt=nt),
        grid_spec=grid_spec,
        out_shape=jax.ShapeDtypeStruct((n_rows, H2W), BF16),
        compiler_params=pltpu.CompilerParams(dimension_semantics=("arbitrary",), vmem_limit_bytes=VMEM_LIMIT),
        name="moe_dispatch",
    )(off, loff, msz, tail, h2ext, lpos)


def _expert_kernel(te_ref, xs_ref, wg_ref, wu_ref, wd_ref, ys_ref, wgu_ref, wdb_ref, *, n_row_tiles):
    j = pl.program_id(0)
    e = te_ref[j]

    @pl.when(j < te_ref[n_row_tiles])
    def _():
        @pl.when((j == 0) | (e != te_ref[jnp.maximum(j - 1, 0)]))
        def _():
            wgu_ref[:, :D_EXPERT] = wg_ref[0].astype(BF16)
            wgu_ref[:, D_EXPERT:] = wu_ref[0].astype(BF16)
            wdb_ref[...] = wd_ref[0].astype(BF16)

        x = xs_ref[:, 0:D_MODEL]
        ext = xs_ref[:, D_MODEL:H2W].astype(F32)
        wts = ext[:, :N_EXPERTS] + ext[:, N_EXPERTS:]
        lane = lax.broadcasted_iota(I32, wts.shape, 1)
        w = jnp.sum(jnp.where(lane == e, wts, 0.0), axis=-1, keepdims=True)
        h = _dot(x, wgu_ref[...])
        hg, hu = h[:, :D_EXPERT], h[:, D_EXPERT:]
        act = hg * _sigmoid(hg) * hu * w
        ys_ref[...] = _dot(act.astype(BF16), wdb_ref[...]).astype(BF16)


def _experts(te, xs, wg, wu, wd, n_row_tiles):
    def row_tile(j, te_ref):
        return jnp.maximum(jnp.minimum(j, te_ref[n_row_tiles] - 1), 0)
    grid_spec = pltpu.PrefetchScalarGridSpec(
        num_scalar_prefetch=1,
        grid=(n_row_tiles,),
        in_specs=[pl.BlockSpec((TR, H2W), lambda j, te_ref: (row_tile(j, te_ref), 0)),
                  pl.BlockSpec((1, D_MODEL, D_EXPERT), lambda j, te_ref: (te_ref[row_tile(j, te_ref)], 0, 0)),
                  pl.BlockSpec((1, D_MODEL, D_EXPERT), lambda j, te_ref: (te_ref[row_tile(j, te_ref)], 0, 0)),
                  pl.BlockSpec((1, D_EXPERT, D_MODEL), lambda j, te_ref: (te_ref[row_tile(j, te_ref)], 0, 0))],
        out_specs=pl.BlockSpec((TR, D_MODEL), lambda j, te_ref: (row_tile(j, te_ref), 0)),
        scratch_shapes=[pltpu.VMEM((D_MODEL, 2 * D_EXPERT), BF16), pltpu.VMEM((D_EXPERT, D_MODEL), BF16)],
    )
    return pl.pallas_call(
        functools.partial(_expert_kernel, n_row_tiles=n_row_tiles),
        grid_spec=grid_spec,
        out_shape=jax.ShapeDtypeStruct((n_row_tiles * TR, D_MODEL), BF16),
        compiler_params=pltpu.CompilerParams(dimension_semantics=("arbitrary",), vmem_limit_bytes=VMEM_LIMIT),
        name="moe_experts",
    )(te, xs, wg, wu, wd)


def _combine_kernel(off_ref, loff_ref, msz_ref, lpos_ref, h2_ref, xp_ref, xs_ref, mod_ref, wgs_ref, wus_ref, wds_ref,
                    fnw_ref, ysrt_hbm, yp_ref, ys_ref, yloc_ref, acc_ref, sem, *, nt, n_p_tiles, tiles_per_mod,
                    final):
    i = pl.program_id(0)
    slot = i % 2

    def copy_for(tile, slot_):
        def make(e, m):
            lo = pl.multiple_of(loff_ref[e, tile], ROW_ALIGN)
            of = pl.multiple_of(off_ref[e, tile], ROW_ALIGN)
            return pltpu.make_async_copy(ysrt_hbm.at[pl.ds(of, m)], yloc_ref.at[slot_, pl.ds(lo, m)], sem.at[slot_])
        return make

    @pl.when(i == 0)
    def _():
        yloc_ref[...] = jnp.zeros(yloc_ref.shape, BF16)
        _start_copies(msz_ref, 0, copy_for(0, 0))

    nxt = jnp.minimum(i + 1, nt - 1)
    _start_copies(msz_ref, nxt, copy_for(nxt, 1 - slot))

    hb = h2_ref[...]
    hg = _dot(hb, wgs_ref[...].astype(BF16))
    hu = _dot(hb, wus_ref[...].astype(BF16))
    acc_ref[...] = _dot((hg * _sigmoid(hg) * hu).astype(BF16), wds_ref[...].astype(BF16))

    def wait_tile(tile, slot_):
        n = _tile_rows(loff_ref, msz_ref, tile)
        pltpu.make_async_copy(ysrt_hbm.at[pl.ds(0, n)], yloc_ref.at[slot_, pl.ds(0, n)], sem.at[slot_]).wait()

    wait_tile(i, slot)

    @pl.when(i == nt - 1)
    def _():
        wait_tile(i, 1 - slot)

    lpos_pad = jnp.concatenate([lpos_ref[0].astype(F32), jnp.zeros((128 - TOP_K, TB), F32)], axis=0)
    lposc = lpos_pad.T.astype(I32)
    ck = 512
    cols = [jnp.broadcast_to(lposc[:, k:k + 1], (TB, ck)).astype(jnp.int16) for k in range(TOP_K)]
    one, zero = jnp.ones((TB, ck), BF16), jnp.zeros((TB, ck), BF16)
    for c in range(R_LOC // ck):
        r = (lax.broadcasted_iota(I32, (TB, ck), 1) + c * ck).astype(jnp.int16)
        cm = zero
        for k in range(TOP_K):
            cm = jnp.where(r == cols[k], one, cm)
        acc_ref[...] += _dot(cm, yloc_ref[slot, c * ck:(c + 1) * ck, :])

    row = _mod_row(i, n_p_tiles, tiles_per_mod)
    g2 = mod_ref[pl.ds(row, 1), 5 * D_MODEL:6 * D_MODEL]
    x2 = _select_x(i, n_p_tiles, xp_ref, xs_ref) + g2 * acc_ref[...]
    y = _rms(x2) * fnw_ref[...] if final else x2

    @pl.when(i < n_p_tiles)
    def _():
        yp_ref[...] = y

    @pl.when(i >= n_p_tiles)
    def _():
        ys_ref[...] = y


def _combine(off, loff, msz, lpos, h2ext, x1p, x1s, mod, wgs, wus, wds, fnw, ysorted, *, tiles_per_mod, final):
    n_p, n_s = x1p.shape[0] // TB, x1s.shape[0] // TB
    nt = n_p + n_s
    const = lambda shape: pl.BlockSpec(shape, lambda i, *_: (0,) * len(shape), pipeline_mode=pl.Buffered(1))
    p_idx = lambda i, *_: (jnp.minimum(i, n_p - 1), 0)
    s_idx = lambda i, *_: (jnp.maximum(i - n_p, 0), 0)
    grid_spec = pltpu.PrefetchScalarGridSpec(
        num_scalar_prefetch=3,
        grid=(nt,),
        in_specs=[pl.BlockSpec((1, TOP_K, TB), lambda i, *_: (i, 0, 0)),
                  pl.BlockSpec((TB, D_MODEL), lambda i, *_: (i, 0)),
                  pl.BlockSpec((TB, D_MODEL), p_idx), pl.BlockSpec((TB, D_MODEL), s_idx),
                  const(mod.shape), const((D_MODEL, D_EXPERT)), const((D_MODEL, D_EXPERT)), const((D_EXPERT, D_MODEL)),
                  const((1, D_MODEL)), pl.BlockSpec(memory_space=pl.ANY)],
        out_specs=[pl.BlockSpec((TB, D_MODEL), p_idx), pl.BlockSpec((TB, D_MODEL), s_idx)],
        scratch_shapes=[pltpu.VMEM((2, R_LOC, D_MODEL), BF16), pltpu.VMEM((TB, D_MODEL), F32),
                        pltpu.SemaphoreType.DMA((2,))],
    )
    kern = functools.partial(_combine_kernel, nt=nt, n_p_tiles=n_p, tiles_per_mod=tiles_per_mod, final=final)
    return pl.pallas_call(
        kern,
        grid_spec=grid_spec,
        out_shape=[jax.ShapeDtypeStruct(x1p.shape, F32), jax.ShapeDtypeStruct(x1s.shape, F32)],
        compiler_params=pltpu.CompilerParams(dimension_semantics=("arbitrary",), vmem_limit_bytes=VMEM_LIMIT),
        name="moe_combine",
    )(off, loff, msz, lpos, h2ext, x1p, x1s, mod, wgs, wus, wds, fnw, ysorted)


def _moe(x1p, x1s, mod, n2w, w_router, b_router, wg, wu, wd, wgs, wus, wds, fnw, *, tokens_per_mod, final):
    assert R_LOC >= TB * TOP_K + N_EXPERTS * ROW_ALIGN and tokens_per_mod % TB == 0
    nt = (x1p.shape[0] + x1s.shape[0]) // TB
    n_rows_max = nt * TB * TOP_K + nt * N_EXPERTS * ROW_ALIGN + N_EXPERTS * (TR - ROW_ALIGN)
    n_row_tiles = -(-n_rows_max // TR)
    tiles_per_mod = tokens_per_mod // TB
    brb = jnp.broadcast_to(b_router.reshape(N_EXPERTS, 1), (N_EXPERTS, TB))
    h2ext, lpos, cnt = _route(x1p, x1s, mod, n2w, w_router.T, brb, tiles_per_mod=tiles_per_mod)
    off, loff, msz, tail, te = _plan(cnt, n_row_tiles)
    xs = _dispatch(off, loff, msz, tail, h2ext, lpos, n_row_tiles * TR)
    ysorted = _experts(te[0], xs, wg, wu, wd, n_row_tiles)
    return _combine(off, loff, msz, lpos, h2ext, x1p, x1s, mod, wgs, wus, wds, fnw, ysorted,
                    tiles_per_mod=tiles_per_mod, final=final)


def kernel(x_prompt, x_sample, state_gla, c, c_ctx, w_mod, b_mod, norm1_w, w_in, conv_w, conv_b, w_decay, b_decay,
           gla_norm_w, w_br_conv, w_br_gla, w_out, norm2_w, w_router, b_router, w_gate_e, w_up_e, w_down_e,
           w_gate_s, w_up_s, w_down_s, final_norm_w):
    depth = w_mod.shape[0]
    nb_p, len_p, _ = x_prompt.shape
    nb_s, len_s, _ = x_sample.shape
    yp = x_prompt.reshape(nb_p * len_p, D_MODEL)
    ys = x_sample.reshape(nb_s * len_s, D_MODEL)
    fnw = final_norm_w.reshape(1, D_MODEL)

    cond = jnp.concatenate([c_ctx[None, :], c, jnp.zeros((8 - 1 - nb_s, D_MODEL), F32)], axis=0)
    states = []
    for l in range(depth):
        mod = _modulation(cond, w_mod[l], b_mod[l].reshape(1, -1))
        mod_p, mod_s = mod[0:1], mod[1:1 + nb_s]

        w_in_r = _w_in_prep(w_in[l].T)
        wdec = jnp.zeros((2, LR_PAD, D_GLA), F32)
        wdec = wdec.at[0, 0:GLA_RANK].set(w_decay[l, 0]).at[1, GLA_RANK:2 * GLA_RANK].set(w_decay[l, 1])
        n1w = norm1_w[l].reshape(1, D_MODEL)
        mix_w = (conv_w[l], conv_b[l].reshape(1, D_CONV), wdec, b_decay[l], gla_norm_w[l].reshape(1, D_GLA),
                 w_br_conv[l].astype(BF16), w_br_gla[l].astype(BF16), w_out[l].astype(BF16))
        moe_w = (norm2_w[l].reshape(1, D_MODEL), w_router[l], b_router[l],
                 w_gate_e[l], w_up_e[l], w_down_e[l], w_gate_s[l], w_up_s[l], w_down_s[l])

        proj_p = _in_proj(yp, mod_p, n1w, w_in_r, rows_per_mod=nb_p * len_p)
        yp, st = _mixer(proj_p, yp, mod_p, *mix_w, None, seq_len=len_p, row_len=len_p, emit_state=True)
        states.append(st)
        proj_s = _in_proj(ys, mod_s, n1w, w_in_r, rows_per_mod=len_s)
        (ys,) = _mixer(proj_s, ys, mod_s, *mix_w, state_gla[:, l], seq_len=len_s, row_len=GRID_W, emit_state=False)

        yp, ys = _moe(yp, ys, mod, *moe_w, fnw, tokens_per_mod=len_s, final=l == depth - 1)
    new_state = jnp.stack(states, axis=1)
    return (yp.reshape(nb_p, len_p, D_MODEL), ys.reshape(nb_s, len_s, D_MODEL), new_state)
```

```python
import functools

import jax
import jax.numpy as jnp
from jax import lax
from jax.experimental import pallas as pl
from jax.experimental.pallas import tpu as pltpu

F32 = jnp.float32
BF16 = jnp.bfloat16

D_MODEL = 1024
GRID_W = 64
D_CONV = 512
N_HEADS = 4
HEAD_D = 128
D_GLA = N_HEADS * HEAD_D
GLA_RANK = 16
GLA_GATE_NORM = 16.0
LOG2_E = 1.4426950408889634
CHUNK = 64
SUB = 8
N_SUB = CHUNK // SUB
N_EXPERTS = 64
TOP_K = 8
D_EXPERT = 256
ROUTED_SCALE = 2.5
EPS = 1e-6

C_U, C_GB, C_GC, C_Q, C_K, C_V, C_GO = 0, 512, 1024, 1536, 2048, 2560, 3072
C_BRC, C_BRG, C_LR = 3584, 4608, 5632
D_PROJ = 5760
LR_PAD = 128

VMEM_LIMIT = 56 * 1024 * 1024


def _dot(a, b):
    return jnp.dot(a, b, preferred_element_type=F32)


def _dot_nt(a, b):
    return lax.dot_general(a, b, (((1,), (1,)), ((), ())), preferred_element_type=F32)


def _dot_tn(a, b):
    return lax.dot_general(a, b, (((0,), (0,)), ((), ())), preferred_element_type=F32)


def _dot_hi(a, b):
    return jnp.dot(a, b, preferred_element_type=F32, precision=lax.Precision.HIGHEST)


def _split_bf16(x):
    hi = x.astype(BF16)
    lo = (x - hi.astype(F32)).astype(BF16)
    return hi, lo


def _rms(x):
    return x * lax.rsqrt(jnp.mean(x * x, axis=-1, keepdims=True) + EPS)


def _sigmoid(x):
    return 0.5 * jnp.tanh(0.5 * x) + 0.5


def _mod_kernel(cond_ref, w_ref, b_ref, o_ref):
    c = cond_ref[...]
    o_ref[...] = _dot_hi(c * jax.nn.sigmoid(c), w_ref[...]) + b_ref[...]


def _modulation(cond, w_mod, b_mod):
    n_rows = cond.shape[0]
    tn = 1536
    return pl.pallas_call(
        _mod_kernel,
        grid=(6 * D_MODEL // tn,),
        in_specs=[pl.BlockSpec((n_rows, D_MODEL), lambda j: (0, 0)),
                  pl.BlockSpec((D_MODEL, tn), lambda j: (0, j)),
                  pl.BlockSpec((1, tn), lambda j: (0, j))],
        out_specs=pl.BlockSpec((n_rows, tn), lambda j: (0, j)),
        out_shape=jax.ShapeDtypeStruct((n_rows, 6 * D_MODEL), F32),
        compiler_params=pltpu.CompilerParams(dimension_semantics=("arbitrary",),
                                             vmem_limit_bytes=VMEM_LIMIT),
        name="modulation",
    )(cond, w_mod, b_mod)


D_IN_PROJ = 5664
C_LR_SRC, C_GATES_SRC = 3584, 3616


def _w_in_prep_kernel(w_ref, o_ref):
    rc = 64

    def copy_rows(dst0, src0, n):
        def body(t, carry):
            off = pl.multiple_of(t * rc, rc)
            o_ref[pl.ds(dst0 + off, rc), :] = w_ref[pl.ds(src0 + off, rc), :].astype(BF16)
            return carry
        lax.fori_loop(0, n // rc, body, 0)

    copy_rows(0, 0, C_LR_SRC)
    copy_rows(C_BRC, C_GATES_SRC, 2 * D_MODEL)
    o_ref[C_LR:C_LR + 2 * GLA_RANK, :] = w_ref[C_LR_SRC:C_GATES_SRC, :].astype(BF16)
    o_ref[C_LR + 2 * GLA_RANK:D_PROJ, :] = jnp.zeros((D_PROJ - C_LR - 2 * GLA_RANK, D_MODEL), BF16)


def _w_in_prep(w_in_t):
    return pl.pallas_call(
        _w_in_prep_kernel,
        out_shape=jax.ShapeDtypeStruct((D_PROJ, D_MODEL), BF16),
        compiler_params=pltpu.CompilerParams(vmem_limit_bytes=VMEM_LIMIT),
        name="w_in_prep",
    )(w_in_t)


def _inproj_kernel(x_ref, mod_ref, nw_ref, w_ref, o_ref, *, rows_per_mod, tm):
    i = pl.program_id(1)
    row = (i * tm) // rows_per_mod
    sh = mod_ref[pl.ds(row, 1), 0:D_MODEL]
    sc = mod_ref[pl.ds(row, 1), D_MODEL:2 * D_MODEL]
    h = _rms(x_ref[...]) * nw_ref[...] * (1.0 + sc) + sh
    o_ref[...] = _dot_nt(h.astype(BF16), w_ref[...]).astype(BF16)


def _in_proj(x2d, mod, norm_w, w_in_r, rows_per_mod):
    t = x2d.shape[0]
    tm, tn = 1024, 1920
    kern = functools.partial(_inproj_kernel, rows_per_mod=rows_per_mod, tm=tm)
    return pl.pallas_call(
        kern,
        grid=(D_PROJ // tn, t // tm),
        in_specs=[pl.BlockSpec((tm, D_MODEL), lambda j, i: (i, 0)),
                  pl.BlockSpec(mod.shape, lambda j, i: (0, 0)),
                  pl.BlockSpec((1, D_MODEL), lambda j, i: (0, 0)),
                  pl.BlockSpec((tn, D_MODEL), lambda j, i: (j, 0))],
        out_specs=pl.BlockSpec((tm, tn), lambda j, i: (i, j)),
        out_shape=jax.ShapeDtypeStruct((t, D_PROJ), BF16),
        compiler_params=pltpu.CompilerParams(dimension_semantics=("arbitrary", "arbitrary"),
                                             vmem_limit_bytes=VMEM_LIMIT),
        name="in_proj",
    )(x2d, mod, norm_w, w_in_r)


def _log2_sigmoid(x):
    return jnp.minimum(x, 0.0) * LOG2_E - jnp.log2(1.0 + jnp.exp2(jnp.abs(x) * (-LOG2_E)))


def _gla_chunk_head(qc, kc, vc, bc, st, rev):
    lane = lax.broadcasted_iota(jnp.int32, (SUB, CHUNK), 1)
    sub = lax.broadcasted_iota(jnp.int32, (SUB, CHUNK), 0)
    tot = bc[0:1] if rev else bc[CHUNK - 1:CHUNK]

    o = _dot_nt((qc * jnp.exp2(bc)).astype(BF16), st.astype(BF16))
    k_tail = kc * jnp.exp2(tot - bc)
    st_new = st * jnp.exp2(tot) + _dot_tn(vc.astype(BF16), k_tail.astype(BF16))

    lhs_segs, rhs_segs = [], []

    def rows(before, mid, after):
        parts = ([jnp.zeros((before, HEAD_D), F32)] if before else []) + [mid]
        parts += [jnp.zeros((after, HEAD_D), F32)] if after else []
        return jnp.concatenate(parts, axis=0) if len(parts) > 1 else mid

    key_blocks = range(1, N_SUB) if rev else range(0, N_SUB - 1)
    for jb in key_blocks:
        r0 = jb * SUB
        ref_row = bc[r0:r0 + 1] if rev else bc[r0 + SUB - 1:r0 + SUB]
        ke = kc[r0:r0 + SUB] * jnp.exp2(ref_row - bc[r0:r0 + SUB])
        rhs_segs.append(rows(r0, ke, CHUNK - r0 - SUB))
        if rev:
            ql = qc[:r0] * jnp.exp2(bc[:r0] - ref_row)
            lhs_segs.append(rows(0, ql, CHUNK - r0))
        else:
            ql = qc[r0 + SUB:] * jnp.exp2(bc[r0 + SUB:] - ref_row)
            lhs_segs.append(rows(r0 + SUB, ql, 0))
    far = _dot_nt(jnp.concatenate(lhs_segs, axis=1).astype(BF16),
                  jnp.concatenate(rhs_segs, axis=1).astype(BF16))

    blocks = []
    for ib in range(N_SUB):
        r0 = ib * SUB
        qi, bi = qc[r0:r0 + SUB], bc[r0:r0 + SUB]
        acc = jnp.zeros((SUB, CHUNK), F32)
        for jj in range(SUB):
            j = r0 + jj
            e = jnp.exp2(bi - bc[j:j + 1])
            col = jnp.sum(qi * (kc[j:j + 1] * e), axis=-1, keepdims=True)
            acc = jnp.where(lane == j, col, acc)
        keep = (lane - r0 >= sub) if rev else (lane - r0 <= sub)
        blocks.append(jnp.where(keep, acc, 0.0))
    scores = far + jnp.concatenate(blocks, axis=0)
    o = o + _dot(scores.astype(BF16), vc.astype(BF16))
    return o, st_new


def _mixer_kernel(*refs, seq_len, row_len, has_s0, emit_state):
    it = iter(refs)
    proj_ref, x_ref, mod_ref, cw_ref, cb_ref, wdec_ref, bdec_ref, gnw_ref = (next(it) for _ in range(8))
    wbc_ref, wbg_ref, wout_ref = (next(it) for _ in range(3))
    s0_ref = next(it) if has_s0 else None
    out_ref = next(it)
    st_out_ref = next(it) if emit_state else None
    la_f_ref, la_b_ref, o_ref, st_ref = (next(it) for _ in range(4))

    L = seq_len
    n_chunks = L // CHUNK
    tr = 256
    assert tr % row_len == 0 and L % tr == 0

    ri = lax.broadcasted_iota(jnp.int32, (tr, tr), 0)
    ci = lax.broadcasted_iota(jnp.int32, (tr, tr), 1)
    same_chunk = (ri // CHUNK) == (ci // CHUNK)

    def decay_body(t, carry):
        r0 = pl.multiple_of(t * tr, tr)
        lr = proj_ref[pl.ds(r0, tr), C_LR:C_LR + LR_PAD]
        for d, ref in ((0, la_f_ref), (1, la_b_ref)):
            whi, wlo = _split_bf16(wdec_ref[d])
            z = _dot(lr, whi) + _dot(lr, wlo) + bdec_ref[d:d + 1]
            la_hi, la_lo = _split_bf16(_log2_sigmoid(z) * (1.0 / GLA_GATE_NORM))
            tri = jnp.where(same_chunk & ((ci >= ri) if d else (ci <= ri)), 1.0, 0.0).astype(BF16)
            ref[pl.ds(r0, tr), :] = _dot(tri, la_hi) + _dot(tri, la_lo)
        return carry
    lax.fori_loop(0, L // tr, decay_body, 0)

    for rev, la_ref in ((False, la_f_ref), (True, la_b_ref)):
        d = 1 if rev else 0
        for h in range(N_HEADS):
            if has_s0:
                st_ref[h] = s0_ref[0, d, h].T
            else:
                st_ref[h] = jnp.zeros((HEAD_D, HEAD_D), F32)

        def chunk_body(c, carry, rev=rev, la_ref=la_ref, heads=range(N_HEADS)):
            cc = (n_chunks - 1 - c) if rev else c
            r0 = pl.multiple_of(cc * CHUNK, CHUNK)
            for h in heads:
                lo, hi = h * HEAD_D, (h + 1) * HEAD_D
                qc = proj_ref[pl.ds(r0, CHUNK), C_Q + lo:C_Q + hi].astype(F32) * (HEAD_D ** -0.5)
                kc = proj_ref[pl.ds(r0, CHUNK), C_K + lo:C_K + hi].astype(F32)
                vc = proj_ref[pl.ds(r0, CHUNK), C_V + lo:C_V + hi].astype(F32)
                oc, st_new = _gla_chunk_head(qc, kc, vc, la_ref[pl.ds(r0, CHUNK), lo:hi], st_ref[h], rev)
                st_ref[h] = st_new
                if rev:
                    o_ref[pl.ds(r0, CHUNK), lo:hi] += oc
                else:
                    o_ref[pl.ds(r0, CHUNK), lo:hi] = oc
            return carry
        lax.fori_loop(0, n_chunks, chunk_body, 0, unroll=4)

        if emit_state:
            for h in range(N_HEADS):
                st_out_ref[0, d, h] = st_ref[h].T

    mod_row = pl.program_id(0) if mod_ref.shape[0] > 1 else 0
    g1 = mod_ref[pl.ds(mod_row, 1), 2 * D_MODEL:3 * D_MODEL]
    hsel_r = lax.broadcasted_iota(jnp.int32, (D_GLA, D_GLA), 0) // HEAD_D
    hsel_c = lax.broadcasted_iota(jnp.int32, (D_GLA, D_GLA), 1) // HEAD_D
    head_avg = jnp.where(hsel_r == hsel_c, 1.0 / HEAD_D, 0.0).astype(BF16)
    pos = lax.broadcasted_iota(jnp.int32, (tr, 1), 0)

    def dense_body(t, carry):
        r0 = pl.multiple_of(t * tr, tr)
        rows = pl.ds(r0, tr)
        cu = (proj_ref[rows, C_GC:C_GC + D_CONV].astype(F32) * proj_ref[rows, C_U:C_U + D_CONV].astype(F32))
        in_row = pos % row_len
        left = jnp.where(in_row == 0, 0.0, pltpu.roll(cu, 1, axis=0))
        right = jnp.where(in_row == row_len - 1, 0.0, pltpu.roll(cu, tr - 1, axis=0))
        conv = cw_ref[0:1] * left + cw_ref[1:2] * cu + cw_ref[2:3] * right + cb_ref[...]
        y_conv = proj_ref[rows, C_GB:C_GB + D_CONV].astype(F32) * conv

        o = o_ref[rows, :]
        osq_hi, osq_lo = _split_bf16(o * o)
        ms = _dot(osq_hi, head_avg) + _dot(osq_lo, head_avg)
        g_out = proj_ref[rows, C_GO:C_GO + D_GLA].astype(F32)
        y_gla = o * lax.rsqrt(ms + EPS) * gnw_ref[...] * (g_out * _sigmoid(g_out))

        merged = (_sigmoid(proj_ref[rows, C_BRC:C_BRC + D_MODEL].astype(F32)) * _dot(y_conv.astype(BF16), wbc_ref[...])
                  + _sigmoid(proj_ref[rows, C_BRG:C_BRG + D_MODEL].astype(F32)) * _dot(y_gla.astype(BF16), wbg_ref[...]))
        out_ref[rows, :] = x_ref[rows, :] + g1 * _dot(merged.astype(BF16), wout_ref[...])
        return carry
    lax.fori_loop(0, L // tr, dense_body, 0)


def _mixer(proj, x2d, mod, conv_w, conv_b, wdec, bdec, gnw, wbc, wbg, wout, s0, *, seq_len, row_len, emit_state):
    t = x2d.shape[0]
    nb = t // seq_len
    has_s0 = s0 is not None
    one = pl.Buffered(1)
    const = lambda shape: pl.BlockSpec(shape, lambda b: (0,) * len(shape), pipeline_mode=one)
    in_specs = [pl.BlockSpec((seq_len, D_PROJ), lambda b: (b, 0), pipeline_mode=one),
                pl.BlockSpec((seq_len, D_MODEL), lambda b: (b, 0)),
                const(mod.shape),
                const((3, D_CONV)), const((1, D_CONV)), const((2, LR_PAD, D_GLA)), const((2, D_GLA)),
                const((1, D_GLA)), const((D_CONV, D_MODEL)), const((D_GLA, D_MODEL)), const((D_MODEL, D_MODEL))]
    args = [proj, x2d, mod, conv_w, conv_b, wdec, bdec, gnw, wbc, wbg, wout]
    if has_s0:
        in_specs.append(pl.BlockSpec((1, 2, N_HEADS, HEAD_D, HEAD_D), lambda b: (b, 0, 0, 0, 0)))
        args.append(s0)
    out_specs = [pl.BlockSpec((seq_len, D_MODEL), lambda b: (b, 0))]
    out_shape = [jax.ShapeDtypeStruct((t, D_MODEL), F32)]
    if emit_state:
        out_specs.append(pl.BlockSpec((1, 2, N_HEADS, HEAD_D, HEAD_D), lambda b: (b, 0, 0, 0, 0)))
        out_shape.append(jax.ShapeDtypeStruct((nb, 2, N_HEADS, HEAD_D, HEAD_D), F32))
    kern = functools.partial(_mixer_kernel, seq_len=seq_len, row_len=row_len, has_s0=has_s0, emit_state=emit_state)
    return pl.pallas_call(
        kern,
        grid=(nb,),
        in_specs=in_specs,
        out_specs=out_specs,
        out_shape=out_shape,
        scratch_shapes=[pltpu.VMEM((seq_len, D_GLA), F32), pltpu.VMEM((seq_len, D_GLA), F32),
                        pltpu.VMEM((seq_len, D_GLA), F32), pltpu.VMEM((N_HEADS, HEAD_D, HEAD_D), F32)],
        compiler_params=pltpu.CompilerParams(dimension_semantics=("arbitrary",),
                                             vmem_limit_bytes=VMEM_LIMIT),
        name="mixer",
    )(*args)


I32 = jnp.int32
TB = 256
ROW_ALIGN = 16
G_ALIGN = 256
E_CHUNK = 1024
R_LOC = 3072
H2W = 1152


def _select_x(i, n_p_tiles, xp_ref, xs_ref):
    return jnp.where(i < n_p_tiles, xp_ref[...], xs_ref[...])


def _mod_row(i, n_p_tiles, tiles_per_mod):
    return jnp.where(i < n_p_tiles, 0, 1 + (i - n_p_tiles) // tiles_per_mod)


def _route_kernel(xp_ref, xs_ref, mod_ref, n2w_ref, wrt_ref, brb_ref, h2_ref, lpos_ref, cnt_ref, *,
                  n_p_tiles, tiles_per_mod):
    i = pl.program_id(0)
    row = _mod_row(i, n_p_tiles, tiles_per_mod)
    x = _select_x(i, n_p_tiles, xp_ref, xs_ref)
    sh = mod_ref[pl.ds(row, 1), 3 * D_MODEL:4 * D_MODEL]
    sc = mod_ref[pl.ds(row, 1), 4 * D_MODEL:5 * D_MODEL]
    h2 = _rms(x) * n2w_ref[...] * (1.0 + sc) + sh

    h_hi, h_lo = _split_bf16(h2)
    w_hi, w_lo = _split_bf16(wrt_ref[...])
    scores = jax.nn.sigmoid(_dot_nt(w_hi, h_hi) + (_dot_nt(w_hi, h_lo) + _dot_nt(w_lo, h_hi)))
    biased = scores + brb_ref[...]
    eidx = lax.broadcasted_iota(I32, scores.shape, 0)
    picks = []
    for _k in range(TOP_K):
        m = jnp.max(biased, axis=0, keepdims=True)
        first = jnp.min(jnp.where(biased == m, eidx, N_EXPERTS), axis=0, keepdims=True)
        pick = eidx == first
        picks.append(pick)
        biased = jnp.where(pick, -jnp.inf, biased)
    sel = jnp.zeros(scores.shape, F32)
    for pick in picks:
        sel = jnp.where(pick, 1.0, sel)
    selsc = sel * scores
    comb = selsc / jnp.sum(selsc, axis=0, keepdims=True) * ROUTED_SCALE

    selb = sel.astype(BF16)
    tr_ = lax.broadcasted_iota(I32, (TB, TB), 0)
    tc_ = lax.broadcasted_iota(I32, (TB, TB), 1)
    rank = _dot(selb, jnp.where(tr_ < tc_, 1.0, 0.0).astype(BF16))
    n_b = _dot(selb, jnp.ones((TB, 128), BF16))
    m_b = jnp.maximum(jnp.floor((n_b + (ROW_ALIGN - 1)) * (1.0 / ROW_ALIGN)), 1.0) * ROW_ALIGN
    er_ = lax.broadcasted_iota(I32, (N_EXPERTS, N_EXPERTS), 0)
    ec_ = lax.broadcasted_iota(I32, (N_EXPERTS, N_EXPERTS), 1)
    loff_b = _dot(jnp.where(ec_ < er_, 1.0, 0.0).astype(BF16), m_b.astype(BF16))
    lposf = jnp.concatenate([loff_b] * (TB // 128), axis=1) + rank
    rows = [jnp.sum(jnp.where(pick, lposf, 0.0), axis=0, keepdims=True) for pick in picks]
    lpos_ref[0] = jnp.concatenate(rows, axis=0).astype(I32)
    cnt_ref[0] = m_b

    combt = comb.T
    chi = combt.astype(BF16).astype(F32)
    h2_ref[:, 0:D_MODEL] = h_hi
    h2_ref[:, D_MODEL:H2W] = jnp.concatenate([chi, combt - chi], axis=1).astype(BF16)


def _route(x1p, x1s, mod, n2w, w_router_t, b_router_b, *, tiles_per_mod):
    n_p, n_s = x1p.shape[0] // TB, x1s.shape[0] // TB
    nt = n_p + n_s
    kern = functools.partial(_route_kernel, n_p_tiles=n_p, tiles_per_mod=tiles_per_mod)
    const = lambda shape: pl.BlockSpec(shape, lambda i: (0,) * len(shape))
    return pl.pallas_call(
        kern,
        grid=(nt,),
        in_specs=[pl.BlockSpec((TB, D_MODEL), lambda i: (jnp.minimum(i, n_p - 1), 0)),
                  pl.BlockSpec((TB, D_MODEL), lambda i: (jnp.maximum(i - n_p, 0), 0)),
                  const(mod.shape), const((1, D_MODEL)), const((N_EXPERTS, D_MODEL)), const((N_EXPERTS, TB))],
        out_specs=[pl.BlockSpec((TB, H2W), lambda i: (i, 0)),
                   pl.BlockSpec((1, TOP_K, TB), lambda i: (i, 0, 0)),
                   pl.BlockSpec((1, N_EXPERTS, 128), lambda i: (i, 0, 0))],
        out_shape=[jax.ShapeDtypeStruct((nt * TB, H2W), BF16),
                   jax.ShapeDtypeStruct((nt, TOP_K, TB), I32),
                   jax.ShapeDtypeStruct((nt, N_EXPERTS, 128), F32)],
        compiler_params=pltpu.CompilerParams(dimension_semantics=("arbitrary",), vmem_limit_bytes=VMEM_LIMIT),
        name="moe_route",
    )(x1p, x1s, mod, n2w, w_router_t, b_router_b)


def _plan_kernel(cnt_ref, off_ref, loff_ref, msz_ref, grp_ref, *, nt):
    lane = lax.broadcasted_iota(I32, (N_EXPERTS, 128), 1)
    m = jnp.zeros((N_EXPERTS, 128), F32)
    for i in range(nt):
        m = jnp.where(lane == i, cnt_ref[i], m)
    total = jnp.broadcast_to(jnp.sum(m, axis=1, keepdims=True), (N_EXPERTS, 128))
    gsz = jnp.floor((total + (G_ALIGN - 1)) * (1.0 / G_ALIGN)) * G_ALIGN
    er_ = lax.broadcasted_iota(I32, (N_EXPERTS, N_EXPERTS), 0)
    ec_ = lax.broadcasted_iota(I32, (N_EXPERTS, N_EXPERTS), 1)
    lstrict = jnp.where(ec_ < er_, 1.0, 0.0)
    ir_ = lax.broadcasted_iota(I32, (128, 128), 0)
    ic_ = lax.broadcasted_iota(I32, (128, 128), 1)
    ustrict = jnp.where(ir_ < ic_, 1.0, 0.0)
    gstart = _dot_hi(lstrict, gsz)
    off_ref[...] = (gstart + _dot_hi(m, ustrict)).astype(I32)
    loff_ref[...] = _dot_hi(lstrict, m).astype(I32)
    msz_ref[...] = m.astype(I32)
    grp = jnp.where(lane == 0, gstart + total, jnp.where(lane == 1, gsz - total, jnp.where(lane == 2, gstart, gsz)))
    grp_ref[...] = grp.astype(I32)


def _plan(cnt):
    nt = cnt.shape[0]
    assert nt <= 128
    tab = jax.ShapeDtypeStruct((N_EXPERTS, 128), I32)
    return pl.pallas_call(
        functools.partial(_plan_kernel, nt=nt),
        out_shape=[tab, tab, tab, tab],
        compiler_params=pltpu.CompilerParams(vmem_limit_bytes=VMEM_LIMIT),
        name="moe_plan",
    )(cnt)


def _start_copies(msz_ref, tile, make_copy):
    for e in range(N_EXPERTS):
        make_copy(e, pl.multiple_of(msz_ref[e, tile], ROW_ALIGN)).start()


def _tile_rows(loff_ref, msz_ref, tile):
    return pl.multiple_of(loff_ref[N_EXPERTS - 1, tile] + msz_ref[N_EXPERTS - 1, tile], ROW_ALIGN)


def _dispatch_kernel(off_ref, loff_ref, msz_ref, tail_ref, h2_ref, lpos_ref, xs_hbm, xloc_ref, zero_ref, sem,
                     tail_sem, *, nt):
    i = pl.program_id(0)
    slot = i % 2

    def copy_for(tile, slot_):
        def make(e, m):
            lo = pl.multiple_of(loff_ref[e, tile], ROW_ALIGN)
            of = pl.multiple_of(off_ref[e, tile], ROW_ALIGN)
            return pltpu.make_async_copy(xloc_ref.at[slot_, pl.ds(lo, m)], xs_hbm.at[pl.ds(of, m)], sem.at[slot_])
        return make

    def wait_tile(tile, slot_):
        n = _tile_rows(loff_ref, msz_ref, tile)
        pltpu.make_async_copy(xloc_ref.at[slot_, pl.ds(0, n)], xs_hbm.at[pl.ds(0, n)], sem.at[slot_]).wait()

    lpos = lpos_ref[0].astype(jnp.int16)
    h2 = h2_ref[...]
    ck = 1024
    one, zero = jnp.ones((ck, TB), BF16), jnp.zeros((ck, TB), BF16)
    for c in range(R_LOC // ck):
        r = (lax.broadcasted_iota(I32, (ck, TB), 0) + c * ck).astype(jnp.int16)
        d = zero
        for k in range(TOP_K):
            d = jnp.where(r == lpos[k:k + 1, :], one, d)
        res = _dot(d, h2)
        xloc_ref[slot, c * ck:(c + 1) * ck, :] = res.astype(BF16)

    _start_copies(msz_ref, i, copy_for(i, slot))

    @pl.when(i > 0)
    def _():
        wait_tile(i - 1, 1 - slot)

    @pl.when(i == nt - 1)
    def _():
        zero_ref[...] = jnp.zeros(zero_ref.shape, BF16)

        def tail_copies(start):
            def body(e, carry):
                n = tail_ref[e, 1]

                @pl.when(n > 0)
                def _():
                    st = pl.multiple_of(tail_ref[e, 0], ROW_ALIGN)
                    nn = pl.multiple_of(n, ROW_ALIGN)
                    cp = pltpu.make_async_copy(zero_ref.at[pl.ds(0, nn)], xs_hbm.at[pl.ds(st, nn)], tail_sem)
                    if start:
                        cp.start()
                    else:
                        cp.wait()
                return carry
            lax.fori_loop(0, N_EXPERTS, body, 0)
        tail_copies(True)
        wait_tile(i, slot)
        tail_copies(False)


def _dispatch(off, loff, msz, tail, h2ext, lpos, n_rows):
    nt = lpos.shape[0]
    grid_spec = pltpu.PrefetchScalarGridSpec(
        num_scalar_prefetch=4,
        grid=(nt,),
        in_specs=[pl.BlockSpec((TB, H2W), lambda i, *_: (i, 0)),
                  pl.BlockSpec((1, TOP_K, TB), lambda i, *_: (i, 0, 0))],
        out_specs=pl.BlockSpec(memory_space=pl.ANY),
        scratch_shapes=[pltpu.VMEM((2, R_LOC, H2W), BF16), pltpu.VMEM((G_ALIGN, H2W), BF16),
                        pltpu.SemaphoreType.DMA((2,)), pltpu.SemaphoreType.DMA],
    )
    return pl.pallas_call(
        functools.partial(_dispatch_kernel, nt=nt),
        grid_spec=grid_spec,
        out_shape=jax.ShapeDtypeStruct((n_rows, H2W), BF16),
        compiler_params=pltpu.CompilerParams(dimension_semantics=("arbitrary",), vmem_limit_bytes=VMEM_LIMIT),
        name="moe_dispatch",
    )(off, loff, msz, tail, h2ext, lpos)


def _expert_kernel(grp_ref, wg_ref, wu_ref, wd_ref, xs_hbm, ys_hbm, xbuf, ybuf, wgu_ref, wdb_ref, st_ref,
                   in_sem, out_sem):
    e = pl.program_id(0)
    n_exp = pl.num_programs(0)

    def in_copy(row0, n, slot):
        return pltpu.make_async_copy(xs_hbm.at[pl.ds(row0, n)], xbuf.at[slot, pl.ds(0, n)], in_sem.at[slot])

    def out_copy(row0, n, slot):
        return pltpu.make_async_copy(ybuf.at[slot, pl.ds(0, n)], ys_hbm.at[pl.ds(row0, n)], out_sem.at[slot])

    def rows_of(ex):
        return grp_ref[jnp.minimum(ex, n_exp - 1), 3]

    def next_nonempty(ex):
        return lax.while_loop(lambda c: (c < n_exp) & (rows_of(c) == 0), lambda c: c + 1, ex + 1)

    def start_first_chunk(ex, slot):
        @pl.when(ex < n_exp)
        def _():
            exc = jnp.minimum(ex, n_exp - 1)
            n = pl.multiple_of(jnp.minimum(grp_ref[exc, 3], E_CHUNK), G_ALIGN)
            in_copy(pl.multiple_of(grp_ref[exc, 2], G_ALIGN), n, slot).start()

    def drain_out(slot):
        pend = st_ref[1 + slot]

        @pl.when(pend > 0)
        def _():
            out_copy(0, pl.multiple_of(pend, G_ALIGN), slot).wait()
            st_ref[1 + slot] = 0

    @pl.when(e == 0)
    def _():
        st_ref[0] = 0
        st_ref[1] = 0
        st_ref[2] = 0
        start_first_chunk(next_nonempty(-1), 0)

    g0 = grp_ref[e, 2]
    gn = grp_ref[e, 3]

    @pl.when(gn > 0)
    def _():
        wgu_ref[:, :D_EXPERT] = wg_ref[0].astype(BF16)
        wgu_ref[:, D_EXPERT:] = wu_ref[0].astype(BF16)
        wdb_ref[...] = wd_ref[0].astype(BF16)
        n_chunks = (gn + (E_CHUNK - 1)) // E_CHUNK

        def compute(n, slot):
            x = xbuf[slot, 0:n, 0:D_MODEL]
            ext = xbuf[slot, 0:n, D_MODEL:H2W].astype(F32)
            wts = ext[:, :N_EXPERTS] + ext[:, N_EXPERTS:]
            lane = lax.broadcasted_iota(I32, wts.shape, 1)
            w = jnp.sum(jnp.where(lane == e, wts, 0.0), axis=-1, keepdims=True)
            h = _dot(x, wgu_ref[...])
            hg, hu = h[:, :D_EXPERT], h[:, D_EXPERT:]
            act = hg * _sigmoid(hg) * hu * w
            ybuf[slot, 0:n, :] = _dot(act.astype(BF16), wdb_ref[...]).astype(BF16)

        def chunk_body(c, slot):
            row0 = pl.multiple_of(g0 + c * E_CHUNK, G_ALIGN)
            n = pl.multiple_of(jnp.minimum(gn - c * E_CHUNK, E_CHUNK), G_ALIGN)
            in_copy(row0, n, slot).wait()

            @pl.when(c + 1 < n_chunks)
            def _():
                n1 = pl.multiple_of(jnp.minimum(gn - (c + 1) * E_CHUNK, E_CHUNK), G_ALIGN)
                in_copy(pl.multiple_of(row0 + E_CHUNK, G_ALIGN), n1, 1 - slot).start()

            @pl.when(c + 1 == n_chunks)
            def _():
                start_first_chunk(next_nonempty(e), 1 - slot)

            drain_out(slot)
            for v in range(G_ALIGN, E_CHUNK + 1, G_ALIGN):
                @pl.when(n == v)
                def _(v=v):
                    compute(v, slot)
            out_copy(row0, n, slot).start()
            st_ref[1 + slot] = n
            return 1 - slot

        st_ref[0] = lax.fori_loop(0, n_chunks, chunk_body, st_ref[0])

    @pl.when(e == n_exp - 1)
    def _():
        drain_out(0)
        drain_out(1)


def _experts(grp, xs, wg, wu, wd):
    w_in = pl.BlockSpec((1, D_MODEL, D_EXPERT), lambda e, grp_ref: (e, 0, 0))
    grid_spec = pltpu.PrefetchScalarGridSpec(
        num_scalar_prefetch=1,
        grid=(N_EXPERTS,),
        in_specs=[w_in, w_in, pl.BlockSpec((1, D_EXPERT, D_MODEL), lambda e, grp_ref: (e, 0, 0)),
                  pl.BlockSpec(memory_space=pl.ANY)],
        out_specs=pl.BlockSpec(memory_space=pl.ANY),
        scratch_shapes=[pltpu.VMEM((2, E_CHUNK, H2W), BF16), pltpu.VMEM((2, E_CHUNK, D_MODEL), BF16),
                        pltpu.VMEM((D_MODEL, 2 * D_EXPERT), BF16), pltpu.VMEM((D_EXPERT, D_MODEL), BF16),
                        pltpu.SMEM((4,), I32), pltpu.SemaphoreType.DMA((2,)), pltpu.SemaphoreType.DMA((2,))],
    )
    return pl.pallas_call(
        _expert_kernel,
        grid_spec=grid_spec,
        out_shape=jax.ShapeDtypeStruct((xs.shape[0], D_MODEL), BF16),
        compiler_params=pltpu.CompilerParams(dimension_semantics=("arbitrary",), vmem_limit_bytes=VMEM_LIMIT),
        name="moe_experts",
    )(grp, wg, wu, wd, xs)


def _combine_kernel(off_ref, loff_ref, msz_ref, lpos_ref, h2_ref, xp_ref, xs_ref, mod_ref, wgs_ref, wus_ref, wds_ref,
                    fnw_ref, ysrt_hbm, yp_ref, ys_ref, yloc_ref, acc_ref, sem, *, nt, n_p_tiles, tiles_per_mod,
                    final):
    i = pl.program_id(0)
    slot = i % 2

    def copy_for(tile, slot_):
        def make(e, m):
            lo = pl.multiple_of(loff_ref[e, tile], ROW_ALIGN)
            of = pl.multiple_of(off_ref[e, tile], ROW_ALIGN)
            return pltpu.make_async_copy(ysrt_hbm.at[pl.ds(of, m)], yloc_ref.at[slot_, pl.ds(lo, m)], sem.at[slot_])
        return make

    @pl.when(i == 0)
    def _():
        yloc_ref[...] = jnp.zeros(yloc_ref.shape, BF16)
        _start_copies(msz_ref, 0, copy_for(0, 0))

    nxt = jnp.minimum(i + 1, nt - 1)
    _start_copies(msz_ref, nxt, copy_for(nxt, 1 - slot))

    hb = h2_ref[...]
    hg = _dot(hb, wgs_ref[...].astype(BF16))
    hu = _dot(hb, wus_ref[...].astype(BF16))
    acc_ref[...] = _dot((hg * _sigmoid(hg) * hu).astype(BF16), wds_ref[...].astype(BF16))

    def wait_tile(tile, slot_):
        n = _tile_rows(loff_ref, msz_ref, tile)
        pltpu.make_async_copy(ysrt_hbm.at[pl.ds(0, n)], yloc_ref.at[slot_, pl.ds(0, n)], sem.at[slot_]).wait()

    wait_tile(i, slot)

    @pl.when(i == nt - 1)
    def _():
        wait_tile(i, 1 - slot)

    lpos_pad = jnp.concatenate([lpos_ref[0].astype(F32), jnp.zeros((128 - TOP_K, TB), F32)], axis=0)
    lposc = lpos_pad.T.astype(I32)
    ck = 512
    cols = [jnp.broadcast_to(lposc[:, k:k + 1], (TB, ck)).astype(jnp.int16) for k in range(TOP_K)]
    one, zero = jnp.ones((TB, ck), BF16), jnp.zeros((TB, ck), BF16)
    for c in range(R_LOC // ck):
        r = (lax.broadcasted_iota(I32, (TB, ck), 1) + c * ck).astype(jnp.int16)
        cm = zero
        for k in range(TOP_K):
            cm = jnp.where(r == cols[k], one, cm)
        acc_ref[...] += _dot(cm, yloc_ref[slot, c * ck:(c + 1) * ck, :])

    row = _mod_row(i, n_p_tiles, tiles_per_mod)
    g2 = mod_ref[pl.ds(row, 1), 5 * D_MODEL:6 * D_MODEL]
    x2 = _select_x(i, n_p_tiles, xp_ref, xs_ref) + g2 * acc_ref[...]
    y = _rms(x2) * fnw_ref[...] if final else x2

    @pl.when(i < n_p_tiles)
    def _():
        yp_ref[...] = y

    @pl.when(i >= n_p_tiles)
    def _():
        ys_ref[...] = y


def _combine(off, loff, msz, lpos, h2ext, x1p, x1s, mod, wgs, wus, wds, fnw, ysorted, *, tiles_per_mod, final):
    n_p, n_s = x1p.shape[0] // TB, x1s.shape[0] // TB
    nt = n_p + n_s
    const = lambda shape: pl.BlockSpec(shape, lambda i, *_: (0,) * len(shape), pipeline_mode=pl.Buffered(1))
    p_idx = lambda i, *_: (jnp.minimum(i, n_p - 1), 0)
    s_idx = lambda i, *_: (jnp.maximum(i - n_p, 0), 0)
    grid_spec = pltpu.PrefetchScalarGridSpec(
        num_scalar_prefetch=3,
        grid=(nt,),
        in_specs=[pl.BlockSpec((1, TOP_K, TB), lambda i, *_: (i, 0, 0)),
                  pl.BlockSpec((TB, D_MODEL), lambda i, *_: (i, 0)),
                  pl.BlockSpec((TB, D_MODEL), p_idx), pl.BlockSpec((TB, D_MODEL), s_idx),
                  const(mod.shape), const((D_MODEL, D_EXPERT)), const((D_MODEL, D_EXPERT)), const((D_EXPERT, D_MODEL)),
                  const((1, D_MODEL)), pl.BlockSpec(memory_space=pl.ANY)],
        out_specs=[pl.BlockSpec((TB, D_MODEL), p_idx), pl.BlockSpec((TB, D_MODEL), s_idx)],
        scratch_shapes=[pltpu.VMEM((2, R_LOC, D_MODEL), BF16), pltpu.VMEM((TB, D_MODEL), F32),
                        pltpu.SemaphoreType.DMA((2,))],
    )
    kern = functools.partial(_combine_kernel, nt=nt, n_p_tiles=n_p, tiles_per_mod=tiles_per_mod, final=final)
    return pl.pallas_call(
        kern,
        grid_spec=grid_spec,
        out_shape=[jax.ShapeDtypeStruct(x1p.shape, F32), jax.ShapeDtypeStruct(x1s.shape, F32)],
        compiler_params=pltpu.CompilerParams(dimension_semantics=("arbitrary",), vmem_limit_bytes=VMEM_LIMIT),
        name="moe_combine",
    )(off, loff, msz, lpos, h2ext, x1p, x1s, mod, wgs, wus, wds, fnw, ysorted)


def _moe(x1p, x1s, mod, n2w, w_router, b_router, wg, wu, wd, wgs, wus, wds, fnw, *, tokens_per_mod, final):
    assert R_LOC >= TB * TOP_K + N_EXPERTS * ROW_ALIGN and tokens_per_mod % TB == 0
    nt = (x1p.shape[0] + x1s.shape[0]) // TB
    n_rows_max = nt * TB * TOP_K + nt * N_EXPERTS * ROW_ALIGN + N_EXPERTS * (G_ALIGN - ROW_ALIGN)
    tiles_per_mod = tokens_per_mod // TB
    brb = jnp.broadcast_to(b_router.reshape(N_EXPERTS, 1), (N_EXPERTS, TB))
    h2ext, lpos, cnt = _route(x1p, x1s, mod, n2w, w_router.T, brb, tiles_per_mod=tiles_per_mod)
    off, loff, msz, grp = _plan(cnt)
    xs = _dispatch(off, loff, msz, grp, h2ext, lpos, n_rows_max)
    ysorted = _experts(grp, xs, wg, wu, wd)
    return _combine(off, loff, msz, lpos, h2ext, x1p, x1s, mod, wgs, wus, wds, fnw, ysorted,
                    tiles_per_mod=tiles_per_mod, final=final)


def kernel(x_prompt, x_sample, state_gla, c, c_ctx, w_mod, b_mod, norm1_w, w_in, conv_w, conv_b, w_decay, b_decay,
           gla_norm_w, w_br_conv, w_br_gla, w_out, norm2_w, w_router, b_router, w_gate_e, w_up_e, w_down_e,
           w_gate_s, w_up_s, w_down_s, final_norm_w):
    depth = w_mod.shape[0]
    nb_p, len_p, _ = x_prompt.shape
    nb_s, len_s, _ = x_sample.shape
    yp = x_prompt.reshape(nb_p * len_p, D_MODEL)
    ys = x_sample.reshape(nb_s * len_s, D_MODEL)
    fnw = final_norm_w.reshape(1, D_MODEL)

    cond = jnp.concatenate([c_ctx[None, :], c, jnp.zeros((8 - 1 - nb_s, D_MODEL), F32)], axis=0)
    states = []
    for l in range(depth):
        mod = _modulation(cond, w_mod[l], b_mod[l].reshape(1, -1))
        mod_p, mod_s = mod[0:1], mod[1:1 + nb_s]

        w_in_r = _w_in_prep(w_in[l].T)
        wdec = jnp.zeros((2, LR_PAD, D_GLA), F32)
        wdec = wdec.at[0, 0:GLA_RANK].set(w_decay[l, 0]).at[1, GLA_RANK:2 * GLA_RANK].set(w_decay[l, 1])
        n1w = norm1_w[l].reshape(1, D_MODEL)
        mix_w = (conv_w[l], conv_b[l].reshape(1, D_CONV), wdec, b_decay[l], gla_norm_w[l].reshape(1, D_GLA),
                 w_br_conv[l].astype(BF16), w_br_gla[l].astype(BF16), w_out[l].astype(BF16))
        moe_w = (norm2_w[l].reshape(1, D_MODEL), w_router[l], b_router[l],
                 w_gate_e[l], w_up_e[l], w_down_e[l], w_gate_s[l], w_up_s[l], w_down_s[l])

        proj_p = _in_proj(yp, mod_p, n1w, w_in_r, rows_per_mod=nb_p * len_p)
        yp, st = _mixer(proj_p, yp, mod_p, *mix_w, None, seq_len=len_p, row_len=len_p, emit_state=True)
        states.append(st)
        proj_s = _in_proj(ys, mod_s, n1w, w_in_r, rows_per_mod=len_s)
        (ys,) = _mixer(proj_s, ys, mod_s, *mix_w, state_gla[:, l], seq_len=len_s, row_len=GRID_W, emit_state=False)

        yp, ys = _moe(yp, ys, mod, *moe_w, fnw, tokens_per_mod=len_s, final=l == depth - 1)
    new_state = jnp.stack(states, axis=1)
    return (yp.reshape(nb_p, len_p, D_MODEL), ys.reshape(nb_s, len_s, D_MODEL), new_state)
```

```python
import functools

import jax
import jax.numpy as jnp
from jax import lax
from jax.experimental import pallas as pl
from jax.experimental.pallas import tpu as pltpu

F32 = jnp.float32
BF16 = jnp.bfloat16

D_MODEL = 1024
GRID_W = 64
D_CONV = 512
N_HEADS = 4
HEAD_D = 128
D_GLA = N_HEADS * HEAD_D
GLA_RANK = 16
GLA_GATE_NORM = 16.0
LOG2_E = 1.4426950408889634
CHUNK = 64
SUB = 8
N_SUB = CHUNK // SUB
N_EXPERTS = 64
TOP_K = 8
D_EXPERT = 256
ROUTED_SCALE = 2.5
EPS = 1e-6

C_U, C_GB, C_GC, C_Q, C_K, C_V, C_GO = 0, 512, 1024, 1536, 2048, 2560, 3072
C_BRC, C_BRG, C_LR = 3584, 4608, 5632
D_PROJ = 5760
LR_PAD = 128

VMEM_LIMIT = 56 * 1024 * 1024


def _dot(a, b):
    return jnp.dot(a, b, preferred_element_type=F32)


def _dot_nt(a, b):
    return lax.dot_general(a, b, (((1,), (1,)), ((), ())), preferred_element_type=F32)


def _dot_tn(a, b):
    return lax.dot_general(a, b, (((0,), (0,)), ((), ())), preferred_element_type=F32)


def _dot_hi(a, b):
    return jnp.dot(a, b, preferred_element_type=F32, precision=lax.Precision.HIGHEST)


def _split_bf16(x):
    hi = x.astype(BF16)
    lo = (x - hi.astype(F32)).astype(BF16)
    return hi, lo


def _rms(x):
    return x * lax.rsqrt(jnp.mean(x * x, axis=-1, keepdims=True) + EPS)


def _sigmoid(x):
    return 0.5 * jnp.tanh(0.5 * x) + 0.5


def _mod_kernel(cond_ref, w_ref, b_ref, o_ref):
    c = cond_ref[...]
    o_ref[...] = _dot_hi(c * jax.nn.sigmoid(c), w_ref[...]) + b_ref[...]


def _modulation(cond, w_mod, b_mod):
    n_rows = cond.shape[0]
    tn = 1536
    return pl.pallas_call(
        _mod_kernel,
        grid=(6 * D_MODEL // tn,),
        in_specs=[pl.BlockSpec((n_rows, D_MODEL), lambda j: (0, 0)),
                  pl.BlockSpec((D_MODEL, tn), lambda j: (0, j)),
                  pl.BlockSpec((1, tn), lambda j: (0, j))],
        out_specs=pl.BlockSpec((n_rows, tn), lambda j: (0, j)),
        out_shape=jax.ShapeDtypeStruct((n_rows, 6 * D_MODEL), F32),
        compiler_params=pltpu.CompilerParams(dimension_semantics=("arbitrary",),
                                             vmem_limit_bytes=VMEM_LIMIT),
        name="modulation",
    )(cond, w_mod, b_mod)


D_IN_PROJ = 5664
C_LR_SRC, C_GATES_SRC = 3584, 3616


def _w_in_prep_kernel(w_ref, o_ref):
    rc = 64

    def copy_rows(dst0, src0, n):
        def body(t, carry):
            off = pl.multiple_of(t * rc, rc)
            o_ref[pl.ds(dst0 + off, rc), :] = w_ref[pl.ds(src0 + off, rc), :].astype(BF16)
            return carry
        lax.fori_loop(0, n // rc, body, 0)

    copy_rows(0, 0, C_LR_SRC)
    copy_rows(C_BRC, C_GATES_SRC, 2 * D_MODEL)
    o_ref[C_LR:C_LR + 2 * GLA_RANK, :] = w_ref[C_LR_SRC:C_GATES_SRC, :].astype(BF16)
    o_ref[C_LR + 2 * GLA_RANK:D_PROJ, :] = jnp.zeros((D_PROJ - C_LR - 2 * GLA_RANK, D_MODEL), BF16)


def _w_in_prep(w_in_t):
    return pl.pallas_call(
        _w_in_prep_kernel,
        out_shape=jax.ShapeDtypeStruct((D_PROJ, D_MODEL), BF16),
        compiler_params=pltpu.CompilerParams(vmem_limit_bytes=VMEM_LIMIT),
        name="w_in_prep",
    )(w_in_t)


def _inproj_kernel(x_ref, mod_ref, nw_ref, w_ref, o_ref, *, rows_per_mod, tm):
    i = pl.program_id(1)
    row = (i * tm) // rows_per_mod
    sh = mod_ref[pl.ds(row, 1), 0:D_MODEL]
    sc = mod_ref[pl.ds(row, 1), D_MODEL:2 * D_MODEL]
    h = _rms(x_ref[...]) * nw_ref[...] * (1.0 + sc) + sh
    o_ref[...] = _dot_nt(h.astype(BF16), w_ref[...]).astype(BF16)


def _in_proj(x2d, mod, norm_w, w_in_r, rows_per_mod):
    t = x2d.shape[0]
    tm, tn = 1024, 1920
    kern = functools.partial(_inproj_kernel, rows_per_mod=rows_per_mod, tm=tm)
    return pl.pallas_call(
        kern,
        grid=(D_PROJ // tn, t // tm),
        in_specs=[pl.BlockSpec((tm, D_MODEL), lambda j, i: (i, 0)),
                  pl.BlockSpec(mod.shape, lambda j, i: (0, 0)),
                  pl.BlockSpec((1, D_MODEL), lambda j, i: (0, 0)),
                  pl.BlockSpec((tn, D_MODEL), lambda j, i: (j, 0))],
        out_specs=pl.BlockSpec((tm, tn), lambda j, i: (i, j)),
        out_shape=jax.ShapeDtypeStruct((t, D_PROJ), BF16),
        compiler_params=pltpu.CompilerParams(dimension_semantics=("arbitrary", "arbitrary"),
                                             vmem_limit_bytes=VMEM_LIMIT),
        name="in_proj",
    )(x2d, mod, norm_w, w_in_r)


def _log2_sigmoid(x):
    return jnp.minimum(x, 0.0) * LOG2_E - jnp.log2(1.0 + jnp.exp2(jnp.abs(x) * (-LOG2_E)))


def _gla_chunk_head(qc, kc, vc, bc, st, rev):
    lane = lax.broadcasted_iota(jnp.int32, (SUB, CHUNK), 1)
    sub = lax.broadcasted_iota(jnp.int32, (SUB, CHUNK), 0)
    tot = bc[0:1] if rev else bc[CHUNK - 1:CHUNK]

    o = _dot_nt((qc * jnp.exp2(bc)).astype(BF16), st.astype(BF16))
    k_tail = kc * jnp.exp2(tot - bc)
    st_new = st * jnp.exp2(tot) + _dot_tn(vc.astype(BF16), k_tail.astype(BF16))

    lhs_segs, rhs_segs = [], []

    def rows(before, mid, after):
        parts = ([jnp.zeros((before, HEAD_D), F32)] if before else []) + [mid]
        parts += [jnp.zeros((after, HEAD_D), F32)] if after else []
        return jnp.concatenate(parts, axis=0) if len(parts) > 1 else mid

    key_blocks = range(1, N_SUB) if rev else range(0, N_SUB - 1)
    for jb in key_blocks:
        r0 = jb * SUB
        ref_row = bc[r0:r0 + 1] if rev else bc[r0 + SUB - 1:r0 + SUB]
        ke = kc[r0:r0 + SUB] * jnp.exp2(ref_row - bc[r0:r0 + SUB])
        rhs_segs.append(rows(r0, ke, CHUNK - r0 - SUB))
        if rev:
            ql = qc[:r0] * jnp.exp2(bc[:r0] - ref_row)
            lhs_segs.append(rows(0, ql, CHUNK - r0))
        else:
            ql = qc[r0 + SUB:] * jnp.exp2(bc[r0 + SUB:] - ref_row)
            lhs_segs.append(rows(r0 + SUB, ql, 0))
    far = _dot_nt(jnp.concatenate(lhs_segs, axis=1).astype(BF16),
                  jnp.concatenate(rhs_segs, axis=1).astype(BF16))

    blocks = []
    for ib in range(N_SUB):
        r0 = ib * SUB
        qi, bi = qc[r0:r0 + SUB], bc[r0:r0 + SUB]
        acc = jnp.zeros((SUB, CHUNK), F32)
        for jj in range(SUB):
            j = r0 + jj
            e = jnp.exp2(bi - bc[j:j + 1])
            col = jnp.sum(qi * (kc[j:j + 1] * e), axis=-1, keepdims=True)
            acc = jnp.where(lane == j, col, acc)
        keep = (lane - r0 >= sub) if rev else (lane - r0 <= sub)
        blocks.append(jnp.where(keep, acc, 0.0))
    scores = far + jnp.concatenate(blocks, axis=0)
    o = o + _dot(scores.astype(BF16), vc.astype(BF16))
    return o, st_new


def _mixer_kernel(*refs, seq_len, row_len, has_s0, emit_state):
    it = iter(refs)
    proj_ref, x_ref, mod_ref, cw_ref, cb_ref, wdec_ref, bdec_ref, gnw_ref = (next(it) for _ in range(8))
    wbc_ref, wbg_ref, wout_ref = (next(it) for _ in range(3))
    s0_ref = next(it) if has_s0 else None
    out_ref = next(it)
    st_out_ref = next(it) if emit_state else None
    la_f_ref, la_b_ref, o_ref, st_ref = (next(it) for _ in range(4))

    L = seq_len
    n_chunks = L // CHUNK
    tr = 256
    assert tr % row_len == 0 and L % tr == 0

    ri = lax.broadcasted_iota(jnp.int32, (tr, tr), 0)
    ci = lax.broadcasted_iota(jnp.int32, (tr, tr), 1)
    same_chunk = (ri // CHUNK) == (ci // CHUNK)

    def decay_body(t, carry):
        r0 = pl.multiple_of(t * tr, tr)
        lr = proj_ref[pl.ds(r0, tr), C_LR:C_LR + LR_PAD]
        for d, ref in ((0, la_f_ref), (1, la_b_ref)):
            whi, wlo = _split_bf16(wdec_ref[d])
            z = _dot(lr, whi) + _dot(lr, wlo) + bdec_ref[d:d + 1]
            la_hi, la_lo = _split_bf16(_log2_sigmoid(z) * (1.0 / GLA_GATE_NORM))
            tri = jnp.where(same_chunk & ((ci >= ri) if d else (ci <= ri)), 1.0, 0.0).astype(BF16)
            ref[pl.ds(r0, tr), :] = _dot(tri, la_hi) + _dot(tri, la_lo)
        return carry
    lax.fori_loop(0, L // tr, decay_body, 0)

    for rev, la_ref in ((False, la_f_ref), (True, la_b_ref)):
        d = 1 if rev else 0
        for h in range(N_HEADS):
            if has_s0:
                st_ref[h] = s0_ref[0, d, h].T
            else:
                st_ref[h] = jnp.zeros((HEAD_D, HEAD_D), F32)

        def chunk_body(c, carry, rev=rev, la_ref=la_ref, heads=range(N_HEADS)):
            cc = (n_chunks - 1 - c) if rev else c
            r0 = pl.multiple_of(cc * CHUNK, CHUNK)
            for h in heads:
                lo, hi = h * HEAD_D, (h + 1) * HEAD_D
                qc = proj_ref[pl.ds(r0, CHUNK), C_Q + lo:C_Q + hi].astype(F32) * (HEAD_D ** -0.5)
                kc = proj_ref[pl.ds(r0, CHUNK), C_K + lo:C_K + hi].astype(F32)
                vc = proj_ref[pl.ds(r0, CHUNK), C_V + lo:C_V + hi].astype(F32)
                oc, st_new = _gla_chunk_head(qc, kc, vc, la_ref[pl.ds(r0, CHUNK), lo:hi], st_ref[h], rev)
                st_ref[h] = st_new
                if rev:
                    o_ref[pl.ds(r0, CHUNK), lo:hi] += oc
                else:
                    o_ref[pl.ds(r0, CHUNK), lo:hi] = oc
            return carry
        lax.fori_loop(0, n_chunks, chunk_body, 0, unroll=4)

        if emit_state:
            for h in range(N_HEADS):
                st_out_ref[0, d, h] = st_ref[h].T

    mod_row = pl.program_id(0) if mod_ref.shape[0] > 1 else 0
    g1 = mod_ref[pl.ds(mod_row, 1), 2 * D_MODEL:3 * D_MODEL]
    hsel_r = lax.broadcasted_iota(jnp.int32, (D_GLA, D_GLA), 0) // HEAD_D
    hsel_c = lax.broadcasted_iota(jnp.int32, (D_GLA, D_GLA), 1) // HEAD_D
    head_avg = jnp.where(hsel_r == hsel_c, 1.0 / HEAD_D, 0.0).astype(BF16)
    pos = lax.broadcasted_iota(jnp.int32, (tr, 1), 0)

    def dense_body(t, carry):
        r0 = pl.multiple_of(t * tr, tr)
        rows = pl.ds(r0, tr)
        cu = (proj_ref[rows, C_GC:C_GC + D_CONV].astype(F32) * proj_ref[rows, C_U:C_U + D_CONV].astype(F32))
        in_row = pos % row_len
        left = jnp.where(in_row == 0, 0.0, pltpu.roll(cu, 1, axis=0))
        right = jnp.where(in_row == row_len - 1, 0.0, pltpu.roll(cu, tr - 1, axis=0))
        conv = cw_ref[0:1] * left + cw_ref[1:2] * cu + cw_ref[2:3] * right + cb_ref[...]
        y_conv = proj_ref[rows, C_GB:C_GB + D_CONV].astype(F32) * conv

        o = o_ref[rows, :]
        osq_hi, osq_lo = _split_bf16(o * o)
        ms = _dot(osq_hi, head_avg) + _dot(osq_lo, head_avg)
        g_out = proj_ref[rows, C_GO:C_GO + D_GLA].astype(F32)
        y_gla = o * lax.rsqrt(ms + EPS) * gnw_ref[...] * (g_out * _sigmoid(g_out))

        merged = (_sigmoid(proj_ref[rows, C_BRC:C_BRC + D_MODEL].astype(F32)) * _dot(y_conv.astype(BF16), wbc_ref[...])
                  + _sigmoid(proj_ref[rows, C_BRG:C_BRG + D_MODEL].astype(F32)) * _dot(y_gla.astype(BF16), wbg_ref[...]))
        out_ref[rows, :] = x_ref[rows, :] + g1 * _dot(merged.astype(BF16), wout_ref[...])
        return carry
    lax.fori_loop(0, L // tr, dense_body, 0)


def _mixer(proj, x2d, mod, conv_w, conv_b, wdec, bdec, gnw, wbc, wbg, wout, s0, *, seq_len, row_len, emit_state):
    t = x2d.shape[0]
    nb = t // seq_len
    has_s0 = s0 is not None
    one = pl.Buffered(1)
    const = lambda shape: pl.BlockSpec(shape, lambda b: (0,) * len(shape), pipeline_mode=one)
    in_specs = [pl.BlockSpec((seq_len, D_PROJ), lambda b: (b, 0), pipeline_mode=one),
                pl.BlockSpec((seq_len, D_MODEL), lambda b: (b, 0)),
                const(mod.shape),
                const((3, D_CONV)), const((1, D_CONV)), const((2, LR_PAD, D_GLA)), const((2, D_GLA)),
                const((1, D_GLA)), const((D_CONV, D_MODEL)), const((D_GLA, D_MODEL)), const((D_MODEL, D_MODEL))]
    args = [proj, x2d, mod, conv_w, conv_b, wdec, bdec, gnw, wbc, wbg, wout]
    if has_s0:
        in_specs.append(pl.BlockSpec((1, 2, N_HEADS, HEAD_D, HEAD_D), lambda b: (b, 0, 0, 0, 0)))
        args.append(s0)
    out_specs = [pl.BlockSpec((seq_len, D_MODEL), lambda b: (b, 0))]
    out_shape = [jax.ShapeDtypeStruct((t, D_MODEL), F32)]
    if emit_state:
        out_specs.append(pl.BlockSpec((1, 2, N_HEADS, HEAD_D, HEAD_D), lambda b: (b, 0, 0, 0, 0)))
        out_shape.append(jax.ShapeDtypeStruct((nb, 2, N_HEADS, HEAD_D, HEAD_D), F32))
    kern = functools.partial(_mixer_kernel, seq_len=seq_len, row_len=row_len, has_s0=has_s0, emit_state=emit_state)
    return pl.pallas_call(
        kern,
        grid=(nb,),
        in_specs=in_specs,
        out_specs=out_specs,
        out_shape=out_shape,
        scratch_shapes=[pltpu.VMEM((seq_len, D_GLA), F32), pltpu.VMEM((seq_len, D_GLA), F32),
                        pltpu.VMEM((seq_len, D_GLA), F32), pltpu.VMEM((N_HEADS, HEAD_D, HEAD_D), F32)],
        compiler_params=pltpu.CompilerParams(dimension_semantics=("arbitrary",),
                                             vmem_limit_bytes=VMEM_LIMIT),
        name="mixer",
    )(*args)


I32 = jnp.int32
TB = 256
ROW_ALIGN = 16
G_ALIGN = 256
E_CHUNK = 2048
R_LOC = 3072
H2W = 1152


def _select_x(i, n_p_tiles, xp_ref, xs_ref):
    return jnp.where(i < n_p_tiles, xp_ref[...], xs_ref[...])


def _mod_row(i, n_p_tiles, tiles_per_mod):
    return jnp.where(i < n_p_tiles, 0, 1 + (i - n_p_tiles) // tiles_per_mod)


def _route_kernel(xp_ref, xs_ref, mod_ref, n2w_ref, wrt_ref, brb_ref, h2_ref, lpos_ref, cnt_ref, *,
                  n_p_tiles, tiles_per_mod):
    i = pl.program_id(0)
    row = _mod_row(i, n_p_tiles, tiles_per_mod)
    x = _select_x(i, n_p_tiles, xp_ref, xs_ref)
    sh = mod_ref[pl.ds(row, 1), 3 * D_MODEL:4 * D_MODEL]
    sc = mod_ref[pl.ds(row, 1), 4 * D_MODEL:5 * D_MODEL]
    h2 = _rms(x) * n2w_ref[...] * (1.0 + sc) + sh

    h_hi, h_lo = _split_bf16(h2)
    w_hi, w_lo = _split_bf16(wrt_ref[...])
    scores = jax.nn.sigmoid(_dot_nt(w_hi, h_hi) + (_dot_nt(w_hi, h_lo) + _dot_nt(w_lo, h_hi)))
    biased = scores + brb_ref[...]
    eidx = lax.broadcasted_iota(I32, scores.shape, 0)
    picks = []
    for _k in range(TOP_K):
        m = jnp.max(biased, axis=0, keepdims=True)
        first = jnp.min(jnp.where(biased == m, eidx, N_EXPERTS), axis=0, keepdims=True)
        pick = eidx == first
        picks.append(pick)
        biased = jnp.where(pick, -jnp.inf, biased)
    sel = jnp.zeros(scores.shape, F32)
    for pick in picks:
        sel = jnp.where(pick, 1.0, sel)
    selsc = sel * scores
    comb = selsc / jnp.sum(selsc, axis=0, keepdims=True) * ROUTED_SCALE

    selb = sel.astype(BF16)
    tr_ = lax.broadcasted_iota(I32, (TB, TB), 0)
    tc_ = lax.broadcasted_iota(I32, (TB, TB), 1)
    rank = _dot(selb, jnp.where(tr_ < tc_, 1.0, 0.0).astype(BF16))
    n_b = _dot(selb, jnp.ones((TB, 128), BF16))
    m_b = jnp.maximum(jnp.floor((n_b + (ROW_ALIGN - 1)) * (1.0 / ROW_ALIGN)), 1.0) * ROW_ALIGN
    er_ = lax.broadcasted_iota(I32, (N_EXPERTS, N_EXPERTS), 0)
    ec_ = lax.broadcasted_iota(I32, (N_EXPERTS, N_EXPERTS), 1)
    loff_b = _dot(jnp.where(ec_ < er_, 1.0, 0.0).astype(BF16), m_b.astype(BF16))
    lposf = jnp.concatenate([loff_b] * (TB // 128), axis=1) + rank
    rows = [jnp.sum(jnp.where(pick, lposf, 0.0), axis=0, keepdims=True) for pick in picks]
    lpos_ref[0] = jnp.concatenate(rows, axis=0).astype(I32)
    cnt_ref[0] = m_b

    combt = comb.T
    chi = combt.astype(BF16).astype(F32)
    h2_ref[:, 0:D_MODEL] = h_hi
    h2_ref[:, D_MODEL:H2W] = jnp.concatenate([chi, combt - chi], axis=1).astype(BF16)


def _route(x1p, x1s, mod, n2w, w_router_t, b_router_b, *, tiles_per_mod):
    n_p, n_s = x1p.shape[0] // TB, x1s.shape[0] // TB
    nt = n_p + n_s
    kern = functools.partial(_route_kernel, n_p_tiles=n_p, tiles_per_mod=tiles_per_mod)
    const = lambda shape: pl.BlockSpec(shape, lambda i: (0,) * len(shape))
    return pl.pallas_call(
        kern,
        grid=(nt,),
        in_specs=[pl.BlockSpec((TB, D_MODEL), lambda i: (jnp.minimum(i, n_p - 1), 0)),
                  pl.BlockSpec((TB, D_MODEL), lambda i: (jnp.maximum(i - n_p, 0), 0)),
                  const(mod.shape), const((1, D_MODEL)), const((N_EXPERTS, D_MODEL)), const((N_EXPERTS, TB))],
        out_specs=[pl.BlockSpec((TB, H2W), lambda i: (i, 0)),
                   pl.BlockSpec((1, TOP_K, TB), lambda i: (i, 0, 0)),
                   pl.BlockSpec((1, N_EXPERTS, 128), lambda i: (i, 0, 0))],
        out_shape=[jax.ShapeDtypeStruct((nt * TB, H2W), BF16),
                   jax.ShapeDtypeStruct((nt, TOP_K, TB), I32),
                   jax.ShapeDtypeStruct((nt, N_EXPERTS, 128), F32)],
        compiler_params=pltpu.CompilerParams(dimension_semantics=("arbitrary",), vmem_limit_bytes=VMEM_LIMIT),
        name="moe_route",
    )(x1p, x1s, mod, n2w, w_router_t, b_router_b)


def _plan_kernel(cnt_ref, off_ref, loff_ref, msz_ref, grp_ref, *, nt):
    lane = lax.broadcasted_iota(I32, (N_EXPERTS, 128), 1)
    m = jnp.zeros((N_EXPERTS, 128), F32)
    for i in range(nt):
        m = jnp.where(lane == i, cnt_ref[i], m)
    total = jnp.broadcast_to(jnp.sum(m, axis=1, keepdims=True), (N_EXPERTS, 128))
    gsz = jnp.floor((total + (G_ALIGN - 1)) * (1.0 / G_ALIGN)) * G_ALIGN
    er_ = lax.broadcasted_iota(I32, (N_EXPERTS, N_EXPERTS), 0)
    ec_ = lax.broadcasted_iota(I32, (N_EXPERTS, N_EXPERTS), 1)
    lstrict = jnp.where(ec_ < er_, 1.0, 0.0)
    ir_ = lax.broadcasted_iota(I32, (128, 128), 0)
    ic_ = lax.broadcasted_iota(I32, (128, 128), 1)
    ustrict = jnp.where(ir_ < ic_, 1.0, 0.0)
    gstart = _dot_hi(lstrict, gsz)
    off_ref[...] = (gstart + _dot_hi(m, ustrict)).astype(I32)
    loff_ref[...] = _dot_hi(lstrict, m).astype(I32)
    msz_ref[...] = m.astype(I32)
    grp = jnp.where(lane == 0, gstart + total, jnp.where(lane == 1, gsz - total, jnp.where(lane == 2, gstart, gsz)))
    grp_ref[...] = grp.astype(I32)


def _plan(cnt):
    nt = cnt.shape[0]
    assert nt <= 128
    tab = jax.ShapeDtypeStruct((N_EXPERTS, 128), I32)
    return pl.pallas_call(
        functools.partial(_plan_kernel, nt=nt),
        out_shape=[tab, tab, tab, tab],
        compiler_params=pltpu.CompilerParams(vmem_limit_bytes=VMEM_LIMIT),
        name="moe_plan",
    )(cnt)


def _start_copies(msz_ref, tile, make_copy):
    for e in range(N_EXPERTS):
        make_copy(e, pl.multiple_of(msz_ref[e, tile], ROW_ALIGN)).start()


def _tile_rows(loff_ref, msz_ref, tile):
    return pl.multiple_of(loff_ref[N_EXPERTS - 1, tile] + msz_ref[N_EXPERTS - 1, tile], ROW_ALIGN)


def _dispatch_kernel(off_ref, loff_ref, msz_ref, tail_ref, h2_ref, lpos_ref, xs_hbm, xloc_ref, zero_ref, sem,
                     tail_sem, *, nt):
    i = pl.program_id(0)
    slot = i % 2

    def copy_for(tile, slot_):
        def make(e, m):
            lo = pl.multiple_of(loff_ref[e, tile], ROW_ALIGN)
            of = pl.multiple_of(off_ref[e, tile], ROW_ALIGN)
            return pltpu.make_async_copy(xloc_ref.at[slot_, pl.ds(lo, m)], xs_hbm.at[pl.ds(of, m)], sem.at[slot_])
        return make

    def wait_tile(tile, slot_):
        n = _tile_rows(loff_ref, msz_ref, tile)
        pltpu.make_async_copy(xloc_ref.at[slot_, pl.ds(0, n)], xs_hbm.at[pl.ds(0, n)], sem.at[slot_]).wait()

    lpos = lpos_ref[0].astype(jnp.int16)
    h2 = h2_ref[...]
    ck = 1024
    one, zero = jnp.ones((ck, TB), BF16), jnp.zeros((ck, TB), BF16)
    for c in range(R_LOC // ck):
        r = (lax.broadcasted_iota(I32, (ck, TB), 0) + c * ck).astype(jnp.int16)
        d = zero
        for k in range(TOP_K):
            d = jnp.where(r == lpos[k:k + 1, :], one, d)
        res = _dot(d, h2)
        xloc_ref[slot, c * ck:(c + 1) * ck, :] = res.astype(BF16)

    _start_copies(msz_ref, i, copy_for(i, slot))

    @pl.when(i > 0)
    def _():
        wait_tile(i - 1, 1 - slot)

    @pl.when(i == nt - 1)
    def _():
        zero_ref[...] = jnp.zeros(zero_ref.shape, BF16)

        def tail_copies(start):
            def body(e, carry):
                n = tail_ref[e, 1]

                @pl.when(n > 0)
                def _():
                    st = pl.multiple_of(tail_ref[e, 0], ROW_ALIGN)
                    nn = pl.multiple_of(n, ROW_ALIGN)
                    cp = pltpu.make_async_copy(zero_ref.at[pl.ds(0, nn)], xs_hbm.at[pl.ds(st, nn)], tail_sem)
                    if start:
                        cp.start()
                    else:
                        cp.wait()
                return carry
            lax.fori_loop(0, N_EXPERTS, body, 0)
        tail_copies(True)
        wait_tile(i, slot)
        tail_copies(False)


def _dispatch(off, loff, msz, tail, h2ext, lpos, n_rows):
    nt = lpos.shape[0]
    grid_spec = pltpu.PrefetchScalarGridSpec(
        num_scalar_prefetch=4,
        grid=(nt,),
        in_specs=[pl.BlockSpec((TB, H2W), lambda i, *_: (i, 0)),
                  pl.BlockSpec((1, TOP_K, TB), lambda i, *_: (i, 0, 0))],
        out_specs=pl.BlockSpec(memory_space=pl.ANY),
        scratch_shapes=[pltpu.VMEM((2, R_LOC, H2W), BF16), pltpu.VMEM((G_ALIGN, H2W), BF16),
                        pltpu.SemaphoreType.DMA((2,)), pltpu.SemaphoreType.DMA],
    )
    return pl.pallas_call(
        functools.partial(_dispatch_kernel, nt=nt),
        grid_spec=grid_spec,
        out_shape=jax.ShapeDtypeStruct((n_rows, H2W), BF16),
        compiler_params=pltpu.CompilerParams(dimension_semantics=("arbitrary",), vmem_limit_bytes=VMEM_LIMIT),
        name="moe_dispatch",
    )(off, loff, msz, tail, h2ext, lpos)


def _expert_kernel(grp_ref, wg_ref, wu_ref, wd_ref, xs_hbm, ys_hbm, xbuf, ybuf, wgu_ref, wdb_ref, st_ref,
                   in_sem, out_sem):
    e = pl.program_id(0)
    n_exp = pl.num_programs(0)

    def in_copy(row0, n, slot):
        return pltpu.make_async_copy(xs_hbm.at[pl.ds(row0, n)], xbuf.at[slot, pl.ds(0, n)], in_sem.at[slot])

    def out_copy(row0, n, slot):
        return pltpu.make_async_copy(ybuf.at[slot, pl.ds(0, n)], ys_hbm.at[pl.ds(row0, n)], out_sem.at[slot])

    def rows_of(ex):
        return grp_ref[jnp.minimum(ex, n_exp - 1), 3]

    def next_nonempty(ex):
        return lax.while_loop(lambda c: (c < n_exp) & (rows_of(c) == 0), lambda c: c + 1, ex + 1)

    def start_first_chunk(ex, slot):
        @pl.when(ex < n_exp)
        def _():
            exc = jnp.minimum(ex, n_exp - 1)
            n = pl.multiple_of(jnp.minimum(grp_ref[exc, 3], E_CHUNK), G_ALIGN)
            in_copy(pl.multiple_of(grp_ref[exc, 2], G_ALIGN), n, slot).start()

    def drain_out(slot):
        pend = st_ref[1 + slot]

        @pl.when(pend > 0)
        def _():
            out_copy(0, pl.multiple_of(pend, G_ALIGN), slot).wait()
            st_ref[1 + slot] = 0

    @pl.when(e == 0)
    def _():
        st_ref[0] = 0
        st_ref[1] = 0
        st_ref[2] = 0
        start_first_chunk(next_nonempty(-1), 0)

    g0 = grp_ref[e, 2]
    gn = grp_ref[e, 3]

    @pl.when(gn > 0)
    def _():
        wgu_ref[:, :D_EXPERT] = wg_ref[0].astype(BF16)
        wgu_ref[:, D_EXPERT:] = wu_ref[0].astype(BF16)
        wdb_ref[...] = wd_ref[0].astype(BF16)
        n_chunks = (gn + (E_CHUNK - 1)) // E_CHUNK

        def compute(n, slot):
            x = xbuf[slot, 0:n, 0:D_MODEL]
            ext = xbuf[slot, 0:n, D_MODEL:H2W].astype(F32)
            wts = ext[:, :N_EXPERTS] + ext[:, N_EXPERTS:]
            lane = lax.broadcasted_iota(I32, wts.shape, 1)
            w = jnp.sum(jnp.where(lane == e, wts, 0.0), axis=-1, keepdims=True)
            h = _dot(x, wgu_ref[...])
            hg, hu = h[:, :D_EXPERT], h[:, D_EXPERT:]
            act = hg * _sigmoid(hg) * hu * w
            ybuf[slot, 0:n, :] = _dot(act.astype(BF16), wdb_ref[...]).astype(BF16)

        def chunk_body(c, slot):
            row0 = pl.multiple_of(g0 + c * E_CHUNK, G_ALIGN)
            n = pl.multiple_of(jnp.minimum(gn - c * E_CHUNK, E_CHUNK), G_ALIGN)
            in_copy(row0, n, slot).wait()

            @pl.when(c + 1 < n_chunks)
            def _():
                n1 = pl.multiple_of(jnp.minimum(gn - (c + 1) * E_CHUNK, E_CHUNK), G_ALIGN)
                in_copy(pl.multiple_of(row0 + E_CHUNK, G_ALIGN), n1, 1 - slot).start()

            @pl.when(c + 1 == n_chunks)
            def _():
                start_first_chunk(next_nonempty(e), 1 - slot)

            drain_out(slot)
            for v in range(G_ALIGN, E_CHUNK + 1, G_ALIGN):
                @pl.when(n == v)
                def _(v=v):
                    compute(v, slot)
            out_copy(row0, n, slot).start()
            st_ref[1 + slot] = n
            return 1 - slot

        st_ref[0] = lax.fori_loop(0, n_chunks, chunk_body, st_ref[0])

    @pl.when(e == n_exp - 1)
    def _():
        drain_out(0)
        drain_out(1)


def _experts(grp, xs, wg, wu, wd):
    w_in = pl.BlockSpec((1, D_MODEL, D_EXPERT), lambda e, grp_ref: (e, 0, 0))
    grid_spec = pltpu.PrefetchScalarGridSpec(
        num_scalar_prefetch=1,
        grid=(N_EXPERTS,),
        in_specs=[w_in, w_in, pl.BlockSpec((1, D_EXPERT, D_MODEL), lambda e, grp_ref: (e, 0, 0)),
                  pl.BlockSpec(memory_space=pl.ANY)],
        out_specs=pl.BlockSpec(memory_space=pl.ANY),
        scratch_shapes=[pltpu.VMEM((2, E_CHUNK, H2W), BF16), pltpu.VMEM((2, E_CHUNK, D_MODEL), BF16),
                        pltpu.VMEM((D_MODEL, 2 * D_EXPERT), BF16), pltpu.VMEM((D_EXPERT, D_MODEL), BF16),
                        pltpu.SMEM((4,), I32), pltpu.SemaphoreType.DMA((2,)), pltpu.SemaphoreType.DMA((2,))],
    )
    return pl.pallas_call(
        _expert_kernel,
        grid_spec=grid_spec,
        out_shape=jax.ShapeDtypeStruct((xs.shape[0], D_MODEL), BF16),
        compiler_params=pltpu.CompilerParams(dimension_semantics=("arbitrary",), vmem_limit_bytes=VMEM_LIMIT),
        name="moe_experts",
    )(grp, wg, wu, wd, xs)


def _combine_kernel(off_ref, loff_ref, msz_ref, lpos_ref, h2_ref, xp_ref, xs_ref, mod_ref, wgs_ref, wus_ref, wds_ref,
                    fnw_ref, ysrt_hbm, yp_ref, ys_ref, yloc_ref, acc_ref, sem, *, nt, n_p_tiles, tiles_per_mod,
                    final):
    i = pl.program_id(0)
    slot = i % 2

    def copy_for(tile, slot_):
        def make(e, m):
            lo = pl.multiple_of(loff_ref[e, tile], ROW_ALIGN)
            of = pl.multiple_of(off_ref[e, tile], ROW_ALIGN)
            return pltpu.make_async_copy(ysrt_hbm.at[pl.ds(of, m)], yloc_ref.at[slot_, pl.ds(lo, m)], sem.at[slot_])
        return make

    @pl.when(i == 0)
    def _():
        yloc_ref[...] = jnp.zeros(yloc_ref.shape, BF16)
        _start_copies(msz_ref, 0, copy_for(0, 0))

    nxt = jnp.minimum(i + 1, nt - 1)
    _start_copies(msz_ref, nxt, copy_for(nxt, 1 - slot))

    hb = h2_ref[...]
    hg = _dot(hb, wgs_ref[...].astype(BF16))
    hu = _dot(hb, wus_ref[...].astype(BF16))
    acc_ref[...] = _dot((hg * _sigmoid(hg) * hu).astype(BF16), wds_ref[...].astype(BF16))

    def wait_tile(tile, slot_):
        n = _tile_rows(loff_ref, msz_ref, tile)
        pltpu.make_async_copy(ysrt_hbm.at[pl.ds(0, n)], yloc_ref.at[slot_, pl.ds(0, n)], sem.at[slot_]).wait()

    wait_tile(i, slot)

    @pl.when(i == nt - 1)
    def _():
        wait_tile(i, 1 - slot)

    lpos_pad = jnp.concatenate([lpos_ref[0].astype(F32), jnp.zeros((128 - TOP_K, TB), F32)], axis=0)
    lposc = lpos_pad.T.astype(I32)
    ck = 512
    cols = [jnp.broadcast_to(lposc[:, k:k + 1], (TB, ck)).astype(jnp.int16) for k in range(TOP_K)]
    one, zero = jnp.ones((TB, ck), BF16), jnp.zeros((TB, ck), BF16)
    for c in range(R_LOC // ck):
        r = (lax.broadcasted_iota(I32, (TB, ck), 1) + c * ck).astype(jnp.int16)
        cm = zero
        for k in range(TOP_K):
            cm = jnp.where(r == cols[k], one, cm)
        acc_ref[...] += _dot(cm, yloc_ref[slot, c * ck:(c + 1) * ck, :])

    row = _mod_row(i, n_p_tiles, tiles_per_mod)
    g2 = mod_ref[pl.ds(row, 1), 5 * D_MODEL:6 * D_MODEL]
    x2 = _select_x(i, n_p_tiles, xp_ref, xs_ref) + g2 * acc_ref[...]
    y = _rms(x2) * fnw_ref[...] if final else x2

    @pl.when(i < n_p_tiles)
    def _():
        yp_ref[...] = y

    @pl.when(i >= n_p_tiles)
    def _():
        ys_ref[...] = y


def _combine(off, loff, msz, lpos, h2ext, x1p, x1s, mod, wgs, wus, wds, fnw, ysorted, *, tiles_per_mod, final):
    n_p, n_s = x1p.shape[0] // TB, x1s.shape[0] // TB
    nt = n_p + n_s
    const = lambda shape: pl.BlockSpec(shape, lambda i, *_: (0,) * len(shape), pipeline_mode=pl.Buffered(1))
    p_idx = lambda i, *_: (jnp.minimum(i, n_p - 1), 0)
    s_idx = lambda i, *_: (jnp.maximum(i - n_p, 0), 0)
    grid_spec = pltpu.PrefetchScalarGridSpec(
        num_scalar_prefetch=3,
        grid=(nt,),
        in_specs=[pl.BlockSpec((1, TOP_K, TB), lambda i, *_: (i, 0, 0)),
                  pl.BlockSpec((TB, D_MODEL), lambda i, *_: (i, 0)),
                  pl.BlockSpec((TB, D_MODEL), p_idx), pl.BlockSpec((TB, D_MODEL), s_idx),
                  const(mod.shape), const((D_MODEL, D_EXPERT)), const((D_MODEL, D_EXPERT)), const((D_EXPERT, D_MODEL)),
                  const((1, D_MODEL)), pl.BlockSpec(memory_space=pl.ANY)],
        out_specs=[pl.BlockSpec((TB, D_MODEL), p_idx), pl.BlockSpec((TB, D_MODEL), s_idx)],
        scratch_shapes=[pltpu.VMEM((2, R_LOC, D_MODEL), BF16), pltpu.VMEM((TB, D_MODEL), F32),
                        pltpu.SemaphoreType.DMA((2,))],
    )
    kern = functools.partial(_combine_kernel, nt=nt, n_p_tiles=n_p, tiles_per_mod=tiles_per_mod, final=final)
    return pl.pallas_call(
        kern,
        grid_spec=grid_spec,
        out_shape=[jax.ShapeDtypeStruct(x1p.shape, F32), jax.ShapeDtypeStruct(x1s.shape, F32)],
        compiler_params=pltpu.CompilerParams(dimension_semantics=("arbitrary",), vmem_limit_bytes=VMEM_LIMIT),
        name="moe_combine",
    )(off, loff, msz, lpos, h2ext, x1p, x1s, mod, wgs, wus, wds, fnw, ysorted)


def _moe(x1p, x1s, mod, n2w, w_router, b_router, wg, wu, wd, wgs, wus, wds, fnw, *, tokens_per_mod, final):
    assert R_LOC >= TB * TOP_K + N_EXPERTS * ROW_ALIGN and tokens_per_mod % TB == 0
    nt = (x1p.shape[0] + x1s.shape[0]) // TB
    n_rows_max = nt * TB * TOP_K + nt * N_EXPERTS * ROW_ALIGN + N_EXPERTS * (G_ALIGN - ROW_ALIGN)
    tiles_per_mod = tokens_per_mod // TB
    brb = jnp.broadcast_to(b_router.reshape(N_EXPERTS, 1), (N_EXPERTS, TB))
    h2ext, lpos, cnt = _route(x1p, x1s, mod, n2w, w_router.T, brb, tiles_per_mod=tiles_per_mod)
    off, loff, msz, grp = _plan(cnt)
    xs = _dispatch(off, loff, msz, grp, h2ext, lpos, n_rows_max)
    ysorted = _experts(grp, xs, wg, wu, wd)
    return _combine(off, loff, msz, lpos, h2ext, x1p, x1s, mod, wgs, wus, wds, fnw, ysorted,
                    tiles_per_mod=tiles_per_mod, final=final)


def kernel(x_prompt, x_sample, state_gla, c, c_ctx, w_mod, b_mod, norm1_w, w_in, conv_w, conv_b, w_decay, b_decay,
           gla_norm_w, w_br_conv, w_br_gla, w_out, norm2_w, w_router, b_router, w_gate_e, w_up_e, w_down_e,
           w_gate_s, w_up_s, w_down_s, final_norm_w):
    depth = w_mod.shape[0]
    nb_p, len_p, _ = x_prompt.shape
    nb_s, len_s, _ = x_sample.shape
    yp = x_prompt.reshape(nb_p * len_p, D_MODEL)
    ys = x_sample.reshape(nb_s * len_s, D_MODEL)
    fnw = final_norm_w.reshape(1, D_MODEL)

    cond = jnp.concatenate([c_ctx[None, :], c, jnp.zeros((8 - 1 - nb_s, D_MODEL), F32)], axis=0)
    states = []
    for l in range(depth):
        mod = _modulation(cond, w_mod[l], b_mod[l].reshape(1, -1))
        mod_p, mod_s = mod[0:1], mod[1:1 + nb_s]

        w_in_r = _w_in_prep(w_in[l].T)
        wdec = jnp.zeros((2, LR_PAD, D_GLA), F32)
        wdec = wdec.at[0, 0:GLA_RANK].set(w_decay[l, 0]).at[1, GLA_RANK:2 * GLA_RANK].set(w_decay[l, 1])
        n1w = norm1_w[l].reshape(1, D_MODEL)
        mix_w = (conv_w[l], conv_b[l].reshape(1, D_CONV), wdec, b_decay[l], gla_norm_w[l].reshape(1, D_GLA),
                 w_br_conv[l].astype(BF16), w_br_gla[l].astype(BF16), w_out[l].astype(BF16))
        moe_w = (norm2_w[l].reshape(1, D_MODEL), w_router[l], b_router[l],
                 w_gate_e[l], w_up_e[l], w_down_e[l], w_gate_s[l], w_up_s[l], w_down_s[l])

        proj_p = _in_proj(yp, mod_p, n1w, w_in_r, rows_per_mod=nb_p * len_p)
        yp, st = _mixer(proj_p, yp, mod_p, *mix_w, None, seq_len=len_p, row_len=len_p, emit_state=True)
        states.append(st)
        proj_s = _in_proj(ys, mod_s, n1w, w_in_r, rows_per_mod=len_s)
        (ys,) = _mixer(proj_s, ys, mod_s, *mix_w, state_gla[:, l], seq_len=len_s, row_len=GRID_W, emit_state=False)

        yp, ys = _moe(yp, ys, mod, *moe_w, fnw, tokens_per_mod=len_s, final=l == depth - 1)
    new_state = jnp.stack(states, axis=1)
    return (yp.reshape(nb_p, len_p, D_MODEL), ys.reshape(nb_s, len_s, D_MODEL), new_state)
```

```python
import functools

import jax
import jax.numpy as jnp
from jax import lax
from jax.experimental import pallas as pl
from jax.experimental.pallas import tpu as pltpu

F32 = jnp.float32
BF16 = jnp.bfloat16

D_MODEL = 1024
GRID_W = 64
D_CONV = 512
N_HEADS = 4
HEAD_D = 128
D_GLA = N_HEADS * HEAD_D
GLA_RANK = 16
GLA_GATE_NORM = 16.0
LOG2_E = 1.4426950408889634
CHUNK = 64
SUB = 8
N_SUB = CHUNK // SUB
N_EXPERTS = 64
TOP_K = 8
D_EXPERT = 256
ROUTED_SCALE = 2.5
EPS = 1e-6

C_U, C_GB, C_GC, C_Q, C_K, C_V, C_GO = 0, 512, 1024, 1536, 2048, 2560, 3072
C_BRC, C_BRG, C_LR = 3584, 4608, 5632
D_PROJ = 5760
LR_PAD = 128

VMEM_LIMIT = 56 * 1024 * 1024


def _dot(a, b):
    return jnp.dot(a, b, preferred_element_type=F32)


def _dot_nt(a, b):
    return lax.dot_general(a, b, (((1,), (1,)), ((), ())), preferred_element_type=F32)


def _dot_tn(a, b):
    return lax.dot_general(a, b, (((0,), (0,)), ((), ())), preferred_element_type=F32)


def _dot_hi(a, b):
    return jnp.dot(a, b, preferred_element_type=F32, precision=lax.Precision.HIGHEST)


def _split_bf16(x):
    hi = x.astype(BF16)
    lo = (x - hi.astype(F32)).astype(BF16)
    return hi, lo


def _rms(x):
    return x * lax.rsqrt(jnp.mean(x * x, axis=-1, keepdims=True) + EPS)


def _sigmoid(x):
    return 0.5 * jnp.tanh(0.5 * x) + 0.5


def _mod_kernel(cond_ref, w_ref, b_ref, o_ref):
    c = cond_ref[...]
    o_ref[...] = _dot_hi(c * jax.nn.sigmoid(c), w_ref[...]) + b_ref[...]


def _modulation(cond, w_mod, b_mod):
    n_rows = cond.shape[0]
    tn = 1536
    return pl.pallas_call(
        _mod_kernel,
        grid=(6 * D_MODEL // tn,),
        in_specs=[pl.BlockSpec((n_rows, D_MODEL), lambda j: (0, 0)),
                  pl.BlockSpec((D_MODEL, tn), lambda j: (0, j)),
                  pl.BlockSpec((1, tn), lambda j: (0, j))],
        out_specs=pl.BlockSpec((n_rows, tn), lambda j: (0, j)),
        out_shape=jax.ShapeDtypeStruct((n_rows, 6 * D_MODEL), F32),
        compiler_params=pltpu.CompilerParams(dimension_semantics=("arbitrary",),
                                             vmem_limit_bytes=VMEM_LIMIT),
        name="modulation",
    )(cond, w_mod, b_mod)


D_IN_PROJ = 5664
C_LR_SRC, C_GATES_SRC = 3584, 3616


def _w_in_prep_kernel(w_ref, o_ref):
    rc = 64

    def copy_rows(dst0, src0, n):
        def body(t, carry):
            off = pl.multiple_of(t * rc, rc)
            o_ref[pl.ds(dst0 + off, rc), :] = w_ref[pl.ds(src0 + off, rc), :].astype(BF16)
            return carry
        lax.fori_loop(0, n // rc, body, 0)

    copy_rows(0, 0, C_LR_SRC)
    copy_rows(C_BRC, C_GATES_SRC, 2 * D_MODEL)
    o_ref[C_LR:C_LR + 2 * GLA_RANK, :] = w_ref[C_LR_SRC:C_GATES_SRC, :].astype(BF16)
    o_ref[C_LR + 2 * GLA_RANK:D_PROJ, :] = jnp.zeros((D_PROJ - C_LR - 2 * GLA_RANK, D_MODEL), BF16)


def _w_in_prep(w_in_t):
    return pl.pallas_call(
        _w_in_prep_kernel,
        out_shape=jax.ShapeDtypeStruct((D_PROJ, D_MODEL), BF16),
        compiler_params=pltpu.CompilerParams(vmem_limit_bytes=VMEM_LIMIT),
        name="w_in_prep",
    )(w_in_t)


def _inproj_kernel(x_ref, mod_ref, nw_ref, w_ref, o_ref, *, rows_per_mod, tm):
    i = pl.program_id(1)
    row = (i * tm) // rows_per_mod
    sh = mod_ref[pl.ds(row, 1), 0:D_MODEL]
    sc = mod_ref[pl.ds(row, 1), D_MODEL:2 * D_MODEL]
    h = _rms(x_ref[...]) * nw_ref[...] * (1.0 + sc) + sh
    o_ref[...] = _dot_nt(h.astype(BF16), w_ref[...]).astype(BF16)


def _in_proj(x2d, mod, norm_w, w_in_r, rows_per_mod):
    t = x2d.shape[0]
    tm, tn = 1024, 1920
    kern = functools.partial(_inproj_kernel, rows_per_mod=rows_per_mod, tm=tm)
    return pl.pallas_call(
        kern,
        grid=(D_PROJ // tn, t // tm),
        in_specs=[pl.BlockSpec((tm, D_MODEL), lambda j, i: (i, 0)),
                  pl.BlockSpec(mod.shape, lambda j, i: (0, 0)),
                  pl.BlockSpec((1, D_MODEL), lambda j, i: (0, 0)),
                  pl.BlockSpec((tn, D_MODEL), lambda j, i: (j, 0))],
        out_specs=pl.BlockSpec((tm, tn), lambda j, i: (i, j)),
        out_shape=jax.ShapeDtypeStruct((t, D_PROJ), BF16),
        compiler_params=pltpu.CompilerParams(dimension_semantics=("arbitrary", "arbitrary"),
                                             vmem_limit_bytes=VMEM_LIMIT),
        name="in_proj",
    )(x2d, mod, norm_w, w_in_r)


def _log2_sigmoid(x):
    return jnp.minimum(x, 0.0) * LOG2_E - jnp.log2(1.0 + jnp.exp2(jnp.abs(x) * (-LOG2_E)))


def _gla_chunk_head(qc, kc, vc, bc, st, rev):
    lane = lax.broadcasted_iota(jnp.int32, (SUB, CHUNK), 1)
    sub = lax.broadcasted_iota(jnp.int32, (SUB, CHUNK), 0)
    tot = bc[0:1] if rev else bc[CHUNK - 1:CHUNK]

    o = _dot_nt((qc * jnp.exp2(bc)).astype(BF16), st.astype(BF16))
    k_tail = kc * jnp.exp2(tot - bc)
    st_new = st * jnp.exp2(tot) + _dot_tn(vc.astype(BF16), k_tail.astype(BF16))

    lhs_segs, rhs_segs = [], []

    def rows(before, mid, after):
        parts = ([jnp.zeros((before, HEAD_D), F32)] if before else []) + [mid]
        parts += [jnp.zeros((after, HEAD_D), F32)] if after else []
        return jnp.concatenate(parts, axis=0) if len(parts) > 1 else mid

    key_blocks = range(1, N_SUB) if rev else range(0, N_SUB - 1)
    for jb in key_blocks:
        r0 = jb * SUB
        ref_row = bc[r0:r0 + 1] if rev else bc[r0 + SUB - 1:r0 + SUB]
        ke = kc[r0:r0 + SUB] * jnp.exp2(ref_row - bc[r0:r0 + SUB])
        rhs_segs.append(rows(r0, ke, CHUNK - r0 - SUB))
        if rev:
            ql = qc[:r0] * jnp.exp2(bc[:r0] - ref_row)
            lhs_segs.append(rows(0, ql, CHUNK - r0))
        else:
            ql = qc[r0 + SUB:] * jnp.exp2(bc[r0 + SUB:] - ref_row)
            lhs_segs.append(rows(r0 + SUB, ql, 0))
    far = _dot_nt(jnp.concatenate(lhs_segs, axis=1).astype(BF16),
                  jnp.concatenate(rhs_segs, axis=1).astype(BF16))

    blocks = []
    for ib in range(N_SUB):
        r0 = ib * SUB
        qi, bi = qc[r0:r0 + SUB], bc[r0:r0 + SUB]
        acc = jnp.zeros((SUB, CHUNK), F32)
        for jj in range(SUB):
            j = r0 + jj
            e = jnp.exp2(bi - bc[j:j + 1])
            col = jnp.sum(qi * (kc[j:j + 1] * e), axis=-1, keepdims=True)
            acc = jnp.where(lane == j, col, acc)
        keep = (lane - r0 >= sub) if rev else (lane - r0 <= sub)
        blocks.append(jnp.where(keep, acc, 0.0))
    scores = far + jnp.concatenate(blocks, axis=0)
    o = o + _dot(scores.astype(BF16), vc.astype(BF16))
    return o, st_new


def _scan_kernel(*refs, seq_len, has_s0, emit_state):
    it = iter(refs)
    qkv_ref, lr_ref, wdec_ref, bdec_ref = (next(it) for _ in range(4))
    s0_ref = next(it) if has_s0 else None
    out_ref = next(it)
    st_out_ref = next(it) if emit_state else None
    la_f_ref, la_b_ref, o_ref, st_ref = (next(it) for _ in range(4))

    L = seq_len
    n_chunks = L // CHUNK
    tr = 256
    assert L % tr == 0

    ri = lax.broadcasted_iota(jnp.int32, (tr, tr), 0)
    ci = lax.broadcasted_iota(jnp.int32, (tr, tr), 1)
    same_chunk = (ri // CHUNK) == (ci // CHUNK)

    def decay_body(t, carry):
        r0 = pl.multiple_of(t * tr, tr)
        lr = lr_ref[pl.ds(r0, tr), :]
        for d, ref in ((0, la_f_ref), (1, la_b_ref)):
            whi, wlo = _split_bf16(wdec_ref[d])
            z = _dot(lr, whi) + _dot(lr, wlo) + bdec_ref[d:d + 1]
            la_hi, la_lo = _split_bf16(_log2_sigmoid(z) * (1.0 / GLA_GATE_NORM))
            tri = jnp.where(same_chunk & ((ci >= ri) if d else (ci <= ri)), 1.0, 0.0).astype(BF16)
            ref[pl.ds(r0, tr), :] = _dot(tri, la_hi) + _dot(tri, la_lo)
        return carry
    lax.fori_loop(0, L // tr, decay_body, 0)

    for rev, la_ref in ((False, la_f_ref), (True, la_b_ref)):
        d = 1 if rev else 0
        for h in range(N_HEADS):
            if has_s0:
                st_ref[h] = s0_ref[0, d, h].T
            else:
                st_ref[h] = jnp.zeros((HEAD_D, HEAD_D), F32)

        def chunk_body(c, carry, rev=rev, la_ref=la_ref, heads=range(N_HEADS)):
            cc = (n_chunks - 1 - c) if rev else c
            r0 = pl.multiple_of(cc * CHUNK, CHUNK)
            for h in heads:
                lo, hi = h * HEAD_D, (h + 1) * HEAD_D
                qc = qkv_ref[pl.ds(r0, CHUNK), lo:hi].astype(F32) * (HEAD_D ** -0.5)
                kc = qkv_ref[pl.ds(r0, CHUNK), D_GLA + lo:D_GLA + hi].astype(F32)
                vc = qkv_ref[pl.ds(r0, CHUNK), 2 * D_GLA + lo:2 * D_GLA + hi].astype(F32)
                oc, st_new = _gla_chunk_head(qc, kc, vc, la_ref[pl.ds(r0, CHUNK), lo:hi], st_ref[h], rev)
                st_ref[h] = st_new
                if rev:
                    o_ref[pl.ds(r0, CHUNK), lo:hi] += oc
                else:
                    o_ref[pl.ds(r0, CHUNK), lo:hi] = oc
            return carry
        lax.fori_loop(0, n_chunks, chunk_body, 0, unroll=4)

        if emit_state:
            for h in range(N_HEADS):
                st_out_ref[0, d, h] = st_ref[h].T

    out_ref[...] = o_ref[...].astype(BF16)


def _gla_scan(proj, wdec, bdec, s0, *, seq_len, emit_state):
    t = proj.shape[0]
    nb = t // seq_len
    has_s0 = s0 is not None
    const = lambda shape: pl.BlockSpec(shape, lambda b: (0,) * len(shape), pipeline_mode=pl.Buffered(1))
    assert C_Q % (3 * D_GLA) == 0 and C_K == C_Q + D_GLA and C_V == C_K + D_GLA and C_LR % LR_PAD == 0
    in_specs = [pl.BlockSpec((seq_len, 3 * D_GLA), lambda b: (b, C_Q // (3 * D_GLA))),
                pl.BlockSpec((seq_len, LR_PAD), lambda b: (b, C_LR // LR_PAD)),
                const((2, LR_PAD, D_GLA)), const((2, D_GLA))]
    args = [proj, proj, wdec, bdec]
    if has_s0:
        in_specs.append(pl.BlockSpec((1, 2, N_HEADS, HEAD_D, HEAD_D), lambda b: (b, 0, 0, 0, 0)))
        args.append(s0)
    out_specs = [pl.BlockSpec((seq_len, D_GLA), lambda b: (b, 0))]
    out_shape = [jax.ShapeDtypeStruct((t, D_GLA), BF16)]
    if emit_state:
        out_specs.append(pl.BlockSpec((1, 2, N_HEADS, HEAD_D, HEAD_D), lambda b: (b, 0, 0, 0, 0)))
        out_shape.append(jax.ShapeDtypeStruct((nb, 2, N_HEADS, HEAD_D, HEAD_D), F32))
    kern = functools.partial(_scan_kernel, seq_len=seq_len, has_s0=has_s0, emit_state=emit_state)
    return pl.pallas_call(
        kern,
        grid=(nb,),
        in_specs=in_specs,
        out_specs=out_specs,
        out_shape=out_shape,
        scratch_shapes=[pltpu.VMEM((seq_len, D_GLA), F32), pltpu.VMEM((seq_len, D_GLA), F32),
                        pltpu.VMEM((seq_len, D_GLA), F32), pltpu.VMEM((N_HEADS, HEAD_D, HEAD_D), F32)],
        compiler_params=pltpu.CompilerParams(dimension_semantics=("arbitrary",),
                                             vmem_limit_bytes=VMEM_LIMIT),
        name="gla_scan",
    )(*args)


MIX_TM = 1024
MIX_SUB = 256


def _mix_dense_kernel(conv_ref, go_ref, gc0_ref, gc1_ref, gg0_ref, gg1_ref, o_ref, x_ref, mod_ref, cw_ref, cb_ref,
                      gnw_ref, wbc_ref, wbg_ref, wout_ref, out_ref, yc_ref, yg_ref, *, row_len, tiles_per_mod):
    i = pl.program_id(0)
    mod_row = i // tiles_per_mod if mod_ref.shape[0] > 1 else 0
    g1 = mod_ref[pl.ds(mod_row, 1), 2 * D_MODEL:3 * D_MODEL]
    hsel_r = lax.broadcasted_iota(jnp.int32, (D_GLA, D_GLA), 0) // HEAD_D
    hsel_c = lax.broadcasted_iota(jnp.int32, (D_GLA, D_GLA), 1) // HEAD_D
    head_avg = jnp.where(hsel_r == hsel_c, 1.0 / HEAD_D, 0.0).astype(BF16)
    pos = lax.broadcasted_iota(jnp.int32, (MIX_SUB, 1), 0)
    in_row = pos % row_len

    def branch_body(t, carry):
        rows = pl.ds(pl.multiple_of(t * MIX_SUB, MIX_SUB), MIX_SUB)
        cu = conv_ref[rows, 2 * D_CONV:3 * D_CONV].astype(F32) * conv_ref[rows, 0:D_CONV].astype(F32)
        left = jnp.where(in_row == 0, 0.0, pltpu.roll(cu, 1, axis=0))
        right = jnp.where(in_row == row_len - 1, 0.0, pltpu.roll(cu, MIX_SUB - 1, axis=0))
        conv = cw_ref[0:1] * left + cw_ref[1:2] * cu + cw_ref[2:3] * right + cb_ref[...]
        yc_ref[rows, :] = (conv_ref[rows, D_CONV:2 * D_CONV].astype(F32) * conv).astype(BF16)
        o = o_ref[rows, :].astype(F32)
        osq_hi, osq_lo = _split_bf16(o * o)
        ms = _dot(osq_hi, head_avg) + _dot(osq_lo, head_avg)
        g_out = go_ref[rows, :].astype(F32)
        yg_ref[rows, :] = (o * lax.rsqrt(ms + EPS) * gnw_ref[...] * (g_out * _sigmoid(g_out))).astype(BF16)
        return carry
    lax.fori_loop(0, MIX_TM // MIX_SUB, branch_body, 0)

    half = D_MODEL // 2
    pc = _dot(yc_ref[...], wbc_ref[...])
    pg = _dot(yg_ref[...], wbg_ref[...])
    merged = jnp.concatenate(
        [_sigmoid(gc0_ref[...].astype(F32)) * pc[:, :half] + _sigmoid(gg0_ref[...].astype(F32)) * pg[:, :half],
         _sigmoid(gc1_ref[...].astype(F32)) * pc[:, half:] + _sigmoid(gg1_ref[...].astype(F32)) * pg[:, half:]],
        axis=1)
    out_ref[...] = x_ref[...] + g1 * _dot(merged.astype(BF16), wout_ref[...])


def _mix_dense(proj, o, x2d, mod, conv_w, conv_b, gnw, wbc, wbg, wout, *, row_len, tokens_per_mod):
    t = x2d.shape[0]
    assert MIX_SUB % row_len == 0 and t % MIX_TM == 0 and tokens_per_mod % MIX_TM == 0
    const = lambda shape: pl.BlockSpec(shape, lambda i: (0,) * len(shape), pipeline_mode=pl.Buffered(1))
    cols = lambda width, start: pl.BlockSpec((MIX_TM, width), lambda i: (i, start // width))
    half = D_MODEL // 2
    assert C_U == 0 and C_GB == D_CONV and C_GC == 2 * D_CONV and C_GO % D_GLA == 0 and C_BRC % half == 0
    kern = functools.partial(_mix_dense_kernel, row_len=row_len, tiles_per_mod=tokens_per_mod // MIX_TM)
    return pl.pallas_call(
        kern,
        grid=(t // MIX_TM,),
        in_specs=[cols(3 * D_CONV, C_U), cols(D_GLA, C_GO),
                  cols(half, C_BRC), cols(half, C_BRC + half), cols(half, C_BRG), cols(half, C_BRG + half),
                  pl.BlockSpec((MIX_TM, D_GLA), lambda i: (i, 0)), pl.BlockSpec((MIX_TM, D_MODEL), lambda i: (i, 0)),
                  const(mod.shape), const((3, D_CONV)), const((1, D_CONV)), const((1, D_GLA)),
                  const((D_CONV, D_MODEL)), const((D_GLA, D_MODEL)), const((D_MODEL, D_MODEL))],
        out_specs=pl.BlockSpec((MIX_TM, D_MODEL), lambda i: (i, 0)),
        out_shape=jax.ShapeDtypeStruct((t, D_MODEL), F32),
        scratch_shapes=[pltpu.VMEM((MIX_TM, D_CONV), BF16), pltpu.VMEM((MIX_TM, D_GLA), BF16)],
        compiler_params=pltpu.CompilerParams(dimension_semantics=("arbitrary",), vmem_limit_bytes=VMEM_LIMIT),
        name="mix_dense",
    )(proj, proj, proj, proj, proj, proj, o, x2d, mod, conv_w, conv_b, gnw, wbc, wbg, wout)


I32 = jnp.int32
TB = 256
ROW_ALIGN = 16
G_ALIGN = 256
E_CHUNK = 2048
R_LOC = 3072
H2W = 1152


def _select_x(i, n_p_tiles, xp_ref, xs_ref):
    return jnp.where(i < n_p_tiles, xp_ref[...], xs_ref[...])


def _mod_row(i, n_p_tiles, tiles_per_mod):
    return jnp.where(i < n_p_tiles, 0, 1 + (i - n_p_tiles) // tiles_per_mod)


def _route_kernel(xp_ref, xs_ref, mod_ref, n2w_ref, wrt_ref, brb_ref, h2_ref, lpos_ref, cnt_ref, *,
                  n_p_tiles, tiles_per_mod):
    i = pl.program_id(0)
    row = _mod_row(i, n_p_tiles, tiles_per_mod)
    x = _select_x(i, n_p_tiles, xp_ref, xs_ref)
    sh = mod_ref[pl.ds(row, 1), 3 * D_MODEL:4 * D_MODEL]
    sc = mod_ref[pl.ds(row, 1), 4 * D_MODEL:5 * D_MODEL]
    h2 = _rms(x) * n2w_ref[...] * (1.0 + sc) + sh

    h_hi, h_lo = _split_bf16(h2)
    w_hi, w_lo = _split_bf16(wrt_ref[...])
    scores = jax.nn.sigmoid(_dot_nt(w_hi, h_hi) + (_dot_nt(w_hi, h_lo) + _dot_nt(w_lo, h_hi)))
    biased = scores + brb_ref[...]
    eidx = lax.broadcasted_iota(I32, scores.shape, 0)
    picks = []
    for _k in range(TOP_K):
        m = jnp.max(biased, axis=0, keepdims=True)
        first = jnp.min(jnp.where(biased == m, eidx, N_EXPERTS), axis=0, keepdims=True)
        pick = eidx == first
        picks.append(pick)
        biased = jnp.where(pick, -jnp.inf, biased)
    sel = jnp.zeros(scores.shape, F32)
    for pick in picks:
        sel = jnp.where(pick, 1.0, sel)
    selsc = sel * scores
    comb = selsc / jnp.sum(selsc, axis=0, keepdims=True) * ROUTED_SCALE

    selb = sel.astype(BF16)
    tr_ = lax.broadcasted_iota(I32, (TB, TB), 0)
    tc_ = lax.broadcasted_iota(I32, (TB, TB), 1)
    rank = _dot(selb, jnp.where(tr_ < tc_, 1.0, 0.0).astype(BF16))
    n_b = _dot(selb, jnp.ones((TB, 128), BF16))
    m_b = jnp.maximum(jnp.floor((n_b + (ROW_ALIGN - 1)) * (1.0 / ROW_ALIGN)), 1.0) * ROW_ALIGN
    er_ = lax.broadcasted_iota(I32, (N_EXPERTS, N_EXPERTS), 0)
    ec_ = lax.broadcasted_iota(I32, (N_EXPERTS, N_EXPERTS), 1)
    loff_b = _dot(jnp.where(ec_ < er_, 1.0, 0.0).astype(BF16), m_b.astype(BF16))
    lposf = jnp.concatenate([loff_b] * (TB // 128), axis=1) + rank
    rows = [jnp.sum(jnp.where(pick, lposf, 0.0), axis=0, keepdims=True) for pick in picks]
    lpos_ref[0] = jnp.concatenate(rows, axis=0).astype(I32)
    cnt_ref[0] = m_b

    combt = comb.T
    chi = combt.astype(BF16).astype(F32)
    h2_ref[:, 0:D_MODEL] = h_hi
    h2_ref[:, D_MODEL:H2W] = jnp.concatenate([chi, combt - chi], axis=1).astype(BF16)


def _route(x1p, x1s, mod, n2w, w_router_t, b_router_b, *, tiles_per_mod):
    n_p, n_s = x1p.shape[0] // TB, x1s.shape[0] // TB
    nt = n_p + n_s
    kern = functools.partial(_route_kernel, n_p_tiles=n_p, tiles_per_mod=tiles_per_mod)
    const = lambda shape: pl.BlockSpec(shape, lambda i: (0,) * len(shape))
    return pl.pallas_call(
        kern,
        grid=(nt,),
        in_specs=[pl.BlockSpec((TB, D_MODEL), lambda i: (jnp.minimum(i, n_p - 1), 0)),
                  pl.BlockSpec((TB, D_MODEL), lambda i: (jnp.maximum(i - n_p, 0), 0)),
                  const(mod.shape), const((1, D_MODEL)), const((N_EXPERTS, D_MODEL)), const((N_EXPERTS, TB))],
        out_specs=[pl.BlockSpec((TB, H2W), lambda i: (i, 0)),
                   pl.BlockSpec((1, TOP_K, TB), lambda i: (i, 0, 0)),
                   pl.BlockSpec((1, N_EXPERTS, 128), lambda i: (i, 0, 0))],
        out_shape=[jax.ShapeDtypeStruct((nt * TB, H2W), BF16),
                   jax.ShapeDtypeStruct((nt, TOP_K, TB), I32),
                   jax.ShapeDtypeStruct((nt, N_EXPERTS, 128), F32)],
        compiler_params=pltpu.CompilerParams(dimension_semantics=("arbitrary",), vmem_limit_bytes=VMEM_LIMIT),
        name="moe_route",
    )(x1p, x1s, mod, n2w, w_router_t, b_router_b)


def _plan_kernel(cnt_ref, off_ref, loff_ref, msz_ref, grp_ref, *, nt):
    lane = lax.broadcasted_iota(I32, (N_EXPERTS, 128), 1)
    m = jnp.zeros((N_EXPERTS, 128), F32)
    for i in range(nt):
        m = jnp.where(lane == i, cnt_ref[i], m)
    total = jnp.broadcast_to(jnp.sum(m, axis=1, keepdims=True), (N_EXPERTS, 128))
    gsz = jnp.floor((total + (G_ALIGN - 1)) * (1.0 / G_ALIGN)) * G_ALIGN
    er_ = lax.broadcasted_iota(I32, (N_EXPERTS, N_EXPERTS), 0)
    ec_ = lax.broadcasted_iota(I32, (N_EXPERTS, N_EXPERTS), 1)
    lstrict = jnp.where(ec_ < er_, 1.0, 0.0)
    ir_ = lax.broadcasted_iota(I32, (128, 128), 0)
    ic_ = lax.broadcasted_iota(I32, (128, 128), 1)
    ustrict = jnp.where(ir_ < ic_, 1.0, 0.0)
    gstart = _dot_hi(lstrict, gsz)
    off_ref[...] = (gstart + _dot_hi(m, ustrict)).astype(I32)
    loff_ref[...] = _dot_hi(lstrict, m).astype(I32)
    msz_ref[...] = m.astype(I32)
    grp = jnp.where(lane == 0, gstart + total, jnp.where(lane == 1, gsz - total, jnp.where(lane == 2, gstart, gsz)))
    grp_ref[...] = grp.astype(I32)


def _plan(cnt):
    nt = cnt.shape[0]
    assert nt <= 128
    tab = jax.ShapeDtypeStruct((N_EXPERTS, 128), I32)
    return pl.pallas_call(
        functools.partial(_plan_kernel, nt=nt),
        out_shape=[tab, tab, tab, tab],
        compiler_params=pltpu.CompilerParams(vmem_limit_bytes=VMEM_LIMIT),
        name="moe_plan",
    )(cnt)


def _start_copies(msz_ref, tile, make_copy):
    for e in range(N_EXPERTS):
        make_copy(e, pl.multiple_of(msz_ref[e, tile], ROW_ALIGN)).start()


def _tile_rows(loff_ref, msz_ref, tile):
    return pl.multiple_of(loff_ref[N_EXPERTS - 1, tile] + msz_ref[N_EXPERTS - 1, tile], ROW_ALIGN)


def _dispatch_kernel(off_ref, loff_ref, msz_ref, tail_ref, h2_ref, lpos_ref, xs_hbm, xloc_ref, zero_ref, sem,
                     tail_sem, *, nt):
    i = pl.program_id(0)
    slot = i % 2

    def copy_for(tile, slot_):
        def make(e, m):
            lo = pl.multiple_of(loff_ref[e, tile], ROW_ALIGN)
            of = pl.multiple_of(off_ref[e, tile], ROW_ALIGN)
            return pltpu.make_async_copy(xloc_ref.at[slot_, pl.ds(lo, m)], xs_hbm.at[pl.ds(of, m)], sem.at[slot_])
        return make

    def wait_tile(tile, slot_):
        n = _tile_rows(loff_ref, msz_ref, tile)
        pltpu.make_async_copy(xloc_ref.at[slot_, pl.ds(0, n)], xs_hbm.at[pl.ds(0, n)], sem.at[slot_]).wait()

    lpos = lpos_ref[0].astype(jnp.int16)
    h2 = h2_ref[...]
    ck = 1024
    one, zero = jnp.ones((ck, TB), BF16), jnp.zeros((ck, TB), BF16)
    for c in range(R_LOC // ck):
        r = (lax.broadcasted_iota(I32, (ck, TB), 0) + c * ck).astype(jnp.int16)
        d = zero
        for k in range(TOP_K):
            d = jnp.where(r == lpos[k:k + 1, :], one, d)
        res = _dot(d, h2)
        xloc_ref[slot, c * ck:(c + 1) * ck, :] = res.astype(BF16)

    _start_copies(msz_ref, i, copy_for(i, slot))

    @pl.when(i > 0)
    def _():
        wait_tile(i - 1, 1 - slot)

    @pl.when(i == nt - 1)
    def _():
        zero_ref[...] = jnp.zeros(zero_ref.shape, BF16)

        def tail_copies(start):
            def body(e, carry):
                n = tail_ref[e, 1]

                @pl.when(n > 0)
                def _():
                    st = pl.multiple_of(tail_ref[e, 0], ROW_ALIGN)
                    nn = pl.multiple_of(n, ROW_ALIGN)
                    cp = pltpu.make_async_copy(zero_ref.at[pl.ds(0, nn)], xs_hbm.at[pl.ds(st, nn)], tail_sem)
                    if start:
                        cp.start()
                    else:
                        cp.wait()
                return carry
            lax.fori_loop(0, N_EXPERTS, body, 0)
        tail_copies(True)
        wait_tile(i, slot)
        tail_copies(False)


def _dispatch(off, loff, msz, tail, h2ext, lpos, n_rows):
    nt = lpos.shape[0]
    grid_spec = pltpu.PrefetchScalarGridSpec(
        num_scalar_prefetch=4,
        grid=(nt,),
        in_specs=[pl.BlockSpec((TB, H2W), lambda i, *_: (i, 0)),
                  pl.BlockSpec((1, TOP_K, TB), lambda i, *_: (i, 0, 0))],
        out_specs=pl.BlockSpec(memory_space=pl.ANY),
        scratch_shapes=[pltpu.VMEM((2, R_LOC, H2W), BF16), pltpu.VMEM((G_ALIGN, H2W), BF16),
                        pltpu.SemaphoreType.DMA((2,)), pltpu.SemaphoreType.DMA],
    )
    return pl.pallas_call(
        functools.partial(_dispatch_kernel, nt=nt),
        grid_spec=grid_spec,
        out_shape=jax.ShapeDtypeStruct((n_rows, H2W), BF16),
        compiler_params=pltpu.CompilerParams(dimension_semantics=("arbitrary",), vmem_limit_bytes=VMEM_LIMIT),
        name="moe_dispatch",
    )(off, loff, msz, tail, h2ext, lpos)


def _expert_kernel(grp_ref, wg_ref, wu_ref, wd_ref, xs_hbm, ys_hbm, xbuf, ybuf, wgu_ref, wdb_ref, st_ref,
                   in_sem, out_sem):
    e = pl.program_id(0)
    n_exp = pl.num_programs(0)

    def in_copy(row0, n, slot):
        return pltpu.make_async_copy(xs_hbm.at[pl.ds(row0, n)], xbuf.at[slot, pl.ds(0, n)], in_sem.at[slot])

    def out_copy(row0, n, slot):
        return pltpu.make_async_copy(ybuf.at[slot, pl.ds(0, n)], ys_hbm.at[pl.ds(row0, n)], out_sem.at[slot])

    def rows_of(ex):
        return grp_ref[jnp.minimum(ex, n_exp - 1), 3]

    def next_nonempty(ex):
        return lax.while_loop(lambda c: (c < n_exp) & (rows_of(c) == 0), lambda c: c + 1, ex + 1)

    def start_first_chunk(ex, slot):
        @pl.when(ex < n_exp)
        def _():
            exc = jnp.minimum(ex, n_exp - 1)
            n = pl.multiple_of(jnp.minimum(grp_ref[exc, 3], E_CHUNK), G_ALIGN)
            in_copy(pl.multiple_of(grp_ref[exc, 2], G_ALIGN), n, slot).start()

    def drain_out(slot):
        pend = st_ref[1 + slot]

        @pl.when(pend > 0)
        def _():
            out_copy(0, pl.multiple_of(pend, G_ALIGN), slot).wait()
            st_ref[1 + slot] = 0

    @pl.when(e == 0)
    def _():
        st_ref[0] = 0
        st_ref[1] = 0
        st_ref[2] = 0
        start_first_chunk(next_nonempty(-1), 0)

    g0 = grp_ref[e, 2]
    gn = grp_ref[e, 3]

    @pl.when(gn > 0)
    def _():
        wgu_ref[:, :D_EXPERT] = wg_ref[0].astype(BF16)
        wgu_ref[:, D_EXPERT:] = wu_ref[0].astype(BF16)
        wdb_ref[...] = wd_ref[0].astype(BF16)
        n_chunks = (gn + (E_CHUNK - 1)) // E_CHUNK

        def compute(n, slot):
            x = xbuf[slot, 0:n, 0:D_MODEL]
            ext = xbuf[slot, 0:n, D_MODEL:H2W].astype(F32)
            wts = ext[:, :N_EXPERTS] + ext[:, N_EXPERTS:]
            lane = lax.broadcasted_iota(I32, wts.shape, 1)
            w = jnp.sum(jnp.where(lane == e, wts, 0.0), axis=-1, keepdims=True)
            h = _dot(x, wgu_ref[...])
            hg, hu = h[:, :D_EXPERT], h[:, D_EXPERT:]
            act = hg * _sigmoid(hg) * hu * w
            ybuf[slot, 0:n, :] = _dot(act.astype(BF16), wdb_ref[...]).astype(BF16)

        def chunk_body(c, slot):
            row0 = pl.multiple_of(g0 + c * E_CHUNK, G_ALIGN)
            n = pl.multiple_of(jnp.minimum(gn - c * E_CHUNK, E_CHUNK), G_ALIGN)
            in_copy(row0, n, slot).wait()

            @pl.when(c + 1 < n_chunks)
            def _():
                n1 = pl.multiple_of(jnp.minimum(gn - (c + 1) * E_CHUNK, E_CHUNK), G_ALIGN)
                in_copy(pl.multiple_of(row0 + E_CHUNK, G_ALIGN), n1, 1 - slot).start()

            @pl.when(c + 1 == n_chunks)
            def _():
                start_first_chunk(next_nonempty(e), 1 - slot)

            drain_out(slot)
            for v in range(G_ALIGN, E_CHUNK + 1, G_ALIGN):
                @pl.when(n == v)
                def _(v=v):
                    compute(v, slot)
            out_copy(row0, n, slot).start()
            st_ref[1 + slot] = n
            return 1 - slot

        st_ref[0] = lax.fori_loop(0, n_chunks, chunk_body, st_ref[0])

    @pl.when(e == n_exp - 1)
    def _():
        drain_out(0)
        drain_out(1)


def _experts(grp, xs, wg, wu, wd):
    w_in = pl.BlockSpec((1, D_MODEL, D_EXPERT), lambda e, grp_ref: (e, 0, 0))
    grid_spec = pltpu.PrefetchScalarGridSpec(
        num_scalar_prefetch=1,
        grid=(N_EXPERTS,),
        in_specs=[w_in, w_in, pl.BlockSpec((1, D_EXPERT, D_MODEL), lambda e, grp_ref: (e, 0, 0)),
                  pl.BlockSpec(memory_space=pl.ANY)],
        out_specs=pl.BlockSpec(memory_space=pl.ANY),
        scratch_shapes=[pltpu.VMEM((2, E_CHUNK, H2W), BF16), pltpu.VMEM((2, E_CHUNK, D_MODEL), BF16),
                        pltpu.VMEM((D_MODEL, 2 * D_EXPERT), BF16), pltpu.VMEM((D_EXPERT, D_MODEL), BF16),
                        pltpu.SMEM((4,), I32), pltpu.SemaphoreType.DMA((2,)), pltpu.SemaphoreType.DMA((2,))],
    )
    return pl.pallas_call(
        _expert_kernel,
        grid_spec=grid_spec,
        out_shape=jax.ShapeDtypeStruct((xs.shape[0], D_MODEL), BF16),
        compiler_params=pltpu.CompilerParams(dimension_semantics=("arbitrary",), vmem_limit_bytes=VMEM_LIMIT),
        name="moe_experts",
    )(grp, wg, wu, wd, xs)


def _combine_kernel(off_ref, loff_ref, msz_ref, lpos_ref, h2_ref, xp_ref, xs_ref, mod_ref, wgs_ref, wus_ref, wds_ref,
                    fnw_ref, ysrt_hbm, yp_ref, ys_ref, yloc_ref, acc_ref, sem, *, nt, n_p_tiles, tiles_per_mod,
                    final):
    i = pl.program_id(0)
    slot = i % 2

    def copy_for(tile, slot_):
        def make(e, m):
            lo = pl.multiple_of(loff_ref[e, tile], ROW_ALIGN)
            of = pl.multiple_of(off_ref[e, tile], ROW_ALIGN)
            return pltpu.make_async_copy(ysrt_hbm.at[pl.ds(of, m)], yloc_ref.at[slot_, pl.ds(lo, m)], sem.at[slot_])
        return make

    @pl.when(i == 0)
    def _():
        yloc_ref[...] = jnp.zeros(yloc_ref.shape, BF16)
        _start_copies(msz_ref, 0, copy_for(0, 0))

    nxt = jnp.minimum(i + 1, nt - 1)
    _start_copies(msz_ref, nxt, copy_for(nxt, 1 - slot))

    hb = h2_ref[...]
    hg = _dot(hb, wgs_ref[...].astype(BF16))
    hu = _dot(hb, wus_ref[...].astype(BF16))
    acc_ref[...] = _dot((hg * _sigmoid(hg) * hu).astype(BF16), wds_ref[...].astype(BF16))

    def wait_tile(tile, slot_):
        n = _tile_rows(loff_ref, msz_ref, tile)
        pltpu.make_async_copy(ysrt_hbm.at[pl.ds(0, n)], yloc_ref.at[slot_, pl.ds(0, n)], sem.at[slot_]).wait()

    wait_tile(i, slot)

    @pl.when(i == nt - 1)
    def _():
        wait_tile(i, 1 - slot)

    lpos_pad = jnp.concatenate([lpos_ref[0].astype(F32), jnp.zeros((128 - TOP_K, TB), F32)], axis=0)
    lposc = lpos_pad.T.astype(I32)
    ck = 512
    cols = [jnp.broadcast_to(lposc[:, k:k + 1], (TB, ck)).astype(jnp.int16) for k in range(TOP_K)]
    one, zero = jnp.ones((TB, ck), BF16), jnp.zeros((TB, ck), BF16)
    for c in range(R_LOC // ck):
        r = (lax.broadcasted_iota(I32, (TB, ck), 1) + c * ck).astype(jnp.int16)
        cm = zero
        for k in range(TOP_K):
            cm = jnp.where(r == cols[k], one, cm)
        acc_ref[...] += _dot(cm, yloc_ref[slot, c * ck:(c + 1) * ck, :])

    row = _mod_row(i, n_p_tiles, tiles_per_mod)
    g2 = mod_ref[pl.ds(row, 1), 5 * D_MODEL:6 * D_MODEL]
    x2 = _select_x(i, n_p_tiles, xp_ref, xs_ref) + g2 * acc_ref[...]
    y = _rms(x2) * fnw_ref[...] if final else x2

    @pl.when(i < n_p_tiles)
    def _():
        yp_ref[...] = y

    @pl.when(i >= n_p_tiles)
    def _():
        ys_ref[...] = y


def _combine(off, loff, msz, lpos, h2ext, x1p, x1s, mod, wgs, wus, wds, fnw, ysorted, *, tiles_per_mod, final):
    n_p, n_s = x1p.shape[0] // TB, x1s.shape[0] // TB
    nt = n_p + n_s
    const = lambda shape: pl.BlockSpec(shape, lambda i, *_: (0,) * len(shape), pipeline_mode=pl.Buffered(1))
    p_idx = lambda i, *_: (jnp.minimum(i, n_p - 1), 0)
    s_idx = lambda i, *_: (jnp.maximum(i - n_p, 0), 0)
    grid_spec = pltpu.PrefetchScalarGridSpec(
        num_scalar_prefetch=3,
        grid=(nt,),
        in_specs=[pl.BlockSpec((1, TOP_K, TB), lambda i, *_: (i, 0, 0)),
                  pl.BlockSpec((TB, D_MODEL), lambda i, *_: (i, 0)),
                  pl.BlockSpec((TB, D_MODEL), p_idx), pl.BlockSpec((TB, D_MODEL), s_idx),
                  const(mod.shape), const((D_MODEL, D_EXPERT)), const((D_MODEL, D_EXPERT)), const((D_EXPERT, D_MODEL)),
                  const((1, D_MODEL)), pl.BlockSpec(memory_space=pl.ANY)],
        out_specs=[pl.BlockSpec((TB, D_MODEL), p_idx), pl.BlockSpec((TB, D_MODEL), s_idx)],
        scratch_shapes=[pltpu.VMEM((2, R_LOC, D_MODEL), BF16), pltpu.VMEM((TB, D_MODEL), F32),
                        pltpu.SemaphoreType.DMA((2,))],
    )
    kern = functools.partial(_combine_kernel, nt=nt, n_p_tiles=n_p, tiles_per_mod=tiles_per_mod, final=final)
    return pl.pallas_call(
        kern,
        grid_spec=grid_spec,
        out_shape=[jax.ShapeDtypeStruct(x1p.shape, F32), jax.ShapeDtypeStruct(x1s.shape, F32)],
        compiler_params=pltpu.CompilerParams(dimension_semantics=("arbitrary",), vmem_limit_bytes=VMEM_LIMIT),
        name="moe_combine",
    )(off, loff, msz, lpos, h2ext, x1p, x1s, mod, wgs, wus, wds, fnw, ysorted)


def _moe(x1p, x1s, mod, n2w, w_router, b_router, wg, wu, wd, wgs, wus, wds, fnw, *, tokens_per_mod, final):
    assert R_LOC >= TB * TOP_K + N_EXPERTS * ROW_ALIGN and tokens_per_mod % TB == 0
    nt = (x1p.shape[0] + x1s.shape[0]) // TB
    n_rows_max = nt * TB * TOP_K + nt * N_EXPERTS * ROW_ALIGN + N_EXPERTS * (G_ALIGN - ROW_ALIGN)
    tiles_per_mod = tokens_per_mod // TB
    brb = jnp.broadcast_to(b_router.reshape(N_EXPERTS, 1), (N_EXPERTS, TB))
    h2ext, lpos, cnt = _route(x1p, x1s, mod, n2w, w_router.T, brb, tiles_per_mod=tiles_per_mod)
    off, loff, msz, grp = _plan(cnt)
    xs = _dispatch(off, loff, msz, grp, h2ext, lpos, n_rows_max)
    ysorted = _experts(grp, xs, wg, wu, wd)
    return _combine(off, loff, msz, lpos, h2ext, x1p, x1s, mod, wgs, wus, wds, fnw, ysorted,
                    tiles_per_mod=tiles_per_mod, final=final)


def kernel(x_prompt, x_sample, state_gla, c, c_ctx, w_mod, b_mod, norm1_w, w_in, conv_w, conv_b, w_decay, b_decay,
           gla_norm_w, w_br_conv, w_br_gla, w_out, norm2_w, w_router, b_router, w_gate_e, w_up_e, w_down_e,
           w_gate_s, w_up_s, w_down_s, final_norm_w):
    depth = w_mod.shape[0]
    nb_p, len_p, _ = x_prompt.shape
    nb_s, len_s, _ = x_sample.shape
    yp = x_prompt.reshape(nb_p * len_p, D_MODEL)
    ys = x_sample.reshape(nb_s * len_s, D_MODEL)
    fnw = final_norm_w.reshape(1, D_MODEL)

    cond = jnp.concatenate([c_ctx[None, :], c, jnp.zeros((8 - 1 - nb_s, D_MODEL), F32)], axis=0)
    states = []
    for l in range(depth):
        mod = _modulation(cond, w_mod[l], b_mod[l].reshape(1, -1))
        mod_p, mod_s = mod[0:1], mod[1:1 + nb_s]

        w_in_r = _w_in_prep(w_in[l].T)
        wdec = jnp.zeros((2, LR_PAD, D_GLA), F32)
        wdec = wdec.at[0, 0:GLA_RANK].set(w_decay[l, 0]).at[1, GLA_RANK:2 * GLA_RANK].set(w_decay[l, 1])
        n1w = norm1_w[l].reshape(1, D_MODEL)
        mix_w = (conv_w[l], conv_b[l].reshape(1, D_CONV), gla_norm_w[l].reshape(1, D_GLA),
                 w_br_conv[l].astype(BF16), w_br_gla[l].astype(BF16), w_out[l].astype(BF16))
        moe_w = (norm2_w[l].reshape(1, D_MODEL), w_router[l], b_router[l],
                 w_gate_e[l], w_up_e[l], w_down_e[l], w_gate_s[l], w_up_s[l], w_down_s[l])

        proj_p = _in_proj(yp, mod_p, n1w, w_in_r, rows_per_mod=nb_p * len_p)
        o_p, st = _gla_scan(proj_p, wdec, b_decay[l], None, seq_len=len_p, emit_state=True)
        yp = _mix_dense(proj_p, o_p, yp, mod_p, *mix_w, row_len=len_p, tokens_per_mod=nb_p * len_p)
        states.append(st)
        proj_s = _in_proj(ys, mod_s, n1w, w_in_r, rows_per_mod=len_s)
        (o_s,) = _gla_scan(proj_s, wdec, b_decay[l], state_gla[:, l], seq_len=len_s, emit_state=False)
        ys = _mix_dense(proj_s, o_s, ys, mod_s, *mix_w, row_len=GRID_W, tokens_per_mod=len_s)

        yp, ys = _moe(yp, ys, mod, *moe_w, fnw, tokens_per_mod=len_s, final=l == depth - 1)
    new_state = jnp.stack(states, axis=1)
    return (yp.reshape(nb_p, len_p, D_MODEL), ys.reshape(nb_s, len_s, D_MODEL), new_state)
```

```python
import functools

import jax
import jax.numpy as jnp
from jax import lax
from jax.experimental import pallas as pl
from jax.experimental.pallas import tpu as pltpu

F32 = jnp.float32
BF16 = jnp.bfloat16

D_MODEL = 1024
GRID_W = 64
D_CONV = 512
N_HEADS = 4
HEAD_D = 128
D_GLA = N_HEADS * HEAD_D
GLA_RANK = 16
GLA_GATE_NORM = 16.0
LOG2_E = 1.4426950408889634
CHUNK = 64
SUB = 8
N_SUB = CHUNK // SUB
N_EXPERTS = 64
TOP_K = 8
D_EXPERT = 256
ROUTED_SCALE = 2.5
EPS = 1e-6

C_U, C_GB, C_GC, C_Q, C_K, C_V, C_GO = 0, 512, 1024, 1536, 2048, 2560, 3072
C_BRC, C_BRG, C_LR = 3584, 4608, 5632
D_PROJ = 5760
LR_PAD = 128

VMEM_LIMIT = 56 * 1024 * 1024


def _dot(a, b):
    return jnp.dot(a, b, preferred_element_type=F32)


def _dot_nt(a, b):
    return lax.dot_general(a, b, (((1,), (1,)), ((), ())), preferred_element_type=F32)


def _dot_tn(a, b):
    return lax.dot_general(a, b, (((0,), (0,)), ((), ())), preferred_element_type=F32)


def _dot_hi(a, b):
    return jnp.dot(a, b, preferred_element_type=F32, precision=lax.Precision.HIGHEST)


def _split_bf16(x):
    hi = x.astype(BF16)
    lo = (x - hi.astype(F32)).astype(BF16)
    return hi, lo


def _rms(x):
    return x * lax.rsqrt(jnp.mean(x * x, axis=-1, keepdims=True) + EPS)


def _sigmoid(x):
    return 0.5 * jnp.tanh(0.5 * x) + 0.5


def _mod_kernel(cond_ref, w_ref, b_ref, o_ref):
    c = cond_ref[...]
    o_ref[...] = _dot_hi(c * jax.nn.sigmoid(c), w_ref[...]) + b_ref[...]


def _modulation(cond, w_mod, b_mod):
    n_rows = cond.shape[0]
    tn = 1536
    return pl.pallas_call(
        _mod_kernel,
        grid=(6 * D_MODEL // tn,),
        in_specs=[pl.BlockSpec((n_rows, D_MODEL), lambda j: (0, 0)),
                  pl.BlockSpec((D_MODEL, tn), lambda j: (0, j)),
                  pl.BlockSpec((1, tn), lambda j: (0, j))],
        out_specs=pl.BlockSpec((n_rows, tn), lambda j: (0, j)),
        out_shape=jax.ShapeDtypeStruct((n_rows, 6 * D_MODEL), F32),
        compiler_params=pltpu.CompilerParams(dimension_semantics=("arbitrary",),
                                             vmem_limit_bytes=VMEM_LIMIT),
        name="modulation",
    )(cond, w_mod, b_mod)


D_IN_PROJ = 5664
C_LR_SRC, C_GATES_SRC = 3584, 3616


def _w_in_prep_kernel(w_ref, o_ref):
    rc = 64

    def copy_rows(dst0, src0, n):
        def body(t, carry):
            off = pl.multiple_of(t * rc, rc)
            o_ref[pl.ds(dst0 + off, rc), :] = w_ref[pl.ds(src0 + off, rc), :].astype(BF16)
            return carry
        lax.fori_loop(0, n // rc, body, 0)

    copy_rows(0, 0, C_LR_SRC)
    copy_rows(C_BRC, C_GATES_SRC, 2 * D_MODEL)
    o_ref[C_LR:C_LR + 2 * GLA_RANK, :] = w_ref[C_LR_SRC:C_GATES_SRC, :].astype(BF16)
    o_ref[C_LR + 2 * GLA_RANK:D_PROJ, :] = jnp.zeros((D_PROJ - C_LR - 2 * GLA_RANK, D_MODEL), BF16)


def _w_in_prep(w_in_t):
    return pl.pallas_call(
        _w_in_prep_kernel,
        out_shape=jax.ShapeDtypeStruct((D_PROJ, D_MODEL), BF16),
        compiler_params=pltpu.CompilerParams(vmem_limit_bytes=VMEM_LIMIT),
        name="w_in_prep",
    )(w_in_t)


def _inproj_kernel(x_ref, mod_ref, nw_ref, w_ref, o_ref, *, rows_per_mod, tm):
    i = pl.program_id(1)
    row = (i * tm) // rows_per_mod
    sh = mod_ref[pl.ds(row, 1), 0:D_MODEL]
    sc = mod_ref[pl.ds(row, 1), D_MODEL:2 * D_MODEL]
    h = _rms(x_ref[...]) * nw_ref[...] * (1.0 + sc) + sh
    o_ref[...] = _dot_nt(h.astype(BF16), w_ref[...]).astype(BF16)


def _in_proj(x2d, mod, norm_w, w_in_r, rows_per_mod):
    t = x2d.shape[0]
    tm, tn = 1024, 1920
    kern = functools.partial(_inproj_kernel, rows_per_mod=rows_per_mod, tm=tm)
    return pl.pallas_call(
        kern,
        grid=(D_PROJ // tn, t // tm),
        in_specs=[pl.BlockSpec((tm, D_MODEL), lambda j, i: (i, 0)),
                  pl.BlockSpec(mod.shape, lambda j, i: (0, 0)),
                  pl.BlockSpec((1, D_MODEL), lambda j, i: (0, 0)),
                  pl.BlockSpec((tn, D_MODEL), lambda j, i: (j, 0))],
        out_specs=pl.BlockSpec((tm, tn), lambda j, i: (i, j)),
        out_shape=jax.ShapeDtypeStruct((t, D_PROJ), BF16),
        compiler_params=pltpu.CompilerParams(dimension_semantics=("arbitrary", "arbitrary"),
                                             vmem_limit_bytes=VMEM_LIMIT),
        name="in_proj",
    )(x2d, mod, norm_w, w_in_r)


def _log2_sigmoid(x):
    return jnp.minimum(x, 0.0) * LOG2_E - jnp.log2(1.0 + jnp.exp2(jnp.abs(x) * (-LOG2_E)))


def _gla_chunk_head(qc, kc, vc, bc, st, rev):
    lane = lax.broadcasted_iota(jnp.int32, (SUB, CHUNK), 1)
    sub = lax.broadcasted_iota(jnp.int32, (SUB, CHUNK), 0)
    tot = bc[0:1] if rev else bc[CHUNK - 1:CHUNK]

    q_in = (qc * jnp.exp2(bc)).astype(BF16)
    k_tail = kc * jnp.exp2(tot - bc)
    vt = vc.T.astype(BF16)
    st_new = st * jnp.exp2(tot) + _dot(vt, k_tail.astype(BF16))

    lhs_segs, rhs_segs = [], []

    def rows(before, mid, after):
        parts = ([jnp.zeros((before, HEAD_D), F32)] if before else []) + [mid]
        parts += [jnp.zeros((after, HEAD_D), F32)] if after else []
        return jnp.concatenate(parts, axis=0) if len(parts) > 1 else mid

    key_blocks = range(1, N_SUB) if rev else range(0, N_SUB - 1)
    for jb in key_blocks:
        r0 = jb * SUB
        ref_row = bc[r0:r0 + 1] if rev else bc[r0 + SUB - 1:r0 + SUB]
        ke = kc[r0:r0 + SUB] * jnp.exp2(ref_row - bc[r0:r0 + SUB])
        rhs_segs.append(rows(r0, ke, CHUNK - r0 - SUB))
        if rev:
            ql = qc[:r0] * jnp.exp2(bc[:r0] - ref_row)
            lhs_segs.append(rows(0, ql, CHUNK - r0))
        else:
            ql = qc[r0 + SUB:] * jnp.exp2(bc[r0 + SUB:] - ref_row)
            lhs_segs.append(rows(r0 + SUB, ql, 0))
    far = _dot_nt(jnp.concatenate(lhs_segs, axis=1).astype(BF16),
                  jnp.concatenate(rhs_segs, axis=1).astype(BF16))

    blocks = []
    for ib in range(N_SUB):
        r0 = ib * SUB
        qi, bi = qc[r0:r0 + SUB], bc[r0:r0 + SUB]
        acc = jnp.zeros((SUB, CHUNK), F32)
        for jj in range(SUB):
            j = r0 + jj
            e = jnp.exp2(bi - bc[j:j + 1])
            col = jnp.sum(qi * (kc[j:j + 1] * e), axis=-1, keepdims=True)
            acc = jnp.where(lane == j, col, acc)
        keep = (lane - r0 >= sub) if rev else (lane - r0 <= sub)
        blocks.append(jnp.where(keep, acc, 0.0))
    scores = far + jnp.concatenate(blocks, axis=0)
    o = _dot_nt(jnp.concatenate([q_in, scores.astype(BF16)], axis=1),
                jnp.concatenate([st.astype(BF16), vt], axis=1))
    return o, st_new


def _scan_kernel(*refs, seq_len, has_s0, emit_state):
    it = iter(refs)
    qkv_ref, lr_ref, wdec_ref, bdec_ref = (next(it) for _ in range(4))
    s0_ref = next(it) if has_s0 else None
    out_ref = next(it)
    st_out_ref = next(it) if emit_state else None
    la_f_ref, la_b_ref, o_ref, st_ref = (next(it) for _ in range(4))

    L = seq_len
    n_chunks = L // CHUNK
    tr = 256
    assert L % tr == 0

    ri = lax.broadcasted_iota(jnp.int32, (tr, tr), 0)
    ci = lax.broadcasted_iota(jnp.int32, (tr, tr), 1)
    same_chunk = (ri // CHUNK) == (ci // CHUNK)

    def decay_body(t, carry):
        r0 = pl.multiple_of(t * tr, tr)
        lr = lr_ref[pl.ds(r0, tr), :]
        for d, ref in ((0, la_f_ref), (1, la_b_ref)):
            whi, wlo = _split_bf16(wdec_ref[d])
            z = _dot(lr, whi) + _dot(lr, wlo) + bdec_ref[d:d + 1]
            la_hi, la_lo = _split_bf16(_log2_sigmoid(z) * (1.0 / GLA_GATE_NORM))
            tri = jnp.where(same_chunk & ((ci >= ri) if d else (ci <= ri)), 1.0, 0.0).astype(BF16)
            ref[pl.ds(r0, tr), :] = _dot(tri, la_hi) + _dot(tri, la_lo)
        return carry
    lax.fori_loop(0, L // tr, decay_body, 0)

    for rev, la_ref in ((False, la_f_ref), (True, la_b_ref)):
        d = 1 if rev else 0
        for h in range(N_HEADS):
            if has_s0:
                st_ref[h] = s0_ref[0, d, h].T
            else:
                st_ref[h] = jnp.zeros((HEAD_D, HEAD_D), F32)

        def chunk_body(c, carry, rev=rev, la_ref=la_ref, heads=range(N_HEADS)):
            cc = (n_chunks - 1 - c) if rev else c
            r0 = pl.multiple_of(cc * CHUNK, CHUNK)
            for h in heads:
                lo, hi = h * HEAD_D, (h + 1) * HEAD_D
                qc = qkv_ref[pl.ds(r0, CHUNK), lo:hi].astype(F32) * (HEAD_D ** -0.5)
                kc = qkv_ref[pl.ds(r0, CHUNK), D_GLA + lo:D_GLA + hi].astype(F32)
                vc = qkv_ref[pl.ds(r0, CHUNK), 2 * D_GLA + lo:2 * D_GLA + hi].astype(F32)
                oc, st_new = _gla_chunk_head(qc, kc, vc, la_ref[pl.ds(r0, CHUNK), lo:hi], st_ref[h], rev)
                st_ref[h] = st_new
                if rev:
                    o_ref[pl.ds(r0, CHUNK), lo:hi] += oc
                else:
                    o_ref[pl.ds(r0, CHUNK), lo:hi] = oc
            return carry
        lax.fori_loop(0, n_chunks, chunk_body, 0, unroll=4)

        if emit_state:
            for h in range(N_HEADS):
                st_out_ref[0, d, h] = st_ref[h].T

    out_ref[...] = o_ref[...].astype(BF16)


def _gla_scan(proj, wdec, bdec, s0, *, seq_len, emit_state):
    t = proj.shape[0]
    nb = t // seq_len
    has_s0 = s0 is not None
    const = lambda shape: pl.BlockSpec(shape, lambda b: (0,) * len(shape), pipeline_mode=pl.Buffered(1))
    assert C_Q % (3 * D_GLA) == 0 and C_K == C_Q + D_GLA and C_V == C_K + D_GLA and C_LR % LR_PAD == 0
    in_specs = [pl.BlockSpec((seq_len, 3 * D_GLA), lambda b: (b, C_Q // (3 * D_GLA))),
                pl.BlockSpec((seq_len, LR_PAD), lambda b: (b, C_LR // LR_PAD)),
                const((2, LR_PAD, D_GLA)), const((2, D_GLA))]
    args = [proj, proj, wdec, bdec]
    if has_s0:
        in_specs.append(pl.BlockSpec((1, 2, N_HEADS, HEAD_D, HEAD_D), lambda b: (b, 0, 0, 0, 0)))
        args.append(s0)
    out_specs = [pl.BlockSpec((seq_len, D_GLA), lambda b: (b, 0))]
    out_shape = [jax.ShapeDtypeStruct((t, D_GLA), BF16)]
    if emit_state:
        out_specs.append(pl.BlockSpec((1, 2, N_HEADS, HEAD_D, HEAD_D), lambda b: (b, 0, 0, 0, 0)))
        out_shape.append(jax.ShapeDtypeStruct((nb, 2, N_HEADS, HEAD_D, HEAD_D), F32))
    kern = functools.partial(_scan_kernel, seq_len=seq_len, has_s0=has_s0, emit_state=emit_state)
    return pl.pallas_call(
        kern,
        grid=(nb,),
        in_specs=in_specs,
        out_specs=out_specs,
        out_shape=out_shape,
        scratch_shapes=[pltpu.VMEM((seq_len, D_GLA), F32), pltpu.VMEM((seq_len, D_GLA), F32),
                        pltpu.VMEM((seq_len, D_GLA), F32), pltpu.VMEM((N_HEADS, HEAD_D, HEAD_D), F32)],
        compiler_params=pltpu.CompilerParams(dimension_semantics=("arbitrary",),
                                             vmem_limit_bytes=VMEM_LIMIT),
        name="gla_scan",
    )(*args)


MIX_TM = 1024
MIX_SUB = 256


def _mix_dense_kernel(conv_ref, go_ref, gc0_ref, gc1_ref, gg0_ref, gg1_ref, o_ref, x_ref, mod_ref, cw_ref, cb_ref,
                      gnw_ref, wbc_ref, wbg_ref, wout_ref, out_ref, yc_ref, yg_ref, *, row_len, tiles_per_mod):
    i = pl.program_id(0)
    mod_row = i // tiles_per_mod if mod_ref.shape[0] > 1 else 0
    g1 = mod_ref[pl.ds(mod_row, 1), 2 * D_MODEL:3 * D_MODEL]
    hsel_r = lax.broadcasted_iota(jnp.int32, (D_GLA, D_GLA), 0) // HEAD_D
    hsel_c = lax.broadcasted_iota(jnp.int32, (D_GLA, D_GLA), 1) // HEAD_D
    head_avg = jnp.where(hsel_r == hsel_c, 1.0 / HEAD_D, 0.0).astype(BF16)
    pos = lax.broadcasted_iota(jnp.int32, (MIX_SUB, 1), 0)
    in_row = pos % row_len

    def branch_body(t, carry):
        rows = pl.ds(pl.multiple_of(t * MIX_SUB, MIX_SUB), MIX_SUB)
        cu = conv_ref[rows, 2 * D_CONV:3 * D_CONV].astype(F32) * conv_ref[rows, 0:D_CONV].astype(F32)
        left = jnp.where(in_row == 0, 0.0, pltpu.roll(cu, 1, axis=0))
        right = jnp.where(in_row == row_len - 1, 0.0, pltpu.roll(cu, MIX_SUB - 1, axis=0))
        conv = cw_ref[0:1] * left + cw_ref[1:2] * cu + cw_ref[2:3] * right + cb_ref[...]
        yc_ref[rows, :] = (conv_ref[rows, D_CONV:2 * D_CONV].astype(F32) * conv).astype(BF16)
        o = o_ref[rows, :].astype(F32)
        osq_hi, osq_lo = _split_bf16(o * o)
        ms = _dot(osq_hi, head_avg) + _dot(osq_lo, head_avg)
        g_out = go_ref[rows, :].astype(F32)
        yg_ref[rows, :] = (o * lax.rsqrt(ms + EPS) * gnw_ref[...] * (g_out * _sigmoid(g_out))).astype(BF16)
        return carry
    lax.fori_loop(0, MIX_TM // MIX_SUB, branch_body, 0)

    half = D_MODEL // 2
    pc = _dot(yc_ref[...], wbc_ref[...])
    pg = _dot(yg_ref[...], wbg_ref[...])
    merged = jnp.concatenate(
        [_sigmoid(gc0_ref[...].astype(F32)) * pc[:, :half] + _sigmoid(gg0_ref[...].astype(F32)) * pg[:, :half],
         _sigmoid(gc1_ref[...].astype(F32)) * pc[:, half:] + _sigmoid(gg1_ref[...].astype(F32)) * pg[:, half:]],
        axis=1)
    out_ref[...] = x_ref[...] + g1 * _dot(merged.astype(BF16), wout_ref[...])


def _mix_dense(proj, o, x2d, mod, conv_w, conv_b, gnw, wbc, wbg, wout, *, row_len, tokens_per_mod):
    t = x2d.shape[0]
    assert MIX_SUB % row_len == 0 and t % MIX_TM == 0 and tokens_per_mod % MIX_TM == 0
    const = lambda shape: pl.BlockSpec(shape, lambda i: (0,) * len(shape), pipeline_mode=pl.Buffered(1))
    cols = lambda width, start: pl.BlockSpec((MIX_TM, width), lambda i: (i, start // width))
    half = D_MODEL // 2
    assert C_U == 0 and C_GB == D_CONV and C_GC == 2 * D_CONV and C_GO % D_GLA == 0 and C_BRC % half == 0
    kern = functools.partial(_mix_dense_kernel, row_len=row_len, tiles_per_mod=tokens_per_mod // MIX_TM)
    return pl.pallas_call(
        kern,
        grid=(t // MIX_TM,),
        in_specs=[cols(3 * D_CONV, C_U), cols(D_GLA, C_GO),
                  cols(half, C_BRC), cols(half, C_BRC + half), cols(half, C_BRG), cols(half, C_BRG + half),
                  pl.BlockSpec((MIX_TM, D_GLA), lambda i: (i, 0)), pl.BlockSpec((MIX_TM, D_MODEL), lambda i: (i, 0)),
                  const(mod.shape), const((3, D_CONV)), const((1, D_CONV)), const((1, D_GLA)),
                  const((D_CONV, D_MODEL)), const((D_GLA, D_MODEL)), const((D_MODEL, D_MODEL))],
        out_specs=pl.BlockSpec((MIX_TM, D_MODEL), lambda i: (i, 0)),
        out_shape=jax.ShapeDtypeStruct((t, D_MODEL), F32),
        scratch_shapes=[pltpu.VMEM((MIX_TM, D_CONV), BF16), pltpu.VMEM((MIX_TM, D_GLA), BF16)],
        compiler_params=pltpu.CompilerParams(dimension_semantics=("arbitrary",), vmem_limit_bytes=VMEM_LIMIT),
        name="mix_dense",
    )(proj, proj, proj, proj, proj, proj, o, x2d, mod, conv_w, conv_b, gnw, wbc, wbg, wout)


I32 = jnp.int32
TB = 256
ROW_ALIGN = 16
G_ALIGN = 256
E_CHUNK = 2048
R_LOC = 3072
H2W = 1152


def _select_x(i, n_p_tiles, xp_ref, xs_ref):
    return jnp.where(i < n_p_tiles, xp_ref[...], xs_ref[...])


def _mod_row(i, n_p_tiles, tiles_per_mod):
    return jnp.where(i < n_p_tiles, 0, 1 + (i - n_p_tiles) // tiles_per_mod)


def _route_kernel(xp_ref, xs_ref, mod_ref, n2w_ref, wrt_ref, brb_ref, h2_ref, lpos_ref, cnt_ref, *,
                  n_p_tiles, tiles_per_mod):
    i = pl.program_id(0)
    row = _mod_row(i, n_p_tiles, tiles_per_mod)
    x = _select_x(i, n_p_tiles, xp_ref, xs_ref)
    sh = mod_ref[pl.ds(row, 1), 3 * D_MODEL:4 * D_MODEL]
    sc = mod_ref[pl.ds(row, 1), 4 * D_MODEL:5 * D_MODEL]
    h2 = _rms(x) * n2w_ref[...] * (1.0 + sc) + sh

    h_hi, h_lo = _split_bf16(h2)
    w_hi, w_lo = _split_bf16(wrt_ref[...])
    scores = jax.nn.sigmoid(_dot_nt(w_hi, h_hi) + (_dot_nt(w_hi, h_lo) + _dot_nt(w_lo, h_hi)))
    biased = scores + brb_ref[...]
    eidx = lax.broadcasted_iota(I32, scores.shape, 0)
    picks = []
    for _k in range(TOP_K):
        m = jnp.max(biased, axis=0, keepdims=True)
        first = jnp.min(jnp.where(biased == m, eidx, N_EXPERTS), axis=0, keepdims=True)
        pick = eidx == first
        picks.append(pick)
        biased = jnp.where(pick, -jnp.inf, biased)
    sel = jnp.zeros(scores.shape, F32)
    for pick in picks:
        sel = jnp.where(pick, 1.0, sel)
    selsc = sel * scores
    comb = selsc / jnp.sum(selsc, axis=0, keepdims=True) * ROUTED_SCALE

    selb = sel.astype(BF16)
    tr_ = lax.broadcasted_iota(I32, (TB, TB), 0)
    tc_ = lax.broadcasted_iota(I32, (TB, TB), 1)
    rank = _dot(selb, jnp.where(tr_ < tc_, 1.0, 0.0).astype(BF16))
    n_b = _dot(selb, jnp.ones((TB, 128), BF16))
    m_b = jnp.maximum(jnp.floor((n_b + (ROW_ALIGN - 1)) * (1.0 / ROW_ALIGN)), 1.0) * ROW_ALIGN
    er_ = lax.broadcasted_iota(I32, (N_EXPERTS, N_EXPERTS), 0)
    ec_ = lax.broadcasted_iota(I32, (N_EXPERTS, N_EXPERTS), 1)
    loff_b = _dot(jnp.where(ec_ < er_, 1.0, 0.0).astype(BF16), m_b.astype(BF16))
    lposf = jnp.concatenate([loff_b] * (TB // 128), axis=1) + rank
    rows = [jnp.sum(jnp.where(pick, lposf, 0.0), axis=0, keepdims=True) for pick in picks]
    lpos_ref[0] = jnp.concatenate(rows, axis=0).astype(I32)
    cnt_ref[0] = m_b

    combt = comb.T
    chi = combt.astype(BF16).astype(F32)
    h2_ref[:, 0:D_MODEL] = h_hi
    h2_ref[:, D_MODEL:H2W] = jnp.concatenate([chi, combt - chi], axis=1).astype(BF16)


def _route(x1p, x1s, mod, n2w, w_router_t, b_router_b, *, tiles_per_mod):
    n_p, n_s = x1p.shape[0] // TB, x1s.shape[0] // TB
    nt = n_p + n_s
    kern = functools.partial(_route_kernel, n_p_tiles=n_p, tiles_per_mod=tiles_per_mod)
    const = lambda shape: pl.BlockSpec(shape, lambda i: (0,) * len(shape))
    return pl.pallas_call(
        kern,
        grid=(nt,),
        in_specs=[pl.BlockSpec((TB, D_MODEL), lambda i: (jnp.minimum(i, n_p - 1), 0)),
                  pl.BlockSpec((TB, D_MODEL), lambda i: (jnp.maximum(i - n_p, 0), 0)),
                  const(mod.shape), const((1, D_MODEL)), const((N_EXPERTS, D_MODEL)), const((N_EXPERTS, TB))],
        out_specs=[pl.BlockSpec((TB, H2W), lambda i: (i, 0)),
                   pl.BlockSpec((1, TOP_K, TB), lambda i: (i, 0, 0)),
                   pl.BlockSpec((1, N_EXPERTS, 128), lambda i: (i, 0, 0))],
        out_shape=[jax.ShapeDtypeStruct((nt * TB, H2W), BF16),
                   jax.ShapeDtypeStruct((nt, TOP_K, TB), I32),
                   jax.ShapeDtypeStruct((nt, N_EXPERTS, 128), F32)],
        compiler_params=pltpu.CompilerParams(dimension_semantics=("arbitrary",), vmem_limit_bytes=VMEM_LIMIT),
        name="moe_route",
    )(x1p, x1s, mod, n2w, w_router_t, b_router_b)


def _plan_kernel(cnt_ref, off_ref, loff_ref, msz_ref, grp_ref, *, nt):
    lane = lax.broadcasted_iota(I32, (N_EXPERTS, 128), 1)
    m = jnp.zeros((N_EXPERTS, 128), F32)
    for i in range(nt):
        m = jnp.where(lane == i, cnt_ref[i], m)
    total = jnp.broadcast_to(jnp.sum(m, axis=1, keepdims=True), (N_EXPERTS, 128))
    gsz = jnp.floor((total + (G_ALIGN - 1)) * (1.0 / G_ALIGN)) * G_ALIGN
    er_ = lax.broadcasted_iota(I32, (N_EXPERTS, N_EXPERTS), 0)
    ec_ = lax.broadcasted_iota(I32, (N_EXPERTS, N_EXPERTS), 1)
    lstrict = jnp.where(ec_ < er_, 1.0, 0.0)
    ir_ = lax.broadcasted_iota(I32, (128, 128), 0)
    ic_ = lax.broadcasted_iota(I32, (128, 128), 1)
    ustrict = jnp.where(ir_ < ic_, 1.0, 0.0)
    gstart = _dot_hi(lstrict, gsz)
    off_ref[...] = (gstart + _dot_hi(m, ustrict)).astype(I32)
    loff_ref[...] = _dot_hi(lstrict, m).astype(I32)
    msz_ref[...] = m.astype(I32)
    grp = jnp.where(lane == 0, gstart + total, jnp.where(lane == 1, gsz - total, jnp.where(lane == 2, gstart, gsz)))
    grp_ref[...] = grp.astype(I32)


def _plan(cnt):
    nt = cnt.shape[0]
    assert nt <= 128
    tab = jax.ShapeDtypeStruct((N_EXPERTS, 128), I32)
    return pl.pallas_call(
        functools.partial(_plan_kernel, nt=nt),
        out_shape=[tab, tab, tab, tab],
        compiler_params=pltpu.CompilerParams(vmem_limit_bytes=VMEM_LIMIT),
        name="moe_plan",
    )(cnt)


def _start_copies(msz_ref, tile, make_copy):
    for e in range(N_EXPERTS):
        make_copy(e, pl.multiple_of(msz_ref[e, tile], ROW_ALIGN)).start()


def _tile_rows(loff_ref, msz_ref, tile):
    return pl.multiple_of(loff_ref[N_EXPERTS - 1, tile] + msz_ref[N_EXPERTS - 1, tile], ROW_ALIGN)


def _dispatch_kernel(off_ref, loff_ref, msz_ref, tail_ref, h2_ref, lpos_ref, xs_hbm, xloc_ref, zero_ref, sem,
                     tail_sem, *, nt):
    i = pl.program_id(0)
    slot = i % 2

    def copy_for(tile, slot_):
        def make(e, m):
            lo = pl.multiple_of(loff_ref[e, tile], ROW_ALIGN)
            of = pl.multiple_of(off_ref[e, tile], ROW_ALIGN)
            return pltpu.make_async_copy(xloc_ref.at[slot_, pl.ds(lo, m)], xs_hbm.at[pl.ds(of, m)], sem.at[slot_])
        return make

    def wait_tile(tile, slot_):
        n = _tile_rows(loff_ref, msz_ref, tile)
        pltpu.make_async_copy(xloc_ref.at[slot_, pl.ds(0, n)], xs_hbm.at[pl.ds(0, n)], sem.at[slot_]).wait()

    lpos = lpos_ref[0].astype(jnp.int16)
    h2 = h2_ref[...]
    ck = 1024
    one, zero = jnp.ones((ck, TB), BF16), jnp.zeros((ck, TB), BF16)
    for c in range(R_LOC // ck):
        r = (lax.broadcasted_iota(I32, (ck, TB), 0) + c * ck).astype(jnp.int16)
        d = zero
        for k in range(TOP_K):
            d = jnp.where(r == lpos[k:k + 1, :], one, d)
        res = _dot(d, h2)
        xloc_ref[slot, c * ck:(c + 1) * ck, :] = res.astype(BF16)

    _start_copies(msz_ref, i, copy_for(i, slot))

    @pl.when(i > 0)
    def _():
        wait_tile(i - 1, 1 - slot)

    @pl.when(i == nt - 1)
    def _():
        zero_ref[...] = jnp.zeros(zero_ref.shape, BF16)

        def tail_copies(start):
            def body(e, carry):
                n = tail_ref[e, 1]

                @pl.when(n > 0)
                def _():
                    st = pl.multiple_of(tail_ref[e, 0], ROW_ALIGN)
                    nn = pl.multiple_of(n, ROW_ALIGN)
                    cp = pltpu.make_async_copy(zero_ref.at[pl.ds(0, nn)], xs_hbm.at[pl.ds(st, nn)], tail_sem)
                    if start:
                        cp.start()
                    else:
                        cp.wait()
                return carry
            lax.fori_loop(0, N_EXPERTS, body, 0)
        tail_copies(True)
        wait_tile(i, slot)
        tail_copies(False)


def _dispatch(off, loff, msz, tail, h2ext, lpos, n_rows):
    nt = lpos.shape[0]
    grid_spec = pltpu.PrefetchScalarGridSpec(
        num_scalar_prefetch=4,
        grid=(nt,),
        in_specs=[pl.BlockSpec((TB, H2W), lambda i, *_: (i, 0)),
                  pl.BlockSpec((1, TOP_K, TB), lambda i, *_: (i, 0, 0))],
        out_specs=pl.BlockSpec(memory_space=pl.ANY),
        scratch_shapes=[pltpu.VMEM((2, R_LOC, H2W), BF16), pltpu.VMEM((G_ALIGN, H2W), BF16),
                        pltpu.SemaphoreType.DMA((2,)), pltpu.SemaphoreType.DMA],
    )
    return pl.pallas_call(
        functools.partial(_dispatch_kernel, nt=nt),
        grid_spec=grid_spec,
        out_shape=jax.ShapeDtypeStruct((n_rows, H2W), BF16),
        compiler_params=pltpu.CompilerParams(dimension_semantics=("arbitrary",), vmem_limit_bytes=VMEM_LIMIT),
        name="moe_dispatch",
    )(off, loff, msz, tail, h2ext, lpos)


def _expert_kernel(grp_ref, wg_ref, wu_ref, wd_ref, xs_hbm, ys_hbm, xbuf, ybuf, wgu_ref, wdb_ref, st_ref,
                   in_sem, out_sem):
    e = pl.program_id(0)
    n_exp = pl.num_programs(0)

    def in_copy(row0, n, slot):
        return pltpu.make_async_copy(xs_hbm.at[pl.ds(row0, n)], xbuf.at[slot, pl.ds(0, n)], in_sem.at[slot])

    def out_copy(row0, n, slot):
        return pltpu.make_async_copy(ybuf.at[slot, pl.ds(0, n)], ys_hbm.at[pl.ds(row0, n)], out_sem.at[slot])

    def rows_of(ex):
        return grp_ref[jnp.minimum(ex, n_exp - 1), 3]

    def next_nonempty(ex):
        return lax.while_loop(lambda c: (c < n_exp) & (rows_of(c) == 0), lambda c: c + 1, ex + 1)

    def start_first_chunk(ex, slot):
        @pl.when(ex < n_exp)
        def _():
            exc = jnp.minimum(ex, n_exp - 1)
            n = pl.multiple_of(jnp.minimum(grp_ref[exc, 3], E_CHUNK), G_ALIGN)
            in_copy(pl.multiple_of(grp_ref[exc, 2], G_ALIGN), n, slot).start()

    def drain_out(slot):
        pend = st_ref[1 + slot]

        @pl.when(pend > 0)
        def _():
            out_copy(0, pl.multiple_of(pend, G_ALIGN), slot).wait()
            st_ref[1 + slot] = 0

    @pl.when(e == 0)
    def _():
        st_ref[0] = 0
        st_ref[1] = 0
        st_ref[2] = 0
        start_first_chunk(next_nonempty(-1), 0)

    g0 = grp_ref[e, 2]
    gn = grp_ref[e, 3]

    @pl.when(gn > 0)
    def _():
        wgu_ref[:, :D_EXPERT] = wg_ref[0].astype(BF16)
        wgu_ref[:, D_EXPERT:] = wu_ref[0].astype(BF16)
        wdb_ref[...] = wd_ref[0].astype(BF16)
        n_chunks = (gn + (E_CHUNK - 1)) // E_CHUNK

        def compute(n, slot):
            x = xbuf[slot, 0:n, 0:D_MODEL]
            ext = xbuf[slot, 0:n, D_MODEL:H2W].astype(F32)
            wts = ext[:, :N_EXPERTS] + ext[:, N_EXPERTS:]
            lane = lax.broadcasted_iota(I32, wts.shape, 1)
            w = jnp.sum(jnp.where(lane == e, wts, 0.0), axis=-1, keepdims=True)
            h = _dot(x, wgu_ref[...])
            hg, hu = h[:, :D_EXPERT], h[:, D_EXPERT:]
            act = hg * _sigmoid(hg) * hu * w
            ybuf[slot, 0:n, :] = _dot(act.astype(BF16), wdb_ref[...]).astype(BF16)

        def chunk_body(c, slot):
            row0 = pl.multiple_of(g0 + c * E_CHUNK, G_ALIGN)
            n = pl.multiple_of(jnp.minimum(gn - c * E_CHUNK, E_CHUNK), G_ALIGN)
            in_copy(row0, n, slot).wait()

            @pl.when(c + 1 < n_chunks)
            def _():
                n1 = pl.multiple_of(jnp.minimum(gn - (c + 1) * E_CHUNK, E_CHUNK), G_ALIGN)
                in_copy(pl.multiple_of(row0 + E_CHUNK, G_ALIGN), n1, 1 - slot).start()

            @pl.when(c + 1 == n_chunks)
            def _():
                start_first_chunk(next_nonempty(e), 1 - slot)

            drain_out(slot)
            for v in range(G_ALIGN, E_CHUNK + 1, G_ALIGN):
                @pl.when(n == v)
                def _(v=v):
                    compute(v, slot)
            out_copy(row0, n, slot).start()
            st_ref[1 + slot] = n
            return 1 - slot

        st_ref[0] = lax.fori_loop(0, n_chunks, chunk_body, st_ref[0])

    @pl.when(e == n_exp - 1)
    def _():
        drain_out(0)
        drain_out(1)


def _experts(grp, xs, wg, wu, wd):
    w_in = pl.BlockSpec((1, D_MODEL, D_EXPERT), lambda e, grp_ref: (e, 0, 0))
    grid_spec = pltpu.PrefetchScalarGridSpec(
        num_scalar_prefetch=1,
        grid=(N_EXPERTS,),
        in_specs=[w_in, w_in, pl.BlockSpec((1, D_EXPERT, D_MODEL), lambda e, grp_ref: (e, 0, 0)),
                  pl.BlockSpec(memory_space=pl.ANY)],
        out_specs=pl.BlockSpec(memory_space=pl.ANY),
        scratch_shapes=[pltpu.VMEM((2, E_CHUNK, H2W), BF16), pltpu.VMEM((2, E_CHUNK, D_MODEL), BF16),
                        pltpu.VMEM((D_MODEL, 2 * D_EXPERT), BF16), pltpu.VMEM((D_EXPERT, D_MODEL), BF16),
                        pltpu.SMEM((4,), I32), pltpu.SemaphoreType.DMA((2,)), pltpu.SemaphoreType.DMA((2,))],
    )
    return pl.pallas_call(
        _expert_kernel,
        grid_spec=grid_spec,
        out_shape=jax.ShapeDtypeStruct((xs.shape[0], D_MODEL), BF16),
        compiler_params=pltpu.CompilerParams(dimension_semantics=("arbitrary",), vmem_limit_bytes=VMEM_LIMIT),
        name="moe_experts",
    )(grp, wg, wu, wd, xs)


def _combine_kernel(off_ref, loff_ref, msz_ref, lpos_ref, h2_ref, xp_ref, xs_ref, mod_ref, wgs_ref, wus_ref, wds_ref,
                    fnw_ref, ysrt_hbm, yp_ref, ys_ref, yloc_ref, acc_ref, sem, *, nt, n_p_tiles, tiles_per_mod,
                    final):
    i = pl.program_id(0)
    slot = i % 2

    def copy_for(tile, slot_):
        def make(e, m):
            lo = pl.multiple_of(loff_ref[e, tile], ROW_ALIGN)
            of = pl.multiple_of(off_ref[e, tile], ROW_ALIGN)
            return pltpu.make_async_copy(ysrt_hbm.at[pl.ds(of, m)], yloc_ref.at[slot_, pl.ds(lo, m)], sem.at[slot_])
        return make

    @pl.when(i == 0)
    def _():
        yloc_ref[...] = jnp.zeros(yloc_ref.shape, BF16)
        _start_copies(msz_ref, 0, copy_for(0, 0))

    nxt = jnp.minimum(i + 1, nt - 1)
    _start_copies(msz_ref, nxt, copy_for(nxt, 1 - slot))

    hb = h2_ref[...]
    hg = _dot(hb, wgs_ref[...].astype(BF16))
    hu = _dot(hb, wus_ref[...].astype(BF16))
    acc_ref[...] = _dot((hg * _sigmoid(hg) * hu).astype(BF16), wds_ref[...].astype(BF16))

    def wait_tile(tile, slot_):
        n = _tile_rows(loff_ref, msz_ref, tile)
        pltpu.make_async_copy(ysrt_hbm.at[pl.ds(0, n)], yloc_ref.at[slot_, pl.ds(0, n)], sem.at[slot_]).wait()

    wait_tile(i, slot)

    @pl.when(i == nt - 1)
    def _():
        wait_tile(i, 1 - slot)

    lpos_pad = jnp.concatenate([lpos_ref[0].astype(F32), jnp.zeros((128 - TOP_K, TB), F32)], axis=0)
    lposc = lpos_pad.T.astype(I32)
    ck = 512
    cols = [jnp.broadcast_to(lposc[:, k:k + 1], (TB, ck)).astype(jnp.int16) for k in range(TOP_K)]
    one, zero = jnp.ones((TB, ck), BF16), jnp.zeros((TB, ck), BF16)
    for c in range(R_LOC // ck):
        r = (lax.broadcasted_iota(I32, (TB, ck), 1) + c * ck).astype(jnp.int16)
        cm = zero
        for k in range(TOP_K):
            cm = jnp.where(r == cols[k], one, cm)
        acc_ref[...] += _dot(cm, yloc_ref[slot, c * ck:(c + 1) * ck, :])

    row = _mod_row(i, n_p_tiles, tiles_per_mod)
    g2 = mod_ref[pl.ds(row, 1), 5 * D_MODEL:6 * D_MODEL]
    x2 = _select_x(i, n_p_tiles, xp_ref, xs_ref) + g2 * acc_ref[...]
    y = _rms(x2) * fnw_ref[...] if final else x2

    @pl.when(i < n_p_tiles)
    def _():
        yp_ref[...] = y

    @pl.when(i >= n_p_tiles)
    def _():
        ys_ref[...] = y


def _combine(off, loff, msz, lpos, h2ext, x1p, x1s, mod, wgs, wus, wds, fnw, ysorted, *, tiles_per_mod, final):
    n_p, n_s = x1p.shape[0] // TB, x1s.shape[0] // TB
    nt = n_p + n_s
    const = lambda shape: pl.BlockSpec(shape, lambda i, *_: (0,) * len(shape), pipeline_mode=pl.Buffered(1))
    p_idx = lambda i, *_: (jnp.minimum(i, n_p - 1), 0)
    s_idx = lambda i, *_: (jnp.maximum(i - n_p, 0), 0)
    grid_spec = pltpu.PrefetchScalarGridSpec(
        num_scalar_prefetch=3,
        grid=(nt,),
        in_specs=[pl.BlockSpec((1, TOP_K, TB), lambda i, *_: (i, 0, 0)),
                  pl.BlockSpec((TB, D_MODEL), lambda i, *_: (i, 0)),
                  pl.BlockSpec((TB, D_MODEL), p_idx), pl.BlockSpec((TB, D_MODEL), s_idx),
                  const(mod.shape), const((D_MODEL, D_EXPERT)), const((D_MODEL, D_EXPERT)), const((D_EXPERT, D_MODEL)),
                  const((1, D_MODEL)), pl.BlockSpec(memory_space=pl.ANY)],
        out_specs=[pl.BlockSpec((TB, D_MODEL), p_idx), pl.BlockSpec((TB, D_MODEL), s_idx)],
        scratch_shapes=[pltpu.VMEM((2, R_LOC, D_MODEL), BF16), pltpu.VMEM((TB, D_MODEL), F32),
                        pltpu.SemaphoreType.DMA((2,))],
    )
    kern = functools.partial(_combine_kernel, nt=nt, n_p_tiles=n_p, tiles_per_mod=tiles_per_mod, final=final)
    return pl.pallas_call(
        kern,
        grid_spec=grid_spec,
        out_shape=[jax.ShapeDtypeStruct(x1p.shape, F32), jax.ShapeDtypeStruct(x1s.shape, F32)],
        compiler_params=pltpu.CompilerParams(dimension_semantics=("arbitrary",), vmem_limit_bytes=VMEM_LIMIT),
        name="moe_combine",
    )(off, loff, msz, lpos, h2ext, x1p, x1s, mod, wgs, wus, wds, fnw, ysorted)


def _moe(x1p, x1s, mod, n2w, w_router, b_router, wg, wu, wd, wgs, wus, wds, fnw, *, tokens_per_mod, final):
    assert R_LOC >= TB * TOP_K + N_EXPERTS * ROW_ALIGN and tokens_per_mod % TB == 0
    nt = (x1p.shape[0] + x1s.shape[0]) // TB
    n_rows_max = nt * TB * TOP_K + nt * N_EXPERTS * ROW_ALIGN + N_EXPERTS * (G_ALIGN - ROW_ALIGN)
    tiles_per_mod = tokens_per_mod // TB
    brb = jnp.broadcast_to(b_router.reshape(N_EXPERTS, 1), (N_EXPERTS, TB))
    h2ext, lpos, cnt = _route(x1p, x1s, mod, n2w, w_router.T, brb, tiles_per_mod=tiles_per_mod)
    off, loff, msz, grp = _plan(cnt)
    xs = _dispatch(off, loff, msz, grp, h2ext, lpos, n_rows_max)
    ysorted = _experts(grp, xs, wg, wu, wd)
    return _combine(off, loff, msz, lpos, h2ext, x1p, x1s, mod, wgs, wus, wds, fnw, ysorted,
                    tiles_per_mod=tiles_per_mod, final=final)


def kernel(x_prompt, x_sample, state_gla, c, c_ctx, w_mod, b_mod, norm1_w, w_in, conv_w, conv_b, w_decay, b_decay,
           gla_norm_w, w_br_conv, w_br_gla, w_out, norm2_w, w_router, b_router, w_gate_e, w_up_e, w_down_e,
           w_gate_s, w_up_s, w_down_s, final_norm_w):
    depth = w_mod.shape[0]
    nb_p, len_p, _ = x_prompt.shape
    nb_s, len_s, _ = x_sample.shape
    yp = x_prompt.reshape(nb_p * len_p, D_MODEL)
    ys = x_sample.reshape(nb_s * len_s, D_MODEL)
    fnw = final_norm_w.reshape(1, D_MODEL)

    cond = jnp.concatenate([c_ctx[None, :], c, jnp.zeros((8 - 1 - nb_s, D_MODEL), F32)], axis=0)
    states = []
    for l in range(depth):
        mod = _modulation(cond, w_mod[l], b_mod[l].reshape(1, -1))
        mod_p, mod_s = mod[0:1], mod[1:1 + nb_s]

        w_in_r = _w_in_prep(w_in[l].T)
        wdec = jnp.zeros((2, LR_PAD, D_GLA), F32)
        wdec = wdec.at[0, 0:GLA_RANK].set(w_decay[l, 0]).at[1, GLA_RANK:2 * GLA_RANK].set(w_decay[l, 1])
        n1w = norm1_w[l].reshape(1, D_MODEL)
        mix_w = (conv_w[l], conv_b[l].reshape(1, D_CONV), gla_norm_w[l].reshape(1, D_GLA),
                 w_br_conv[l].astype(BF16), w_br_gla[l].astype(BF16), w_out[l].astype(BF16))
        moe_w = (norm2_w[l].reshape(1, D_MODEL), w_router[l], b_router[l],
                 w_gate_e[l], w_up_e[l], w_down_e[l], w_gate_s[l], w_up_s[l], w_down_s[l])

        proj_p = _in_proj(yp, mod_p, n1w, w_in_r, rows_per_mod=nb_p * len_p)
        o_p, st = _gla_scan(proj_p, wdec, b_decay[l], None, seq_len=len_p, emit_state=True)
        yp = _mix_dense(proj_p, o_p, yp, mod_p, *mix_w, row_len=len_p, tokens_per_mod=nb_p * len_p)
        states.append(st)
        proj_s = _in_proj(ys, mod_s, n1w, w_in_r, rows_per_mod=len_s)
        (o_s,) = _gla_scan(proj_s, wdec, b_decay[l], state_gla[:, l], seq_len=len_s, emit_state=False)
        ys = _mix_dense(proj_s, o_s, ys, mod_s, *mix_w, row_len=GRID_W, tokens_per_mod=len_s)

        yp, ys = _moe(yp, ys, mod, *moe_w, fnw, tokens_per_mod=len_s, final=l == depth - 1)
    new_state = jnp.stack(states, axis=1)
    return (yp.reshape(nb_p, len_p, D_MODEL), ys.reshape(nb_s, len_s, D_MODEL), new_state)
```

```python
import functools

import jax
import jax.numpy as jnp
from jax import lax
from jax.experimental import pallas as pl
from jax.experimental.pallas import tpu as pltpu

F32 = jnp.float32
BF16 = jnp.bfloat16

D_MODEL = 1024
GRID_W = 64
D_CONV = 512
N_HEADS = 4
HEAD_D = 128
D_GLA = N_HEADS * HEAD_D
GLA_RANK = 16
GLA_GATE_NORM = 16.0
LOG2_E = 1.4426950408889634
CHUNK = 64
SUB = 8
N_SUB = CHUNK // SUB
N_EXPERTS = 64
TOP_K = 8
D_EXPERT = 256
ROUTED_SCALE = 2.5
EPS = 1e-6

C_U, C_GB, C_GC, C_Q, C_K, C_V, C_GO = 0, 512, 1024, 1536, 2048, 2560, 3072
C_BRC, C_BRG, C_LR = 3584, 4608, 5632
D_PROJ = 5760
LR_PAD = 128

VMEM_LIMIT = 56 * 1024 * 1024


def _dot(a, b):
    return jnp.dot(a, b, preferred_element_type=F32)


def _dot_nt(a, b):
    return lax.dot_general(a, b, (((1,), (1,)), ((), ())), preferred_element_type=F32)


def _dot_tn(a, b):
    return lax.dot_general(a, b, (((0,), (0,)), ((), ())), preferred_element_type=F32)


def _dot_hi(a, b):
    return jnp.dot(a, b, preferred_element_type=F32, precision=lax.Precision.HIGHEST)


def _split_bf16(x):
    hi = x.astype(BF16)
    lo = (x - hi.astype(F32)).astype(BF16)
    return hi, lo


def _rms(x):
    return x * lax.rsqrt(jnp.mean(x * x, axis=-1, keepdims=True) + EPS)


def _sigmoid(x):
    return 0.5 * jnp.tanh(0.5 * x) + 0.5


def _mod_kernel(cond_ref, w_ref, b_ref, o_ref):
    c = cond_ref[...]
    o_ref[...] = _dot_hi(c * jax.nn.sigmoid(c), w_ref[...]) + b_ref[...]


def _modulation(cond, w_mod, b_mod):
    n_rows = cond.shape[0]
    tn = 1536
    return pl.pallas_call(
        _mod_kernel,
        grid=(6 * D_MODEL // tn,),
        in_specs=[pl.BlockSpec((n_rows, D_MODEL), lambda j: (0, 0)),
                  pl.BlockSpec((D_MODEL, tn), lambda j: (0, j)),
                  pl.BlockSpec((1, tn), lambda j: (0, j))],
        out_specs=pl.BlockSpec((n_rows, tn), lambda j: (0, j)),
        out_shape=jax.ShapeDtypeStruct((n_rows, 6 * D_MODEL), F32),
        compiler_params=pltpu.CompilerParams(dimension_semantics=("arbitrary",),
                                             vmem_limit_bytes=VMEM_LIMIT),
        name="modulation",
    )(cond, w_mod, b_mod)


D_IN_PROJ = 5664
C_LR_SRC, C_GATES_SRC = 3584, 3616


def _w_in_prep_kernel(w_ref, o_ref):
    rc = 64

    def copy_rows(dst0, src0, n):
        def body(t, carry):
            off = pl.multiple_of(t * rc, rc)
            o_ref[pl.ds(dst0 + off, rc), :] = w_ref[pl.ds(src0 + off, rc), :].astype(BF16)
            return carry
        lax.fori_loop(0, n // rc, body, 0)

    copy_rows(0, 0, C_LR_SRC)
    copy_rows(C_BRC, C_GATES_SRC, 2 * D_MODEL)
    o_ref[C_LR:C_LR + 2 * GLA_RANK, :] = w_ref[C_LR_SRC:C_GATES_SRC, :].astype(BF16)
    o_ref[C_LR + 2 * GLA_RANK:D_PROJ, :] = jnp.zeros((D_PROJ - C_LR - 2 * GLA_RANK, D_MODEL), BF16)


def _w_in_prep(w_in_t):
    return pl.pallas_call(
        _w_in_prep_kernel,
        out_shape=jax.ShapeDtypeStruct((D_PROJ, D_MODEL), BF16),
        compiler_params=pltpu.CompilerParams(vmem_limit_bytes=VMEM_LIMIT),
        name="w_in_prep",
    )(w_in_t)


def _inproj_kernel(x_ref, mod_ref, nw_ref, w_ref, o_ref, *, rows_per_mod, tm):
    i = pl.program_id(1)
    row = (i * tm) // rows_per_mod
    sh = mod_ref[pl.ds(row, 1), 0:D_MODEL]
    sc = mod_ref[pl.ds(row, 1), D_MODEL:2 * D_MODEL]
    h = _rms(x_ref[...]) * nw_ref[...] * (1.0 + sc) + sh
    o_ref[...] = _dot_nt(h.astype(BF16), w_ref[...]).astype(BF16)


def _in_proj(x2d, mod, norm_w, w_in_r, rows_per_mod):
    t = x2d.shape[0]
    tm, tn = 1024, 1920
    kern = functools.partial(_inproj_kernel, rows_per_mod=rows_per_mod, tm=tm)
    return pl.pallas_call(
        kern,
        grid=(D_PROJ // tn, t // tm),
        in_specs=[pl.BlockSpec((tm, D_MODEL), lambda j, i: (i, 0)),
                  pl.BlockSpec(mod.shape, lambda j, i: (0, 0)),
                  pl.BlockSpec((1, D_MODEL), lambda j, i: (0, 0)),
                  pl.BlockSpec((tn, D_MODEL), lambda j, i: (j, 0))],
        out_specs=pl.BlockSpec((tm, tn), lambda j, i: (i, j)),
        out_shape=jax.ShapeDtypeStruct((t, D_PROJ), BF16),
        compiler_params=pltpu.CompilerParams(dimension_semantics=("arbitrary", "arbitrary"),
                                             vmem_limit_bytes=VMEM_LIMIT),
        name="in_proj",
    )(x2d, mod, norm_w, w_in_r)


def _log2_sigmoid(x):
    return jnp.minimum(x, 0.0) * LOG2_E - jnp.log2(1.0 + jnp.exp2(jnp.abs(x) * (-LOG2_E)))


def _gla_chunk_head(qc, kc, vc, bc, st, rev):
    lane = lax.broadcasted_iota(jnp.int32, (SUB, CHUNK), 1)
    sub = lax.broadcasted_iota(jnp.int32, (SUB, CHUNK), 0)
    tot = bc[0:1] if rev else bc[CHUNK - 1:CHUNK]

    q_in = (qc * jnp.exp2(bc)).astype(BF16)
    k_tail = kc * jnp.exp2(tot - bc)
    vt = vc.T.astype(BF16)
    st_new = st * jnp.exp2(tot) + _dot(vt, k_tail.astype(BF16))

    lhs_segs, rhs_segs = [], []

    def rows(before, mid, after):
        parts = ([jnp.zeros((before, HEAD_D), F32)] if before else []) + [mid]
        parts += [jnp.zeros((after, HEAD_D), F32)] if after else []
        return jnp.concatenate(parts, axis=0) if len(parts) > 1 else mid

    key_blocks = range(1, N_SUB) if rev else range(0, N_SUB - 1)
    for jb in key_blocks:
        r0 = jb * SUB
        ref_row = bc[r0:r0 + 1] if rev else bc[r0 + SUB - 1:r0 + SUB]
        ke = kc[r0:r0 + SUB] * jnp.exp2(ref_row - bc[r0:r0 + SUB])
        rhs_segs.append(rows(r0, ke, CHUNK - r0 - SUB))
        if rev:
            ql = qc[:r0] * jnp.exp2(bc[:r0] - ref_row)
            lhs_segs.append(rows(0, ql, CHUNK - r0))
        else:
            ql = qc[r0 + SUB:] * jnp.exp2(bc[r0 + SUB:] - ref_row)
            lhs_segs.append(rows(r0 + SUB, ql, 0))
    far = _dot_nt(jnp.concatenate(lhs_segs, axis=1).astype(BF16),
                  jnp.concatenate(rhs_segs, axis=1).astype(BF16))

    blocks = []
    for ib in range(N_SUB):
        r0 = ib * SUB
        qi, bi = qc[r0:r0 + SUB], bc[r0:r0 + SUB]
        acc = jnp.zeros((SUB, CHUNK), F32)
        for jj in range(SUB):
            j = r0 + jj
            e = jnp.exp2(bi - bc[j:j + 1])
            col = jnp.sum(qi * (kc[j:j + 1] * e), axis=-1, keepdims=True)
            acc = jnp.where(lane == j, col, acc)
        keep = (lane - r0 >= sub) if rev else (lane - r0 <= sub)
        blocks.append(jnp.where(keep, acc, 0.0))
    scores = far + jnp.concatenate(blocks, axis=0)
    o = _dot_nt(jnp.concatenate([q_in, scores.astype(BF16)], axis=1),
                jnp.concatenate([st.astype(BF16), vt], axis=1))
    return o, st_new


def _scan_kernel(*refs, seq_len, has_s0, emit_state):
    it = iter(refs)
    qkv_ref, lr_ref, wdec_ref, bdec_ref = (next(it) for _ in range(4))
    s0_ref = next(it) if has_s0 else None
    out_ref = next(it)
    st_out_ref = next(it) if emit_state else None
    la_f_ref, la_b_ref, o_ref, st_ref = (next(it) for _ in range(4))

    L = seq_len
    n_chunks = L // CHUNK
    tr = 256
    assert L % tr == 0

    ri = lax.broadcasted_iota(jnp.int32, (tr, tr), 0)
    ci = lax.broadcasted_iota(jnp.int32, (tr, tr), 1)
    same_chunk = (ri // CHUNK) == (ci // CHUNK)

    def decay_body(t, carry):
        r0 = pl.multiple_of(t * tr, tr)
        lr = lr_ref[pl.ds(r0, tr), :]
        for d, ref in ((0, la_f_ref), (1, la_b_ref)):
            whi, wlo = _split_bf16(wdec_ref[d])
            z = _dot(lr, whi) + _dot(lr, wlo) + bdec_ref[d:d + 1]
            la_hi, la_lo = _split_bf16(_log2_sigmoid(z) * (1.0 / GLA_GATE_NORM))
            tri = jnp.where(same_chunk & ((ci >= ri) if d else (ci <= ri)), 1.0, 0.0).astype(BF16)
            ref[pl.ds(r0, tr), :] = _dot(tri, la_hi) + _dot(tri, la_lo)
        return carry
    lax.fori_loop(0, L // tr, decay_body, 0)

    for rev, la_ref in ((False, la_f_ref), (True, la_b_ref)):
        d = 1 if rev else 0
        for h in range(N_HEADS):
            if has_s0:
                st_ref[h] = s0_ref[0, d, h].T
            else:
                st_ref[h] = jnp.zeros((HEAD_D, HEAD_D), F32)

        def chunk_body(c, carry, rev=rev, la_ref=la_ref, heads=range(N_HEADS)):
            cc = (n_chunks - 1 - c) if rev else c
            r0 = pl.multiple_of(cc * CHUNK, CHUNK)
            for h in heads:
                lo, hi = h * HEAD_D, (h + 1) * HEAD_D
                qc = qkv_ref[pl.ds(r0, CHUNK), lo:hi].astype(F32) * (HEAD_D ** -0.5)
                kc = qkv_ref[pl.ds(r0, CHUNK), D_GLA + lo:D_GLA + hi].astype(F32)
                vc = qkv_ref[pl.ds(r0, CHUNK), 2 * D_GLA + lo:2 * D_GLA + hi].astype(F32)
                oc, st_new = _gla_chunk_head(qc, kc, vc, la_ref[pl.ds(r0, CHUNK), lo:hi], st_ref[h], rev)
                st_ref[h] = st_new
                if rev:
                    o_ref[pl.ds(r0, CHUNK), lo:hi] += oc
                else:
                    o_ref[pl.ds(r0, CHUNK), lo:hi] = oc
            return carry
        lax.fori_loop(0, n_chunks, chunk_body, 0, unroll=4)

        if emit_state:
            for h in range(N_HEADS):
                st_out_ref[0, d, h] = st_ref[h].T

    out_ref[...] = o_ref[...].astype(BF16)


def _gla_scan(proj, wdec, bdec, s0, *, seq_len, emit_state):
    t = proj.shape[0]
    nb = t // seq_len
    has_s0 = s0 is not None
    const = lambda shape: pl.BlockSpec(shape, lambda b: (0,) * len(shape), pipeline_mode=pl.Buffered(1))
    assert C_Q % (3 * D_GLA) == 0 and C_K == C_Q + D_GLA and C_V == C_K + D_GLA and C_LR % LR_PAD == 0
    in_specs = [pl.BlockSpec((seq_len, 3 * D_GLA), lambda b: (b, C_Q // (3 * D_GLA))),
                pl.BlockSpec((seq_len, LR_PAD), lambda b: (b, C_LR // LR_PAD)),
                const((2, LR_PAD, D_GLA)), const((2, D_GLA))]
    args = [proj, proj, wdec, bdec]
    if has_s0:
        in_specs.append(pl.BlockSpec((1, 2, N_HEADS, HEAD_D, HEAD_D), lambda b: (b, 0, 0, 0, 0)))
        args.append(s0)
    out_specs = [pl.BlockSpec((seq_len, D_GLA), lambda b: (b, 0))]
    out_shape = [jax.ShapeDtypeStruct((t, D_GLA), BF16)]
    if emit_state:
        out_specs.append(pl.BlockSpec((1, 2, N_HEADS, HEAD_D, HEAD_D), lambda b: (b, 0, 0, 0, 0)))
        out_shape.append(jax.ShapeDtypeStruct((nb, 2, N_HEADS, HEAD_D, HEAD_D), F32))
    kern = functools.partial(_scan_kernel, seq_len=seq_len, has_s0=has_s0, emit_state=emit_state)
    return pl.pallas_call(
        kern,
        grid=(nb,),
        in_specs=in_specs,
        out_specs=out_specs,
        out_shape=out_shape,
        scratch_shapes=[pltpu.VMEM((seq_len, D_GLA), F32), pltpu.VMEM((seq_len, D_GLA), F32),
                        pltpu.VMEM((seq_len, D_GLA), F32), pltpu.VMEM((N_HEADS, HEAD_D, HEAD_D), F32)],
        compiler_params=pltpu.CompilerParams(dimension_semantics=("arbitrary",),
                                             vmem_limit_bytes=VMEM_LIMIT),
        name="gla_scan",
    )(*args)


MIX_TM = 1024
MIX_SUB = 256


def _mix_dense_kernel(conv_ref, go_ref, gc0_ref, gc1_ref, gg0_ref, gg1_ref, o_ref, x_ref, mod_ref, cw_ref, cb_ref,
                      gnw_ref, wbc_ref, wbg_ref, wout_ref, out_ref, yc_ref, yg_ref, *, row_len, tiles_per_mod):
    i = pl.program_id(0)
    mod_row = i // tiles_per_mod if mod_ref.shape[0] > 1 else 0
    g1 = mod_ref[pl.ds(mod_row, 1), 2 * D_MODEL:3 * D_MODEL]
    hsel_r = lax.broadcasted_iota(jnp.int32, (D_GLA, D_GLA), 0) // HEAD_D
    hsel_c = lax.broadcasted_iota(jnp.int32, (D_GLA, D_GLA), 1) // HEAD_D
    head_avg = jnp.where(hsel_r == hsel_c, 1.0 / HEAD_D, 0.0).astype(BF16)
    pos = lax.broadcasted_iota(jnp.int32, (MIX_SUB, 1), 0)
    in_row = pos % row_len

    def branch_body(t, carry):
        rows = pl.ds(pl.multiple_of(t * MIX_SUB, MIX_SUB), MIX_SUB)
        cu = conv_ref[rows, 2 * D_CONV:3 * D_CONV].astype(F32) * conv_ref[rows, 0:D_CONV].astype(F32)
        left = jnp.where(in_row == 0, 0.0, pltpu.roll(cu, 1, axis=0))
        right = jnp.where(in_row == row_len - 1, 0.0, pltpu.roll(cu, MIX_SUB - 1, axis=0))
        conv = cw_ref[0:1] * left + cw_ref[1:2] * cu + cw_ref[2:3] * right + cb_ref[...]
        yc_ref[rows, :] = (conv_ref[rows, D_CONV:2 * D_CONV].astype(F32) * conv).astype(BF16)
        o = o_ref[rows, :].astype(F32)
        osq_hi, osq_lo = _split_bf16(o * o)
        ms = _dot(osq_hi, head_avg) + _dot(osq_lo, head_avg)
        g_out = go_ref[rows, :].astype(F32)
        yg_ref[rows, :] = (o * lax.rsqrt(ms + EPS) * gnw_ref[...] * (g_out * _sigmoid(g_out))).astype(BF16)
        return carry
    lax.fori_loop(0, MIX_TM // MIX_SUB, branch_body, 0)

    half = D_MODEL // 2
    pc = _dot(yc_ref[...], wbc_ref[...])
    pg = _dot(yg_ref[...], wbg_ref[...])
    merged = jnp.concatenate(
        [_sigmoid(gc0_ref[...].astype(F32)) * pc[:, :half] + _sigmoid(gg0_ref[...].astype(F32)) * pg[:, :half],
         _sigmoid(gc1_ref[...].astype(F32)) * pc[:, half:] + _sigmoid(gg1_ref[...].astype(F32)) * pg[:, half:]],
        axis=1)
    out_ref[...] = x_ref[...] + g1 * _dot(merged.astype(BF16), wout_ref[...])


def _mix_dense(proj, o, x2d, mod, conv_w, conv_b, gnw, wbc, wbg, wout, *, row_len, tokens_per_mod):
    t = x2d.shape[0]
    assert MIX_SUB % row_len == 0 and t % MIX_TM == 0 and tokens_per_mod % MIX_TM == 0
    const = lambda shape: pl.BlockSpec(shape, lambda i: (0,) * len(shape), pipeline_mode=pl.Buffered(1))
    cols = lambda width, start: pl.BlockSpec((MIX_TM, width), lambda i: (i, start // width))
    half = D_MODEL // 2
    assert C_U == 0 and C_GB == D_CONV and C_GC == 2 * D_CONV and C_GO % D_GLA == 0 and C_BRC % half == 0
    kern = functools.partial(_mix_dense_kernel, row_len=row_len, tiles_per_mod=tokens_per_mod // MIX_TM)
    return pl.pallas_call(
        kern,
        grid=(t // MIX_TM,),
        in_specs=[cols(3 * D_CONV, C_U), cols(D_GLA, C_GO),
                  cols(half, C_BRC), cols(half, C_BRC + half), cols(half, C_BRG), cols(half, C_BRG + half),
                  pl.BlockSpec((MIX_TM, D_GLA), lambda i: (i, 0)), pl.BlockSpec((MIX_TM, D_MODEL), lambda i: (i, 0)),
                  const(mod.shape), const((3, D_CONV)), const((1, D_CONV)), const((1, D_GLA)),
                  const((D_CONV, D_MODEL)), const((D_GLA, D_MODEL)), const((D_MODEL, D_MODEL))],
        out_specs=pl.BlockSpec((MIX_TM, D_MODEL), lambda i: (i, 0)),
        out_shape=jax.ShapeDtypeStruct((t, D_MODEL), F32),
        scratch_shapes=[pltpu.VMEM((MIX_TM, D_CONV), BF16), pltpu.VMEM((MIX_TM, D_GLA), BF16)],
        compiler_params=pltpu.CompilerParams(dimension_semantics=("arbitrary",), vmem_limit_bytes=VMEM_LIMIT),
        name="mix_dense",
    )(proj, proj, proj, proj, proj, proj, o, x2d, mod, conv_w, conv_b, gnw, wbc, wbg, wout)


I32 = jnp.int32
TB = 256
ROW_ALIGN = 16
G_ALIGN = 128
E_CHUNK = 2048
R_LOC = 3072
H2W = 1152


def _select_x(i, n_p_tiles, xp_ref, xs_ref):
    return jnp.where(i < n_p_tiles, xp_ref[...], xs_ref[...])


def _mod_row(i, n_p_tiles, tiles_per_mod):
    return jnp.where(i < n_p_tiles, 0, 1 + (i - n_p_tiles) // tiles_per_mod)


def _route_kernel(xp_ref, xs_ref, mod_ref, n2w_ref, wrt_ref, brb_ref, h2_ref, lpos_ref, cnt_ref, *,
                  n_p_tiles, tiles_per_mod):
    i = pl.program_id(0)
    row = _mod_row(i, n_p_tiles, tiles_per_mod)
    x = _select_x(i, n_p_tiles, xp_ref, xs_ref)
    sh = mod_ref[pl.ds(row, 1), 3 * D_MODEL:4 * D_MODEL]
    sc = mod_ref[pl.ds(row, 1), 4 * D_MODEL:5 * D_MODEL]
    h2 = _rms(x) * n2w_ref[...] * (1.0 + sc) + sh

    h_hi, h_lo = _split_bf16(h2)
    w_hi, w_lo = _split_bf16(wrt_ref[...])
    scores = jax.nn.sigmoid(_dot_nt(w_hi, h_hi) + (_dot_nt(w_hi, h_lo) + _dot_nt(w_lo, h_hi)))
    biased = scores + brb_ref[...]
    eidx = lax.broadcasted_iota(I32, scores.shape, 0)
    picks = []
    for _k in range(TOP_K):
        m = jnp.max(biased, axis=0, keepdims=True)
        first = jnp.min(jnp.where(biased == m, eidx, N_EXPERTS), axis=0, keepdims=True)
        pick = eidx == first
        picks.append(pick)
        biased = jnp.where(pick, -jnp.inf, biased)
    sel = jnp.zeros(scores.shape, F32)
    for pick in picks:
        sel = jnp.where(pick, 1.0, sel)
    selsc = sel * scores
    comb = selsc / jnp.sum(selsc, axis=0, keepdims=True) * ROUTED_SCALE

    selb = sel.astype(BF16)
    tr_ = lax.broadcasted_iota(I32, (TB, TB), 0)
    tc_ = lax.broadcasted_iota(I32, (TB, TB), 1)
    rank = _dot(selb, jnp.where(tr_ < tc_, 1.0, 0.0).astype(BF16))
    n_b = _dot(selb, jnp.ones((TB, 128), BF16))
    m_b = jnp.maximum(jnp.floor((n_b + (ROW_ALIGN - 1)) * (1.0 / ROW_ALIGN)), 1.0) * ROW_ALIGN
    er_ = lax.broadcasted_iota(I32, (N_EXPERTS, N_EXPERTS), 0)
    ec_ = lax.broadcasted_iota(I32, (N_EXPERTS, N_EXPERTS), 1)
    loff_b = _dot(jnp.where(ec_ < er_, 1.0, 0.0).astype(BF16), m_b.astype(BF16))
    lposf = jnp.concatenate([loff_b] * (TB // 128), axis=1) + rank
    rows = [jnp.sum(jnp.where(pick, lposf, 0.0), axis=0, keepdims=True) for pick in picks]
    lpos_ref[0] = jnp.concatenate(rows, axis=0).astype(I32)
    cnt_ref[0] = m_b

    combt = comb.T
    chi = combt.astype(BF16).astype(F32)
    h2_ref[:, 0:D_MODEL] = h_hi
    h2_ref[:, D_MODEL:H2W] = jnp.concatenate([chi, combt - chi], axis=1).astype(BF16)


def _route(x1p, x1s, mod, n2w, w_router_t, b_router_b, *, tiles_per_mod):
    n_p, n_s = x1p.shape[0] // TB, x1s.shape[0] // TB
    nt = n_p + n_s
    kern = functools.partial(_route_kernel, n_p_tiles=n_p, tiles_per_mod=tiles_per_mod)
    const = lambda shape: pl.BlockSpec(shape, lambda i: (0,) * len(shape))
    return pl.pallas_call(
        kern,
        grid=(nt,),
        in_specs=[pl.BlockSpec((TB, D_MODEL), lambda i: (jnp.minimum(i, n_p - 1), 0)),
                  pl.BlockSpec((TB, D_MODEL), lambda i: (jnp.maximum(i - n_p, 0), 0)),
                  const(mod.shape), const((1, D_MODEL)), const((N_EXPERTS, D_MODEL)), const((N_EXPERTS, TB))],
        out_specs=[pl.BlockSpec((TB, H2W), lambda i: (i, 0)),
                   pl.BlockSpec((1, TOP_K, TB), lambda i: (i, 0, 0)),
                   pl.BlockSpec((1, N_EXPERTS, 128), lambda i: (i, 0, 0))],
        out_shape=[jax.ShapeDtypeStruct((nt * TB, H2W), BF16),
                   jax.ShapeDtypeStruct((nt, TOP_K, TB), I32),
                   jax.ShapeDtypeStruct((nt, N_EXPERTS, 128), F32)],
        compiler_params=pltpu.CompilerParams(dimension_semantics=("arbitrary",), vmem_limit_bytes=VMEM_LIMIT),
        name="moe_route",
    )(x1p, x1s, mod, n2w, w_router_t, b_router_b)


def _plan_kernel(cnt_ref, off_ref, loff_ref, msz_ref, grp_ref, *, nt):
    lane = lax.broadcasted_iota(I32, (N_EXPERTS, 128), 1)
    m = jnp.zeros((N_EXPERTS, 128), F32)
    for i in range(nt):
        m = jnp.where(lane == i, cnt_ref[i], m)
    total = jnp.broadcast_to(jnp.sum(m, axis=1, keepdims=True), (N_EXPERTS, 128))
    gsz = jnp.floor((total + (G_ALIGN - 1)) * (1.0 / G_ALIGN)) * G_ALIGN
    er_ = lax.broadcasted_iota(I32, (N_EXPERTS, N_EXPERTS), 0)
    ec_ = lax.broadcasted_iota(I32, (N_EXPERTS, N_EXPERTS), 1)
    lstrict = jnp.where(ec_ < er_, 1.0, 0.0)
    ir_ = lax.broadcasted_iota(I32, (128, 128), 0)
    ic_ = lax.broadcasted_iota(I32, (128, 128), 1)
    ustrict = jnp.where(ir_ < ic_, 1.0, 0.0)
    gstart = _dot_hi(lstrict, gsz)
    off_ref[...] = (gstart + _dot_hi(m, ustrict)).astype(I32)
    loff_ref[...] = _dot_hi(lstrict, m).astype(I32)
    msz_ref[...] = m.astype(I32)
    grp = jnp.where(lane == 0, gstart + total, jnp.where(lane == 1, gsz - total, jnp.where(lane == 2, gstart, gsz)))
    grp_ref[...] = grp.astype(I32)


def _plan(cnt):
    nt = cnt.shape[0]
    assert nt <= 128
    tab = jax.ShapeDtypeStruct((N_EXPERTS, 128), I32)
    return pl.pallas_call(
        functools.partial(_plan_kernel, nt=nt),
        out_shape=[tab, tab, tab, tab],
        compiler_params=pltpu.CompilerParams(vmem_limit_bytes=VMEM_LIMIT),
        name="moe_plan",
    )(cnt)


def _start_copies(msz_ref, tile, make_copy):
    for e in range(N_EXPERTS):
        make_copy(e, pl.multiple_of(msz_ref[e, tile], ROW_ALIGN)).start()


def _tile_rows(loff_ref, msz_ref, tile):
    return pl.multiple_of(loff_ref[N_EXPERTS - 1, tile] + msz_ref[N_EXPERTS - 1, tile], ROW_ALIGN)


def _dispatch_kernel(off_ref, loff_ref, msz_ref, tail_ref, h2_ref, lpos_ref, xs_hbm, xloc_ref, zero_ref, sem,
                     tail_sem, *, nt):
    i = pl.program_id(0)
    slot = i % 2

    def copy_for(tile, slot_):
        def make(e, m):
            lo = pl.multiple_of(loff_ref[e, tile], ROW_ALIGN)
            of = pl.multiple_of(off_ref[e, tile], ROW_ALIGN)
            return pltpu.make_async_copy(xloc_ref.at[slot_, pl.ds(lo, m)], xs_hbm.at[pl.ds(of, m)], sem.at[slot_])
        return make

    def wait_tile(tile, slot_):
        n = _tile_rows(loff_ref, msz_ref, tile)
        pltpu.make_async_copy(xloc_ref.at[slot_, pl.ds(0, n)], xs_hbm.at[pl.ds(0, n)], sem.at[slot_]).wait()

    lpos = lpos_ref[0].astype(jnp.int16)
    h2 = h2_ref[...]
    ck = 1024
    one, zero = jnp.ones((ck, TB), BF16), jnp.zeros((ck, TB), BF16)
    for c in range(R_LOC // ck):
        r = (lax.broadcasted_iota(I32, (ck, TB), 0) + c * ck).astype(jnp.int16)
        d = zero
        for k in range(TOP_K):
            d = jnp.where(r == lpos[k:k + 1, :], one, d)
        res = _dot(d, h2)
        xloc_ref[slot, c * ck:(c + 1) * ck, :] = res.astype(BF16)

    _start_copies(msz_ref, i, copy_for(i, slot))

    @pl.when(i > 0)
    def _():
        wait_tile(i - 1, 1 - slot)

    @pl.when(i == nt - 1)
    def _():
        zero_ref[...] = jnp.zeros(zero_ref.shape, BF16)

        def tail_copies(start):
            def body(e, carry):
                n = tail_ref[e, 1]

                @pl.when(n > 0)
                def _():
                    st = pl.multiple_of(tail_ref[e, 0], ROW_ALIGN)
                    nn = pl.multiple_of(n, ROW_ALIGN)
                    cp = pltpu.make_async_copy(zero_ref.at[pl.ds(0, nn)], xs_hbm.at[pl.ds(st, nn)], tail_sem)
                    if start:
                        cp.start()
                    else:
                        cp.wait()
                return carry
            lax.fori_loop(0, N_EXPERTS, body, 0)
        tail_copies(True)
        wait_tile(i, slot)
        tail_copies(False)


def _dispatch(off, loff, msz, tail, h2ext, lpos, n_rows):
    nt = lpos.shape[0]
    grid_spec = pltpu.PrefetchScalarGridSpec(
        num_scalar_prefetch=4,
        grid=(nt,),
        in_specs=[pl.BlockSpec((TB, H2W), lambda i, *_: (i, 0)),
                  pl.BlockSpec((1, TOP_K, TB), lambda i, *_: (i, 0, 0))],
        out_specs=pl.BlockSpec(memory_space=pl.ANY),
        scratch_shapes=[pltpu.VMEM((2, R_LOC, H2W), BF16), pltpu.VMEM((G_ALIGN, H2W), BF16),
                        pltpu.SemaphoreType.DMA((2,)), pltpu.SemaphoreType.DMA],
    )
    return pl.pallas_call(
        functools.partial(_dispatch_kernel, nt=nt),
        grid_spec=grid_spec,
        out_shape=jax.ShapeDtypeStruct((n_rows, H2W), BF16),
        compiler_params=pltpu.CompilerParams(dimension_semantics=("arbitrary",), vmem_limit_bytes=VMEM_LIMIT),
        name="moe_dispatch",
    )(off, loff, msz, tail, h2ext, lpos)


def _expert_kernel(grp_ref, wg_ref, wu_ref, wd_ref, xs_hbm, ys_hbm, xbuf, ybuf, wgu_ref, wdb_ref, st_ref,
                   in_sem, out_sem):
    e = pl.program_id(0)
    n_exp = pl.num_programs(0)

    def in_copy(row0, n, slot):
        return pltpu.make_async_copy(xs_hbm.at[pl.ds(row0, n)], xbuf.at[slot, pl.ds(0, n)], in_sem.at[slot])

    def out_copy(row0, n, slot):
        return pltpu.make_async_copy(ybuf.at[slot, pl.ds(0, n)], ys_hbm.at[pl.ds(row0, n)], out_sem.at[slot])

    def rows_of(ex):
        return grp_ref[jnp.minimum(ex, n_exp - 1), 3]

    def next_nonempty(ex):
        return lax.while_loop(lambda c: (c < n_exp) & (rows_of(c) == 0), lambda c: c + 1, ex + 1)

    def start_first_chunk(ex, slot):
        @pl.when(ex < n_exp)
        def _():
            exc = jnp.minimum(ex, n_exp - 1)
            n = pl.multiple_of(jnp.minimum(grp_ref[exc, 3], E_CHUNK), G_ALIGN)
            in_copy(pl.multiple_of(grp_ref[exc, 2], G_ALIGN), n, slot).start()

    def drain_out(slot):
        pend = st_ref[1 + slot]

        @pl.when(pend > 0)
        def _():
            out_copy(0, pl.multiple_of(pend, G_ALIGN), slot).wait()
            st_ref[1 + slot] = 0

    @pl.when(e == 0)
    def _():
        st_ref[0] = 0
        st_ref[1] = 0
        st_ref[2] = 0
        start_first_chunk(next_nonempty(-1), 0)

    g0 = grp_ref[e, 2]
    gn = grp_ref[e, 3]

    @pl.when(gn > 0)
    def _():
        wgu_ref[:, :D_EXPERT] = wg_ref[0].astype(BF16)
        wgu_ref[:, D_EXPERT:] = wu_ref[0].astype(BF16)
        wdb_ref[...] = wd_ref[0].astype(BF16)
        n_chunks = (gn + (E_CHUNK - 1)) // E_CHUNK

        def compute(n, slot):
            x = xbuf[slot, 0:n, 0:D_MODEL]
            ext = xbuf[slot, 0:n, D_MODEL:H2W].astype(F32)
            wts = ext[:, :N_EXPERTS] + ext[:, N_EXPERTS:]
            lane = lax.broadcasted_iota(I32, wts.shape, 1)
            w = jnp.sum(jnp.where(lane == e, wts, 0.0), axis=-1, keepdims=True)
            h = _dot(x, wgu_ref[...])
            hg, hu = h[:, :D_EXPERT], h[:, D_EXPERT:]
            act = hg * _sigmoid(hg) * hu * w
            ybuf[slot, 0:n, :] = _dot(act.astype(BF16), wdb_ref[...]).astype(BF16)

        def chunk_body(c, slot):
            row0 = pl.multiple_of(g0 + c * E_CHUNK, G_ALIGN)
            n = pl.multiple_of(jnp.minimum(gn - c * E_CHUNK, E_CHUNK), G_ALIGN)
            in_copy(row0, n, slot).wait()

            @pl.when(c + 1 < n_chunks)
            def _():
                n1 = pl.multiple_of(jnp.minimum(gn - (c + 1) * E_CHUNK, E_CHUNK), G_ALIGN)
                in_copy(pl.multiple_of(row0 + E_CHUNK, G_ALIGN), n1, 1 - slot).start()

            @pl.when(c + 1 == n_chunks)
            def _():
                start_first_chunk(next_nonempty(e), 1 - slot)

            drain_out(slot)
            for v in range(G_ALIGN, E_CHUNK + 1, G_ALIGN):
                @pl.when(n == v)
                def _(v=v):
                    compute(v, slot)
            out_copy(row0, n, slot).start()
            st_ref[1 + slot] = n
            return 1 - slot

        st_ref[0] = lax.fori_loop(0, n_chunks, chunk_body, st_ref[0])

    @pl.when(e == n_exp - 1)
    def _():
        drain_out(0)
        drain_out(1)


def _experts(grp, xs, wg, wu, wd):
    w_in = pl.BlockSpec((1, D_MODEL, D_EXPERT), lambda e, grp_ref: (e, 0, 0))
    grid_spec = pltpu.PrefetchScalarGridSpec(
        num_scalar_prefetch=1,
        grid=(N_EXPERTS,),
        in_specs=[w_in, w_in, pl.BlockSpec((1, D_EXPERT, D_MODEL), lambda e, grp_ref: (e, 0, 0)),
                  pl.BlockSpec(memory_space=pl.ANY)],
        out_specs=pl.BlockSpec(memory_space=pl.ANY),
        scratch_shapes=[pltpu.VMEM((2, E_CHUNK, H2W), BF16), pltpu.VMEM((2, E_CHUNK, D_MODEL), BF16),
                        pltpu.VMEM((D_MODEL, 2 * D_EXPERT), BF16), pltpu.VMEM((D_EXPERT, D_MODEL), BF16),
                        pltpu.SMEM((4,), I32), pltpu.SemaphoreType.DMA((2,)), pltpu.SemaphoreType.DMA((2,))],
    )
    return pl.pallas_call(
        _expert_kernel,
        grid_spec=grid_spec,
        out_shape=jax.ShapeDtypeStruct((xs.shape[0], D_MODEL), BF16),
        compiler_params=pltpu.CompilerParams(dimension_semantics=("arbitrary",), vmem_limit_bytes=VMEM_LIMIT),
        name="moe_experts",
    )(grp, wg, wu, wd, xs)


def _combine_kernel(off_ref, loff_ref, msz_ref, lpos_ref, h2_ref, xp_ref, xs_ref, mod_ref, wgs_ref, wus_ref, wds_ref,
                    fnw_ref, ysrt_hbm, yp_ref, ys_ref, yloc_ref, acc_ref, sem, *, nt, n_p_tiles, tiles_per_mod,
                    final):
    i = pl.program_id(0)
    slot = i % 2

    def copy_for(tile, slot_):
        def make(e, m):
            lo = pl.multiple_of(loff_ref[e, tile], ROW_ALIGN)
            of = pl.multiple_of(off_ref[e, tile], ROW_ALIGN)
            return pltpu.make_async_copy(ysrt_hbm.at[pl.ds(of, m)], yloc_ref.at[slot_, pl.ds(lo, m)], sem.at[slot_])
        return make

    @pl.when(i == 0)
    def _():
        yloc_ref[...] = jnp.zeros(yloc_ref.shape, BF16)
        _start_copies(msz_ref, 0, copy_for(0, 0))

    nxt = jnp.minimum(i + 1, nt - 1)
    _start_copies(msz_ref, nxt, copy_for(nxt, 1 - slot))

    hb = h2_ref[...]
    hg = _dot(hb, wgs_ref[...].astype(BF16))
    hu = _dot(hb, wus_ref[...].astype(BF16))
    acc_ref[...] = _dot((hg * _sigmoid(hg) * hu).astype(BF16), wds_ref[...].astype(BF16))

    def wait_tile(tile, slot_):
        n = _tile_rows(loff_ref, msz_ref, tile)
        pltpu.make_async_copy(ysrt_hbm.at[pl.ds(0, n)], yloc_ref.at[slot_, pl.ds(0, n)], sem.at[slot_]).wait()

    wait_tile(i, slot)

    @pl.when(i == nt - 1)
    def _():
        wait_tile(i, 1 - slot)

    lpos_pad = jnp.concatenate([lpos_ref[0].astype(F32), jnp.zeros((128 - TOP_K, TB), F32)], axis=0)
    lposc = lpos_pad.T.astype(I32)
    ck = 512
    cols = [jnp.broadcast_to(lposc[:, k:k + 1], (TB, ck)).astype(jnp.int16) for k in range(TOP_K)]
    one, zero = jnp.ones((TB, ck), BF16), jnp.zeros((TB, ck), BF16)
    for c in range(R_LOC // ck):
        r = (lax.broadcasted_iota(I32, (TB, ck), 1) + c * ck).astype(jnp.int16)
        cm = zero
        for k in range(TOP_K):
            cm = jnp.where(r == cols[k], one, cm)
        acc_ref[...] += _dot(cm, yloc_ref[slot, c * ck:(c + 1) * ck, :])

    row = _mod_row(i, n_p_tiles, tiles_per_mod)
    g2 = mod_ref[pl.ds(row, 1), 5 * D_MODEL:6 * D_MODEL]
    x2 = _select_x(i, n_p_tiles, xp_ref, xs_ref) + g2 * acc_ref[...]
    y = _rms(x2) * fnw_ref[...] if final else x2

    @pl.when(i < n_p_tiles)
    def _():
        yp_ref[...] = y

    @pl.when(i >= n_p_tiles)
    def _():
        ys_ref[...] = y


def _combine(off, loff, msz, lpos, h2ext, x1p, x1s, mod, wgs, wus, wds, fnw, ysorted, *, tiles_per_mod, final):
    n_p, n_s = x1p.shape[0] // TB, x1s.shape[0] // TB
    nt = n_p + n_s
    const = lambda shape: pl.BlockSpec(shape, lambda i, *_: (0,) * len(shape), pipeline_mode=pl.Buffered(1))
    p_idx = lambda i, *_: (jnp.minimum(i, n_p - 1), 0)
    s_idx = lambda i, *_: (jnp.maximum(i - n_p, 0), 0)
    grid_spec = pltpu.PrefetchScalarGridSpec(
        num_scalar_prefetch=3,
        grid=(nt,),
        in_specs=[pl.BlockSpec((1, TOP_K, TB), lambda i, *_: (i, 0, 0)),
                  pl.BlockSpec((TB, D_MODEL), lambda i, *_: (i, 0)),
                  pl.BlockSpec((TB, D_MODEL), p_idx), pl.BlockSpec((TB, D_MODEL), s_idx),
                  const(mod.shape), const((D_MODEL, D_EXPERT)), const((D_MODEL, D_EXPERT)), const((D_EXPERT, D_MODEL)),
                  const((1, D_MODEL)), pl.BlockSpec(memory_space=pl.ANY)],
        out_specs=[pl.BlockSpec((TB, D_MODEL), p_idx), pl.BlockSpec((TB, D_MODEL), s_idx)],
        scratch_shapes=[pltpu.VMEM((2, R_LOC, D_MODEL), BF16), pltpu.VMEM((TB, D_MODEL), F32),
                        pltpu.SemaphoreType.DMA((2,))],
    )
    kern = functools.partial(_combine_kernel, nt=nt, n_p_tiles=n_p, tiles_per_mod=tiles_per_mod, final=final)
    return pl.pallas_call(
        kern,
        grid_spec=grid_spec,
        out_shape=[jax.ShapeDtypeStruct(x1p.shape, F32), jax.ShapeDtypeStruct(x1s.shape, F32)],
        compiler_params=pltpu.CompilerParams(dimension_semantics=("arbitrary",), vmem_limit_bytes=VMEM_LIMIT),
        name="moe_combine",
    )(off, loff, msz, lpos, h2ext, x1p, x1s, mod, wgs, wus, wds, fnw, ysorted)


def _moe(x1p, x1s, mod, n2w, w_router, b_router, wg, wu, wd, wgs, wus, wds, fnw, *, tokens_per_mod, final):
    assert R_LOC >= TB * TOP_K + N_EXPERTS * ROW_ALIGN and tokens_per_mod % TB == 0
    nt = (x1p.shape[0] + x1s.shape[0]) // TB
    n_rows_max = nt * TB * TOP_K + nt * N_EXPERTS * ROW_ALIGN + N_EXPERTS * (G_ALIGN - ROW_ALIGN)
    tiles_per_mod = tokens_per_mod // TB
    brb = jnp.broadcast_to(b_router.reshape(N_EXPERTS, 1), (N_EXPERTS, TB))
    h2ext, lpos, cnt = _route(x1p, x1s, mod, n2w, w_router.T, brb, tiles_per_mod=tiles_per_mod)
    off, loff, msz, grp = _plan(cnt)
    xs = _dispatch(off, loff, msz, grp, h2ext, lpos, n_rows_max)
    ysorted = _experts(grp, xs, wg, wu, wd)
    return _combine(off, loff, msz, lpos, h2ext, x1p, x1s, mod, wgs, wus, wds, fnw, ysorted,
                    tiles_per_mod=tiles_per_mod, final=final)


def kernel(x_prompt, x_sample, state_gla, c, c_ctx, w_mod, b_mod, norm1_w, w_in, conv_w, conv_b, w_decay, b_decay,
           gla_norm_w, w_br_conv, w_br_gla, w_out, norm2_w, w_router, b_router, w_gate_e, w_up_e, w_down_e,
           w_gate_s, w_up_s, w_down_s, final_norm_w):
    depth = w_mod.shape[0]
    nb_p, len_p, _ = x_prompt.shape
    nb_s, len_s, _ = x_sample.shape
    yp = x_prompt.reshape(nb_p * len_p, D_MODEL)
    ys = x_sample.reshape(nb_s * len_s, D_MODEL)
    fnw = final_norm_w.reshape(1, D_MODEL)

    cond = jnp.concatenate([c_ctx[None, :], c, jnp.zeros((8 - 1 - nb_s, D_MODEL), F32)], axis=0)
    states = []
    for l in range(depth):
        mod = _modulation(cond, w_mod[l], b_mod[l].reshape(1, -1))
        mod_p, mod_s = mod[0:1], mod[1:1 + nb_s]

        w_in_r = _w_in_prep(w_in[l].T)
        wdec = jnp.zeros((2, LR_PAD, D_GLA), F32)
        wdec = wdec.at[0, 0:GLA_RANK].set(w_decay[l, 0]).at[1, GLA_RANK:2 * GLA_RANK].set(w_decay[l, 1])
        n1w = norm1_w[l].reshape(1, D_MODEL)
        mix_w = (conv_w[l], conv_b[l].reshape(1, D_CONV), gla_norm_w[l].reshape(1, D_GLA),
                 w_br_conv[l].astype(BF16), w_br_gla[l].astype(BF16), w_out[l].astype(BF16))
        moe_w = (norm2_w[l].reshape(1, D_MODEL), w_router[l], b_router[l],
                 w_gate_e[l], w_up_e[l], w_down_e[l], w_gate_s[l], w_up_s[l], w_down_s[l])

        proj_p = _in_proj(yp, mod_p, n1w, w_in_r, rows_per_mod=nb_p * len_p)
        o_p, st = _gla_scan(proj_p, wdec, b_decay[l], None, seq_len=len_p, emit_state=True)
        yp = _mix_dense(proj_p, o_p, yp, mod_p, *mix_w, row_len=len_p, tokens_per_mod=nb_p * len_p)
        states.append(st)
        proj_s = _in_proj(ys, mod_s, n1w, w_in_r, rows_per_mod=len_s)
        (o_s,) = _gla_scan(proj_s, wdec, b_decay[l], state_gla[:, l], seq_len=len_s, emit_state=False)
        ys = _mix_dense(proj_s, o_s, ys, mod_s, *mix_w, row_len=GRID_W, tokens_per_mod=len_s)

        yp, ys = _moe(yp, ys, mod, *moe_w, fnw, tokens_per_mod=len_s, final=l == depth - 1)
    new_state = jnp.stack(states, axis=1)
    return (yp.reshape(nb_p, len_p, D_MODEL), ys.reshape(nb_s, len_s, D_MODEL), new_state)
```

```python
import functools

import jax
import jax.numpy as jnp
from jax import lax
from jax.experimental import pallas as pl
from jax.experimental.pallas import tpu as pltpu

F32 = jnp.float32
BF16 = jnp.bfloat16

D_MODEL = 1024
GRID_W = 64
D_CONV = 512
N_HEADS = 4
HEAD_D = 128
D_GLA = N_HEADS * HEAD_D
GLA_RANK = 16
GLA_GATE_NORM = 16.0
LOG2_E = 1.4426950408889634
CHUNK = 64
SUB = 8
N_SUB = CHUNK // SUB
N_EXPERTS = 64
TOP_K = 8
D_EXPERT = 256
ROUTED_SCALE = 2.5
EPS = 1e-6

C_U, C_GB, C_GC, C_Q, C_K, C_V, C_GO = 0, 512, 1024, 1536, 2048, 2560, 3072
C_BRC, C_BRG, C_LR = 3584, 4608, 5632
D_PROJ = 5760
LR_PAD = 128

VMEM_LIMIT = 56 * 1024 * 1024


def _dot(a, b):
    return jnp.dot(a, b, preferred_element_type=F32)


def _dot_nt(a, b):
    return lax.dot_general(a, b, (((1,), (1,)), ((), ())), preferred_element_type=F32)


def _dot_tn(a, b):
    return lax.dot_general(a, b, (((0,), (0,)), ((), ())), preferred_element_type=F32)


def _dot_hi(a, b):
    return jnp.dot(a, b, preferred_element_type=F32, precision=lax.Precision.HIGHEST)


def _split_bf16(x):
    hi = x.astype(BF16)
    lo = (x - hi.astype(F32)).astype(BF16)
    return hi, lo


def _rms(x):
    return x * lax.rsqrt(jnp.mean(x * x, axis=-1, keepdims=True) + EPS)


def _sigmoid(x):
    return 0.5 * jnp.tanh(0.5 * x) + 0.5


def _mod_kernel(cond_ref, w_ref, b_ref, o_ref):
    c = cond_ref[...]
    c_hi, c_lo = _split_bf16(c * jax.nn.sigmoid(c))
    w_hi, w_lo = _split_bf16(w_ref[...])
    o_ref[...] = _dot(jnp.concatenate([c_hi, c_hi, c_lo], axis=1),
                      jnp.concatenate([w_hi, w_lo, w_hi], axis=0)) + b_ref[...]


def _modulation(cond, w_mod, b_mod):
    n_rows = cond.shape[0]
    tn = 1536
    return pl.pallas_call(
        _mod_kernel,
        grid=(6 * D_MODEL // tn,),
        in_specs=[pl.BlockSpec((n_rows, D_MODEL), lambda j: (0, 0)),
                  pl.BlockSpec((D_MODEL, tn), lambda j: (0, j)),
                  pl.BlockSpec((1, tn), lambda j: (0, j))],
        out_specs=pl.BlockSpec((n_rows, tn), lambda j: (0, j)),
        out_shape=jax.ShapeDtypeStruct((n_rows, 6 * D_MODEL), F32),
        compiler_params=pltpu.CompilerParams(dimension_semantics=("arbitrary",),
                                             vmem_limit_bytes=VMEM_LIMIT),
        name="modulation",
    )(cond, w_mod, b_mod)


D_IN_PROJ = 5664
C_LR_SRC, C_GATES_SRC = 3584, 3616


def _w_in_prep_kernel(w_ref, o_ref):
    rc = 64

    def copy_rows(dst0, src0, n):
        def body(t, carry):
            off = pl.multiple_of(t * rc, rc)
            o_ref[pl.ds(dst0 + off, rc), :] = w_ref[pl.ds(src0 + off, rc), :].astype(BF16)
            return carry
        lax.fori_loop(0, n // rc, body, 0)

    copy_rows(0, 0, C_LR_SRC)
    copy_rows(C_BRC, C_GATES_SRC, 2 * D_MODEL)
    o_ref[C_LR:C_LR + 2 * GLA_RANK, :] = w_ref[C_LR_SRC:C_GATES_SRC, :].astype(BF16)
    o_ref[C_LR + 2 * GLA_RANK:D_PROJ, :] = jnp.zeros((D_PROJ - C_LR - 2 * GLA_RANK, D_MODEL), BF16)


def _w_in_prep(w_in_t):
    return pl.pallas_call(
        _w_in_prep_kernel,
        out_shape=jax.ShapeDtypeStruct((D_PROJ, D_MODEL), BF16),
        compiler_params=pltpu.CompilerParams(vmem_limit_bytes=VMEM_LIMIT),
        name="w_in_prep",
    )(w_in_t)


def _inproj_kernel(x_ref, mod_ref, nw_ref, w_ref, o_ref, *, rows_per_mod, tm):
    i = pl.program_id(1)
    row = (i * tm) // rows_per_mod
    sh = mod_ref[pl.ds(row, 1), 0:D_MODEL]
    sc = mod_ref[pl.ds(row, 1), D_MODEL:2 * D_MODEL]
    h = _rms(x_ref[...]) * nw_ref[...] * (1.0 + sc) + sh
    o_ref[...] = _dot_nt(h.astype(BF16), w_ref[...]).astype(BF16)


def _in_proj(x2d, mod, norm_w, w_in_r, rows_per_mod):
    t = x2d.shape[0]
    tm, tn = 1024, 1920
    kern = functools.partial(_inproj_kernel, rows_per_mod=rows_per_mod, tm=tm)
    return pl.pallas_call(
        kern,
        grid=(D_PROJ // tn, t // tm),
        in_specs=[pl.BlockSpec((tm, D_MODEL), lambda j, i: (i, 0)),
                  pl.BlockSpec(mod.shape, lambda j, i: (0, 0)),
                  pl.BlockSpec((1, D_MODEL), lambda j, i: (0, 0)),
                  pl.BlockSpec((tn, D_MODEL), lambda j, i: (j, 0))],
        out_specs=pl.BlockSpec((tm, tn), lambda j, i: (i, j)),
        out_shape=jax.ShapeDtypeStruct((t, D_PROJ), BF16),
        compiler_params=pltpu.CompilerParams(dimension_semantics=("arbitrary", "arbitrary"),
                                             vmem_limit_bytes=VMEM_LIMIT),
        name="in_proj",
    )(x2d, mod, norm_w, w_in_r)


def _log2_sigmoid(x):
    return jnp.minimum(x, 0.0) * LOG2_E - jnp.log2(1.0 + jnp.exp2(jnp.abs(x) * (-LOG2_E)))


def _gla_chunk_head(qc, kc, vc, bc, st, rev):
    lane = lax.broadcasted_iota(jnp.int32, (SUB, CHUNK), 1)
    sub = lax.broadcasted_iota(jnp.int32, (SUB, CHUNK), 0)
    tot = bc[0:1] if rev else bc[CHUNK - 1:CHUNK]

    q_in = (qc * jnp.exp2(bc)).astype(BF16)
    k_tail = kc * jnp.exp2(tot - bc)
    vt = vc.T.astype(BF16)
    st_new = st * jnp.exp2(tot) + _dot(vt, k_tail.astype(BF16))

    lhs_segs, rhs_segs = [], []

    def rows(before, mid, after):
        parts = ([jnp.zeros((before, HEAD_D), F32)] if before else []) + [mid]
        parts += [jnp.zeros((after, HEAD_D), F32)] if after else []
        return jnp.concatenate(parts, axis=0) if len(parts) > 1 else mid

    key_blocks = range(1, N_SUB) if rev else range(0, N_SUB - 1)
    for jb in key_blocks:
        r0 = jb * SUB
        ref_row = bc[r0:r0 + 1] if rev else bc[r0 + SUB - 1:r0 + SUB]
        ke = kc[r0:r0 + SUB] * jnp.exp2(ref_row - bc[r0:r0 + SUB])
        rhs_segs.append(rows(r0, ke, CHUNK - r0 - SUB))
        if rev:
            ql = qc[:r0] * jnp.exp2(bc[:r0] - ref_row)
            lhs_segs.append(rows(0, ql, CHUNK - r0))
        else:
            ql = qc[r0 + SUB:] * jnp.exp2(bc[r0 + SUB:] - ref_row)
            lhs_segs.append(rows(r0 + SUB, ql, 0))
    far = _dot_nt(jnp.concatenate(lhs_segs, axis=1).astype(BF16),
                  jnp.concatenate(rhs_segs, axis=1).astype(BF16))

    blocks = []
    for ib in range(N_SUB):
        r0 = ib * SUB
        qi, bi = qc[r0:r0 + SUB], bc[r0:r0 + SUB]
        acc = jnp.zeros((SUB, CHUNK), F32)
        for jj in range(SUB):
            j = r0 + jj
            e = jnp.exp2(bi - bc[j:j + 1])
            col = jnp.sum(qi * (kc[j:j + 1] * e), axis=-1, keepdims=True)
            acc = jnp.where(lane == j, col, acc)
        keep = (lane - r0 >= sub) if rev else (lane - r0 <= sub)
        blocks.append(jnp.where(keep, acc, 0.0))
    scores = far + jnp.concatenate(blocks, axis=0)
    o = _dot_nt(jnp.concatenate([q_in, scores.astype(BF16)], axis=1),
                jnp.concatenate([st.astype(BF16), vt], axis=1))
    return o, st_new


def _scan_kernel(*refs, seq_len, has_s0, emit_state):
    it = iter(refs)
    qkv_ref, lr_ref, wdec_ref, bdec_ref = (next(it) for _ in range(4))
    s0_ref = next(it) if has_s0 else None
    out_ref = next(it)
    st_out_ref = next(it) if emit_state else None
    la_f_ref, la_b_ref, o_ref, st_ref = (next(it) for _ in range(4))

    L = seq_len
    n_chunks = L // CHUNK
    tr = 256
    assert L % tr == 0

    ri = lax.broadcasted_iota(jnp.int32, (tr, tr), 0)
    ci = lax.broadcasted_iota(jnp.int32, (tr, tr), 1)
    same_chunk = (ri // CHUNK) == (ci // CHUNK)

    def decay_body(t, carry):
        r0 = pl.multiple_of(t * tr, tr)
        lr = lr_ref[pl.ds(r0, tr), :]
        lr2 = jnp.concatenate([lr, lr], axis=1)
        for d, ref in ((0, la_f_ref), (1, la_b_ref)):
            z = _dot(lr2, jnp.concatenate(_split_bf16(wdec_ref[d]), axis=0)) + bdec_ref[d:d + 1]
            la = jnp.concatenate(_split_bf16(_log2_sigmoid(z) * (1.0 / GLA_GATE_NORM)), axis=0)
            tri = jnp.where(same_chunk & ((ci >= ri) if d else (ci <= ri)), 1.0, 0.0).astype(BF16)
            ref[pl.ds(r0, tr), :] = _dot(jnp.concatenate([tri, tri], axis=1), la)
        return carry
    lax.fori_loop(0, L // tr, decay_body, 0)

    for rev, la_ref in ((False, la_f_ref), (True, la_b_ref)):
        d = 1 if rev else 0
        for h in range(N_HEADS):
            if has_s0:
                st_ref[h] = s0_ref[0, d, h].T
            else:
                st_ref[h] = jnp.zeros((HEAD_D, HEAD_D), F32)

        def chunk_body(c, carry, rev=rev, la_ref=la_ref, heads=range(N_HEADS)):
            cc = (n_chunks - 1 - c) if rev else c
            r0 = pl.multiple_of(cc * CHUNK, CHUNK)
            for h in heads:
                lo, hi = h * HEAD_D, (h + 1) * HEAD_D
                qc = qkv_ref[pl.ds(r0, CHUNK), lo:hi].astype(F32) * (HEAD_D ** -0.5)
                kc = qkv_ref[pl.ds(r0, CHUNK), D_GLA + lo:D_GLA + hi].astype(F32)
                vc = qkv_ref[pl.ds(r0, CHUNK), 2 * D_GLA + lo:2 * D_GLA + hi].astype(F32)
                oc, st_new = _gla_chunk_head(qc, kc, vc, la_ref[pl.ds(r0, CHUNK), lo:hi], st_ref[h], rev)
                st_ref[h] = st_new
                if rev:
                    o_ref[pl.ds(r0, CHUNK), lo:hi] += oc
                else:
                    o_ref[pl.ds(r0, CHUNK), lo:hi] = oc
            return carry
        lax.fori_loop(0, n_chunks, chunk_body, 0, unroll=4)

        if emit_state:
            for h in range(N_HEADS):
                st_out_ref[0, d, h] = st_ref[h].T

    out_ref[...] = o_ref[...].astype(BF16)


def _gla_scan(proj, wdec, bdec, s0, *, seq_len, emit_state):
    t = proj.shape[0]
    nb = t // seq_len
    has_s0 = s0 is not None
    const = lambda shape: pl.BlockSpec(shape, lambda b: (0,) * len(shape), pipeline_mode=pl.Buffered(1))
    assert C_Q % (3 * D_GLA) == 0 and C_K == C_Q + D_GLA and C_V == C_K + D_GLA and C_LR % LR_PAD == 0
    in_specs = [pl.BlockSpec((seq_len, 3 * D_GLA), lambda b: (b, C_Q // (3 * D_GLA))),
                pl.BlockSpec((seq_len, LR_PAD), lambda b: (b, C_LR // LR_PAD)),
                const((2, LR_PAD, D_GLA)), const((2, D_GLA))]
    args = [proj, proj, wdec, bdec]
    if has_s0:
        in_specs.append(pl.BlockSpec((1, 2, N_HEADS, HEAD_D, HEAD_D), lambda b: (b, 0, 0, 0, 0)))
        args.append(s0)
    out_specs = [pl.BlockSpec((seq_len, D_GLA), lambda b: (b, 0))]
    out_shape = [jax.ShapeDtypeStruct((t, D_GLA), BF16)]
    if emit_state:
        out_specs.append(pl.BlockSpec((1, 2, N_HEADS, HEAD_D, HEAD_D), lambda b: (b, 0, 0, 0, 0)))
        out_shape.append(jax.ShapeDtypeStruct((nb, 2, N_HEADS, HEAD_D, HEAD_D), F32))
    kern = functools.partial(_scan_kernel, seq_len=seq_len, has_s0=has_s0, emit_state=emit_state)
    return pl.pallas_call(
        kern,
        grid=(nb,),
        in_specs=in_specs,
        out_specs=out_specs,
        out_shape=out_shape,
        scratch_shapes=[pltpu.VMEM((seq_len, D_GLA), F32), pltpu.VMEM((seq_len, D_GLA), F32),
                        pltpu.VMEM((seq_len, D_GLA), F32), pltpu.VMEM((N_HEADS, HEAD_D, HEAD_D), F32)],
        compiler_params=pltpu.CompilerParams(dimension_semantics=("arbitrary",),
                                             vmem_limit_bytes=VMEM_LIMIT),
        name="gla_scan",
    )(*args)


MIX_TM = 1024
MIX_SUB = 256


def _mix_dense_kernel(conv_ref, go_ref, gc0_ref, gc1_ref, gg0_ref, gg1_ref, o_ref, x_ref, mod_ref, cw_ref, cb_ref,
                      gnw_ref, wbc_ref, wbg_ref, wout_ref, out_ref, yc_ref, yg_ref, *, row_len, tiles_per_mod):
    i = pl.program_id(0)
    mod_row = i // tiles_per_mod if mod_ref.shape[0] > 1 else 0
    g1 = mod_ref[pl.ds(mod_row, 1), 2 * D_MODEL:3 * D_MODEL]
    hsel_r = lax.broadcasted_iota(jnp.int32, (D_GLA, D_GLA), 0) // HEAD_D
    hsel_c = lax.broadcasted_iota(jnp.int32, (D_GLA, D_GLA), 1) // HEAD_D
    head_avg = jnp.where(hsel_r == hsel_c, 1.0 / HEAD_D, 0.0).astype(BF16)
    head_avg2 = jnp.concatenate([head_avg, head_avg], axis=0)
    pos = lax.broadcasted_iota(jnp.int32, (MIX_SUB, 1), 0)
    in_row = pos % row_len

    def branch_body(t, carry):
        rows = pl.ds(pl.multiple_of(t * MIX_SUB, MIX_SUB), MIX_SUB)
        cu = conv_ref[rows, 2 * D_CONV:3 * D_CONV].astype(F32) * conv_ref[rows, 0:D_CONV].astype(F32)
        left = jnp.where(in_row == 0, 0.0, pltpu.roll(cu, 1, axis=0))
        right = jnp.where(in_row == row_len - 1, 0.0, pltpu.roll(cu, MIX_SUB - 1, axis=0))
        conv = cw_ref[0:1] * left + cw_ref[1:2] * cu + cw_ref[2:3] * right + cb_ref[...]
        yc_ref[rows, :] = (conv_ref[rows, D_CONV:2 * D_CONV].astype(F32) * conv).astype(BF16)
        o = o_ref[rows, :].astype(F32)
        ms = _dot(jnp.concatenate(_split_bf16(o * o), axis=1), head_avg2)
        g_out = go_ref[rows, :].astype(F32)
        yg_ref[rows, :] = (o * lax.rsqrt(ms + EPS) * gnw_ref[...] * (g_out * _sigmoid(g_out))).astype(BF16)
        return carry
    lax.fori_loop(0, MIX_TM // MIX_SUB, branch_body, 0)

    half = D_MODEL // 2
    pc = _dot(yc_ref[...], wbc_ref[...])
    pg = _dot(yg_ref[...], wbg_ref[...])
    merged = jnp.concatenate(
        [_sigmoid(gc0_ref[...].astype(F32)) * pc[:, :half] + _sigmoid(gg0_ref[...].astype(F32)) * pg[:, :half],
         _sigmoid(gc1_ref[...].astype(F32)) * pc[:, half:] + _sigmoid(gg1_ref[...].astype(F32)) * pg[:, half:]],
        axis=1)
    out_ref[...] = x_ref[...] + g1 * _dot(merged.astype(BF16), wout_ref[...])


def _mix_dense(proj, o, x2d, mod, conv_w, conv_b, gnw, wbc, wbg, wout, *, row_len, tokens_per_mod):
    t = x2d.shape[0]
    assert MIX_SUB % row_len == 0 and t % MIX_TM == 0 and tokens_per_mod % MIX_TM == 0
    const = lambda shape: pl.BlockSpec(shape, lambda i: (0,) * len(shape), pipeline_mode=pl.Buffered(1))
    cols = lambda width, start: pl.BlockSpec((MIX_TM, width), lambda i: (i, start // width))
    half = D_MODEL // 2
    assert C_U == 0 and C_GB == D_CONV and C_GC == 2 * D_CONV and C_GO % D_GLA == 0 and C_BRC % half == 0
    kern = functools.partial(_mix_dense_kernel, row_len=row_len, tiles_per_mod=tokens_per_mod // MIX_TM)
    return pl.pallas_call(
        kern,
        grid=(t // MIX_TM,),
        in_specs=[cols(3 * D_CONV, C_U), cols(D_GLA, C_GO),
                  cols(half, C_BRC), cols(half, C_BRC + half), cols(half, C_BRG), cols(half, C_BRG + half),
                  pl.BlockSpec((MIX_TM, D_GLA), lambda i: (i, 0)), pl.BlockSpec((MIX_TM, D_MODEL), lambda i: (i, 0)),
                  const(mod.shape), const((3, D_CONV)), const((1, D_CONV)), const((1, D_GLA)),
                  const((D_CONV, D_MODEL)), const((D_GLA, D_MODEL)), const((D_MODEL, D_MODEL))],
        out_specs=pl.BlockSpec((MIX_TM, D_MODEL), lambda i: (i, 0)),
        out_shape=jax.ShapeDtypeStruct((t, D_MODEL), F32),
        scratch_shapes=[pltpu.VMEM((MIX_TM, D_CONV), BF16), pltpu.VMEM((MIX_TM, D_GLA), BF16)],
        compiler_params=pltpu.CompilerParams(dimension_semantics=("arbitrary",), vmem_limit_bytes=VMEM_LIMIT),
        name="mix_dense",
    )(proj, proj, proj, proj, proj, proj, o, x2d, mod, conv_w, conv_b, gnw, wbc, wbg, wout)


I32 = jnp.int32
TB = 256
ROW_ALIGN = 16
G_ALIGN = 256
E_CHUNK = 2048
R_LOC = 3072
H2W = 1152


def _select_x(i, n_p_tiles, xp_ref, xs_ref):
    return jnp.where(i < n_p_tiles, xp_ref[...], xs_ref[...])


def _mod_row(i, n_p_tiles, tiles_per_mod):
    return jnp.where(i < n_p_tiles, 0, 1 + (i - n_p_tiles) // tiles_per_mod)


def _route_kernel(xp_ref, xs_ref, mod_ref, n2w_ref, wrt_ref, brb_ref, h2_ref, lpos_ref, cnt_ref, *,
                  n_p_tiles, tiles_per_mod):
    i = pl.program_id(0)
    row = _mod_row(i, n_p_tiles, tiles_per_mod)
    x = _select_x(i, n_p_tiles, xp_ref, xs_ref)
    sh = mod_ref[pl.ds(row, 1), 3 * D_MODEL:4 * D_MODEL]
    sc = mod_ref[pl.ds(row, 1), 4 * D_MODEL:5 * D_MODEL]
    h2 = _rms(x) * n2w_ref[...] * (1.0 + sc) + sh

    h_hi, h_lo = _split_bf16(h2)
    w_hi, w_lo = _split_bf16(wrt_ref[...])
    scores = jax.nn.sigmoid(_dot_nt(jnp.concatenate([w_hi, w_hi, w_lo], axis=1),
                                    jnp.concatenate([h_hi, h_lo, h_hi], axis=1)))
    biased = scores + brb_ref[...]
    eidx = lax.broadcasted_iota(I32, scores.shape, 0)
    picks = []
    for _k in range(TOP_K):
        m = jnp.max(biased, axis=0, keepdims=True)
        first = jnp.min(jnp.where(biased == m, eidx, N_EXPERTS), axis=0, keepdims=True)
        pick = eidx == first
        picks.append(pick)
        biased = jnp.where(pick, -jnp.inf, biased)
    sel = jnp.zeros(scores.shape, F32)
    for pick in picks:
        sel = jnp.where(pick, 1.0, sel)
    selsc = sel * scores
    comb = selsc / jnp.sum(selsc, axis=0, keepdims=True) * ROUTED_SCALE

    selb = sel.astype(BF16)
    tr_ = lax.broadcasted_iota(I32, (TB, TB), 0)
    tc_ = lax.broadcasted_iota(I32, (TB, TB), 1)
    rank_n = _dot(selb, jnp.concatenate([jnp.where(tr_ < tc_, 1.0, 0.0).astype(BF16), jnp.ones((TB, 128), BF16)],
                                        axis=1))
    rank, n_b = rank_n[:, :TB], rank_n[:, TB:]
    m_b = jnp.maximum(jnp.floor((n_b + (ROW_ALIGN - 1)) * (1.0 / ROW_ALIGN)), 1.0) * ROW_ALIGN
    er_ = lax.broadcasted_iota(I32, (N_EXPERTS, N_EXPERTS), 0)
    ec_ = lax.broadcasted_iota(I32, (N_EXPERTS, N_EXPERTS), 1)
    loff_b = _dot(jnp.where(ec_ < er_, 1.0, 0.0).astype(BF16), m_b.astype(BF16))
    lposf = jnp.concatenate([loff_b] * (TB // 128), axis=1) + rank
    rows = [jnp.sum(jnp.where(pick, lposf, 0.0), axis=0, keepdims=True) for pick in picks]
    lpos_ref[0] = jnp.concatenate(rows, axis=0).astype(I32)
    cnt_ref[0] = m_b

    combt = comb.T
    chi = combt.astype(BF16).astype(F32)
    h2_ref[:, 0:D_MODEL] = h_hi
    h2_ref[:, D_MODEL:H2W] = jnp.concatenate([chi, combt - chi], axis=1).astype(BF16)


def _route(x1p, x1s, mod, n2w, w_router_t, b_router_b, *, tiles_per_mod):
    n_p, n_s = x1p.shape[0] // TB, x1s.shape[0] // TB
    nt = n_p + n_s
    kern = functools.partial(_route_kernel, n_p_tiles=n_p, tiles_per_mod=tiles_per_mod)
    const = lambda shape: pl.BlockSpec(shape, lambda i: (0,) * len(shape))
    return pl.pallas_call(
        kern,
        grid=(nt,),
        in_specs=[pl.BlockSpec((TB, D_MODEL), lambda i: (jnp.minimum(i, n_p - 1), 0)),
                  pl.BlockSpec((TB, D_MODEL), lambda i: (jnp.maximum(i - n_p, 0), 0)),
                  const(mod.shape), const((1, D_MODEL)), const((N_EXPERTS, D_MODEL)), const((N_EXPERTS, TB))],
        out_specs=[pl.BlockSpec((TB, H2W), lambda i: (i, 0)),
                   pl.BlockSpec((1, TOP_K, TB), lambda i: (i, 0, 0)),
                   pl.BlockSpec((1, N_EXPERTS, 128), lambda i: (i, 0, 0))],
        out_shape=[jax.ShapeDtypeStruct((nt * TB, H2W), BF16),
                   jax.ShapeDtypeStruct((nt, TOP_K, TB), I32),
                   jax.ShapeDtypeStruct((nt, N_EXPERTS, 128), F32)],
        compiler_params=pltpu.CompilerParams(dimension_semantics=("arbitrary",), vmem_limit_bytes=VMEM_LIMIT),
        name="moe_route",
    )(x1p, x1s, mod, n2w, w_router_t, b_router_b)


def _plan_kernel(cnt_ref, off_ref, loff_ref, msz_ref, grp_ref, *, nt):
    lane = lax.broadcasted_iota(I32, (N_EXPERTS, 128), 1)
    m = jnp.zeros((N_EXPERTS, 128), F32)
    for i in range(nt):
        m = jnp.where(lane == i, cnt_ref[i], m)
    total = jnp.broadcast_to(jnp.sum(m, axis=1, keepdims=True), (N_EXPERTS, 128))
    gsz = jnp.floor((total + (G_ALIGN - 1)) * (1.0 / G_ALIGN)) * G_ALIGN
    er_ = lax.broadcasted_iota(I32, (N_EXPERTS, N_EXPERTS), 0)
    ec_ = lax.broadcasted_iota(I32, (N_EXPERTS, N_EXPERTS), 1)
    lstrict = jnp.where(ec_ < er_, 1.0, 0.0)
    ir_ = lax.broadcasted_iota(I32, (128, 128), 0)
    ic_ = lax.broadcasted_iota(I32, (128, 128), 1)
    ustrict = jnp.where(ir_ < ic_, 1.0, 0.0)
    gstart = _dot_hi(lstrict, gsz)
    off_ref[...] = (gstart + _dot_hi(m, ustrict)).astype(I32)
    loff_ref[...] = _dot_hi(lstrict, m).astype(I32)
    msz_ref[...] = m.astype(I32)
    grp = jnp.where(lane == 0, gstart + total, jnp.where(lane == 1, gsz - total, jnp.where(lane == 2, gstart, gsz)))
    grp_ref[...] = grp.astype(I32)


def _plan(cnt):
    nt = cnt.shape[0]
    assert nt <= 128
    tab = jax.ShapeDtypeStruct((N_EXPERTS, 128), I32)
    return pl.pallas_call(
        functools.partial(_plan_kernel, nt=nt),
        out_shape=[tab, tab, tab, tab],
        compiler_params=pltpu.CompilerParams(vmem_limit_bytes=VMEM_LIMIT),
        name="moe_plan",
    )(cnt)


def _start_copies(msz_ref, tile, make_copy):
    for e in range(N_EXPERTS):
        make_copy(e, pl.multiple_of(msz_ref[e, tile], ROW_ALIGN)).start()


def _tile_rows(loff_ref, msz_ref, tile):
    return pl.multiple_of(loff_ref[N_EXPERTS - 1, tile] + msz_ref[N_EXPERTS - 1, tile], ROW_ALIGN)


def _dispatch_kernel(off_ref, loff_ref, msz_ref, tail_ref, h2_ref, lpos_ref, xs_hbm, xloc_ref, zero_ref, sem,
                     tail_sem, *, nt):
    i = pl.program_id(0)
    slot = i % 2

    def copy_for(tile, slot_):
        def make(e, m):
            lo = pl.multiple_of(loff_ref[e, tile], ROW_ALIGN)
            of = pl.multiple_of(off_ref[e, tile], ROW_ALIGN)
            return pltpu.make_async_copy(xloc_ref.at[slot_, pl.ds(lo, m)], xs_hbm.at[pl.ds(of, m)], sem.at[slot_])
        return make

    def wait_tile(tile, slot_):
        n = _tile_rows(loff_ref, msz_ref, tile)
        pltpu.make_async_copy(xloc_ref.at[slot_, pl.ds(0, n)], xs_hbm.at[pl.ds(0, n)], sem.at[slot_]).wait()

    lpos = lpos_ref[0].astype(jnp.int16)
    h2 = h2_ref[...]
    ck = 1024
    one, zero = jnp.ones((ck, TB), BF16), jnp.zeros((ck, TB), BF16)
    for c in range(R_LOC // ck):
        r = (lax.broadcasted_iota(I32, (ck, TB), 0) + c * ck).astype(jnp.int16)
        d = zero
        for k in range(TOP_K):
            d = jnp.where(r == lpos[k:k + 1, :], one, d)
        res = _dot(d, h2)
        xloc_ref[slot, c * ck:(c + 1) * ck, :] = res.astype(BF16)

    _start_copies(msz_ref, i, copy_for(i, slot))

    @pl.when(i > 0)
    def _():
        wait_tile(i - 1, 1 - slot)

    @pl.when(i == nt - 1)
    def _():
        zero_ref[...] = jnp.zeros(zero_ref.shape, BF16)

        def tail_copies(start):
            def body(e, carry):
                n = tail_ref[e, 1]

                @pl.when(n > 0)
                def _():
                    st = pl.multiple_of(tail_ref[e, 0], ROW_ALIGN)
                    nn = pl.multiple_of(n, ROW_ALIGN)
                    cp = pltpu.make_async_copy(zero_ref.at[pl.ds(0, nn)], xs_hbm.at[pl.ds(st, nn)], tail_sem)
                    if start:
                        cp.start()
                    else:
                        cp.wait()
                return carry
            lax.fori_loop(0, N_EXPERTS, body, 0)
        tail_copies(True)
        wait_tile(i, slot)
        tail_copies(False)


def _dispatch(off, loff, msz, tail, h2ext, lpos, n_rows):
    nt = lpos.shape[0]
    grid_spec = pltpu.PrefetchScalarGridSpec(
        num_scalar_prefetch=4,
        grid=(nt,),
        in_specs=[pl.BlockSpec((TB, H2W), lambda i, *_: (i, 0)),
                  pl.BlockSpec((1, TOP_K, TB), lambda i, *_: (i, 0, 0))],
        out_specs=pl.BlockSpec(memory_space=pl.ANY),
        scratch_shapes=[pltpu.VMEM((2, R_LOC, H2W), BF16), pltpu.VMEM((G_ALIGN, H2W), BF16),
                        pltpu.SemaphoreType.DMA((2,)), pltpu.SemaphoreType.DMA],
    )
    return pl.pallas_call(
        functools.partial(_dispatch_kernel, nt=nt),
        grid_spec=grid_spec,
        out_shape=jax.ShapeDtypeStruct((n_rows, H2W), BF16),
        compiler_params=pltpu.CompilerParams(dimension_semantics=("arbitrary",), vmem_limit_bytes=VMEM_LIMIT),
        name="moe_dispatch",
    )(off, loff, msz, tail, h2ext, lpos)


def _expert_kernel(grp_ref, wg_ref, wu_ref, wd_ref, xs_hbm, ys_hbm, xbuf, ybuf, wgu_ref, wdb_ref, st_ref,
                   in_sem, out_sem):
    e = pl.program_id(0)
    n_exp = pl.num_programs(0)

    def in_copy(row0, n, slot):
        return pltpu.make_async_copy(xs_hbm.at[pl.ds(row0, n)], xbuf.at[slot, pl.ds(0, n)], in_sem.at[slot])

    def out_copy(row0, n, slot):
        return pltpu.make_async_copy(ybuf.at[slot, pl.ds(0, n)], ys_hbm.at[pl.ds(row0, n)], out_sem.at[slot])

    def rows_of(ex):
        return grp_ref[jnp.minimum(ex, n_exp - 1), 3]

    def next_nonempty(ex):
        return lax.while_loop(lambda c: (c < n_exp) & (rows_of(c) == 0), lambda c: c + 1, ex + 1)

    def start_first_chunk(ex, slot):
        @pl.when(ex < n_exp)
        def _():
            exc = jnp.minimum(ex, n_exp - 1)
            n = pl.multiple_of(jnp.minimum(grp_ref[exc, 3], E_CHUNK), G_ALIGN)
            in_copy(pl.multiple_of(grp_ref[exc, 2], G_ALIGN), n, slot).start()

    def drain_out(slot):
        pend = st_ref[1 + slot]

        @pl.when(pend > 0)
        def _():
            out_copy(0, pl.multiple_of(pend, G_ALIGN), slot).wait()
            st_ref[1 + slot] = 0

    @pl.when(e == 0)
    def _():
        st_ref[0] = 0
        st_ref[1] = 0
        st_ref[2] = 0
        start_first_chunk(next_nonempty(-1), 0)

    g0 = grp_ref[e, 2]
    gn = grp_ref[e, 3]

    @pl.when(gn > 0)
    def _():
        wgu_ref[:, :D_EXPERT] = wg_ref[0].astype(BF16)
        wgu_ref[:, D_EXPERT:] = wu_ref[0].astype(BF16)
        wdb_ref[...] = wd_ref[0].astype(BF16)
        n_chunks = (gn + (E_CHUNK - 1)) // E_CHUNK

        def compute(n, slot):
            x = xbuf[slot, 0:n, 0:D_MODEL]
            ext = xbuf[slot, 0:n, D_MODEL:H2W].astype(F32)
            wts = ext[:, :N_EXPERTS] + ext[:, N_EXPERTS:]
            lane = lax.broadcasted_iota(I32, wts.shape, 1)
            w = jnp.sum(jnp.where(lane == e, wts, 0.0), axis=-1, keepdims=True)
            h = _dot(x, wgu_ref[...])
            hg, hu = h[:, :D_EXPERT], h[:, D_EXPERT:]
            act = hg * _sigmoid(hg) * hu * w
            ybuf[slot, 0:n, :] = _dot(act.astype(BF16), wdb_ref[...]).astype(BF16)

        def chunk_body(c, slot):
            row0 = pl.multiple_of(g0 + c * E_CHUNK, G_ALIGN)
            n = pl.multiple_of(jnp.minimum(gn - c * E_CHUNK, E_CHUNK), G_ALIGN)
            in_copy(row0, n, slot).wait()

            @pl.when(c + 1 < n_chunks)
            def _():
                n1 = pl.multiple_of(jnp.minimum(gn - (c + 1) * E_CHUNK, E_CHUNK), G_ALIGN)
                in_copy(pl.multiple_of(row0 + E_CHUNK, G_ALIGN), n1, 1 - slot).start()

            @pl.when(c + 1 == n_chunks)
            def _():
                start_first_chunk(next_nonempty(e), 1 - slot)

            drain_out(slot)
            for v in range(G_ALIGN, E_CHUNK + 1, G_ALIGN):
                @pl.when(n == v)
                def _(v=v):
                    compute(v, slot)
            out_copy(row0, n, slot).start()
            st_ref[1 + slot] = n
            return 1 - slot

        st_ref[0] = lax.fori_loop(0, n_chunks, chunk_body, st_ref[0])

    @pl.when(e == n_exp - 1)
    def _():
        drain_out(0)
        drain_out(1)


def _experts(grp, xs, wg, wu, wd):
    w_in = pl.BlockSpec((1, D_MODEL, D_EXPERT), lambda e, grp_ref: (e, 0, 0))
    grid_spec = pltpu.PrefetchScalarGridSpec(
        num_scalar_prefetch=1,
        grid=(N_EXPERTS,),
        in_specs=[w_in, w_in, pl.BlockSpec((1, D_EXPERT, D_MODEL), lambda e, grp_ref: (e, 0, 0)),
                  pl.BlockSpec(memory_space=pl.ANY)],
        out_specs=pl.BlockSpec(memory_space=pl.ANY),
        scratch_shapes=[pltpu.VMEM((2, E_CHUNK, H2W), BF16), pltpu.VMEM((2, E_CHUNK, D_MODEL), BF16),
                        pltpu.VMEM((D_MODEL, 2 * D_EXPERT), BF16), pltpu.VMEM((D_EXPERT, D_MODEL), BF16),
                        pltpu.SMEM((4,), I32), pltpu.SemaphoreType.DMA((2,)), pltpu.SemaphoreType.DMA((2,))],
    )
    return pl.pallas_call(
        _expert_kernel,
        grid_spec=grid_spec,
        out_shape=jax.ShapeDtypeStruct((xs.shape[0], D_MODEL), BF16),
        compiler_params=pltpu.CompilerParams(dimension_semantics=("arbitrary",), vmem_limit_bytes=VMEM_LIMIT),
        name="moe_experts",
    )(grp, wg, wu, wd, xs)


def _combine_kernel(off_ref, loff_ref, msz_ref, lpos_ref, h2_ref, xp_ref, xs_ref, mod_ref, wgs_ref, wus_ref, wds_ref,
                    fnw_ref, ysrt_hbm, yp_ref, ys_ref, yloc_ref, acc_ref, sem, *, nt, n_p_tiles, tiles_per_mod,
                    final):
    i = pl.program_id(0)
    slot = i % 2

    def copy_for(tile, slot_):
        def make(e, m):
            lo = pl.multiple_of(loff_ref[e, tile], ROW_ALIGN)
            of = pl.multiple_of(off_ref[e, tile], ROW_ALIGN)
            return pltpu.make_async_copy(ysrt_hbm.at[pl.ds(of, m)], yloc_ref.at[slot_, pl.ds(lo, m)], sem.at[slot_])
        return make

    @pl.when(i == 0)
    def _():
        yloc_ref[...] = jnp.zeros(yloc_ref.shape, BF16)
        _start_copies(msz_ref, 0, copy_for(0, 0))

    nxt = jnp.minimum(i + 1, nt - 1)
    _start_copies(msz_ref, nxt, copy_for(nxt, 1 - slot))

    hb = h2_ref[...]
    hg = _dot(hb, wgs_ref[...].astype(BF16))
    hu = _dot(hb, wus_ref[...].astype(BF16))
    acc_ref[...] = _dot((hg * _sigmoid(hg) * hu).astype(BF16), wds_ref[...].astype(BF16))

    def wait_tile(tile, slot_):
        n = _tile_rows(loff_ref, msz_ref, tile)
        pltpu.make_async_copy(ysrt_hbm.at[pl.ds(0, n)], yloc_ref.at[slot_, pl.ds(0, n)], sem.at[slot_]).wait()

    wait_tile(i, slot)

    @pl.when(i == nt - 1)
    def _():
        wait_tile(i, 1 - slot)

    lpos_pad = jnp.concatenate([lpos_ref[0].astype(F32), jnp.zeros((128 - TOP_K, TB), F32)], axis=0)
    lposc = lpos_pad.T.astype(I32)
    ck = 512
    cols = [jnp.broadcast_to(lposc[:, k:k + 1], (TB, ck)).astype(jnp.int16) for k in range(TOP_K)]
    one, zero = jnp.ones((TB, ck), BF16), jnp.zeros((TB, ck), BF16)
    for c in range(R_LOC // ck):
        r = (lax.broadcasted_iota(I32, (TB, ck), 1) + c * ck).astype(jnp.int16)
        cm = zero
        for k in range(TOP_K):
            cm = jnp.where(r == cols[k], one, cm)
        acc_ref[...] += _dot(cm, yloc_ref[slot, c * ck:(c + 1) * ck, :])

    row = _mod_row(i, n_p_tiles, tiles_per_mod)
    g2 = mod_ref[pl.ds(row, 1), 5 * D_MODEL:6 * D_MODEL]
    x2 = _select_x(i, n_p_tiles, xp_ref, xs_ref) + g2 * acc_ref[...]
    y = _rms(x2) * fnw_ref[...] if final else x2

    @pl.when(i < n_p_tiles)
    def _():
        yp_ref[...] = y

    @pl.when(i >= n_p_tiles)
    def _():
        ys_ref[...] = y


def _combine(off, loff, msz, lpos, h2ext, x1p, x1s, mod, wgs, wus, wds, fnw, ysorted, *, tiles_per_mod, final):
    n_p, n_s = x1p.shape[0] // TB, x1s.shape[0] // TB
    nt = n_p + n_s
    const = lambda shape: pl.BlockSpec(shape, lambda i, *_: (0,) * len(shape), pipeline_mode=pl.Buffered(1))
    p_idx = lambda i, *_: (jnp.minimum(i, n_p - 1), 0)
    s_idx = lambda i, *_: (jnp.maximum(i - n_p, 0), 0)
    grid_spec = pltpu.PrefetchScalarGridSpec(
        num_scalar_prefetch=3,
        grid=(nt,),
        in_specs=[pl.BlockSpec((1, TOP_K, TB), lambda i, *_: (i, 0, 0)),
                  pl.BlockSpec((TB, D_MODEL), lambda i, *_: (i, 0)),
                  pl.BlockSpec((TB, D_MODEL), p_idx), pl.BlockSpec((TB, D_MODEL), s_idx),
                  const(mod.shape), const((D_MODEL, D_EXPERT)), const((D_MODEL, D_EXPERT)), const((D_EXPERT, D_MODEL)),
                  const((1, D_MODEL)), pl.BlockSpec(memory_space=pl.ANY)],
        out_specs=[pl.BlockSpec((TB, D_MODEL), p_idx), pl.BlockSpec((TB, D_MODEL), s_idx)],
        scratch_shapes=[pltpu.VMEM((2, R_LOC, D_MODEL), BF16), pltpu.VMEM((TB, D_MODEL), F32),
                        pltpu.SemaphoreType.DMA((2,))],
    )
    kern = functools.partial(_combine_kernel, nt=nt, n_p_tiles=n_p, tiles_per_mod=tiles_per_mod, final=final)
    return pl.pallas_call(
        kern,
        grid_spec=grid_spec,
        out_shape=[jax.ShapeDtypeStruct(x1p.shape, F32), jax.ShapeDtypeStruct(x1s.shape, F32)],
        compiler_params=pltpu.CompilerParams(dimension_semantics=("arbitrary",), vmem_limit_bytes=VMEM_LIMIT),
        name="moe_combine",
    )(off, loff, msz, lpos, h2ext, x1p, x1s, mod, wgs, wus, wds, fnw, ysorted)


def _moe(x1p, x1s, mod, n2w, w_router, b_router, wg, wu, wd, wgs, wus, wds, fnw, *, tokens_per_mod, final):
    assert R_LOC >= TB * TOP_K + N_EXPERTS * ROW_ALIGN and tokens_per_mod % TB == 0
    nt = (x1p.shape[0] + x1s.shape[0]) // TB
    n_rows_max = nt * TB * TOP_K + nt * N_EXPERTS * ROW_ALIGN + N_EXPERTS * (G_ALIGN - ROW_ALIGN)
    tiles_per_mod = tokens_per_mod // TB
    brb = jnp.broadcast_to(b_router.reshape(N_EXPERTS, 1), (N_EXPERTS, TB))
    h2ext, lpos, cnt = _route(x1p, x1s, mod, n2w, w_router.T, brb, tiles_per_mod=tiles_per_mod)
    off, loff, msz, grp = _plan(cnt)
    xs = _dispatch(off, loff, msz, grp, h2ext, lpos, n_rows_max)
    ysorted = _experts(grp, xs, wg, wu, wd)
    return _combine(off, loff, msz, lpos, h2ext, x1p, x1s, mod, wgs, wus, wds, fnw, ysorted,
                    tiles_per_mod=tiles_per_mod, final=final)


def kernel(x_prompt, x_sample, state_gla, c, c_ctx, w_mod, b_mod, norm1_w, w_in, conv_w, conv_b, w_decay, b_decay,
           gla_norm_w, w_br_conv, w_br_gla, w_out, norm2_w, w_router, b_router, w_gate_e, w_up_e, w_down_e,
           w_gate_s, w_up_s, w_down_s, final_norm_w):
    depth = w_mod.shape[0]
    nb_p, len_p, _ = x_prompt.shape
    nb_s, len_s, _ = x_sample.shape
    yp = x_prompt.reshape(nb_p * len_p, D_MODEL)
    ys = x_sample.reshape(nb_s * len_s, D_MODEL)
    fnw = final_norm_w.reshape(1, D_MODEL)

    cond = jnp.concatenate([c_ctx[None, :], c, jnp.zeros((8 - 1 - nb_s, D_MODEL), F32)], axis=0)
    states = []
    for l in range(depth):
        mod = _modulation(cond, w_mod[l], b_mod[l].reshape(1, -1))
        mod_p, mod_s = mod[0:1], mod[1:1 + nb_s]

        w_in_r = _w_in_prep(w_in[l].T)
        wdec = jnp.zeros((2, LR_PAD, D_GLA), F32)
        wdec = wdec.at[0, 0:GLA_RANK].set(w_decay[l, 0]).at[1, GLA_RANK:2 * GLA_RANK].set(w_decay[l, 1])
        n1w = norm1_w[l].reshape(1, D_MODEL)
        mix_w = (conv_w[l], conv_b[l].reshape(1, D_CONV), gla_norm_w[l].reshape(1, D_GLA),
                 w_br_conv[l].astype(BF16), w_br_gla[l].astype(BF16), w_out[l].astype(BF16))
        moe_w = (norm2_w[l].reshape(1, D_MODEL), w_router[l], b_router[l],
                 w_gate_e[l], w_up_e[l], w_down_e[l], w_gate_s[l], w_up_s[l], w_down_s[l])

        proj_p = _in_proj(yp, mod_p, n1w, w_in_r, rows_per_mod=nb_p * len_p)
        o_p, st = _gla_scan(proj_p, wdec, b_decay[l], None, seq_len=len_p, emit_state=True)
        yp = _mix_dense(proj_p, o_p, yp, mod_p, *mix_w, row_len=len_p, tokens_per_mod=nb_p * len_p)
        states.append(st)
        proj_s = _in_proj(ys, mod_s, n1w, w_in_r, rows_per_mod=len_s)
        (o_s,) = _gla_scan(proj_s, wdec, b_decay[l], state_gla[:, l], seq_len=len_s, emit_state=False)
        ys = _mix_dense(proj_s, o_s, ys, mod_s, *mix_w, row_len=GRID_W, tokens_per_mod=len_s)

        yp, ys = _moe(yp, ys, mod, *moe_w, fnw, tokens_per_mod=len_s, final=l == depth - 1)
    new_state = jnp.stack(states, axis=1)
    return (yp.reshape(nb_p, len_p, D_MODEL), ys.reshape(nb_s, len_s, D_MODEL), new_state)
```

```python
import functools

import jax
import jax.numpy as jnp
from jax import lax
from jax.experimental import pallas as pl
from jax.experimental.pallas import tpu as pltpu

F32 = jnp.float32
BF16 = jnp.bfloat16

D_MODEL = 1024
GRID_W = 64
D_CONV = 512
N_HEADS = 4
HEAD_D = 128
D_GLA = N_HEADS * HEAD_D
GLA_RANK = 16
GLA_GATE_NORM = 16.0
LOG2_E = 1.4426950408889634
CHUNK = 64
SUB = 8
N_SUB = CHUNK // SUB
N_EXPERTS = 64
TOP_K = 8
D_EXPERT = 256
ROUTED_SCALE = 2.5
EPS = 1e-6

C_U, C_GB, C_GC, C_Q, C_K, C_V, C_GO = 0, 512, 1024, 1536, 2048, 2560, 3072
C_BRC, C_BRG, C_LR = 3584, 4608, 5632
D_PROJ = 5760
LR_PAD = 128

VMEM_LIMIT = 56 * 1024 * 1024


def _dot(a, b):
    return jnp.dot(a, b, preferred_element_type=F32)


def _dot_nt(a, b):
    return lax.dot_general(a, b, (((1,), (1,)), ((), ())), preferred_element_type=F32)


def _dot_tn(a, b):
    return lax.dot_general(a, b, (((0,), (0,)), ((), ())), preferred_element_type=F32)


def _dot_hi(a, b):
    return jnp.dot(a, b, preferred_element_type=F32, precision=lax.Precision.HIGHEST)


def _split_bf16(x):
    hi = x.astype(BF16)
    lo = (x - hi.astype(F32)).astype(BF16)
    return hi, lo


def _rms(x):
    return x * lax.rsqrt(jnp.mean(x * x, axis=-1, keepdims=True) + EPS)


def _sigmoid(x):
    return 0.5 * jnp.tanh(0.5 * x) + 0.5


def _mod_kernel(cond_ref, w_ref, b_ref, o_ref):
    c = cond_ref[...]
    c_hi, c_lo = _split_bf16(c * jax.nn.sigmoid(c))
    w_hi, w_lo = _split_bf16(w_ref[...])
    o_ref[...] = _dot(jnp.concatenate([c_hi, c_hi, c_lo], axis=1),
                      jnp.concatenate([w_hi, w_lo, w_hi], axis=0)) + b_ref[...]


def _modulation(cond, w_mod, b_mod):
    n_rows = cond.shape[0]
    tn = 1536
    return pl.pallas_call(
        _mod_kernel,
        grid=(6 * D_MODEL // tn,),
        in_specs=[pl.BlockSpec((n_rows, D_MODEL), lambda j: (0, 0)),
                  pl.BlockSpec((D_MODEL, tn), lambda j: (0, j)),
                  pl.BlockSpec((1, tn), lambda j: (0, j))],
        out_specs=pl.BlockSpec((n_rows, tn), lambda j: (0, j)),
        out_shape=jax.ShapeDtypeStruct((n_rows, 6 * D_MODEL), F32),
        compiler_params=pltpu.CompilerParams(dimension_semantics=("arbitrary",),
                                             vmem_limit_bytes=VMEM_LIMIT),
        name="modulation",
    )(cond, w_mod, b_mod)


D_IN_PROJ = 5664
C_LR_SRC, C_GATES_SRC = 3584, 3616


def _w_in_prep_kernel(w_ref, o_ref):
    rc = 64

    def copy_rows(dst0, src0, n):
        def body(t, carry):
            off = pl.multiple_of(t * rc, rc)
            o_ref[pl.ds(dst0 + off, rc), :] = w_ref[pl.ds(src0 + off, rc), :].astype(BF16)
            return carry
        lax.fori_loop(0, n // rc, body, 0)

    copy_rows(0, 0, C_LR_SRC)
    copy_rows(C_BRC, C_GATES_SRC, 2 * D_MODEL)
    o_ref[C_LR:C_LR + 2 * GLA_RANK, :] = w_ref[C_LR_SRC:C_GATES_SRC, :].astype(BF16)
    o_ref[C_LR + 2 * GLA_RANK:D_PROJ, :] = jnp.zeros((D_PROJ - C_LR - 2 * GLA_RANK, D_MODEL), BF16)


def _w_in_prep(w_in_t):
    return pl.pallas_call(
        _w_in_prep_kernel,
        out_shape=jax.ShapeDtypeStruct((D_PROJ, D_MODEL), BF16),
        compiler_params=pltpu.CompilerParams(vmem_limit_bytes=VMEM_LIMIT),
        name="w_in_prep",
    )(w_in_t)


def _inproj_kernel(x_ref, mod_ref, nw_ref, w_ref, o_ref, *, rows_per_mod, tm):
    i = pl.program_id(1)
    row = (i * tm) // rows_per_mod
    sh = mod_ref[pl.ds(row, 1), 0:D_MODEL]
    sc = mod_ref[pl.ds(row, 1), D_MODEL:2 * D_MODEL]
    h = _rms(x_ref[...]) * nw_ref[...] * (1.0 + sc) + sh
    o_ref[...] = _dot_nt(h.astype(BF16), w_ref[...]).astype(BF16)


def _in_proj(x2d, mod, norm_w, w_in_r, rows_per_mod):
    t = x2d.shape[0]
    tm, tn = 1024, 1920
    kern = functools.partial(_inproj_kernel, rows_per_mod=rows_per_mod, tm=tm)
    return pl.pallas_call(
        kern,
        grid=(D_PROJ // tn, t // tm),
        in_specs=[pl.BlockSpec((tm, D_MODEL), lambda j, i: (i, 0)),
                  pl.BlockSpec(mod.shape, lambda j, i: (0, 0)),
                  pl.BlockSpec((1, D_MODEL), lambda j, i: (0, 0)),
                  pl.BlockSpec((tn, D_MODEL), lambda j, i: (j, 0))],
        out_specs=pl.BlockSpec((tm, tn), lambda j, i: (i, j)),
        out_shape=jax.ShapeDtypeStruct((t, D_PROJ), BF16),
        compiler_params=pltpu.CompilerParams(dimension_semantics=("arbitrary", "arbitrary"),
                                             vmem_limit_bytes=VMEM_LIMIT),
        name="in_proj",
    )(x2d, mod, norm_w, w_in_r)


def _log2_sigmoid(x):
    return jnp.minimum(x, 0.0) * LOG2_E - jnp.log2(1.0 + jnp.exp2(jnp.abs(x) * (-LOG2_E)))


def _gla_head_operands(qc, kc, bc, rev, lane0):
    lane = lax.broadcasted_iota(jnp.int32, (SUB, HEAD_D), 1) - lane0
    sub = lax.broadcasted_iota(jnp.int32, (SUB, HEAD_D), 0)
    tot = bc[0:1] if rev else bc[CHUNK - 1:CHUNK]
    q_in = (qc * jnp.exp2(bc)).astype(BF16)
    k_tail = (kc * jnp.exp2(tot - bc)).astype(BF16)

    lhs_segs, rhs_segs = [], []

    def rows(before, mid, after):
        parts = ([jnp.zeros((before, HEAD_D), F32)] if before else []) + [mid]
        parts += [jnp.zeros((after, HEAD_D), F32)] if after else []
        return jnp.concatenate(parts, axis=0) if len(parts) > 1 else mid

    key_blocks = range(1, N_SUB) if rev else range(0, N_SUB - 1)
    for jb in key_blocks:
        r0 = jb * SUB
        ref_row = bc[r0:r0 + 1] if rev else bc[r0 + SUB - 1:r0 + SUB]
        ke = kc[r0:r0 + SUB] * jnp.exp2(ref_row - bc[r0:r0 + SUB])
        rhs_segs.append(rows(r0, ke, CHUNK - r0 - SUB))
        if rev:
            ql = qc[:r0] * jnp.exp2(bc[:r0] - ref_row)
            lhs_segs.append(rows(0, ql, CHUNK - r0))
        else:
            ql = qc[r0 + SUB:] * jnp.exp2(bc[r0 + SUB:] - ref_row)
            lhs_segs.append(rows(r0 + SUB, ql, 0))
    lhs = jnp.concatenate(lhs_segs, axis=1).astype(BF16)
    rhs = jnp.concatenate(rhs_segs, axis=1).astype(BF16)

    blocks = []
    for ib in range(N_SUB):
        r0 = ib * SUB
        qi, bi = qc[r0:r0 + SUB], bc[r0:r0 + SUB]
        acc = jnp.zeros((SUB, HEAD_D), F32)
        for jj in range(SUB):
            j = r0 + jj
            e = jnp.exp2(bi - bc[j:j + 1])
            col = jnp.sum(qi * (kc[j:j + 1] * e), axis=-1, keepdims=True)
            acc = jnp.where(lane == j, col, acc)
        keep = (lane - r0 >= sub) if rev else (lane - r0 <= sub)
        blocks.append(jnp.where(keep, acc, 0.0))
    return q_in, k_tail, jnp.exp2(tot), lhs, rhs, jnp.concatenate(blocks, axis=0)


def _gla_chunk_pair(ops_a, ops_b, v_a, v_b, st_a, st_b):
    qa, ka, da, lhs_a, rhs_a, near_a = ops_a
    qb, kb, db, lhs_b, rhs_b, near_b = ops_b
    zr = jnp.zeros(rhs_a.shape, BF16)
    far = _dot_nt(jnp.concatenate([lhs_a, lhs_b], axis=1),
                  jnp.concatenate([jnp.concatenate([rhs_a, zr], axis=1),
                                   jnp.concatenate([zr, rhs_b], axis=1)], axis=0))
    scores = (far + near_a + near_b).astype(BF16)
    zv = jnp.zeros(v_a.shape, F32)
    vt = jnp.concatenate([jnp.concatenate([v_a, zv], axis=1),
                          jnp.concatenate([zv, v_b], axis=1)], axis=0).T.astype(BF16)
    zs = jnp.zeros((HEAD_D, HEAD_D), BF16)
    o = _dot_nt(jnp.concatenate([qa, scores, qb], axis=1),
                jnp.concatenate([jnp.concatenate([st_a.astype(BF16), zs], axis=0), vt,
                                 jnp.concatenate([zs, st_b.astype(BF16)], axis=0)], axis=1))
    ut = _dot(vt, jnp.concatenate([ka, kb], axis=0))
    return o, (st_a * da + ut[:HEAD_D], st_b * db + ut[HEAD_D:])


def _scan_kernel(*refs, seq_len, has_s0, emit_state):
    it = iter(refs)
    qkv_ref, lr_ref, wdec_ref, bdec_ref = (next(it) for _ in range(4))
    s0_ref = next(it) if has_s0 else None
    out_ref = next(it)
    st_out_ref = next(it) if emit_state else None
    la_f_ref, la_b_ref, o_ref, st_ref = (next(it) for _ in range(4))

    L = seq_len
    n_chunks = L // CHUNK
    tr = 256
    assert L % tr == 0

    ri = lax.broadcasted_iota(jnp.int32, (tr, tr), 0)
    ci = lax.broadcasted_iota(jnp.int32, (tr, tr), 1)
    same_chunk = (ri // CHUNK) == (ci // CHUNK)

    def decay_body(t, carry):
        r0 = pl.multiple_of(t * tr, tr)
        lr = lr_ref[pl.ds(r0, tr), :]
        lr2 = jnp.concatenate([lr, lr], axis=1)
        for d, ref in ((0, la_f_ref), (1, la_b_ref)):
            z = _dot(lr2, jnp.concatenate(_split_bf16(wdec_ref[d]), axis=0)) + bdec_ref[d:d + 1]
            la = jnp.concatenate(_split_bf16(_log2_sigmoid(z) * (1.0 / GLA_GATE_NORM)), axis=0)
            tri = jnp.where(same_chunk & ((ci >= ri) if d else (ci <= ri)), 1.0, 0.0).astype(BF16)
            ref[pl.ds(r0, tr), :] = _dot(jnp.concatenate([tri, tri], axis=1), la)
        return carry
    lax.fori_loop(0, L // tr, decay_body, 0)

    for rev, la_ref in ((False, la_f_ref), (True, la_b_ref)):
        d = 1 if rev else 0
        for h in range(N_HEADS):
            if has_s0:
                st_ref[h] = s0_ref[0, d, h].T
            else:
                st_ref[h] = jnp.zeros((HEAD_D, HEAD_D), F32)

        def chunk_body(c, carry, rev=rev, la_ref=la_ref):
            cc = (n_chunks - 1 - c) if rev else c
            rows = pl.ds(pl.multiple_of(cc * CHUNK, CHUNK), CHUNK)
            for h0 in range(0, N_HEADS, 2):
                ops, vs = [], []
                for idx in range(2):
                    lo, hi = (h0 + idx) * HEAD_D, (h0 + idx + 1) * HEAD_D
                    qc = qkv_ref[rows, lo:hi].astype(F32) * (HEAD_D ** -0.5)
                    kc = qkv_ref[rows, D_GLA + lo:D_GLA + hi].astype(F32)
                    ops.append(_gla_head_operands(qc, kc, la_ref[rows, lo:hi], rev, idx * CHUNK))
                    vs.append(qkv_ref[rows, 2 * D_GLA + lo:2 * D_GLA + hi].astype(F32))
                oc, (st_a, st_b) = _gla_chunk_pair(ops[0], ops[1], vs[0], vs[1], st_ref[h0], st_ref[h0 + 1])
                st_ref[h0] = st_a
                st_ref[h0 + 1] = st_b
                cols = slice(h0 * HEAD_D, (h0 + 2) * HEAD_D)
                if rev:
                    o_ref[rows, cols] += oc
                else:
                    o_ref[rows, cols] = oc
            return carry
        lax.fori_loop(0, n_chunks, chunk_body, 0, unroll=4)

        if emit_state:
            for h in range(N_HEADS):
                st_out_ref[0, d, h] = st_ref[h].T

    out_ref[...] = o_ref[...].astype(BF16)


def _gla_scan(proj, wdec, bdec, s0, *, seq_len, emit_state):
    t = proj.shape[0]
    nb = t // seq_len
    has_s0 = s0 is not None
    const = lambda shape: pl.BlockSpec(shape, lambda b: (0,) * len(shape), pipeline_mode=pl.Buffered(1))
    assert C_Q % (3 * D_GLA) == 0 and C_K == C_Q + D_GLA and C_V == C_K + D_GLA and C_LR % LR_PAD == 0
    in_specs = [pl.BlockSpec((seq_len, 3 * D_GLA), lambda b: (b, C_Q // (3 * D_GLA))),
                pl.BlockSpec((seq_len, LR_PAD), lambda b: (b, C_LR // LR_PAD)),
                const((2, LR_PAD, D_GLA)), const((2, D_GLA))]
    args = [proj, proj, wdec, bdec]
    if has_s0:
        in_specs.append(pl.BlockSpec((1, 2, N_HEADS, HEAD_D, HEAD_D), lambda b: (b, 0, 0, 0, 0)))
        args.append(s0)
    out_specs = [pl.BlockSpec((seq_len, D_GLA), lambda b: (b, 0))]
    out_shape = [jax.ShapeDtypeStruct((t, D_GLA), BF16)]
    if emit_state:
        out_specs.append(pl.BlockSpec((1, 2, N_HEADS, HEAD_D, HEAD_D), lambda b: (b, 0, 0, 0, 0)))
        out_shape.append(jax.ShapeDtypeStruct((nb, 2, N_HEADS, HEAD_D, HEAD_D), F32))
    kern = functools.partial(_scan_kernel, seq_len=seq_len, has_s0=has_s0, emit_state=emit_state)
    return pl.pallas_call(
        kern,
        grid=(nb,),
        in_specs=in_specs,
        out_specs=out_specs,
        out_shape=out_shape,
        scratch_shapes=[pltpu.VMEM((seq_len, D_GLA), F32), pltpu.VMEM((seq_len, D_GLA), F32),
                        pltpu.VMEM((seq_len, D_GLA), F32), pltpu.VMEM((N_HEADS, HEAD_D, HEAD_D), F32)],
        compiler_params=pltpu.CompilerParams(dimension_semantics=("arbitrary",),
                                             vmem_limit_bytes=VMEM_LIMIT),
        name="gla_scan",
    )(*args)


MIX_TM = 1024
MIX_SUB = 256


def _mix_dense_kernel(conv_ref, go_ref, gc0_ref, gc1_ref, gg0_ref, gg1_ref, o_ref, x_ref, mod_ref, cw_ref, cb_ref,
                      gnw_ref, wbc_ref, wbg_ref, wout_ref, out_ref, yc_ref, yg_ref, *, row_len, tiles_per_mod):
    i = pl.program_id(0)
    mod_row = i // tiles_per_mod if mod_ref.shape[0] > 1 else 0
    g1 = mod_ref[pl.ds(mod_row, 1), 2 * D_MODEL:3 * D_MODEL]
    hsel_r = lax.broadcasted_iota(jnp.int32, (D_GLA, D_GLA), 0) // HEAD_D
    hsel_c = lax.broadcasted_iota(jnp.int32, (D_GLA, D_GLA), 1) // HEAD_D
    head_avg = jnp.where(hsel_r == hsel_c, 1.0 / HEAD_D, 0.0).astype(BF16)
    head_avg2 = jnp.concatenate([head_avg, head_avg], axis=0)
    pos = lax.broadcasted_iota(jnp.int32, (MIX_SUB, 1), 0)
    in_row = pos % row_len

    def branch_body(t, carry):
        rows = pl.ds(pl.multiple_of(t * MIX_SUB, MIX_SUB), MIX_SUB)
        cu = conv_ref[rows, 2 * D_CONV:3 * D_CONV].astype(F32) * conv_ref[rows, 0:D_CONV].astype(F32)
        left = jnp.where(in_row == 0, 0.0, pltpu.roll(cu, 1, axis=0))
        right = jnp.where(in_row == row_len - 1, 0.0, pltpu.roll(cu, MIX_SUB - 1, axis=0))
        conv = cw_ref[0:1] * left + cw_ref[1:2] * cu + cw_ref[2:3] * right + cb_ref[...]
        yc_ref[rows, :] = (conv_ref[rows, D_CONV:2 * D_CONV].astype(F32) * conv).astype(BF16)
        o = o_ref[rows, :].astype(F32)
        ms = _dot(jnp.concatenate(_split_bf16(o * o), axis=1), head_avg2)
        g_out = go_ref[rows, :].astype(F32)
        yg_ref[rows, :] = (o * lax.rsqrt(ms + EPS) * gnw_ref[...] * (g_out * _sigmoid(g_out))).astype(BF16)
        return carry
    lax.fori_loop(0, MIX_TM // MIX_SUB, branch_body, 0)

    half = D_MODEL // 2
    pc = _dot(yc_ref[...], wbc_ref[...])
    pg = _dot(yg_ref[...], wbg_ref[...])
    merged = jnp.concatenate(
        [_sigmoid(gc0_ref[...].astype(F32)) * pc[:, :half] + _sigmoid(gg0_ref[...].astype(F32)) * pg[:, :half],
         _sigmoid(gc1_ref[...].astype(F32)) * pc[:, half:] + _sigmoid(gg1_ref[...].astype(F32)) * pg[:, half:]],
        axis=1)
    out_ref[...] = x_ref[...] + g1 * _dot(merged.astype(BF16), wout_ref[...])


def _mix_dense(proj, o, x2d, mod, conv_w, conv_b, gnw, wbc, wbg, wout, *, row_len, tokens_per_mod):
    t = x2d.shape[0]
    assert MIX_SUB % row_len == 0 and t % MIX_TM == 0 and tokens_per_mod % MIX_TM == 0
    const = lambda shape: pl.BlockSpec(shape, lambda i: (0,) * len(shape), pipeline_mode=pl.Buffered(1))
    cols = lambda width, start: pl.BlockSpec((MIX_TM, width), lambda i: (i, start // width))
    half = D_MODEL // 2
    assert C_U == 0 and C_GB == D_CONV and C_GC == 2 * D_CONV and C_GO % D_GLA == 0 and C_BRC % half == 0
    kern = functools.partial(_mix_dense_kernel, row_len=row_len, tiles_per_mod=tokens_per_mod // MIX_TM)
    return pl.pallas_call(
        kern,
        grid=(t // MIX_TM,),
        in_specs=[cols(3 * D_CONV, C_U), cols(D_GLA, C_GO),
                  cols(half, C_BRC), cols(half, C_BRC + half), cols(half, C_BRG), cols(half, C_BRG + half),
                  pl.BlockSpec((MIX_TM, D_GLA), lambda i: (i, 0)), pl.BlockSpec((MIX_TM, D_MODEL), lambda i: (i, 0)),
                  const(mod.shape), const((3, D_CONV)), const((1, D_CONV)), const((1, D_GLA)),
                  const((D_CONV, D_MODEL)), const((D_GLA, D_MODEL)), const((D_MODEL, D_MODEL))],
        out_specs=pl.BlockSpec((MIX_TM, D_MODEL), lambda i: (i, 0)),
        out_shape=jax.ShapeDtypeStruct((t, D_MODEL), F32),
        scratch_shapes=[pltpu.VMEM((MIX_TM, D_CONV), BF16), pltpu.VMEM((MIX_TM, D_GLA), BF16)],
        compiler_params=pltpu.CompilerParams(dimension_semantics=("arbitrary",), vmem_limit_bytes=VMEM_LIMIT),
        name="mix_dense",
    )(proj, proj, proj, proj, proj, proj, o, x2d, mod, conv_w, conv_b, gnw, wbc, wbg, wout)


I32 = jnp.int32
TB = 256
ROW_ALIGN = 16
G_ALIGN = 256
E_CHUNK = 2048
R_LOC = 3072
H2W = 1152


def _select_x(i, n_p_tiles, xp_ref, xs_ref):
    return jnp.where(i < n_p_tiles, xp_ref[...], xs_ref[...])


def _mod_row(i, n_p_tiles, tiles_per_mod):
    return jnp.where(i < n_p_tiles, 0, 1 + (i - n_p_tiles) // tiles_per_mod)


def _route_kernel(xp_ref, xs_ref, mod_ref, n2w_ref, wrt_ref, brb_ref, h2_ref, lpos_ref, cnt_ref, *,
                  n_p_tiles, tiles_per_mod):
    i = pl.program_id(0)
    row = _mod_row(i, n_p_tiles, tiles_per_mod)
    x = _select_x(i, n_p_tiles, xp_ref, xs_ref)
    sh = mod_ref[pl.ds(row, 1), 3 * D_MODEL:4 * D_MODEL]
    sc = mod_ref[pl.ds(row, 1), 4 * D_MODEL:5 * D_MODEL]
    h2 = _rms(x) * n2w_ref[...] * (1.0 + sc) + sh

    h_hi, h_lo = _split_bf16(h2)
    w_hi, w_lo = _split_bf16(wrt_ref[...])
    scores = jax.nn.sigmoid(_dot_nt(jnp.concatenate([w_hi, w_hi, w_lo], axis=1),
                                    jnp.concatenate([h_hi, h_lo, h_hi], axis=1)))
    biased = scores + brb_ref[...]
    eidx = lax.broadcasted_iota(I32, scores.shape, 0)
    picks = []
    for _k in range(TOP_K):
        m = jnp.max(biased, axis=0, keepdims=True)
        first = jnp.min(jnp.where(biased == m, eidx, N_EXPERTS), axis=0, keepdims=True)
        pick = eidx == first
        picks.append(pick)
        biased = jnp.where(pick, -jnp.inf, biased)
    sel = jnp.zeros(scores.shape, F32)
    for pick in picks:
        sel = jnp.where(pick, 1.0, sel)
    selsc = sel * scores
    comb = selsc / jnp.sum(selsc, axis=0, keepdims=True) * ROUTED_SCALE

    selb = sel.astype(BF16)
    tr_ = lax.broadcasted_iota(I32, (TB, TB), 0)
    tc_ = lax.broadcasted_iota(I32, (TB, TB), 1)
    rank_n = _dot(selb, jnp.concatenate([jnp.where(tr_ < tc_, 1.0, 0.0).astype(BF16), jnp.ones((TB, 128), BF16)],
                                        axis=1))
    rank, n_b = rank_n[:, :TB], rank_n[:, TB:]
    m_b = jnp.maximum(jnp.floor((n_b + (ROW_ALIGN - 1)) * (1.0 / ROW_ALIGN)), 1.0) * ROW_ALIGN
    er_ = lax.broadcasted_iota(I32, (N_EXPERTS, N_EXPERTS), 0)
    ec_ = lax.broadcasted_iota(I32, (N_EXPERTS, N_EXPERTS), 1)
    loff_b = _dot(jnp.where(ec_ < er_, 1.0, 0.0).astype(BF16), m_b.astype(BF16))
    lposf = jnp.concatenate([loff_b] * (TB // 128), axis=1) + rank
    rows = [jnp.sum(jnp.where(pick, lposf, 0.0), axis=0, keepdims=True) for pick in picks]
    lpos_ref[0] = jnp.concatenate(rows, axis=0).astype(I32)
    cnt_ref[0] = m_b

    combt = comb.T
    chi = combt.astype(BF16).astype(F32)
    h2_ref[:, 0:D_MODEL] = h_hi
    h2_ref[:, D_MODEL:H2W] = jnp.concatenate([chi, combt - chi], axis=1).astype(BF16)


def _route(x1p, x1s, mod, n2w, w_router_t, b_router_b, *, tiles_per_mod):
    n_p, n_s = x1p.shape[0] // TB, x1s.shape[0] // TB
    nt = n_p + n_s
    kern = functools.partial(_route_kernel, n_p_tiles=n_p, tiles_per_mod=tiles_per_mod)
    const = lambda shape: pl.BlockSpec(shape, lambda i: (0,) * len(shape))
    return pl.pallas_call(
        kern,
        grid=(nt,),
        in_specs=[pl.BlockSpec((TB, D_MODEL), lambda i: (jnp.minimum(i, n_p - 1), 0)),
                  pl.BlockSpec((TB, D_MODEL), lambda i: (jnp.maximum(i - n_p, 0), 0)),
                  const(mod.shape), const((1, D_MODEL)), const((N_EXPERTS, D_MODEL)), const((N_EXPERTS, TB))],
        out_specs=[pl.BlockSpec((TB, H2W), lambda i: (i, 0)),
                   pl.BlockSpec((1, TOP_K, TB), lambda i: (i, 0, 0)),
                   pl.BlockSpec((1, N_EXPERTS, 128), lambda i: (i, 0, 0))],
        out_shape=[jax.ShapeDtypeStruct((nt * TB, H2W), BF16),
                   jax.ShapeDtypeStruct((nt, TOP_K, TB), I32),
                   jax.ShapeDtypeStruct((nt, N_EXPERTS, 128), F32)],
        compiler_params=pltpu.CompilerParams(dimension_semantics=("arbitrary",), vmem_limit_bytes=VMEM_LIMIT),
        name="moe_route",
    )(x1p, x1s, mod, n2w, w_router_t, b_router_b)


def _plan_kernel(cnt_ref, off_ref, loff_ref, msz_ref, grp_ref, *, nt):
    lane = lax.broadcasted_iota(I32, (N_EXPERTS, 128), 1)
    m = jnp.zeros((N_EXPERTS, 128), F32)
    for i in range(nt):
        m = jnp.where(lane == i, cnt_ref[i], m)
    total = jnp.broadcast_to(jnp.sum(m, axis=1, keepdims=True), (N_EXPERTS, 128))
    gsz = jnp.floor((total + (G_ALIGN - 1)) * (1.0 / G_ALIGN)) * G_ALIGN
    er_ = lax.broadcasted_iota(I32, (N_EXPERTS, N_EXPERTS), 0)
    ec_ = lax.broadcasted_iota(I32, (N_EXPERTS, N_EXPERTS), 1)
    lstrict = jnp.where(ec_ < er_, 1.0, 0.0)
    ir_ = lax.broadcasted_iota(I32, (128, 128), 0)
    ic_ = lax.broadcasted_iota(I32, (128, 128), 1)
    ustrict = jnp.where(ir_ < ic_, 1.0, 0.0)
    gstart = _dot_hi(lstrict, gsz)
    off_ref[...] = (gstart + _dot_hi(m, ustrict)).astype(I32)
    loff_ref[...] = _dot_hi(lstrict, m).astype(I32)
    msz_ref[...] = m.astype(I32)
    grp = jnp.where(lane == 0, gstart + total, jnp.where(lane == 1, gsz - total, jnp.where(lane == 2, gstart, gsz)))
    grp_ref[...] = grp.astype(I32)


def _plan(cnt):
    nt = cnt.shape[0]
    assert nt <= 128
    tab = jax.ShapeDtypeStruct((N_EXPERTS, 128), I32)
    return pl.pallas_call(
        functools.partial(_plan_kernel, nt=nt),
        out_shape=[tab, tab, tab, tab],
        compiler_params=pltpu.CompilerParams(vmem_limit_bytes=VMEM_LIMIT),
        name="moe_plan",
    )(cnt)


def _start_copies(msz_ref, tile, make_copy):
    for e in range(N_EXPERTS):
        make_copy(e, pl.multiple_of(msz_ref[e, tile], ROW_ALIGN)).start()


def _tile_rows(loff_ref, msz_ref, tile):
    return pl.multiple_of(loff_ref[N_EXPERTS - 1, tile] + msz_ref[N_EXPERTS - 1, tile], ROW_ALIGN)


def _dispatch_kernel(off_ref, loff_ref, msz_ref, tail_ref, h2_ref, lpos_ref, xs_hbm, xloc_ref, zero_ref, sem,
                     tail_sem, *, nt):
    i = pl.program_id(0)
    slot = i % 2

    def copy_for(tile, slot_):
        def make(e, m):
            lo = pl.multiple_of(loff_ref[e, tile], ROW_ALIGN)
            of = pl.multiple_of(off_ref[e, tile], ROW_ALIGN)
            return pltpu.make_async_copy(xloc_ref.at[slot_, pl.ds(lo, m)], xs_hbm.at[pl.ds(of, m)], sem.at[slot_])
        return make

    def wait_tile(tile, slot_):
        n = _tile_rows(loff_ref, msz_ref, tile)
        pltpu.make_async_copy(xloc_ref.at[slot_, pl.ds(0, n)], xs_hbm.at[pl.ds(0, n)], sem.at[slot_]).wait()

    lpos = lpos_ref[0].astype(jnp.int16)
    h2 = h2_ref[...]
    ck = 1024
    one, zero = jnp.ones((ck, TB), BF16), jnp.zeros((ck, TB), BF16)
    for c in range(R_LOC // ck):
        r = (lax.broadcasted_iota(I32, (ck, TB), 0) + c * ck).astype(jnp.int16)
        d = zero
        for k in range(TOP_K):
            d = jnp.where(r == lpos[k:k + 1, :], one, d)
        res = _dot(d, h2)
        xloc_ref[slot, c * ck:(c + 1) * ck, :] = res.astype(BF16)

    _start_copies(msz_ref, i, copy_for(i, slot))

    @pl.when(i > 0)
    def _():
        wait_tile(i - 1, 1 - slot)

    @pl.when(i == nt - 1)
    def _():
        zero_ref[...] = jnp.zeros(zero_ref.shape, BF16)

        def tail_copies(start):
            def body(e, carry):
                n = tail_ref[e, 1]

                @pl.when(n > 0)
                def _():
                    st = pl.multiple_of(tail_ref[e, 0], ROW_ALIGN)
                    nn = pl.multiple_of(n, ROW_ALIGN)
                    cp = pltpu.make_async_copy(zero_ref.at[pl.ds(0, nn)], xs_hbm.at[pl.ds(st, nn)], tail_sem)
                    if start:
                        cp.start()
                    else:
                        cp.wait()
                return carry
            lax.fori_loop(0, N_EXPERTS, body, 0)
        tail_copies(True)
        wait_tile(i, slot)
        tail_copies(False)


def _dispatch(off, loff, msz, tail, h2ext, lpos, n_rows):
    nt = lpos.shape[0]
    grid_spec = pltpu.PrefetchScalarGridSpec(
        num_scalar_prefetch=4,
        grid=(nt,),
        in_specs=[pl.BlockSpec((TB, H2W), lambda i, *_: (i, 0)),
                  pl.BlockSpec((1, TOP_K, TB), lambda i, *_: (i, 0, 0))],
        out_specs=pl.BlockSpec(memory_space=pl.ANY),
        scratch_shapes=[pltpu.VMEM((2, R_LOC, H2W), BF16), pltpu.VMEM((G_ALIGN, H2W), BF16),
                        pltpu.SemaphoreType.DMA((2,)), pltpu.SemaphoreType.DMA],
    )
    return pl.pallas_call(
        functools.partial(_dispatch_kernel, nt=nt),
        grid_spec=grid_spec,
        out_shape=jax.ShapeDtypeStruct((n_rows, H2W), BF16),
        compiler_params=pltpu.CompilerParams(dimension_semantics=("arbitrary",), vmem_limit_bytes=VMEM_LIMIT),
        name="moe_dispatch",
    )(off, loff, msz, tail, h2ext, lpos)


def _expert_kernel(grp_ref, wg_ref, wu_ref, wd_ref, xs_hbm, ys_hbm, xbuf, ybuf, wgu_ref, wdb_ref, st_ref,
                   in_sem, out_sem):
    e = pl.program_id(0)
    n_exp = pl.num_programs(0)

    def in_copy(row0, n, slot):
        return pltpu.make_async_copy(xs_hbm.at[pl.ds(row0, n)], xbuf.at[slot, pl.ds(0, n)], in_sem.at[slot])

    def out_copy(row0, n, slot):
        return pltpu.make_async_copy(ybuf.at[slot, pl.ds(0, n)], ys_hbm.at[pl.ds(row0, n)], out_sem.at[slot])

    def rows_of(ex):
        return grp_ref[jnp.minimum(ex, n_exp - 1), 3]

    def next_nonempty(ex):
        return lax.while_loop(lambda c: (c < n_exp) & (rows_of(c) == 0), lambda c: c + 1, ex + 1)

    def start_first_chunk(ex, slot):
        @pl.when(ex < n_exp)
        def _():
            exc = jnp.minimum(ex, n_exp - 1)
            n = pl.multiple_of(jnp.minimum(grp_ref[exc, 3], E_CHUNK), G_ALIGN)
            in_copy(pl.multiple_of(grp_ref[exc, 2], G_ALIGN), n, slot).start()

    def drain_out(slot):
        pend = st_ref[1 + slot]

        @pl.when(pend > 0)
        def _():
            out_copy(0, pl.multiple_of(pend, G_ALIGN), slot).wait()
            st_ref[1 + slot] = 0

    @pl.when(e == 0)
    def _():
        st_ref[0] = 0
        st_ref[1] = 0
        st_ref[2] = 0
        start_first_chunk(next_nonempty(-1), 0)

    g0 = grp_ref[e, 2]
    gn = grp_ref[e, 3]

    @pl.when(gn > 0)
    def _():
        wgu_ref[:, :D_EXPERT] = wg_ref[0].astype(BF16)
        wgu_ref[:, D_EXPERT:] = wu_ref[0].astype(BF16)
        wdb_ref[...] = wd_ref[0].astype(BF16)
        n_chunks = (gn + (E_CHUNK - 1)) // E_CHUNK

        def compute(n, slot):
            x = xbuf[slot, 0:n, 0:D_MODEL]
            ext = xbuf[slot, 0:n, D_MODEL:H2W].astype(F32)
            wts = ext[:, :N_EXPERTS] + ext[:, N_EXPERTS:]
            lane = lax.broadcasted_iota(I32, wts.shape, 1)
            w = jnp.sum(jnp.where(lane == e, wts, 0.0), axis=-1, keepdims=True)
            h = _dot(x, wgu_ref[...])
            hg, hu = h[:, :D_EXPERT], h[:, D_EXPERT:]
            act = hg * _sigmoid(hg) * hu * w
            ybuf[slot, 0:n, :] = _dot(act.astype(BF16), wdb_ref[...]).astype(BF16)

        def chunk_body(c, slot):
            row0 = pl.multiple_of(g0 + c * E_CHUNK, G_ALIGN)
            n = pl.multiple_of(jnp.minimum(gn - c * E_CHUNK, E_CHUNK), G_ALIGN)
            in_copy(row0, n, slot).wait()

            @pl.when(c + 1 < n_chunks)
            def _():
                n1 = pl.multiple_of(jnp.minimum(gn - (c + 1) * E_CHUNK, E_CHUNK), G_ALIGN)
                in_copy(pl.multiple_of(row0 + E_CHUNK, G_ALIGN), n1, 1 - slot).start()

            @pl.when(c + 1 == n_chunks)
            def _():
                start_first_chunk(next_nonempty(e), 1 - slot)

            drain_out(slot)
            for v in range(G_ALIGN, E_CHUNK + 1, G_ALIGN):
                @pl.when(n == v)
                def _(v=v):
                    compute(v, slot)
            out_copy(row0, n, slot).start()
            st_ref[1 + slot] = n
            return 1 - slot

        st_ref[0] = lax.fori_loop(0, n_chunks, chunk_body, st_ref[0])

    @pl.when(e == n_exp - 1)
    def _():
        drain_out(0)
        drain_out(1)


def _experts(grp, xs, wg, wu, wd):
    w_in = pl.BlockSpec((1, D_MODEL, D_EXPERT), lambda e, grp_ref: (e, 0, 0))
    grid_spec = pltpu.PrefetchScalarGridSpec(
        num_scalar_prefetch=1,
        grid=(N_EXPERTS,),
        in_specs=[w_in, w_in, pl.BlockSpec((1, D_EXPERT, D_MODEL), lambda e, grp_ref: (e, 0, 0)),
                  pl.BlockSpec(memory_space=pl.ANY)],
        out_specs=pl.BlockSpec(memory_space=pl.ANY),
        scratch_shapes=[pltpu.VMEM((2, E_CHUNK, H2W), BF16), pltpu.VMEM((2, E_CHUNK, D_MODEL), BF16),
                        pltpu.VMEM((D_MODEL, 2 * D_EXPERT), BF16), pltpu.VMEM((D_EXPERT, D_MODEL), BF16),
                        pltpu.SMEM((4,), I32), pltpu.SemaphoreType.DMA((2,)), pltpu.SemaphoreType.DMA((2,))],
    )
    return pl.pallas_call(
        _expert_kernel,
        grid_spec=grid_spec,
        out_shape=jax.ShapeDtypeStruct((xs.shape[0], D_MODEL), BF16),
        compiler_params=pltpu.CompilerParams(dimension_semantics=("arbitrary",), vmem_limit_bytes=VMEM_LIMIT),
        name="moe_experts",
    )(grp, wg, wu, wd, xs)


def _combine_kernel(off_ref, loff_ref, msz_ref, lpos_ref, h2_ref, xp_ref, xs_ref, mod_ref, wgs_ref, wus_ref, wds_ref,
                    fnw_ref, ysrt_hbm, yp_ref, ys_ref, yloc_ref, acc_ref, sem, *, nt, n_p_tiles, tiles_per_mod,
                    final):
    i = pl.program_id(0)
    slot = i % 2

    def copy_for(tile, slot_):
        def make(e, m):
            lo = pl.multiple_of(loff_ref[e, tile], ROW_ALIGN)
            of = pl.multiple_of(off_ref[e, tile], ROW_ALIGN)
            return pltpu.make_async_copy(ysrt_hbm.at[pl.ds(of, m)], yloc_ref.at[slot_, pl.ds(lo, m)], sem.at[slot_])
        return make

    @pl.when(i == 0)
    def _():
        yloc_ref[...] = jnp.zeros(yloc_ref.shape, BF16)
        _start_copies(msz_ref, 0, copy_for(0, 0))

    nxt = jnp.minimum(i + 1, nt - 1)
    _start_copies(msz_ref, nxt, copy_for(nxt, 1 - slot))

    hb = h2_ref[...]
    hg = _dot(hb, wgs_ref[...].astype(BF16))
    hu = _dot(hb, wus_ref[...].astype(BF16))
    acc_ref[...] = _dot((hg * _sigmoid(hg) * hu).astype(BF16), wds_ref[...].astype(BF16))

    def wait_tile(tile, slot_):
        n = _tile_rows(loff_ref, msz_ref, tile)
        pltpu.make_async_copy(ysrt_hbm.at[pl.ds(0, n)], yloc_ref.at[slot_, pl.ds(0, n)], sem.at[slot_]).wait()

    wait_tile(i, slot)

    @pl.when(i == nt - 1)
    def _():
        wait_tile(i, 1 - slot)

    lpos_pad = jnp.concatenate([lpos_ref[0].astype(F32), jnp.zeros((128 - TOP_K, TB), F32)], axis=0)
    lposc = lpos_pad.T.astype(I32)
    ck = 512
    cols = [jnp.broadcast_to(lposc[:, k:k + 1], (TB, ck)).astype(jnp.int16) for k in range(TOP_K)]
    one, zero = jnp.ones((TB, ck), BF16), jnp.zeros((TB, ck), BF16)
    for c in range(R_LOC // ck):
        r = (lax.broadcasted_iota(I32, (TB, ck), 1) + c * ck).astype(jnp.int16)
        cm = zero
        for k in range(TOP_K):
            cm = jnp.where(r == cols[k], one, cm)
        acc_ref[...] += _dot(cm, yloc_ref[slot, c * ck:(c + 1) * ck, :])

    row = _mod_row(i, n_p_tiles, tiles_per_mod)
    g2 = mod_ref[pl.ds(row, 1), 5 * D_MODEL:6 * D_MODEL]
    x2 = _select_x(i, n_p_tiles, xp_ref, xs_ref) + g2 * acc_ref[...]
    y = _rms(x2) * fnw_ref[...] if final else x2

    @pl.when(i < n_p_tiles)
    def _():
        yp_ref[...] = y

    @pl.when(i >= n_p_tiles)
    def _():
        ys_ref[...] = y


def _combine(off, loff, msz, lpos, h2ext, x1p, x1s, mod, wgs, wus, wds, fnw, ysorted, *, tiles_per_mod, final):
    n_p, n_s = x1p.shape[0] // TB, x1s.shape[0] // TB
    nt = n_p + n_s
    const = lambda shape: pl.BlockSpec(shape, lambda i, *_: (0,) * len(shape), pipeline_mode=pl.Buffered(1))
    p_idx = lambda i, *_: (jnp.minimum(i, n_p - 1), 0)
    s_idx = lambda i, *_: (jnp.maximum(i - n_p, 0), 0)
    grid_spec = pltpu.PrefetchScalarGridSpec(
        num_scalar_prefetch=3,
        grid=(nt,),
        in_specs=[pl.BlockSpec((1, TOP_K, TB), lambda i, *_: (i, 0, 0)),
                  pl.BlockSpec((TB, D_MODEL), lambda i, *_: (i, 0)),
                  pl.BlockSpec((TB, D_MODEL), p_idx), pl.BlockSpec((TB, D_MODEL), s_idx),
                  const(mod.shape), const((D_MODEL, D_EXPERT)), const((D_MODEL, D_EXPERT)), const((D_EXPERT, D_MODEL)),
                  const((1, D_MODEL)), pl.BlockSpec(memory_space=pl.ANY)],
        out_specs=[pl.BlockSpec((TB, D_MODEL), p_idx), pl.BlockSpec((TB, D_MODEL), s_idx)],
        scratch_shapes=[pltpu.VMEM((2, R_LOC, D_MODEL), BF16), pltpu.VMEM((TB, D_MODEL), F32),
                        pltpu.SemaphoreType.DMA((2,))],
    )
    kern = functools.partial(_combine_kernel, nt=nt, n_p_tiles=n_p, tiles_per_mod=tiles_per_mod, final=final)
    return pl.pallas_call(
        kern,
        grid_spec=grid_spec,
        out_shape=[jax.ShapeDtypeStruct(x1p.shape, F32), jax.ShapeDtypeStruct(x1s.shape, F32)],
        compiler_params=pltpu.CompilerParams(dimension_semantics=("arbitrary",), vmem_limit_bytes=VMEM_LIMIT),
        name="moe_combine",
    )(off, loff, msz, lpos, h2ext, x1p, x1s, mod, wgs, wus, wds, fnw, ysorted)


def _moe(x1p, x1s, mod, n2w, w_router, b_router, wg, wu, wd, wgs, wus, wds, fnw, *, tokens_per_mod, final):
    assert R_LOC >= TB * TOP_K + N_EXPERTS * ROW_ALIGN and tokens_per_mod % TB == 0
    nt = (x1p.shape[0] + x1s.shape[0]) // TB
    n_rows_max = nt * TB * TOP_K + nt * N_EXPERTS * ROW_ALIGN + N_EXPERTS * (G_ALIGN - ROW_ALIGN)
    tiles_per_mod = tokens_per_mod // TB
    brb = jnp.broadcast_to(b_router.reshape(N_EXPERTS, 1), (N_EXPERTS, TB))
    h2ext, lpos, cnt = _route(x1p, x1s, mod, n2w, w_router.T, brb, tiles_per_mod=tiles_per_mod)
    off, loff, msz, grp = _plan(cnt)
    xs = _dispatch(off, loff, msz, grp, h2ext, lpos, n_rows_max)
    ysorted = _experts(grp, xs, wg, wu, wd)
    return _combine(off, loff, msz, lpos, h2ext, x1p, x1s, mod, wgs, wus, wds, fnw, ysorted,
                    tiles_per_mod=tiles_per_mod, final=final)


def kernel(x_prompt, x_sample, state_gla, c, c_ctx, w_mod, b_mod, norm1_w, w_in, conv_w, conv_b, w_decay, b_decay,
           gla_norm_w, w_br_conv, w_br_gla, w_out, norm2_w, w_router, b_router, w_gate_e, w_up_e, w_down_e,
           w_gate_s, w_up_s, w_down_s, final_norm_w):
    depth = w_mod.shape[0]
    nb_p, len_p, _ = x_prompt.shape
    nb_s, len_s, _ = x_sample.shape
    yp = x_prompt.reshape(nb_p * len_p, D_MODEL)
    ys = x_sample.reshape(nb_s * len_s, D_MODEL)
    fnw = final_norm_w.reshape(1, D_MODEL)

    cond = jnp.concatenate([c_ctx[None, :], c, jnp.zeros((8 - 1 - nb_s, D_MODEL), F32)], axis=0)
    states = []
    for l in range(depth):
        mod = _modulation(cond, w_mod[l], b_mod[l].reshape(1, -1))
        mod_p, mod_s = mod[0:1], mod[1:1 + nb_s]

        w_in_r = _w_in_prep(w_in[l].T)
        wdec = jnp.zeros((2, LR_PAD, D_GLA), F32)
        wdec = wdec.at[0, 0:GLA_RANK].set(w_decay[l, 0]).at[1, GLA_RANK:2 * GLA_RANK].set(w_decay[l, 1])
        n1w = norm1_w[l].reshape(1, D_MODEL)
        mix_w = (conv_w[l], conv_b[l].reshape(1, D_CONV), gla_norm_w[l].reshape(1, D_GLA),
                 w_br_conv[l].astype(BF16), w_br_gla[l].astype(BF16), w_out[l].astype(BF16))
        moe_w = (norm2_w[l].reshape(1, D_MODEL), w_router[l], b_router[l],
                 w_gate_e[l], w_up_e[l], w_down_e[l], w_gate_s[l], w_up_s[l], w_down_s[l])

        proj_p = _in_proj(yp, mod_p, n1w, w_in_r, rows_per_mod=nb_p * len_p)
        o_p, st = _gla_scan(proj_p, wdec, b_decay[l], None, seq_len=len_p, emit_state=True)
        yp = _mix_dense(proj_p, o_p, yp, mod_p, *mix_w, row_len=len_p, tokens_per_mod=nb_p * len_p)
        states.append(st)
        proj_s = _in_proj(ys, mod_s, n1w, w_in_r, rows_per_mod=len_s)
        (o_s,) = _gla_scan(proj_s, wdec, b_decay[l], state_gla[:, l], seq_len=len_s, emit_state=False)
        ys = _mix_dense(proj_s, o_s, ys, mod_s, *mix_w, row_len=GRID_W, tokens_per_mod=len_s)

        yp, ys = _moe(yp, ys, mod, *moe_w, fnw, tokens_per_mod=len_s, final=l == depth - 1)
    new_state = jnp.stack(states, axis=1)
    return (yp.reshape(nb_p, len_p, D_MODEL), ys.reshape(nb_s, len_s, D_MODEL), new_state)
```

```python
import functools

import jax
import jax.numpy as jnp
from jax import lax
from jax.experimental import pallas as pl
from jax.experimental.pallas import tpu as pltpu

F32 = jnp.float32
BF16 = jnp.bfloat16

D_MODEL = 1024
GRID_W = 64
D_CONV = 512
N_HEADS = 4
HEAD_D = 128
D_GLA = N_HEADS * HEAD_D
GLA_RANK = 16
GLA_GATE_NORM = 16.0
LOG2_E = 1.4426950408889634
CHUNK = 64
SUB = 8
N_SUB = CHUNK // SUB
N_EXPERTS = 64
TOP_K = 8
D_EXPERT = 256
ROUTED_SCALE = 2.5
EPS = 1e-6

C_U, C_GB, C_GC, C_Q, C_K, C_V, C_GO = 0, 512, 1024, 1536, 2048, 2560, 3072
C_BRC, C_BRG, C_LR = 3584, 4608, 5632
D_PROJ = 5760
LR_PAD = 128

VMEM_LIMIT = 56 * 1024 * 1024


def _dot(a, b):
    return jnp.dot(a, b, preferred_element_type=F32)


def _dot_nt(a, b):
    return lax.dot_general(a, b, (((1,), (1,)), ((), ())), preferred_element_type=F32)


def _dot_hi(a, b):
    return jnp.dot(a, b, preferred_element_type=F32, precision=lax.Precision.HIGHEST)


def _split_bf16(x):
    hi = x.astype(BF16)
    lo = (x - hi.astype(F32)).astype(BF16)
    return hi, lo


def _rms(x):
    return x * lax.rsqrt(jnp.mean(x * x, axis=-1, keepdims=True) + EPS)


def _sigmoid(x):
    return 0.5 * jnp.tanh(0.5 * x) + 0.5


def _mod_kernel(cond_ref, w_ref, b_ref, o_ref):
    c = cond_ref[...]
    c_hi, c_lo = _split_bf16(c * jax.nn.sigmoid(c))
    w_hi, w_lo = _split_bf16(w_ref[...])
    o_ref[...] = _dot(jnp.concatenate([c_hi, c_hi, c_lo], axis=1),
                      jnp.concatenate([w_hi, w_lo, w_hi], axis=0)) + b_ref[...]


def _modulation(cond, w_mod, b_mod):
    n_rows = cond.shape[0]
    tn = 1536
    return pl.pallas_call(
        _mod_kernel,
        grid=(6 * D_MODEL // tn,),
        in_specs=[pl.BlockSpec((n_rows, D_MODEL), lambda j: (0, 0)),
                  pl.BlockSpec((D_MODEL, tn), lambda j: (0, j)),
                  pl.BlockSpec((1, tn), lambda j: (0, j))],
        out_specs=pl.BlockSpec((n_rows, tn), lambda j: (0, j)),
        out_shape=jax.ShapeDtypeStruct((n_rows, 6 * D_MODEL), F32),
        compiler_params=pltpu.CompilerParams(dimension_semantics=("arbitrary",),
                                             vmem_limit_bytes=VMEM_LIMIT),
        name="modulation",
    )(cond, w_mod, b_mod)


D_IN_PROJ = 5664
C_LR_SRC, C_GATES_SRC = 3584, 3616


def _w_in_prep_kernel(w_ref, o_ref):
    rc = 64

    def copy_rows(dst0, src0, n):
        def body(t, carry):
            off = pl.multiple_of(t * rc, rc)
            o_ref[pl.ds(dst0 + off, rc), :] = w_ref[pl.ds(src0 + off, rc), :].astype(BF16)
            return carry
        lax.fori_loop(0, n // rc, body, 0)

    copy_rows(0, 0, C_LR_SRC)
    copy_rows(C_BRC, C_GATES_SRC, 2 * D_MODEL)
    o_ref[C_LR:C_LR + 2 * GLA_RANK, :] = w_ref[C_LR_SRC:C_GATES_SRC, :].astype(BF16)
    o_ref[C_LR + 2 * GLA_RANK:D_PROJ, :] = jnp.zeros((D_PROJ - C_LR - 2 * GLA_RANK, D_MODEL), BF16)


def _w_in_prep(w_in_t):
    return pl.pallas_call(
        _w_in_prep_kernel,
        out_shape=jax.ShapeDtypeStruct((D_PROJ, D_MODEL), BF16),
        compiler_params=pltpu.CompilerParams(vmem_limit_bytes=VMEM_LIMIT),
        name="w_in_prep",
    )(w_in_t)


def _inproj_kernel(x_ref, mod_ref, nw_ref, w_ref, o_ref, *, rows_per_mod, tm):
    i = pl.program_id(1)
    row = (i * tm) // rows_per_mod
    sh = mod_ref[pl.ds(row, 1), 0:D_MODEL]
    sc = mod_ref[pl.ds(row, 1), D_MODEL:2 * D_MODEL]
    h = _rms(x_ref[...]) * nw_ref[...] * (1.0 + sc) + sh
    o_ref[...] = _dot_nt(h.astype(BF16), w_ref[...]).astype(BF16)


def _in_proj(x2d, mod, norm_w, w_in_r, rows_per_mod):
    t = x2d.shape[0]
    tm, tn = 1024, 1920
    kern = functools.partial(_inproj_kernel, rows_per_mod=rows_per_mod, tm=tm)
    return pl.pallas_call(
        kern,
        grid=(D_PROJ // tn, t // tm),
        in_specs=[pl.BlockSpec((tm, D_MODEL), lambda j, i: (i, 0)),
                  pl.BlockSpec(mod.shape, lambda j, i: (0, 0)),
                  pl.BlockSpec((1, D_MODEL), lambda j, i: (0, 0)),
                  pl.BlockSpec((tn, D_MODEL), lambda j, i: (j, 0))],
        out_specs=pl.BlockSpec((tm, tn), lambda j, i: (i, j)),
        out_shape=jax.ShapeDtypeStruct((t, D_PROJ), BF16),
        compiler_params=pltpu.CompilerParams(dimension_semantics=("arbitrary", "arbitrary"),
                                             vmem_limit_bytes=VMEM_LIMIT),
        name="in_proj",
    )(x2d, mod, norm_w, w_in_r)


def _log2_sigmoid(x):
    return jnp.minimum(x, 0.0) * LOG2_E - jnp.log2(1.0 + jnp.exp2(jnp.abs(x) * (-LOG2_E)))


def _gla_head_operands(qc, kc, bc, rev, lane0):
    lane = lax.broadcasted_iota(jnp.int32, (SUB, HEAD_D), 1) - lane0
    sub = lax.broadcasted_iota(jnp.int32, (SUB, HEAD_D), 0)
    tot = bc[0:1] if rev else bc[CHUNK - 1:CHUNK]
    q_in = (qc * jnp.exp2(bc)).astype(BF16)
    k_tail = (kc * jnp.exp2(tot - bc)).astype(BF16)

    lhs_segs, rhs_segs = [], []

    def rows(before, mid, after):
        parts = ([jnp.zeros((before, HEAD_D), F32)] if before else []) + [mid]
        parts += [jnp.zeros((after, HEAD_D), F32)] if after else []
        return jnp.concatenate(parts, axis=0) if len(parts) > 1 else mid

    key_blocks = range(1, N_SUB) if rev else range(0, N_SUB - 1)
    for jb in key_blocks:
        r0 = jb * SUB
        ref_row = bc[r0:r0 + 1] if rev else bc[r0 + SUB - 1:r0 + SUB]
        ke = kc[r0:r0 + SUB] * jnp.exp2(ref_row - bc[r0:r0 + SUB])
        rhs_segs.append(rows(r0, ke, CHUNK - r0 - SUB))
        if rev:
            ql = qc[:r0] * jnp.exp2(bc[:r0] - ref_row)
            lhs_segs.append(rows(0, ql, CHUNK - r0))
        else:
            ql = qc[r0 + SUB:] * jnp.exp2(bc[r0 + SUB:] - ref_row)
            lhs_segs.append(rows(r0 + SUB, ql, 0))
    lhs = jnp.concatenate(lhs_segs, axis=1).astype(BF16)
    rhs = jnp.concatenate(rhs_segs, axis=1).astype(BF16)

    bit = [(lane & (1 << t)) != 0 for t in range(3)]
    blocks = []
    for ib in range(N_SUB):
        r0 = ib * SUB
        qi, bi = qc[r0:r0 + SUB], bc[r0:r0 + SUB]
        cols = []
        for jj in range(SUB):
            j = r0 + jj
            e = jnp.exp2(bi - bc[j:j + 1])
            cols.append(jnp.sum(qi * (kc[j:j + 1] * e), axis=-1, keepdims=True))
        for t in range(3):
            cols = [jnp.where(bit[t], cols[i + 1], cols[i]) for i in range(0, len(cols), 2)]
        rel = lane - r0
        keep = ((rel >= sub) & (rel < SUB)) if rev else ((rel <= sub) & (rel >= 0))
        blocks.append(jnp.where(keep, cols[0], 0.0))
    return q_in, k_tail, jnp.exp2(tot), lhs, rhs, jnp.concatenate(blocks, axis=0)


def _gla_chunk_pair(ops_a, ops_b, v_a, v_b, st_a, st_b):
    qa, ka, da, lhs_a, rhs_a, near_a = ops_a
    qb, kb, db, lhs_b, rhs_b, near_b = ops_b
    zr = jnp.zeros(rhs_a.shape, BF16)
    far = _dot_nt(jnp.concatenate([lhs_a, lhs_b], axis=1),
                  jnp.concatenate([jnp.concatenate([rhs_a, zr], axis=1),
                                   jnp.concatenate([zr, rhs_b], axis=1)], axis=0))
    scores = (far + near_a + near_b).astype(BF16)
    zv = jnp.zeros(v_a.shape, F32)
    vt = jnp.concatenate([jnp.concatenate([v_a, zv], axis=1),
                          jnp.concatenate([zv, v_b], axis=1)], axis=0).T.astype(BF16)
    zs = jnp.zeros((HEAD_D, HEAD_D), BF16)
    o = _dot_nt(jnp.concatenate([qa, scores, qb], axis=1),
                jnp.concatenate([jnp.concatenate([st_a.astype(BF16), zs], axis=0), vt,
                                 jnp.concatenate([zs, st_b.astype(BF16)], axis=0)], axis=1))
    ut = _dot(vt, jnp.concatenate([ka, kb], axis=0))
    return o, (st_a * da + ut[:HEAD_D], st_b * db + ut[HEAD_D:])


def _scan_kernel(*refs, seq_len, has_s0, emit_state):
    it = iter(refs)
    qkv_ref, lr_ref, wdec_ref, bdec_ref = (next(it) for _ in range(4))
    s0_ref = next(it) if has_s0 else None
    out_ref = next(it)
    st_out_ref = next(it) if emit_state else None
    la_f_ref, la_b_ref, o_ref, st_ref = (next(it) for _ in range(4))

    L = seq_len
    n_chunks = L // CHUNK
    tr = 256
    assert L % tr == 0

    ri = lax.broadcasted_iota(jnp.int32, (tr, tr), 0)
    ci = lax.broadcasted_iota(jnp.int32, (tr, tr), 1)
    same_chunk = (ri // CHUNK) == (ci // CHUNK)

    def decay_body(t, carry):
        r0 = pl.multiple_of(t * tr, tr)
        lr = lr_ref[pl.ds(r0, tr), :]
        lr2 = jnp.concatenate([lr, lr], axis=1)
        for d, ref in ((0, la_f_ref), (1, la_b_ref)):
            z = _dot(lr2, jnp.concatenate(_split_bf16(wdec_ref[d]), axis=0)) + bdec_ref[d:d + 1]
            la = jnp.concatenate(_split_bf16(_log2_sigmoid(z) * (1.0 / GLA_GATE_NORM)), axis=0)
            tri = jnp.where(same_chunk & ((ci >= ri) if d else (ci <= ri)), 1.0, 0.0).astype(BF16)
            ref[pl.ds(r0, tr), :] = _dot(jnp.concatenate([tri, tri], axis=1), la)
        return carry
    lax.fori_loop(0, L // tr, decay_body, 0)

    for rev, la_ref in ((False, la_f_ref), (True, la_b_ref)):
        d = 1 if rev else 0
        for h in range(N_HEADS):
            if has_s0:
                st_ref[h] = s0_ref[0, d, h].T
            else:
                st_ref[h] = jnp.zeros((HEAD_D, HEAD_D), F32)

        def chunk_body(c, carry, rev=rev, la_ref=la_ref):
            cc = (n_chunks - 1 - c) if rev else c
            rows = pl.ds(pl.multiple_of(cc * CHUNK, CHUNK), CHUNK)
            for h0 in range(0, N_HEADS, 2):
                ops, vs = [], []
                for idx in range(2):
                    lo, hi = (h0 + idx) * HEAD_D, (h0 + idx + 1) * HEAD_D
                    qc = qkv_ref[rows, lo:hi].astype(F32) * (HEAD_D ** -0.5)
                    kc = qkv_ref[rows, D_GLA + lo:D_GLA + hi].astype(F32)
                    ops.append(_gla_head_operands(qc, kc, la_ref[rows, lo:hi], rev, idx * CHUNK))
                    vs.append(qkv_ref[rows, 2 * D_GLA + lo:2 * D_GLA + hi].astype(F32))
                oc, (st_a, st_b) = _gla_chunk_pair(ops[0], ops[1], vs[0], vs[1], st_ref[h0], st_ref[h0 + 1])
                st_ref[h0] = st_a
                st_ref[h0 + 1] = st_b
                cols = slice(h0 * HEAD_D, (h0 + 2) * HEAD_D)
                if rev:
                    o_ref[rows, cols] += oc
                else:
                    o_ref[rows, cols] = oc
            return carry
        lax.fori_loop(0, n_chunks, chunk_body, 0, unroll=min(n_chunks, 8))

        if emit_state:
            for h in range(N_HEADS):
                st_out_ref[0, d, h] = st_ref[h].T

    out_ref[...] = o_ref[...].astype(BF16)


def _gla_scan(proj, wdec, bdec, s0, *, seq_len, emit_state):
    t = proj.shape[0]
    nb = t // seq_len
    has_s0 = s0 is not None
    const = lambda shape: pl.BlockSpec(shape, lambda b: (0,) * len(shape), pipeline_mode=pl.Buffered(1))
    assert C_Q % (3 * D_GLA) == 0 and C_K == C_Q + D_GLA and C_V == C_K + D_GLA and C_LR % LR_PAD == 0
    in_specs = [pl.BlockSpec((seq_len, 3 * D_GLA), lambda b: (b, C_Q // (3 * D_GLA))),
                pl.BlockSpec((seq_len, LR_PAD), lambda b: (b, C_LR // LR_PAD)),
                const((2, LR_PAD, D_GLA)), const((2, D_GLA))]
    args = [proj, proj, wdec, bdec]
    if has_s0:
        in_specs.append(pl.BlockSpec((1, 2, N_HEADS, HEAD_D, HEAD_D), lambda b: (b, 0, 0, 0, 0)))
        args.append(s0)
    out_specs = [pl.BlockSpec((seq_len, D_GLA), lambda b: (b, 0))]
    out_shape = [jax.ShapeDtypeStruct((t, D_GLA), BF16)]
    if emit_state:
        out_specs.append(pl.BlockSpec((1, 2, N_HEADS, HEAD_D, HEAD_D), lambda b: (b, 0, 0, 0, 0)))
        out_shape.append(jax.ShapeDtypeStruct((nb, 2, N_HEADS, HEAD_D, HEAD_D), F32))
    kern = functools.partial(_scan_kernel, seq_len=seq_len, has_s0=has_s0, emit_state=emit_state)
    return pl.pallas_call(
        kern,
        grid=(nb,),
        in_specs=in_specs,
        out_specs=out_specs,
        out_shape=out_shape,
        scratch_shapes=[pltpu.VMEM((seq_len, D_GLA), F32), pltpu.VMEM((seq_len, D_GLA), F32),
                        pltpu.VMEM((seq_len, D_GLA), F32), pltpu.VMEM((N_HEADS, HEAD_D, HEAD_D), F32)],
        compiler_params=pltpu.CompilerParams(dimension_semantics=("arbitrary",),
                                             vmem_limit_bytes=VMEM_LIMIT),
        name="gla_scan",
    )(*args)


MIX_TM = 1024
MIX_SUB = 256


def _mix_dense_kernel(conv_ref, go_ref, gc0_ref, gc1_ref, gg0_ref, gg1_ref, o_ref, x_ref, mod_ref, cw_ref, cb_ref,
                      gnw_ref, wbc_ref, wbg_ref, wout_ref, out_ref, yc_ref, yg_ref, *, row_len, tiles_per_mod):
    i = pl.program_id(0)
    mod_row = i // tiles_per_mod if mod_ref.shape[0] > 1 else 0
    g1 = mod_ref[pl.ds(mod_row, 1), 2 * D_MODEL:3 * D_MODEL]
    hsel_r = lax.broadcasted_iota(jnp.int32, (D_GLA, D_GLA), 0) // HEAD_D
    hsel_c = lax.broadcasted_iota(jnp.int32, (D_GLA, D_GLA), 1) // HEAD_D
    head_avg = jnp.where(hsel_r == hsel_c, 1.0 / HEAD_D, 0.0).astype(BF16)
    head_avg2 = jnp.concatenate([head_avg, head_avg], axis=0)
    pos = lax.broadcasted_iota(jnp.int32, (MIX_SUB, 1), 0)
    in_row = pos % row_len

    def branch_body(t, carry):
        rows = pl.ds(pl.multiple_of(t * MIX_SUB, MIX_SUB), MIX_SUB)
        cu = conv_ref[rows, 2 * D_CONV:3 * D_CONV].astype(F32) * conv_ref[rows, 0:D_CONV].astype(F32)
        left = jnp.where(in_row == 0, 0.0, pltpu.roll(cu, 1, axis=0))
        right = jnp.where(in_row == row_len - 1, 0.0, pltpu.roll(cu, MIX_SUB - 1, axis=0))
        conv = cw_ref[0:1] * left + cw_ref[1:2] * cu + cw_ref[2:3] * right + cb_ref[...]
        yc_ref[rows, :] = (conv_ref[rows, D_CONV:2 * D_CONV].astype(F32) * conv).astype(BF16)
        o = o_ref[rows, :].astype(F32)
        ms = _dot(jnp.concatenate(_split_bf16(o * o), axis=1), head_avg2)
        g_out = go_ref[rows, :].astype(F32)
        yg_ref[rows, :] = (o * lax.rsqrt(ms + EPS) * gnw_ref[...] * (g_out * _sigmoid(g_out))).astype(BF16)
        return carry
    lax.fori_loop(0, MIX_TM // MIX_SUB, branch_body, 0)

    half = D_MODEL // 2
    pc = _dot(yc_ref[...], wbc_ref[...])
    pg = _dot(yg_ref[...], wbg_ref[...])
    merged = jnp.concatenate(
        [_sigmoid(gc0_ref[...].astype(F32)) * pc[:, :half] + _sigmoid(gg0_ref[...].astype(F32)) * pg[:, :half],
         _sigmoid(gc1_ref[...].astype(F32)) * pc[:, half:] + _sigmoid(gg1_ref[...].astype(F32)) * pg[:, half:]],
        axis=1)
    out_ref[...] = x_ref[...] + g1 * _dot(merged.astype(BF16), wout_ref[...])


def _mix_dense(proj, o, x2d, mod, conv_w, conv_b, gnw, wbc, wbg, wout, *, row_len, tokens_per_mod):
    t = x2d.shape[0]
    assert MIX_SUB % row_len == 0 and t % MIX_TM == 0 and tokens_per_mod % MIX_TM == 0
    const = lambda shape: pl.BlockSpec(shape, lambda i: (0,) * len(shape), pipeline_mode=pl.Buffered(1))
    cols = lambda width, start: pl.BlockSpec((MIX_TM, width), lambda i: (i, start // width))
    half = D_MODEL // 2
    assert C_U == 0 and C_GB == D_CONV and C_GC == 2 * D_CONV and C_GO % D_GLA == 0 and C_BRC % half == 0
    kern = functools.partial(_mix_dense_kernel, row_len=row_len, tiles_per_mod=tokens_per_mod // MIX_TM)
    return pl.pallas_call(
        kern,
        grid=(t // MIX_TM,),
        in_specs=[cols(3 * D_CONV, C_U), cols(D_GLA, C_GO),
                  cols(half, C_BRC), cols(half, C_BRC + half), cols(half, C_BRG), cols(half, C_BRG + half),
                  pl.BlockSpec((MIX_TM, D_GLA), lambda i: (i, 0)), pl.BlockSpec((MIX_TM, D_MODEL), lambda i: (i, 0)),
                  const(mod.shape), const((3, D_CONV)), const((1, D_CONV)), const((1, D_GLA)),
                  const((D_CONV, D_MODEL)), const((D_GLA, D_MODEL)), const((D_MODEL, D_MODEL))],
        out_specs=pl.BlockSpec((MIX_TM, D_MODEL), lambda i: (i, 0)),
        out_shape=jax.ShapeDtypeStruct((t, D_MODEL), F32),
        scratch_shapes=[pltpu.VMEM((MIX_TM, D_CONV), BF16), pltpu.VMEM((MIX_TM, D_GLA), BF16)],
        compiler_params=pltpu.CompilerParams(dimension_semantics=("arbitrary",), vmem_limit_bytes=VMEM_LIMIT),
        name="mix_dense",
    )(proj, proj, proj, proj, proj, proj, o, x2d, mod, conv_w, conv_b, gnw, wbc, wbg, wout)


I32 = jnp.int32
TB = 256
ROW_ALIGN = 16
G_ALIGN = 256
E_CHUNK = 2048
R_LOC = 3072
H2W = 1152


def _select_x(i, n_p_tiles, xp_ref, xs_ref):
    return jnp.where(i < n_p_tiles, xp_ref[...], xs_ref[...])


def _mod_row(i, n_p_tiles, tiles_per_mod):
    return jnp.where(i < n_p_tiles, 0, 1 + (i - n_p_tiles) // tiles_per_mod)


def _route_kernel(xp_ref, xs_ref, mod_ref, n2w_ref, wrt_ref, brb_ref, h2_ref, lpos_ref, cnt_ref, *,
                  n_p_tiles, tiles_per_mod):
    i = pl.program_id(0)
    row = _mod_row(i, n_p_tiles, tiles_per_mod)
    x = _select_x(i, n_p_tiles, xp_ref, xs_ref)
    sh = mod_ref[pl.ds(row, 1), 3 * D_MODEL:4 * D_MODEL]
    sc = mod_ref[pl.ds(row, 1), 4 * D_MODEL:5 * D_MODEL]
    h2 = _rms(x) * n2w_ref[...] * (1.0 + sc) + sh

    h_hi, h_lo = _split_bf16(h2)
    w_hi, w_lo = _split_bf16(wrt_ref[...])
    scores = jax.nn.sigmoid(_dot_nt(jnp.concatenate([w_hi, w_hi, w_lo], axis=1),
                                    jnp.concatenate([h_hi, h_lo, h_hi], axis=1)))
    biased = scores + brb_ref[...]
    eidx = lax.broadcasted_iota(I32, scores.shape, 0)
    picks = []
    for _k in range(TOP_K):
        m = jnp.max(biased, axis=0, keepdims=True)
        first = jnp.min(jnp.where(biased == m, eidx, N_EXPERTS), axis=0, keepdims=True)
        pick = eidx == first
        picks.append(pick)
        biased = jnp.where(pick, -jnp.inf, biased)
    sel = jnp.zeros(scores.shape, F32)
    for pick in picks:
        sel = jnp.where(pick, 1.0, sel)
    selsc = sel * scores
    comb = selsc / jnp.sum(selsc, axis=0, keepdims=True) * ROUTED_SCALE

    selb = sel.astype(BF16)
    tr_ = lax.broadcasted_iota(I32, (TB, TB), 0)
    tc_ = lax.broadcasted_iota(I32, (TB, TB), 1)
    rank_n = _dot(selb, jnp.concatenate([jnp.where(tr_ < tc_, 1.0, 0.0).astype(BF16), jnp.ones((TB, 128), BF16)],
                                        axis=1))
    rank, n_b = rank_n[:, :TB], rank_n[:, TB:]
    m_b = jnp.maximum(jnp.floor((n_b + (ROW_ALIGN - 1)) * (1.0 / ROW_ALIGN)), 1.0) * ROW_ALIGN
    er_ = lax.broadcasted_iota(I32, (N_EXPERTS, N_EXPERTS), 0)
    ec_ = lax.broadcasted_iota(I32, (N_EXPERTS, N_EXPERTS), 1)
    loff_b = _dot(jnp.where(ec_ < er_, 1.0, 0.0).astype(BF16), m_b.astype(BF16))
    lposf = jnp.concatenate([loff_b] * (TB // 128), axis=1) + rank
    rows = [jnp.sum(jnp.where(pick, lposf, 0.0), axis=0, keepdims=True) for pick in picks]
    lpos_ref[0] = jnp.concatenate(rows, axis=0).astype(I32)
    cnt_ref[0] = m_b

    combt = comb.T
    chi = combt.astype(BF16).astype(F32)
    h2_ref[:, 0:D_MODEL] = h_hi
    h2_ref[:, D_MODEL:H2W] = jnp.concatenate([chi, combt - chi], axis=1).astype(BF16)


def _route(x1p, x1s, mod, n2w, w_router_t, b_router_b, *, tiles_per_mod):
    n_p, n_s = x1p.shape[0] // TB, x1s.shape[0] // TB
    nt = n_p + n_s
    kern = functools.partial(_route_kernel, n_p_tiles=n_p, tiles_per_mod=tiles_per_mod)
    const = lambda shape: pl.BlockSpec(shape, lambda i: (0,) * len(shape))
    return pl.pallas_call(
        kern,
        grid=(nt,),
        in_specs=[pl.BlockSpec((TB, D_MODEL), lambda i: (jnp.minimum(i, n_p - 1), 0)),
                  pl.BlockSpec((TB, D_MODEL), lambda i: (jnp.maximum(i - n_p, 0), 0)),
                  const(mod.shape), const((1, D_MODEL)), const((N_EXPERTS, D_MODEL)), const((N_EXPERTS, TB))],
        out_specs=[pl.BlockSpec((TB, H2W), lambda i: (i, 0)),
                   pl.BlockSpec((1, TOP_K, TB), lambda i: (i, 0, 0)),
                   pl.BlockSpec((1, N_EXPERTS, 128), lambda i: (i, 0, 0))],
        out_shape=[jax.ShapeDtypeStruct((nt * TB, H2W), BF16),
                   jax.ShapeDtypeStruct((nt, TOP_K, TB), I32),
                   jax.ShapeDtypeStruct((nt, N_EXPERTS, 128), F32)],
        compiler_params=pltpu.CompilerParams(dimension_semantics=("arbitrary",), vmem_limit_bytes=VMEM_LIMIT),
        name="moe_route",
    )(x1p, x1s, mod, n2w, w_router_t, b_router_b)


def _plan_kernel(cnt_ref, off_ref, loff_ref, msz_ref, grp_ref, *, nt):
    lane = lax.broadcasted_iota(I32, (N_EXPERTS, 128), 1)
    m = jnp.zeros((N_EXPERTS, 128), F32)
    for i in range(nt):
        m = jnp.where(lane == i, cnt_ref[i], m)
    total = jnp.broadcast_to(jnp.sum(m, axis=1, keepdims=True), (N_EXPERTS, 128))
    gsz = jnp.floor((total + (G_ALIGN - 1)) * (1.0 / G_ALIGN)) * G_ALIGN
    er_ = lax.broadcasted_iota(I32, (N_EXPERTS, N_EXPERTS), 0)
    ec_ = lax.broadcasted_iota(I32, (N_EXPERTS, N_EXPERTS), 1)
    lstrict = jnp.where(ec_ < er_, 1.0, 0.0)
    ir_ = lax.broadcasted_iota(I32, (128, 128), 0)
    ic_ = lax.broadcasted_iota(I32, (128, 128), 1)
    ustrict = jnp.where(ir_ < ic_, 1.0, 0.0)
    gstart = _dot_hi(lstrict, gsz)
    off_ref[...] = (gstart + _dot_hi(m, ustrict)).astype(I32)
    loff_ref[...] = _dot_hi(lstrict, m).astype(I32)
    msz_ref[...] = m.astype(I32)
    grp = jnp.where(lane == 0, gstart + total, jnp.where(lane == 1, gsz - total, jnp.where(lane == 2, gstart, gsz)))
    grp_ref[...] = grp.astype(I32)


def _plan(cnt):
    nt = cnt.shape[0]
    assert nt <= 128
    tab = jax.ShapeDtypeStruct((N_EXPERTS, 128), I32)
    return pl.pallas_call(
        functools.partial(_plan_kernel, nt=nt),
        out_shape=[tab, tab, tab, tab],
        compiler_params=pltpu.CompilerParams(vmem_limit_bytes=VMEM_LIMIT),
        name="moe_plan",
    )(cnt)


def _start_copies(msz_ref, tile, make_copy):
    for e in range(N_EXPERTS):
        make_copy(e, pl.multiple_of(msz_ref[e, tile], ROW_ALIGN)).start()


def _tile_rows(loff_ref, msz_ref, tile):
    return pl.multiple_of(loff_ref[N_EXPERTS - 1, tile] + msz_ref[N_EXPERTS - 1, tile], ROW_ALIGN)


def _dispatch_kernel(off_ref, loff_ref, msz_ref, tail_ref, h2_ref, lpos_ref, xs_hbm, xloc_ref, zero_ref, sem,
                     tail_sem, *, nt):
    i = pl.program_id(0)
    slot = i % 2

    def copy_for(tile, slot_):
        def make(e, m):
            lo = pl.multiple_of(loff_ref[e, tile], ROW_ALIGN)
            of = pl.multiple_of(off_ref[e, tile], ROW_ALIGN)
            return pltpu.make_async_copy(xloc_ref.at[slot_, pl.ds(lo, m)], xs_hbm.at[pl.ds(of, m)], sem.at[slot_])
        return make

    def wait_tile(tile, slot_):
        n = _tile_rows(loff_ref, msz_ref, tile)
        pltpu.make_async_copy(xloc_ref.at[slot_, pl.ds(0, n)], xs_hbm.at[pl.ds(0, n)], sem.at[slot_]).wait()

    lpos = lpos_ref[0].astype(jnp.int16)
    h2 = h2_ref[...]
    ck = 1024
    one, zero = jnp.ones((ck, TB), BF16), jnp.zeros((ck, TB), BF16)
    for c in range(R_LOC // ck):
        r = (lax.broadcasted_iota(I32, (ck, TB), 0) + c * ck).astype(jnp.int16)
        d = zero
        for k in range(TOP_K):
            d = jnp.where(r == lpos[k:k + 1, :], one, d)
        res = _dot(d, h2)
        xloc_ref[slot, c * ck:(c + 1) * ck, :] = res.astype(BF16)

    _start_copies(msz_ref, i, copy_for(i, slot))

    @pl.when(i > 0)
    def _():
        wait_tile(i - 1, 1 - slot)

    @pl.when(i == nt - 1)
    def _():
        zero_ref[...] = jnp.zeros(zero_ref.shape, BF16)

        def tail_copies(start):
            def body(e, carry):
                n = tail_ref[e, 1]

                @pl.when(n > 0)
                def _():
                    st = pl.multiple_of(tail_ref[e, 0], ROW_ALIGN)
                    nn = pl.multiple_of(n, ROW_ALIGN)
                    cp = pltpu.make_async_copy(zero_ref.at[pl.ds(0, nn)], xs_hbm.at[pl.ds(st, nn)], tail_sem)
                    if start:
                        cp.start()
                    else:
                        cp.wait()
                return carry
            lax.fori_loop(0, N_EXPERTS, body, 0)
        tail_copies(True)
        wait_tile(i, slot)
        tail_copies(False)


def _dispatch(off, loff, msz, tail, h2ext, lpos, n_rows):
    nt = lpos.shape[0]
    grid_spec = pltpu.PrefetchScalarGridSpec(
        num_scalar_prefetch=4,
        grid=(nt,),
        in_specs=[pl.BlockSpec((TB, H2W), lambda i, *_: (i, 0)),
                  pl.BlockSpec((1, TOP_K, TB), lambda i, *_: (i, 0, 0))],
        out_specs=pl.BlockSpec(memory_space=pl.ANY),
        scratch_shapes=[pltpu.VMEM((2, R_LOC, H2W), BF16), pltpu.VMEM((G_ALIGN, H2W), BF16),
                        pltpu.SemaphoreType.DMA((2,)), pltpu.SemaphoreType.DMA],
    )
    return pl.pallas_call(
        functools.partial(_dispatch_kernel, nt=nt),
        grid_spec=grid_spec,
        out_shape=jax.ShapeDtypeStruct((n_rows, H2W), BF16),
        compiler_params=pltpu.CompilerParams(dimension_semantics=("arbitrary",), vmem_limit_bytes=VMEM_LIMIT),
        name="moe_dispatch",
    )(off, loff, msz, tail, h2ext, lpos)


def _expert_kernel(grp_ref, wg_ref, wu_ref, wd_ref, xs_hbm, ys_hbm, xbuf, ybuf, wgu_ref, wdb_ref, st_ref,
                   in_sem, out_sem):
    e = pl.program_id(0)
    n_exp = pl.num_programs(0)

    def in_copy(row0, n, slot):
        return pltpu.make_async_copy(xs_hbm.at[pl.ds(row0, n)], xbuf.at[slot, pl.ds(0, n)], in_sem.at[slot])

    def out_copy(row0, n, slot):
        return pltpu.make_async_copy(ybuf.at[slot, pl.ds(0, n)], ys_hbm.at[pl.ds(row0, n)], out_sem.at[slot])

    def rows_of(ex):
        return grp_ref[jnp.minimum(ex, n_exp - 1), 3]

    def next_nonempty(ex):
        return lax.while_loop(lambda c: (c < n_exp) & (rows_of(c) == 0), lambda c: c + 1, ex + 1)

    def start_first_chunk(ex, slot):
        @pl.when(ex < n_exp)
        def _():
            exc = jnp.minimum(ex, n_exp - 1)
            n = pl.multiple_of(jnp.minimum(grp_ref[exc, 3], E_CHUNK), G_ALIGN)
            in_copy(pl.multiple_of(grp_ref[exc, 2], G_ALIGN), n, slot).start()

    def drain_out(slot):
        pend = st_ref[1 + slot]

        @pl.when(pend > 0)
        def _():
            out_copy(0, pl.multiple_of(pend, G_ALIGN), slot).wait()
            st_ref[1 + slot] = 0

    @pl.when(e == 0)
    def _():
        st_ref[0] = 0
        st_ref[1] = 0
        st_ref[2] = 0
        start_first_chunk(next_nonempty(-1), 0)

    g0 = grp_ref[e, 2]
    gn = grp_ref[e, 3]

    @pl.when(gn > 0)
    def _():
        wgu_ref[:, :D_EXPERT] = wg_ref[0].astype(BF16)
        wgu_ref[:, D_EXPERT:] = wu_ref[0].astype(BF16)
        wdb_ref[...] = wd_ref[0].astype(BF16)
        n_chunks = (gn + (E_CHUNK - 1)) // E_CHUNK

        def compute(n, slot):
            x = xbuf[slot, 0:n, 0:D_MODEL]
            ext = xbuf[slot, 0:n, D_MODEL:H2W].astype(F32)
            wts = ext[:, :N_EXPERTS] + ext[:, N_EXPERTS:]
            lane = lax.broadcasted_iota(I32, wts.shape, 1)
            w = jnp.sum(jnp.where(lane == e, wts, 0.0), axis=-1, keepdims=True)
            h = _dot(x, wgu_ref[...])
            hg, hu = h[:, :D_EXPERT], h[:, D_EXPERT:]
            act = hg * _sigmoid(hg) * hu * w
            ybuf[slot, 0:n, :] = _dot(act.astype(BF16), wdb_ref[...]).astype(BF16)

        def chunk_body(c, slot):
            row0 = pl.multiple_of(g0 + c * E_CHUNK, G_ALIGN)
            n = pl.multiple_of(jnp.minimum(gn - c * E_CHUNK, E_CHUNK), G_ALIGN)
            in_copy(row0, n, slot).wait()

            @pl.when(c + 1 < n_chunks)
            def _():
                n1 = pl.multiple_of(jnp.minimum(gn - (c + 1) * E_CHUNK, E_CHUNK), G_ALIGN)
                in_copy(pl.multiple_of(row0 + E_CHUNK, G_ALIGN), n1, 1 - slot).start()

            @pl.when(c + 1 == n_chunks)
            def _():
                start_first_chunk(next_nonempty(e), 1 - slot)

            drain_out(slot)
            for v in range(G_ALIGN, E_CHUNK + 1, G_ALIGN):
                @pl.when(n == v)
                def _(v=v):
                    compute(v, slot)
            out_copy(row0, n, slot).start()
            st_ref[1 + slot] = n
            return 1 - slot

        st_ref[0] = lax.fori_loop(0, n_chunks, chunk_body, st_ref[0])

    @pl.when(e == n_exp - 1)
    def _():
        drain_out(0)
        drain_out(1)


def _experts(grp, xs, wg, wu, wd):
    w_in = pl.BlockSpec((1, D_MODEL, D_EXPERT), lambda e, grp_ref: (e, 0, 0))
    grid_spec = pltpu.PrefetchScalarGridSpec(
        num_scalar_prefetch=1,
        grid=(N_EXPERTS,),
        in_specs=[w_in, w_in, pl.BlockSpec((1, D_EXPERT, D_MODEL), lambda e, grp_ref: (e, 0, 0)),
                  pl.BlockSpec(memory_space=pl.ANY)],
        out_specs=pl.BlockSpec(memory_space=pl.ANY),
        scratch_shapes=[pltpu.VMEM((2, E_CHUNK, H2W), BF16), pltpu.VMEM((2, E_CHUNK, D_MODEL), BF16),
                        pltpu.VMEM((D_MODEL, 2 * D_EXPERT), BF16), pltpu.VMEM((D_EXPERT, D_MODEL), BF16),
                        pltpu.SMEM((4,), I32), pltpu.SemaphoreType.DMA((2,)), pltpu.SemaphoreType.DMA((2,))],
    )
    return pl.pallas_call(
        _expert_kernel,
        grid_spec=grid_spec,
        out_shape=jax.ShapeDtypeStruct((xs.shape[0], D_MODEL), BF16),
        compiler_params=pltpu.CompilerParams(dimension_semantics=("arbitrary",), vmem_limit_bytes=VMEM_LIMIT),
        name="moe_experts",
    )(grp, wg, wu, wd, xs)


def _combine_kernel(off_ref, loff_ref, msz_ref, lpos_ref, h2_ref, xp_ref, xs_ref, mod_ref, wgs_ref, wus_ref, wds_ref,
                    fnw_ref, ysrt_hbm, yp_ref, ys_ref, yloc_ref, acc_ref, sem, *, nt, n_p_tiles, tiles_per_mod,
                    final):
    i = pl.program_id(0)
    slot = i % 2

    def copy_for(tile, slot_):
        def make(e, m):
            lo = pl.multiple_of(loff_ref[e, tile], ROW_ALIGN)
            of = pl.multiple_of(off_ref[e, tile], ROW_ALIGN)
            return pltpu.make_async_copy(ysrt_hbm.at[pl.ds(of, m)], yloc_ref.at[slot_, pl.ds(lo, m)], sem.at[slot_])
        return make

    @pl.when(i == 0)
    def _():
        yloc_ref[...] = jnp.zeros(yloc_ref.shape, BF16)
        _start_copies(msz_ref, 0, copy_for(0, 0))

    nxt = jnp.minimum(i + 1, nt - 1)
    _start_copies(msz_ref, nxt, copy_for(nxt, 1 - slot))

    hb = h2_ref[...]
    hg = _dot(hb, wgs_ref[...].astype(BF16))
    hu = _dot(hb, wus_ref[...].astype(BF16))
    acc_ref[...] = _dot((hg * _sigmoid(hg) * hu).astype(BF16), wds_ref[...].astype(BF16))

    def wait_tile(tile, slot_):
        n = _tile_rows(loff_ref, msz_ref, tile)
        pltpu.make_async_copy(ysrt_hbm.at[pl.ds(0, n)], yloc_ref.at[slot_, pl.ds(0, n)], sem.at[slot_]).wait()

    wait_tile(i, slot)

    @pl.when(i == nt - 1)
    def _():
        wait_tile(i, 1 - slot)

    lpos_pad = jnp.concatenate([lpos_ref[0].astype(F32), jnp.zeros((128 - TOP_K, TB), F32)], axis=0)
    lposc = lpos_pad.T.astype(I32)
    ck = 512
    cols = [jnp.broadcast_to(lposc[:, k:k + 1], (TB, ck)).astype(jnp.int16) for k in range(TOP_K)]
    one, zero = jnp.ones((TB, ck), BF16), jnp.zeros((TB, ck), BF16)
    for c in range(R_LOC // ck):
        r = (lax.broadcasted_iota(I32, (TB, ck), 1) + c * ck).astype(jnp.int16)
        cm = zero
        for k in range(TOP_K):
            cm = jnp.where(r == cols[k], one, cm)
        acc_ref[...] += _dot(cm, yloc_ref[slot, c * ck:(c + 1) * ck, :])

    row = _mod_row(i, n_p_tiles, tiles_per_mod)
    g2 = mod_ref[pl.ds(row, 1), 5 * D_MODEL:6 * D_MODEL]
    x2 = _select_x(i, n_p_tiles, xp_ref, xs_ref) + g2 * acc_ref[...]
    y = _rms(x2) * fnw_ref[...] if final else x2

    @pl.when(i < n_p_tiles)
    def _():
        yp_ref[...] = y

    @pl.when(i >= n_p_tiles)
    def _():
        ys_ref[...] = y


def _combine(off, loff, msz, lpos, h2ext, x1p, x1s, mod, wgs, wus, wds, fnw, ysorted, *, tiles_per_mod, final):
    n_p, n_s = x1p.shape[0] // TB, x1s.shape[0] // TB
    nt = n_p + n_s
    const = lambda shape: pl.BlockSpec(shape, lambda i, *_: (0,) * len(shape), pipeline_mode=pl.Buffered(1))
    p_idx = lambda i, *_: (jnp.minimum(i, n_p - 1), 0)
    s_idx = lambda i, *_: (jnp.maximum(i - n_p, 0), 0)
    grid_spec = pltpu.PrefetchScalarGridSpec(
        num_scalar_prefetch=3,
        grid=(nt,),
        in_specs=[pl.BlockSpec((1, TOP_K, TB), lambda i, *_: (i, 0, 0)),
                  pl.BlockSpec((TB, D_MODEL), lambda i, *_: (i, 0)),
                  pl.BlockSpec((TB, D_MODEL), p_idx), pl.BlockSpec((TB, D_MODEL), s_idx),
                  const(mod.shape), const((D_MODEL, D_EXPERT)), const((D_MODEL, D_EXPERT)), const((D_EXPERT, D_MODEL)),
                  const((1, D_MODEL)), pl.BlockSpec(memory_space=pl.ANY)],
        out_specs=[pl.BlockSpec((TB, D_MODEL), p_idx), pl.BlockSpec((TB, D_MODEL), s_idx)],
        scratch_shapes=[pltpu.VMEM((2, R_LOC, D_MODEL), BF16), pltpu.VMEM((TB, D_MODEL), F32),
                        pltpu.SemaphoreType.DMA((2,))],
    )
    kern = functools.partial(_combine_kernel, nt=nt, n_p_tiles=n_p, tiles_per_mod=tiles_per_mod, final=final)
    return pl.pallas_call(
        kern,
        grid_spec=grid_spec,
        out_shape=[jax.ShapeDtypeStruct(x1p.shape, F32), jax.ShapeDtypeStruct(x1s.shape, F32)],
        compiler_params=pltpu.CompilerParams(dimension_semantics=("arbitrary",), vmem_limit_bytes=VMEM_LIMIT),
        name="moe_combine",
    )(off, loff, msz, lpos, h2ext, x1p, x1s, mod, wgs, wus, wds, fnw, ysorted)


def _moe(x1p, x1s, mod, n2w, w_router, b_router, wg, wu, wd, wgs, wus, wds, fnw, *, tokens_per_mod, final):
    assert R_LOC >= TB * TOP_K + N_EXPERTS * ROW_ALIGN and tokens_per_mod % TB == 0
    nt = (x1p.shape[0] + x1s.shape[0]) // TB
    n_rows_max = nt * TB * TOP_K + nt * N_EXPERTS * ROW_ALIGN + N_EXPERTS * (G_ALIGN - ROW_ALIGN)
    tiles_per_mod = tokens_per_mod // TB
    brb = jnp.broadcast_to(b_router.reshape(N_EXPERTS, 1), (N_EXPERTS, TB))
    h2ext, lpos, cnt = _route(x1p, x1s, mod, n2w, w_router.T, brb, tiles_per_mod=tiles_per_mod)
    off, loff, msz, grp = _plan(cnt)
    xs = _dispatch(off, loff, msz, grp, h2ext, lpos, n_rows_max)
    ysorted = _experts(grp, xs, wg, wu, wd)
    return _combine(off, loff, msz, lpos, h2ext, x1p, x1s, mod, wgs, wus, wds, fnw, ysorted,
                    tiles_per_mod=tiles_per_mod, final=final)


def kernel(x_prompt, x_sample, state_gla, c, c_ctx, w_mod, b_mod, norm1_w, w_in, conv_w, conv_b, w_decay, b_decay,
           gla_norm_w, w_br_conv, w_br_gla, w_out, norm2_w, w_router, b_router, w_gate_e, w_up_e, w_down_e,
           w_gate_s, w_up_s, w_down_s, final_norm_w):
    depth = w_mod.shape[0]
    nb_p, len_p, _ = x_prompt.shape
    nb_s, len_s, _ = x_sample.shape
    yp = x_prompt.reshape(nb_p * len_p, D_MODEL)
    ys = x_sample.reshape(nb_s * len_s, D_MODEL)
    fnw = final_norm_w.reshape(1, D_MODEL)

    cond = jnp.concatenate([c_ctx[None, :], c, jnp.zeros((8 - 1 - nb_s, D_MODEL), F32)], axis=0)
    states = []
    for l in range(depth):
        mod = _modulation(cond, w_mod[l], b_mod[l].reshape(1, -1))
        mod_p, mod_s = mod[0:1], mod[1:1 + nb_s]

        w_in_r = _w_in_prep(w_in[l].T)
        wdec = jnp.zeros((2, LR_PAD, D_GLA), F32)
        wdec = wdec.at[0, 0:GLA_RANK].set(w_decay[l, 0]).at[1, GLA_RANK:2 * GLA_RANK].set(w_decay[l, 1])
        n1w = norm1_w[l].reshape(1, D_MODEL)
        mix_w = (conv_w[l], conv_b[l].reshape(1, D_CONV), gla_norm_w[l].reshape(1, D_GLA),
                 w_br_conv[l].astype(BF16), w_br_gla[l].astype(BF16), w_out[l].astype(BF16))
        moe_w = (norm2_w[l].reshape(1, D_MODEL), w_router[l], b_router[l],
                 w_gate_e[l], w_up_e[l], w_down_e[l], w_gate_s[l], w_up_s[l], w_down_s[l])

        proj_p = _in_proj(yp, mod_p, n1w, w_in_r, rows_per_mod=nb_p * len_p)
        o_p, st = _gla_scan(proj_p, wdec, b_decay[l], None, seq_len=len_p, emit_state=True)
        yp = _mix_dense(proj_p, o_p, yp, mod_p, *mix_w, row_len=len_p, tokens_per_mod=nb_p * len_p)
        states.append(st)
        proj_s = _in_proj(ys, mod_s, n1w, w_in_r, rows_per_mod=len_s)
        (o_s,) = _gla_scan(proj_s, wdec, b_decay[l], state_gla[:, l], seq_len=len_s, emit_state=False)
        ys = _mix_dense(proj_s, o_s, ys, mod_s, *mix_w, row_len=GRID_W, tokens_per_mod=len_s)

        yp, ys = _moe(yp, ys, mod, *moe_w, fnw, tokens_per_mod=len_s, final=l == depth - 1)
    new_state = jnp.stack(states, axis=1)
    return (yp.reshape(nb_p, len_p, D_MODEL), ys.reshape(nb_s, len_s, D_MODEL), new_state)
```

```python
import functools

import jax
import jax.numpy as jnp
from jax import lax
from jax.experimental import pallas as pl
from jax.experimental.pallas import tpu as pltpu

F32 = jnp.float32
BF16 = jnp.bfloat16

D_MODEL = 1024
GRID_W = 64
D_CONV = 512
N_HEADS = 4
HEAD_D = 128
D_GLA = N_HEADS * HEAD_D
GLA_RANK = 16
GLA_GATE_NORM = 16.0
LOG2_E = 1.4426950408889634
CHUNK = 64
SUB = 8
N_SUB = CHUNK // SUB
N_EXPERTS = 64
TOP_K = 8
D_EXPERT = 256
ROUTED_SCALE = 2.5
EPS = 1e-6

C_U, C_GB, C_GC, C_Q, C_K, C_V, C_GO = 0, 512, 1024, 1536, 2048, 2560, 3072
C_BRC, C_BRG, C_LR = 3584, 4608, 5632
D_PROJ = 5760
LR_PAD = 128

VMEM_LIMIT = 56 * 1024 * 1024


def _dot(a, b):
    return jnp.dot(a, b, preferred_element_type=F32)


def _dot_nt(a, b):
    return lax.dot_general(a, b, (((1,), (1,)), ((), ())), preferred_element_type=F32)


def _dot_hi(a, b):
    return jnp.dot(a, b, preferred_element_type=F32, precision=lax.Precision.HIGHEST)


def _split_bf16(x):
    hi = x.astype(BF16)
    lo = (x - hi.astype(F32)).astype(BF16)
    return hi, lo


def _rms(x):
    return x * lax.rsqrt(jnp.mean(x * x, axis=-1, keepdims=True) + EPS)


def _sigmoid(x):
    return 0.5 * jnp.tanh(0.5 * x) + 0.5


def _mod_kernel(cond_ref, w_ref, b_ref, o_ref):
    c = cond_ref[...]
    c_hi, c_lo = _split_bf16(c * jax.nn.sigmoid(c))
    w_hi, w_lo = _split_bf16(w_ref[...])
    o_ref[...] = _dot(jnp.concatenate([c_hi, c_hi, c_lo], axis=1),
                      jnp.concatenate([w_hi, w_lo, w_hi], axis=0)) + b_ref[...]


def _modulation(cond, w_mod, b_mod):
    n_rows = cond.shape[0]
    tn = 1536
    return pl.pallas_call(
        _mod_kernel,
        grid=(6 * D_MODEL // tn,),
        in_specs=[pl.BlockSpec((n_rows, D_MODEL), lambda j: (0, 0)),
                  pl.BlockSpec((D_MODEL, tn), lambda j: (0, j)),
                  pl.BlockSpec((1, tn), lambda j: (0, j))],
        out_specs=pl.BlockSpec((n_rows, tn), lambda j: (0, j)),
        out_shape=jax.ShapeDtypeStruct((n_rows, 6 * D_MODEL), F32),
        compiler_params=pltpu.CompilerParams(dimension_semantics=("arbitrary",),
                                             vmem_limit_bytes=VMEM_LIMIT),
        name="modulation",
    )(cond, w_mod, b_mod)


D_IN_PROJ = 5664
C_LR_SRC, C_GATES_SRC = 3584, 3616


def _w_in_prep_kernel(w_ref, o_ref):
    rc = 64

    def copy_rows(dst0, src0, n):
        def body(t, carry):
            off = pl.multiple_of(t * rc, rc)
            o_ref[pl.ds(dst0 + off, rc), :] = w_ref[pl.ds(src0 + off, rc), :].astype(BF16)
            return carry
        lax.fori_loop(0, n // rc, body, 0)

    copy_rows(0, 0, C_LR_SRC)
    copy_rows(C_BRC, C_GATES_SRC, 2 * D_MODEL)
    o_ref[C_LR:C_LR + 2 * GLA_RANK, :] = w_ref[C_LR_SRC:C_GATES_SRC, :].astype(BF16)
    o_ref[C_LR + 2 * GLA_RANK:D_PROJ, :] = jnp.zeros((D_PROJ - C_LR - 2 * GLA_RANK, D_MODEL), BF16)


def _w_in_prep(w_in_t):
    return pl.pallas_call(
        _w_in_prep_kernel,
        out_shape=jax.ShapeDtypeStruct((D_PROJ, D_MODEL), BF16),
        compiler_params=pltpu.CompilerParams(vmem_limit_bytes=VMEM_LIMIT),
        name="w_in_prep",
    )(w_in_t)


def _inproj_kernel(x_ref, mod_ref, nw_ref, w_ref, o_ref, *, rows_per_mod, tm):
    i = pl.program_id(1)
    row = (i * tm) // rows_per_mod
    sh = mod_ref[pl.ds(row, 1), 0:D_MODEL]
    sc = mod_ref[pl.ds(row, 1), D_MODEL:2 * D_MODEL]
    h = _rms(x_ref[...]) * nw_ref[...] * (1.0 + sc) + sh
    o_ref[...] = _dot_nt(h.astype(BF16), w_ref[...]).astype(BF16)


def _in_proj(x2d, mod, norm_w, w_in_r, rows_per_mod):
    t = x2d.shape[0]
    tm, tn = 1024, 1920
    kern = functools.partial(_inproj_kernel, rows_per_mod=rows_per_mod, tm=tm)
    return pl.pallas_call(
        kern,
        grid=(D_PROJ // tn, t // tm),
        in_specs=[pl.BlockSpec((tm, D_MODEL), lambda j, i: (i, 0)),
                  pl.BlockSpec(mod.shape, lambda j, i: (0, 0)),
                  pl.BlockSpec((1, D_MODEL), lambda j, i: (0, 0)),
                  pl.BlockSpec((tn, D_MODEL), lambda j, i: (j, 0))],
        out_specs=pl.BlockSpec((tm, tn), lambda j, i: (i, j)),
        out_shape=jax.ShapeDtypeStruct((t, D_PROJ), BF16),
        compiler_params=pltpu.CompilerParams(dimension_semantics=("arbitrary", "arbitrary"),
                                             vmem_limit_bytes=VMEM_LIMIT),
        name="in_proj",
    )(x2d, mod, norm_w, w_in_r)


def _log2_sigmoid(x):
    return jnp.minimum(x, 0.0) * LOG2_E - jnp.log2(1.0 + jnp.exp2(jnp.abs(x) * (-LOG2_E)))


def _gla_head_operands(qc, kc, bc, rev, lane0):
    lane = lax.broadcasted_iota(jnp.int32, (SUB, HEAD_D), 1) - lane0
    sub = lax.broadcasted_iota(jnp.int32, (SUB, HEAD_D), 0)
    tot = bc[0:1] if rev else bc[CHUNK - 1:CHUNK]
    q_in = (qc * jnp.exp2(bc)).astype(BF16)
    k_tail = (kc * jnp.exp2(tot - bc)).astype(BF16)

    lhs_segs, rhs_segs = [], []

    def rows(before, mid, after):
        parts = ([jnp.zeros((before, HEAD_D), F32)] if before else []) + [mid]
        parts += [jnp.zeros((after, HEAD_D), F32)] if after else []
        return jnp.concatenate(parts, axis=0) if len(parts) > 1 else mid

    key_blocks = range(1, N_SUB) if rev else range(0, N_SUB - 1)
    for jb in key_blocks:
        r0 = jb * SUB
        ref_row = bc[r0:r0 + 1] if rev else bc[r0 + SUB - 1:r0 + SUB]
        ke = kc[r0:r0 + SUB] * jnp.exp2(ref_row - bc[r0:r0 + SUB])
        rhs_segs.append(rows(r0, ke, CHUNK - r0 - SUB))
        if rev:
            ql = qc[:r0] * jnp.exp2(bc[:r0] - ref_row)
            lhs_segs.append(rows(0, ql, CHUNK - r0))
        else:
            ql = qc[r0 + SUB:] * jnp.exp2(bc[r0 + SUB:] - ref_row)
            lhs_segs.append(rows(r0 + SUB, ql, 0))
    lhs = jnp.concatenate(lhs_segs, axis=1).astype(BF16)
    rhs = jnp.concatenate(rhs_segs, axis=1).astype(BF16)

    bit = [(lane & (1 << t)) != 0 for t in range(3)]
    blocks = []
    for ib in range(N_SUB):
        r0 = ib * SUB
        qi, bi = qc[r0:r0 + SUB], bc[r0:r0 + SUB]
        cols = []
        for jj in range(SUB):
            j = r0 + jj
            e = jnp.exp2(bi - bc[j:j + 1])
            cols.append(jnp.sum(qi * (kc[j:j + 1] * e), axis=-1, keepdims=True))
        for t in range(3):
            cols = [jnp.where(bit[t], cols[i + 1], cols[i]) for i in range(0, len(cols), 2)]
        rel = lane - r0
        keep = ((rel >= sub) & (rel < SUB)) if rev else ((rel <= sub) & (rel >= 0))
        blocks.append(jnp.where(keep, cols[0], 0.0))
    return q_in, k_tail, jnp.exp2(tot), lhs, rhs, jnp.concatenate(blocks, axis=0)


def _gla_chunk_pair(ops_a, ops_b, v_a, v_b, st_a, st_b):
    qa, ka, da, lhs_a, rhs_a, near_a = ops_a
    qb, kb, db, lhs_b, rhs_b, near_b = ops_b
    zr = jnp.zeros(rhs_a.shape, BF16)
    far = _dot_nt(jnp.concatenate([lhs_a, lhs_b], axis=1),
                  jnp.concatenate([jnp.concatenate([rhs_a, zr], axis=1),
                                   jnp.concatenate([zr, rhs_b], axis=1)], axis=0))
    scores = (far + near_a + near_b).astype(BF16)
    zv = jnp.zeros(v_a.shape, F32)
    vt = jnp.concatenate([jnp.concatenate([v_a, zv], axis=1),
                          jnp.concatenate([zv, v_b], axis=1)], axis=0).T.astype(BF16)
    zs = jnp.zeros((HEAD_D, HEAD_D), BF16)
    o = _dot_nt(jnp.concatenate([qa, scores, qb], axis=1),
                jnp.concatenate([jnp.concatenate([st_a.astype(BF16), zs], axis=0), vt,
                                 jnp.concatenate([zs, st_b.astype(BF16)], axis=0)], axis=1))
    ut = _dot(vt, jnp.concatenate([ka, kb], axis=0))
    return o, (st_a * da + ut[:HEAD_D], st_b * db + ut[HEAD_D:])


def _scan_kernel(*refs, seq_len, has_s0, emit_state):
    it = iter(refs)
    qkv_ref, lr_ref, wdec_ref, bdec_ref = (next(it) for _ in range(4))
    s0_ref = next(it) if has_s0 else None
    out_ref = next(it)
    st_out_ref = next(it) if emit_state else None
    la_f_ref, la_b_ref, o_ref, st_ref = (next(it) for _ in range(4))

    L = seq_len
    n_chunks = L // CHUNK
    tr = 256
    assert L % tr == 0

    ri = lax.broadcasted_iota(jnp.int32, (tr, tr), 0)
    ci = lax.broadcasted_iota(jnp.int32, (tr, tr), 1)
    same_chunk = (ri // CHUNK) == (ci // CHUNK)

    def decay_body(t, carry):
        r0 = pl.multiple_of(t * tr, tr)
        lr = lr_ref[pl.ds(r0, tr), :]
        lr2 = jnp.concatenate([lr, lr], axis=1)
        for d, ref in ((0, la_f_ref), (1, la_b_ref)):
            z = _dot(lr2, jnp.concatenate(_split_bf16(wdec_ref[d]), axis=0)) + bdec_ref[d:d + 1]
            la = jnp.concatenate(_split_bf16(_log2_sigmoid(z) * (1.0 / GLA_GATE_NORM)), axis=0)
            tri = jnp.where(same_chunk & ((ci >= ri) if d else (ci <= ri)), 1.0, 0.0).astype(BF16)
            ref[pl.ds(r0, tr), :] = _dot(jnp.concatenate([tri, tri], axis=1), la)
        return carry
    lax.fori_loop(0, L // tr, decay_body, 0)

    for rev, la_ref in ((False, la_f_ref), (True, la_b_ref)):
        d = 1 if rev else 0
        for h in range(N_HEADS):
            if has_s0:
                st_ref[h] = s0_ref[0, d, h].T
            else:
                st_ref[h] = jnp.zeros((HEAD_D, HEAD_D), F32)

        def chunk_body(c, carry, rev=rev, la_ref=la_ref):
            cc = (n_chunks - 1 - c) if rev else c
            rows = pl.ds(pl.multiple_of(cc * CHUNK, CHUNK), CHUNK)
            for h0 in range(0, N_HEADS, 2):
                ops, vs = [], []
                for idx in range(2):
                    lo, hi = (h0 + idx) * HEAD_D, (h0 + idx + 1) * HEAD_D
                    qc = qkv_ref[rows, lo:hi].astype(F32) * (HEAD_D ** -0.5)
                    kc = qkv_ref[rows, D_GLA + lo:D_GLA + hi].astype(F32)
                    ops.append(_gla_head_operands(qc, kc, la_ref[rows, lo:hi], rev, idx * CHUNK))
                    vs.append(qkv_ref[rows, 2 * D_GLA + lo:2 * D_GLA + hi].astype(F32))
                oc, (st_a, st_b) = _gla_chunk_pair(ops[0], ops[1], vs[0], vs[1], st_ref[h0], st_ref[h0 + 1])
                st_ref[h0] = st_a
                st_ref[h0 + 1] = st_b
                cols = slice(h0 * HEAD_D, (h0 + 2) * HEAD_D)
                if rev:
                    o_ref[rows, cols] += oc
                else:
                    o_ref[rows, cols] = oc
            return carry
        lax.fori_loop(0, n_chunks, chunk_body, 0, unroll=min(n_chunks, 8))

        if emit_state:
            for h in range(N_HEADS):
                st_out_ref[0, d, h] = st_ref[h].T

    out_ref[...] = o_ref[...].astype(BF16)


def _gla_scan(proj, wdec, bdec, s0, *, seq_len, emit_state):
    t = proj.shape[0]
    nb = t // seq_len
    has_s0 = s0 is not None
    const = lambda shape: pl.BlockSpec(shape, lambda b: (0,) * len(shape), pipeline_mode=pl.Buffered(1))
    assert C_Q % (3 * D_GLA) == 0 and C_K == C_Q + D_GLA and C_V == C_K + D_GLA and C_LR % LR_PAD == 0
    in_specs = [pl.BlockSpec((seq_len, 3 * D_GLA), lambda b: (b, C_Q // (3 * D_GLA))),
                pl.BlockSpec((seq_len, LR_PAD), lambda b: (b, C_LR // LR_PAD)),
                const((2, LR_PAD, D_GLA)), const((2, D_GLA))]
    args = [proj, proj, wdec, bdec]
    if has_s0:
        in_specs.append(pl.BlockSpec((1, 2, N_HEADS, HEAD_D, HEAD_D), lambda b: (b, 0, 0, 0, 0)))
        args.append(s0)
    out_specs = [pl.BlockSpec((seq_len, D_GLA), lambda b: (b, 0))]
    out_shape = [jax.ShapeDtypeStruct((t, D_GLA), BF16)]
    if emit_state:
        out_specs.append(pl.BlockSpec((1, 2, N_HEADS, HEAD_D, HEAD_D), lambda b: (b, 0, 0, 0, 0)))
        out_shape.append(jax.ShapeDtypeStruct((nb, 2, N_HEADS, HEAD_D, HEAD_D), F32))
    kern = functools.partial(_scan_kernel, seq_len=seq_len, has_s0=has_s0, emit_state=emit_state)
    return pl.pallas_call(
        kern,
        grid=(nb,),
        in_specs=in_specs,
        out_specs=out_specs,
        out_shape=out_shape,
        scratch_shapes=[pltpu.VMEM((seq_len, D_GLA), F32), pltpu.VMEM((seq_len, D_GLA), F32),
                        pltpu.VMEM((seq_len, D_GLA), F32), pltpu.VMEM((N_HEADS, HEAD_D, HEAD_D), F32)],
        compiler_params=pltpu.CompilerParams(dimension_semantics=("arbitrary",),
                                             vmem_limit_bytes=VMEM_LIMIT),
        name="gla_scan",
    )(*args)


MIX_TM = 1024
MIX_SUB = 256


def _mix_dense_kernel(conv_ref, go_ref, gc0_ref, gc1_ref, gg0_ref, gg1_ref, o_ref, x_ref, mod_ref, cw_ref, cb_ref,
                      gnw_ref, wbc_ref, wbg_ref, wout_ref, out_ref, yc_ref, yg_ref, *, row_len, tiles_per_mod):
    i = pl.program_id(0)
    mod_row = i // tiles_per_mod if mod_ref.shape[0] > 1 else 0
    g1 = mod_ref[pl.ds(mod_row, 1), 2 * D_MODEL:3 * D_MODEL]
    hsel_r = lax.broadcasted_iota(jnp.int32, (D_GLA, D_GLA), 0) // HEAD_D
    hsel_c = lax.broadcasted_iota(jnp.int32, (D_GLA, D_GLA), 1) // HEAD_D
    head_avg = jnp.where(hsel_r == hsel_c, 1.0 / HEAD_D, 0.0).astype(BF16)
    head_avg2 = jnp.concatenate([head_avg, head_avg], axis=0)
    pos = lax.broadcasted_iota(jnp.int32, (MIX_SUB, 1), 0)
    in_row = pos % row_len

    def branch_body(t, carry):
        rows = pl.ds(pl.multiple_of(t * MIX_SUB, MIX_SUB), MIX_SUB)
        cu = conv_ref[rows, 2 * D_CONV:3 * D_CONV].astype(F32) * conv_ref[rows, 0:D_CONV].astype(F32)
        left = jnp.where(in_row == 0, 0.0, pltpu.roll(cu, 1, axis=0))
        right = jnp.where(in_row == row_len - 1, 0.0, pltpu.roll(cu, MIX_SUB - 1, axis=0))
        conv = cw_ref[0:1] * left + cw_ref[1:2] * cu + cw_ref[2:3] * right + cb_ref[...]
        yc_ref[rows, :] = (conv_ref[rows, D_CONV:2 * D_CONV].astype(F32) * conv).astype(BF16)
        o = o_ref[rows, :].astype(F32)
        ms = _dot(jnp.concatenate(_split_bf16(o * o), axis=1), head_avg2)
        g_out = go_ref[rows, :].astype(F32)
        yg_ref[rows, :] = (o * lax.rsqrt(ms + EPS) * gnw_ref[...] * (g_out * _sigmoid(g_out))).astype(BF16)
        return carry
    lax.fori_loop(0, MIX_TM // MIX_SUB, branch_body, 0)

    half = D_MODEL // 2
    pc = _dot(yc_ref[...], wbc_ref[...])
    pg = _dot(yg_ref[...], wbg_ref[...])
    merged = jnp.concatenate(
        [_sigmoid(gc0_ref[...].astype(F32)) * pc[:, :half] + _sigmoid(gg0_ref[...].astype(F32)) * pg[:, :half],
         _sigmoid(gc1_ref[...].astype(F32)) * pc[:, half:] + _sigmoid(gg1_ref[...].astype(F32)) * pg[:, half:]],
        axis=1)
    out_ref[...] = x_ref[...] + g1 * _dot(merged.astype(BF16), wout_ref[...])


def _mix_dense(proj, o, x2d, mod, conv_w, conv_b, gnw, wbc, wbg, wout, *, row_len, tokens_per_mod):
    t = x2d.shape[0]
    assert MIX_SUB % row_len == 0 and t % MIX_TM == 0 and tokens_per_mod % MIX_TM == 0
    const = lambda shape: pl.BlockSpec(shape, lambda i: (0,) * len(shape), pipeline_mode=pl.Buffered(1))
    cols = lambda width, start: pl.BlockSpec((MIX_TM, width), lambda i: (i, start // width))
    half = D_MODEL // 2
    assert C_U == 0 and C_GB == D_CONV and C_GC == 2 * D_CONV and C_GO % D_GLA == 0 and C_BRC % half == 0
    kern = functools.partial(_mix_dense_kernel, row_len=row_len, tiles_per_mod=tokens_per_mod // MIX_TM)
    return pl.pallas_call(
        kern,
        grid=(t // MIX_TM,),
        in_specs=[cols(3 * D_CONV, C_U), cols(D_GLA, C_GO),
                  cols(half, C_BRC), cols(half, C_BRC + half), cols(half, C_BRG), cols(half, C_BRG + half),
                  pl.BlockSpec((MIX_TM, D_GLA), lambda i: (i, 0)), pl.BlockSpec((MIX_TM, D_MODEL), lambda i: (i, 0)),
                  const(mod.shape), const((3, D_CONV)), const((1, D_CONV)), const((1, D_GLA)),
                  const((D_CONV, D_MODEL)), const((D_GLA, D_MODEL)), const((D_MODEL, D_MODEL))],
        out_specs=pl.BlockSpec((MIX_TM, D_MODEL), lambda i: (i, 0)),
        out_shape=jax.ShapeDtypeStruct((t, D_MODEL), F32),
        scratch_shapes=[pltpu.VMEM((MIX_TM, D_CONV), BF16), pltpu.VMEM((MIX_TM, D_GLA), BF16)],
        compiler_params=pltpu.CompilerParams(dimension_semantics=("arbitrary",), vmem_limit_bytes=VMEM_LIMIT),
        name="mix_dense",
    )(proj, proj, proj, proj, proj, proj, o, x2d, mod, conv_w, conv_b, gnw, wbc, wbg, wout)


I32 = jnp.int32
TB = 256
ROW_ALIGN = 16
G_ALIGN = 256
E_CHUNK = 2048
R_LOC = 3072
H2W = 1152


def _select_x(i, n_p_tiles, xp_ref, xs_ref):
    return jnp.where(i < n_p_tiles, xp_ref[...], xs_ref[...])


def _mod_row(i, n_p_tiles, tiles_per_mod):
    return jnp.where(i < n_p_tiles, 0, 1 + (i - n_p_tiles) // tiles_per_mod)


def _route_kernel(xp_ref, xs_ref, mod_ref, n2w_ref, wrt_ref, brb_ref, h2_ref, lpos_ref, cnt_ref, *,
                  n_p_tiles, tiles_per_mod):
    i = pl.program_id(0)
    row = _mod_row(i, n_p_tiles, tiles_per_mod)
    x = _select_x(i, n_p_tiles, xp_ref, xs_ref)
    sh = mod_ref[pl.ds(row, 1), 3 * D_MODEL:4 * D_MODEL]
    sc = mod_ref[pl.ds(row, 1), 4 * D_MODEL:5 * D_MODEL]
    h2 = _rms(x) * n2w_ref[...] * (1.0 + sc) + sh

    h_hi, h_lo = _split_bf16(h2)
    w_hi, w_lo = _split_bf16(wrt_ref[...])
    scores = jax.nn.sigmoid(_dot_nt(jnp.concatenate([w_hi, w_hi, w_lo], axis=1),
                                    jnp.concatenate([h_hi, h_lo, h_hi], axis=1)))
    biased = scores + brb_ref[...]
    eidx = lax.broadcasted_iota(I32, scores.shape, 0)
    picks = []
    for _k in range(TOP_K):
        m = jnp.max(biased, axis=0, keepdims=True)
        first = jnp.min(jnp.where(biased == m, eidx, N_EXPERTS), axis=0, keepdims=True)
        pick = eidx == first
        picks.append(pick)
        biased = jnp.where(pick, -jnp.inf, biased)
    sel = jnp.zeros(scores.shape, F32)
    for pick in picks:
        sel = jnp.where(pick, 1.0, sel)
    selsc = sel * scores
    comb = selsc / jnp.sum(selsc, axis=0, keepdims=True) * ROUTED_SCALE

    selb = sel.astype(BF16)
    tr_ = lax.broadcasted_iota(I32, (TB, TB), 0)
    tc_ = lax.broadcasted_iota(I32, (TB, TB), 1)
    rank_n = _dot(selb, jnp.concatenate([jnp.where(tr_ < tc_, 1.0, 0.0).astype(BF16), jnp.ones((TB, 128), BF16)],
                                        axis=1))
    rank, n_b = rank_n[:, :TB], rank_n[:, TB:]
    m_b = jnp.maximum(jnp.floor((n_b + (ROW_ALIGN - 1)) * (1.0 / ROW_ALIGN)), 1.0) * ROW_ALIGN
    er_ = lax.broadcasted_iota(I32, (N_EXPERTS, N_EXPERTS), 0)
    ec_ = lax.broadcasted_iota(I32, (N_EXPERTS, N_EXPERTS), 1)
    loff_b = _dot(jnp.where(ec_ < er_, 1.0, 0.0).astype(BF16), m_b.astype(BF16))
    lposf = jnp.concatenate([loff_b] * (TB // 128), axis=1) + rank
    rows = [jnp.sum(jnp.where(pick, lposf, 0.0), axis=0, keepdims=True) for pick in picks]
    lpos_ref[0] = jnp.concatenate(rows, axis=0).astype(I32)
    cnt_ref[0] = m_b

    combt = comb.T
    chi = combt.astype(BF16).astype(F32)
    h2_ref[:, 0:D_MODEL] = h_hi
    h2_ref[:, D_MODEL:H2W] = jnp.concatenate([chi, combt - chi], axis=1).astype(BF16)


def _route(x1p, x1s, mod, n2w, w_router_t, b_router_b, *, tiles_per_mod):
    n_p, n_s = x1p.shape[0] // TB, x1s.shape[0] // TB
    nt = n_p + n_s
    kern = functools.partial(_route_kernel, n_p_tiles=n_p, tiles_per_mod=tiles_per_mod)
    const = lambda shape: pl.BlockSpec(shape, lambda i: (0,) * len(shape))
    return pl.pallas_call(
        kern,
        grid=(nt,),
        in_specs=[pl.BlockSpec((TB, D_MODEL), lambda i: (jnp.minimum(i, n_p - 1), 0)),
                  pl.BlockSpec((TB, D_MODEL), lambda i: (jnp.maximum(i - n_p, 0), 0)),
                  const(mod.shape), const((1, D_MODEL)), const((N_EXPERTS, D_MODEL)), const((N_EXPERTS, TB))],
        out_specs=[pl.BlockSpec((TB, H2W), lambda i: (i, 0)),
                   pl.BlockSpec((1, TOP_K, TB), lambda i: (i, 0, 0)),
                   pl.BlockSpec((1, N_EXPERTS, 128), lambda i: (i, 0, 0))],
        out_shape=[jax.ShapeDtypeStruct((nt * TB, H2W), BF16),
                   jax.ShapeDtypeStruct((nt, TOP_K, TB), I32),
                   jax.ShapeDtypeStruct((nt, N_EXPERTS, 128), F32)],
        compiler_params=pltpu.CompilerParams(dimension_semantics=("arbitrary",), vmem_limit_bytes=VMEM_LIMIT),
        name="moe_route",
    )(x1p, x1s, mod, n2w, w_router_t, b_router_b)


def _plan_kernel(cnt_ref, off_ref, loff_ref, msz_ref, grp_ref, *, nt):
    lane = lax.broadcasted_iota(I32, (N_EXPERTS, 128), 1)
    m = jnp.zeros((N_EXPERTS, 128), F32)
    for i in range(nt):
        m = jnp.where(lane == i, cnt_ref[i], m)
    total = jnp.broadcast_to(jnp.sum(m, axis=1, keepdims=True), (N_EXPERTS, 128))
    gsz = jnp.floor((total + (G_ALIGN - 1)) * (1.0 / G_ALIGN)) * G_ALIGN
    er_ = lax.broadcasted_iota(I32, (N_EXPERTS, N_EXPERTS), 0)
    ec_ = lax.broadcasted_iota(I32, (N_EXPERTS, N_EXPERTS), 1)
    lstrict = jnp.where(ec_ < er_, 1.0, 0.0)
    ir_ = lax.broadcasted_iota(I32, (128, 128), 0)
    ic_ = lax.broadcasted_iota(I32, (128, 128), 1)
    ustrict = jnp.where(ir_ < ic_, 1.0, 0.0)
    gstart = _dot_hi(lstrict, gsz)
    off_ref[...] = (gstart + _dot_hi(m, ustrict)).astype(I32)
    loff_ref[...] = _dot_hi(lstrict, m).astype(I32)
    msz_ref[...] = m.astype(I32)
    grp = jnp.where(lane == 0, gstart + total, jnp.where(lane == 1, gsz - total, jnp.where(lane == 2, gstart, gsz)))
    grp_ref[...] = grp.astype(I32)


def _plan(cnt):
    nt = cnt.shape[0]
    assert nt <= 128
    tab = jax.ShapeDtypeStruct((N_EXPERTS, 128), I32)
    return pl.pallas_call(
        functools.partial(_plan_kernel, nt=nt),
        out_shape=[tab, tab, tab, tab],
        compiler_params=pltpu.CompilerParams(vmem_limit_bytes=VMEM_LIMIT),
        name="moe_plan",
    )(cnt)


def _start_copies(msz_ref, tile, make_copy):
    for e in range(N_EXPERTS):
        make_copy(e, pl.multiple_of(msz_ref[e, tile], ROW_ALIGN)).start(priority=e % 2)


def _tile_rows(loff_ref, msz_ref, tile):
    return pl.multiple_of(loff_ref[N_EXPERTS - 1, tile] + msz_ref[N_EXPERTS - 1, tile], ROW_ALIGN)


def _dispatch_kernel(off_ref, loff_ref, msz_ref, tail_ref, h2_ref, lpos_ref, xs_hbm, xloc_ref, zero_ref, sem,
                     tail_sem, *, nt):
    i = pl.program_id(0)
    slot = i % 2

    def copy_for(tile, slot_):
        def make(e, m):
            lo = pl.multiple_of(loff_ref[e, tile], ROW_ALIGN)
            of = pl.multiple_of(off_ref[e, tile], ROW_ALIGN)
            return pltpu.make_async_copy(xloc_ref.at[slot_, pl.ds(lo, m)], xs_hbm.at[pl.ds(of, m)], sem.at[slot_])
        return make

    def wait_tile(tile, slot_):
        n = _tile_rows(loff_ref, msz_ref, tile)
        pltpu.make_async_copy(xloc_ref.at[slot_, pl.ds(0, n)], xs_hbm.at[pl.ds(0, n)], sem.at[slot_]).wait()

    lpos = lpos_ref[0].astype(jnp.int16)
    h2 = h2_ref[...]
    ck = 1024
    one, zero = jnp.ones((ck, TB), BF16), jnp.zeros((ck, TB), BF16)
    for c in range(R_LOC // ck):
        r = (lax.broadcasted_iota(I32, (ck, TB), 0) + c * ck).astype(jnp.int16)
        d = zero
        for k in range(TOP_K):
            d = jnp.where(r == lpos[k:k + 1, :], one, d)
        res = _dot(d, h2)
        xloc_ref[slot, c * ck:(c + 1) * ck, :] = res.astype(BF16)

    _start_copies(msz_ref, i, copy_for(i, slot))

    @pl.when(i > 0)
    def _():
        wait_tile(i - 1, 1 - slot)

    @pl.when(i == nt - 1)
    def _():
        zero_ref[...] = jnp.zeros(zero_ref.shape, BF16)

        def tail_copies(start):
            def body(e, carry):
                n = tail_ref[e, 1]

                @pl.when(n > 0)
                def _():
                    st = pl.multiple_of(tail_ref[e, 0], ROW_ALIGN)
                    nn = pl.multiple_of(n, ROW_ALIGN)
                    cp = pltpu.make_async_copy(zero_ref.at[pl.ds(0, nn)], xs_hbm.at[pl.ds(st, nn)], tail_sem)
                    if start:
                        cp.start()
                    else:
                        cp.wait()
                return carry
            lax.fori_loop(0, N_EXPERTS, body, 0)
        tail_copies(True)
        wait_tile(i, slot)
        tail_copies(False)


def _dispatch(off, loff, msz, tail, h2ext, lpos, n_rows):
    nt = lpos.shape[0]
    grid_spec = pltpu.PrefetchScalarGridSpec(
        num_scalar_prefetch=4,
        grid=(nt,),
        in_specs=[pl.BlockSpec((TB, H2W), lambda i, *_: (i, 0)),
                  pl.BlockSpec((1, TOP_K, TB), lambda i, *_: (i, 0, 0))],
        out_specs=pl.BlockSpec(memory_space=pl.ANY),
        scratch_shapes=[pltpu.VMEM((2, R_LOC, H2W), BF16), pltpu.VMEM((G_ALIGN, H2W), BF16),
                        pltpu.SemaphoreType.DMA((2,)), pltpu.SemaphoreType.DMA],
    )
    return pl.pallas_call(
        functools.partial(_dispatch_kernel, nt=nt),
        grid_spec=grid_spec,
        out_shape=jax.ShapeDtypeStruct((n_rows, H2W), BF16),
        compiler_params=pltpu.CompilerParams(dimension_semantics=("arbitrary",), vmem_limit_bytes=VMEM_LIMIT),
        name="moe_dispatch",
    )(off, loff, msz, tail, h2ext, lpos)


def _expert_kernel(grp_ref, wg_ref, wu_ref, wd_ref, xs_hbm, ys_hbm, xbuf, ybuf, wgu_ref, wdb_ref, st_ref,
                   in_sem, out_sem):
    e = pl.program_id(0)
    n_exp = pl.num_programs(0)

    def in_copy(row0, n, slot):
        return pltpu.make_async_copy(xs_hbm.at[pl.ds(row0, n)], xbuf.at[slot, pl.ds(0, n)], in_sem.at[slot])

    def out_copy(row0, n, slot):
        return pltpu.make_async_copy(ybuf.at[slot, pl.ds(0, n)], ys_hbm.at[pl.ds(row0, n)], out_sem.at[slot])

    def rows_of(ex):
        return grp_ref[jnp.minimum(ex, n_exp - 1), 3]

    def next_nonempty(ex):
        return lax.while_loop(lambda c: (c < n_exp) & (rows_of(c) == 0), lambda c: c + 1, ex + 1)

    def start_first_chunk(ex, slot):
        @pl.when(ex < n_exp)
        def _():
            exc = jnp.minimum(ex, n_exp - 1)
            n = pl.multiple_of(jnp.minimum(grp_ref[exc, 3], E_CHUNK), G_ALIGN)
            in_copy(pl.multiple_of(grp_ref[exc, 2], G_ALIGN), n, slot).start()

    def drain_out(slot):
        pend = st_ref[1 + slot]

        @pl.when(pend > 0)
        def _():
            out_copy(0, pl.multiple_of(pend, G_ALIGN), slot).wait()
            st_ref[1 + slot] = 0

    @pl.when(e == 0)
    def _():
        st_ref[0] = 0
        st_ref[1] = 0
        st_ref[2] = 0
        start_first_chunk(next_nonempty(-1), 0)

    g0 = grp_ref[e, 2]
    gn = grp_ref[e, 3]

    @pl.when(gn > 0)
    def _():
        wgu_ref[:, :D_EXPERT] = wg_ref[0].astype(BF16)
        wgu_ref[:, D_EXPERT:] = wu_ref[0].astype(BF16)
        wdb_ref[...] = wd_ref[0].astype(BF16)
        n_chunks = (gn + (E_CHUNK - 1)) // E_CHUNK

        def compute(n, slot):
            x = xbuf[slot, 0:n, 0:D_MODEL]
            ext = xbuf[slot, 0:n, D_MODEL:H2W].astype(F32)
            wts = ext[:, :N_EXPERTS] + ext[:, N_EXPERTS:]
            lane = lax.broadcasted_iota(I32, wts.shape, 1)
            w = jnp.sum(jnp.where(lane == e, wts, 0.0), axis=-1, keepdims=True)
            h = _dot(x, wgu_ref[...])
            hg, hu = h[:, :D_EXPERT], h[:, D_EXPERT:]
            act = hg * _sigmoid(hg) * hu * w
            ybuf[slot, 0:n, :] = _dot(act.astype(BF16), wdb_ref[...]).astype(BF16)

        def chunk_body(c, slot):
            row0 = pl.multiple_of(g0 + c * E_CHUNK, G_ALIGN)
            n = pl.multiple_of(jnp.minimum(gn - c * E_CHUNK, E_CHUNK), G_ALIGN)
            in_copy(row0, n, slot).wait()

            @pl.when(c + 1 < n_chunks)
            def _():
                n1 = pl.multiple_of(jnp.minimum(gn - (c + 1) * E_CHUNK, E_CHUNK), G_ALIGN)
                in_copy(pl.multiple_of(row0 + E_CHUNK, G_ALIGN), n1, 1 - slot).start()

            @pl.when(c + 1 == n_chunks)
            def _():
                start_first_chunk(next_nonempty(e), 1 - slot)

            drain_out(slot)
            for v in range(G_ALIGN, E_CHUNK + 1, G_ALIGN):
                @pl.when(n == v)
                def _(v=v):
                    compute(v, slot)
            out_copy(row0, n, slot).start()
            st_ref[1 + slot] = n
            return 1 - slot

        st_ref[0] = lax.fori_loop(0, n_chunks, chunk_body, st_ref[0])

    @pl.when(e == n_exp - 1)
    def _():
        drain_out(0)
        drain_out(1)


def _experts(grp, xs, wg, wu, wd):
    w_in = pl.BlockSpec((1, D_MODEL, D_EXPERT), lambda e, grp_ref: (e, 0, 0))
    grid_spec = pltpu.PrefetchScalarGridSpec(
        num_scalar_prefetch=1,
        grid=(N_EXPERTS,),
        in_specs=[w_in, w_in, pl.BlockSpec((1, D_EXPERT, D_MODEL), lambda e, grp_ref: (e, 0, 0)),
                  pl.BlockSpec(memory_space=pl.ANY)],
        out_specs=pl.BlockSpec(memory_space=pl.ANY),
        scratch_shapes=[pltpu.VMEM((2, E_CHUNK, H2W), BF16), pltpu.VMEM((2, E_CHUNK, D_MODEL), BF16),
                        pltpu.VMEM((D_MODEL, 2 * D_EXPERT), BF16), pltpu.VMEM((D_EXPERT, D_MODEL), BF16),
                        pltpu.SMEM((4,), I32), pltpu.SemaphoreType.DMA((2,)), pltpu.SemaphoreType.DMA((2,))],
    )
    return pl.pallas_call(
        _expert_kernel,
        grid_spec=grid_spec,
        out_shape=jax.ShapeDtypeStruct((xs.shape[0], D_MODEL), BF16),
        compiler_params=pltpu.CompilerParams(dimension_semantics=("arbitrary",), vmem_limit_bytes=VMEM_LIMIT),
        name="moe_experts",
    )(grp, wg, wu, wd, xs)


def _combine_kernel(off_ref, loff_ref, msz_ref, lpos_ref, h2_ref, xp_ref, xs_ref, mod_ref, wgs_ref, wus_ref, wds_ref,
                    fnw_ref, ysrt_hbm, yp_ref, ys_ref, yloc_ref, acc_ref, wgus_ref, wdsb_ref, sem, *, nt, n_p_tiles,
                    tiles_per_mod, final):
    i = pl.program_id(0)
    slot = i % 2

    def copy_for(tile, slot_):
        def make(e, m):
            lo = pl.multiple_of(loff_ref[e, tile], ROW_ALIGN)
            of = pl.multiple_of(off_ref[e, tile], ROW_ALIGN)
            return pltpu.make_async_copy(ysrt_hbm.at[pl.ds(of, m)], yloc_ref.at[slot_, pl.ds(lo, m)], sem.at[slot_])
        return make

    @pl.when(i == 0)
    def _():
        yloc_ref[...] = jnp.zeros(yloc_ref.shape, BF16)
        _start_copies(msz_ref, 0, copy_for(0, 0))
        wgus_ref[:, :D_EXPERT] = wgs_ref[...].astype(BF16)
        wgus_ref[:, D_EXPERT:] = wus_ref[...].astype(BF16)
        wdsb_ref[...] = wds_ref[...].astype(BF16)

    nxt = jnp.minimum(i + 1, nt - 1)
    _start_copies(msz_ref, nxt, copy_for(nxt, 1 - slot))

    hgu = _dot(h2_ref[...], wgus_ref[...])
    hg, hu = hgu[:, :D_EXPERT], hgu[:, D_EXPERT:]
    acc_ref[...] = _dot((hg * _sigmoid(hg) * hu).astype(BF16), wdsb_ref[...])

    def wait_tile(tile, slot_):
        n = _tile_rows(loff_ref, msz_ref, tile)
        pltpu.make_async_copy(ysrt_hbm.at[pl.ds(0, n)], yloc_ref.at[slot_, pl.ds(0, n)], sem.at[slot_]).wait()

    wait_tile(i, slot)

    @pl.when(i == nt - 1)
    def _():
        wait_tile(i, 1 - slot)

    lpos_pad = jnp.concatenate([lpos_ref[0].astype(F32), jnp.zeros((128 - TOP_K, TB), F32)], axis=0)
    lposc = lpos_pad.T.astype(I32)
    ck = 512
    cols = [jnp.broadcast_to(lposc[:, k:k + 1], (TB, ck)).astype(jnp.int16) for k in range(TOP_K)]
    one, zero = jnp.ones((TB, ck), BF16), jnp.zeros((TB, ck), BF16)
    for c in range(R_LOC // ck):
        r = (lax.broadcasted_iota(I32, (TB, ck), 1) + c * ck).astype(jnp.int16)
        cm = zero
        for k in range(TOP_K):
            cm = jnp.where(r == cols[k], one, cm)
        acc_ref[...] += _dot(cm, yloc_ref[slot, c * ck:(c + 1) * ck, :])

    row = _mod_row(i, n_p_tiles, tiles_per_mod)
    g2 = mod_ref[pl.ds(row, 1), 5 * D_MODEL:6 * D_MODEL]
    x2 = _select_x(i, n_p_tiles, xp_ref, xs_ref) + g2 * acc_ref[...]
    y = _rms(x2) * fnw_ref[...] if final else x2

    @pl.when(i < n_p_tiles)
    def _():
        yp_ref[...] = y

    @pl.when(i >= n_p_tiles)
    def _():
        ys_ref[...] = y


def _combine(off, loff, msz, lpos, h2ext, x1p, x1s, mod, wgs, wus, wds, fnw, ysorted, *, tiles_per_mod, final):
    n_p, n_s = x1p.shape[0] // TB, x1s.shape[0] // TB
    nt = n_p + n_s
    const = lambda shape: pl.BlockSpec(shape, lambda i, *_: (0,) * len(shape), pipeline_mode=pl.Buffered(1))
    p_idx = lambda i, *_: (jnp.minimum(i, n_p - 1), 0)
    s_idx = lambda i, *_: (jnp.maximum(i - n_p, 0), 0)
    grid_spec = pltpu.PrefetchScalarGridSpec(
        num_scalar_prefetch=3,
        grid=(nt,),
        in_specs=[pl.BlockSpec((1, TOP_K, TB), lambda i, *_: (i, 0, 0)),
                  pl.BlockSpec((TB, D_MODEL), lambda i, *_: (i, 0)),
                  pl.BlockSpec((TB, D_MODEL), p_idx), pl.BlockSpec((TB, D_MODEL), s_idx),
                  const(mod.shape), const((D_MODEL, D_EXPERT)), const((D_MODEL, D_EXPERT)), const((D_EXPERT, D_MODEL)),
                  const((1, D_MODEL)), pl.BlockSpec(memory_space=pl.ANY)],
        out_specs=[pl.BlockSpec((TB, D_MODEL), p_idx), pl.BlockSpec((TB, D_MODEL), s_idx)],
        scratch_shapes=[pltpu.VMEM((2, R_LOC, D_MODEL), BF16), pltpu.VMEM((TB, D_MODEL), F32),
                        pltpu.VMEM((D_MODEL, 2 * D_EXPERT), BF16), pltpu.VMEM((D_EXPERT, D_MODEL), BF16),
                        pltpu.SemaphoreType.DMA((2,))],
    )
    kern = functools.partial(_combine_kernel, nt=nt, n_p_tiles=n_p, tiles_per_mod=tiles_per_mod, final=final)
    return pl.pallas_call(
        kern,
        grid_spec=grid_spec,
        out_shape=[jax.ShapeDtypeStruct(x1p.shape, F32), jax.ShapeDtypeStruct(x1s.shape, F32)],
        compiler_params=pltpu.CompilerParams(dimension_semantics=("arbitrary",), vmem_limit_bytes=VMEM_LIMIT),
        name="moe_combine",
    )(off, loff, msz, lpos, h2ext, x1p, x1s, mod, wgs, wus, wds, fnw, ysorted)


def _moe(x1p, x1s, mod, n2w, w_router, b_router, wg, wu, wd, wgs, wus, wds, fnw, *, tokens_per_mod, final):
    assert R_LOC >= TB * TOP_K + N_EXPERTS * ROW_ALIGN and tokens_per_mod % TB == 0
    nt = (x1p.shape[0] + x1s.shape[0]) // TB
    n_rows_max = nt * TB * TOP_K + nt * N_EXPERTS * ROW_ALIGN + N_EXPERTS * (G_ALIGN - ROW_ALIGN)
    tiles_per_mod = tokens_per_mod // TB
    brb = jnp.broadcast_to(b_router.reshape(N_EXPERTS, 1), (N_EXPERTS, TB))
    h2ext, lpos, cnt = _route(x1p, x1s, mod, n2w, w_router.T, brb, tiles_per_mod=tiles_per_mod)
    off, loff, msz, grp = _plan(cnt)
    xs = _dispatch(off, loff, msz, grp, h2ext, lpos, n_rows_max)
    ysorted = _experts(grp, xs, wg, wu, wd)
    return _combine(off, loff, msz, lpos, h2ext, x1p, x1s, mod, wgs, wus, wds, fnw, ysorted,
                    tiles_per_mod=tiles_per_mod, final=final)


def kernel(x_prompt, x_sample, state_gla, c, c_ctx, w_mod, b_mod, norm1_w, w_in, conv_w, conv_b, w_decay, b_decay,
           gla_norm_w, w_br_conv, w_br_gla, w_out, norm2_w, w_router, b_router, w_gate_e, w_up_e, w_down_e,
           w_gate_s, w_up_s, w_down_s, final_norm_w):
    depth = w_mod.shape[0]
    nb_p, len_p, _ = x_prompt.shape
    nb_s, len_s, _ = x_sample.shape
    yp = x_prompt.reshape(nb_p * len_p, D_MODEL)
    ys = x_sample.reshape(nb_s * len_s, D_MODEL)
    fnw = final_norm_w.reshape(1, D_MODEL)

    cond = jnp.concatenate([c_ctx[None, :], c, jnp.zeros((8 - 1 - nb_s, D_MODEL), F32)], axis=0)
    states = []
    for l in range(depth):
        mod = _modulation(cond, w_mod[l], b_mod[l].reshape(1, -1))
        mod_p, mod_s = mod[0:1], mod[1:1 + nb_s]

        w_in_r = _w_in_prep(w_in[l].T)
        wdec = jnp.zeros((2, LR_PAD, D_GLA), F32)
        wdec = wdec.at[0, 0:GLA_RANK].set(w_decay[l, 0]).at[1, GLA_RANK:2 * GLA_RANK].set(w_decay[l, 1])
        n1w = norm1_w[l].reshape(1, D_MODEL)
        mix_w = (conv_w[l], conv_b[l].reshape(1, D_CONV), gla_norm_w[l].reshape(1, D_GLA),
                 w_br_conv[l].astype(BF16), w_br_gla[l].astype(BF16), w_out[l].astype(BF16))
        moe_w = (norm2_w[l].reshape(1, D_MODEL), w_router[l], b_router[l],
                 w_gate_e[l], w_up_e[l], w_down_e[l], w_gate_s[l], w_up_s[l], w_down_s[l])

        proj_p = _in_proj(yp, mod_p, n1w, w_in_r, rows_per_mod=nb_p * len_p)
        o_p, st = _gla_scan(proj_p, wdec, b_decay[l], None, seq_len=len_p, emit_state=True)
        yp = _mix_dense(proj_p, o_p, yp, mod_p, *mix_w, row_len=len_p, tokens_per_mod=nb_p * len_p)
        states.append(st)
        proj_s = _in_proj(ys, mod_s, n1w, w_in_r, rows_per_mod=len_s)
        (o_s,) = _gla_scan(proj_s, wdec, b_decay[l], state_gla[:, l], seq_len=len_s, emit_state=False)
        ys = _mix_dense(proj_s, o_s, ys, mod_s, *mix_w, row_len=GRID_W, tokens_per_mod=len_s)

        yp, ys = _moe(yp, ys, mod, *moe_w, fnw, tokens_per_mod=len_s, final=l == depth - 1)
    new_state = jnp.stack(states, axis=1)
    return (yp.reshape(nb_p, len_p, D_MODEL), ys.reshape(nb_s, len_s, D_MODEL), new_state)
```

```python
import functools

import jax
import jax.numpy as jnp
from jax import lax
from jax.experimental import pallas as pl
from jax.experimental.pallas import tpu as pltpu

F32 = jnp.float32
BF16 = jnp.bfloat16

D_MODEL = 1024
GRID_W = 64
D_CONV = 512
N_HEADS = 4
HEAD_D = 128
D_GLA = N_HEADS * HEAD_D
GLA_RANK = 16
GLA_GATE_NORM = 16.0
LOG2_E = 1.4426950408889634
CHUNK = 64
SUB = 8
N_SUB = CHUNK // SUB
N_EXPERTS = 64
TOP_K = 8
D_EXPERT = 256
ROUTED_SCALE = 2.5
EPS = 1e-6

C_U, C_GB, C_GC, C_Q, C_K, C_V, C_GO = 0, 512, 1024, 1536, 2048, 2560, 3072
C_BRC, C_BRG, C_LR = 3584, 4608, 5632
D_PROJ = 5760
LR_PAD = 128

VMEM_LIMIT = 56 * 1024 * 1024


def _dot(a, b):
    return jnp.dot(a, b, preferred_element_type=F32)


def _dot_nt(a, b):
    return lax.dot_general(a, b, (((1,), (1,)), ((), ())), preferred_element_type=F32)


def _dot_hi(a, b):
    return jnp.dot(a, b, preferred_element_type=F32, precision=lax.Precision.HIGHEST)


def _split_bf16(x):
    hi = x.astype(BF16)
    lo = (x - hi.astype(F32)).astype(BF16)
    return hi, lo


def _rms(x):
    return x * lax.rsqrt(jnp.mean(x * x, axis=-1, keepdims=True) + EPS)


def _sigmoid(x):
    return 0.5 * jnp.tanh(0.5 * x) + 0.5


def _mod_kernel(cond_ref, w_ref, b_ref, o_ref):
    c = cond_ref[...]
    c_hi, c_lo = _split_bf16(c * jax.nn.sigmoid(c))
    w_hi, w_lo = _split_bf16(w_ref[...])
    o_ref[...] = _dot(jnp.concatenate([c_hi, c_hi, c_lo], axis=1),
                      jnp.concatenate([w_hi, w_lo, w_hi], axis=0)) + b_ref[...]


def _modulation(cond, w_mod, b_mod):
    n_rows = cond.shape[0]
    tn = 1536
    return pl.pallas_call(
        _mod_kernel,
        grid=(6 * D_MODEL // tn,),
        in_specs=[pl.BlockSpec((n_rows, D_MODEL), lambda j: (0, 0)),
                  pl.BlockSpec((D_MODEL, tn), lambda j: (0, j)),
                  pl.BlockSpec((1, tn), lambda j: (0, j))],
        out_specs=pl.BlockSpec((n_rows, tn), lambda j: (0, j)),
        out_shape=jax.ShapeDtypeStruct((n_rows, 6 * D_MODEL), F32),
        compiler_params=pltpu.CompilerParams(dimension_semantics=("arbitrary",),
                                             vmem_limit_bytes=VMEM_LIMIT),
        name="modulation",
    )(cond, w_mod, b_mod)


D_IN_PROJ = 5664
C_LR_SRC, C_GATES_SRC = 3584, 3616


def _w_in_prep_kernel(w_ref, o_ref):
    rc = 64

    def copy_rows(dst0, src0, n):
        def body(t, carry):
            off = pl.multiple_of(t * rc, rc)
            o_ref[pl.ds(dst0 + off, rc), :] = w_ref[pl.ds(src0 + off, rc), :].astype(BF16)
            return carry
        lax.fori_loop(0, n // rc, body, 0)

    copy_rows(0, 0, C_LR_SRC)
    copy_rows(C_BRC, C_GATES_SRC, 2 * D_MODEL)
    o_ref[C_LR:C_LR + 2 * GLA_RANK, :] = w_ref[C_LR_SRC:C_GATES_SRC, :].astype(BF16)
    o_ref[C_LR + 2 * GLA_RANK:D_PROJ, :] = jnp.zeros((D_PROJ - C_LR - 2 * GLA_RANK, D_MODEL), BF16)


def _w_in_prep(w_in_t):
    return pl.pallas_call(
        _w_in_prep_kernel,
        out_shape=jax.ShapeDtypeStruct((D_PROJ, D_MODEL), BF16),
        compiler_params=pltpu.CompilerParams(vmem_limit_bytes=VMEM_LIMIT),
        name="w_in_prep",
    )(w_in_t)


def _inproj_kernel(x_ref, mod_ref, nw_ref, w_ref, o_ref, *, rows_per_mod, tm):
    i = pl.program_id(1)
    row = (i * tm) // rows_per_mod
    sh = mod_ref[pl.ds(row, 1), 0:D_MODEL]
    sc = mod_ref[pl.ds(row, 1), D_MODEL:2 * D_MODEL]
    h = _rms(x_ref[...]) * nw_ref[...] * (1.0 + sc) + sh
    o_ref[...] = _dot_nt(h.astype(BF16), w_ref[...]).astype(BF16)


def _in_proj(x2d, mod, norm_w, w_in_r, rows_per_mod):
    t = x2d.shape[0]
    tm, tn = 1024, 1920
    kern = functools.partial(_inproj_kernel, rows_per_mod=rows_per_mod, tm=tm)
    return pl.pallas_call(
        kern,
        grid=(D_PROJ // tn, t // tm),
        in_specs=[pl.BlockSpec((tm, D_MODEL), lambda j, i: (i, 0)),
                  pl.BlockSpec(mod.shape, lambda j, i: (0, 0)),
                  pl.BlockSpec((1, D_MODEL), lambda j, i: (0, 0)),
                  pl.BlockSpec((tn, D_MODEL), lambda j, i: (j, 0))],
        out_specs=pl.BlockSpec((tm, tn), lambda j, i: (i, j)),
        out_shape=jax.ShapeDtypeStruct((t, D_PROJ), BF16),
        compiler_params=pltpu.CompilerParams(dimension_semantics=("arbitrary", "arbitrary"),
                                             vmem_limit_bytes=VMEM_LIMIT),
        name="in_proj",
    )(x2d, mod, norm_w, w_in_r)


def _log2_sigmoid(x):
    return jnp.minimum(x, 0.0) * LOG2_E - jnp.log2(1.0 + jnp.exp2(jnp.abs(x) * (-LOG2_E)))


def _gla_head_operands(qc, kc, bc, rev, lane0):
    lane = lax.broadcasted_iota(jnp.int32, (SUB, HEAD_D), 1) - lane0
    sub = lax.broadcasted_iota(jnp.int32, (SUB, HEAD_D), 0)
    tot = bc[0:1] if rev else bc[CHUNK - 1:CHUNK]
    q_in = (qc * jnp.exp2(bc)).astype(BF16)
    k_tail = (kc * jnp.exp2(tot - bc)).astype(BF16)

    lhs_segs, rhs_segs = [], []

    def rows(before, mid, after):
        parts = ([jnp.zeros((before, HEAD_D), F32)] if before else []) + [mid]
        parts += [jnp.zeros((after, HEAD_D), F32)] if after else []
        return jnp.concatenate(parts, axis=0) if len(parts) > 1 else mid

    key_blocks = range(1, N_SUB) if rev else range(0, N_SUB - 1)
    for jb in key_blocks:
        r0 = jb * SUB
        ref_row = bc[r0:r0 + 1] if rev else bc[r0 + SUB - 1:r0 + SUB]
        ke = kc[r0:r0 + SUB] * jnp.exp2(ref_row - bc[r0:r0 + SUB])
        rhs_segs.append(rows(r0, ke, CHUNK - r0 - SUB))
        if rev:
            ql = qc[:r0] * jnp.exp2(bc[:r0] - ref_row)
            lhs_segs.append(rows(0, ql, CHUNK - r0))
        else:
            ql = qc[r0 + SUB:] * jnp.exp2(bc[r0 + SUB:] - ref_row)
            lhs_segs.append(rows(r0 + SUB, ql, 0))
    lhs = jnp.concatenate(lhs_segs, axis=1).astype(BF16)
    rhs = jnp.concatenate(rhs_segs, axis=1).astype(BF16)

    bit = [(lane & (1 << t)) != 0 for t in range(3)]
    blocks = []
    for ib in range(N_SUB):
        r0 = ib * SUB
        qi, bi = qc[r0:r0 + SUB], bc[r0:r0 + SUB]
        cols = []
        for jj in range(SUB):
            j = r0 + jj
            e = jnp.exp2(bi - bc[j:j + 1])
            cols.append(jnp.sum(qi * (kc[j:j + 1] * e), axis=-1, keepdims=True))
        for t in range(3):
            cols = [jnp.where(bit[t], cols[i + 1], cols[i]) for i in range(0, len(cols), 2)]
        rel = lane - r0
        keep = ((rel >= sub) & (rel < SUB)) if rev else ((rel <= sub) & (rel >= 0))
        blocks.append(jnp.where(keep, cols[0], 0.0))
    return q_in, k_tail, jnp.exp2(tot), lhs, rhs, jnp.concatenate(blocks, axis=0)


def _gla_chunk_pair(ops_a, ops_b, v_a, v_b, st_a, st_b):
    qa, ka, da, lhs_a, rhs_a, near_a = ops_a
    qb, kb, db, lhs_b, rhs_b, near_b = ops_b
    zr = jnp.zeros(rhs_a.shape, BF16)
    far = _dot_nt(jnp.concatenate([lhs_a, lhs_b], axis=1),
                  jnp.concatenate([jnp.concatenate([rhs_a, zr], axis=1),
                                   jnp.concatenate([zr, rhs_b], axis=1)], axis=0))
    scores = (far + near_a + near_b).astype(BF16)
    zv = jnp.zeros(v_a.shape, F32)
    vt = jnp.concatenate([jnp.concatenate([v_a, zv], axis=1),
                          jnp.concatenate([zv, v_b], axis=1)], axis=0).T.astype(BF16)
    zs = jnp.zeros((HEAD_D, HEAD_D), BF16)
    o = _dot_nt(jnp.concatenate([qa, scores, qb], axis=1),
                jnp.concatenate([jnp.concatenate([st_a.astype(BF16), zs], axis=0), vt,
                                 jnp.concatenate([zs, st_b.astype(BF16)], axis=0)], axis=1))
    ut = _dot(vt, jnp.concatenate([ka, kb], axis=0))
    return o, (st_a * da + ut[:HEAD_D], st_b * db + ut[HEAD_D:])


def _scan_kernel(*refs, seq_len, has_s0, emit_state):
    it = iter(refs)
    qkv_ref, lr_ref, wdec_ref, bdec_ref = (next(it) for _ in range(4))
    s0_ref = next(it) if has_s0 else None
    out_ref = next(it)
    st_out_ref = next(it) if emit_state else None
    la_f_ref, la_b_ref, o_ref, st_ref = (next(it) for _ in range(4))

    L = seq_len
    n_chunks = L // CHUNK
    tr = 256
    assert L % tr == 0

    ri = lax.broadcasted_iota(jnp.int32, (tr, tr), 0)
    ci = lax.broadcasted_iota(jnp.int32, (tr, tr), 1)
    same_chunk = (ri // CHUNK) == (ci // CHUNK)

    def decay_body(t, carry):
        r0 = pl.multiple_of(t * tr, tr)
        lr = lr_ref[pl.ds(r0, tr), :]
        lr2 = jnp.concatenate([lr, lr], axis=1)
        for d, ref in ((0, la_f_ref), (1, la_b_ref)):
            z = _dot(lr2, jnp.concatenate(_split_bf16(wdec_ref[d]), axis=0)) + bdec_ref[d:d + 1]
            la = jnp.concatenate(_split_bf16(_log2_sigmoid(z) * (1.0 / GLA_GATE_NORM)), axis=0)
            tri = jnp.where(same_chunk & ((ci >= ri) if d else (ci <= ri)), 1.0, 0.0).astype(BF16)
            ref[pl.ds(r0, tr), :] = _dot(jnp.concatenate([tri, tri], axis=1), la)
        return carry
    lax.fori_loop(0, L // tr, decay_body, 0)

    for rev, la_ref in ((False, la_f_ref), (True, la_b_ref)):
        d = 1 if rev else 0
        for h in range(N_HEADS):
            if has_s0:
                st_ref[h] = s0_ref[0, d, h].T
            else:
                st_ref[h] = jnp.zeros((HEAD_D, HEAD_D), F32)

        def chunk_body(c, carry, rev=rev, la_ref=la_ref):
            cc = (n_chunks - 1 - c) if rev else c
            rows = pl.ds(pl.multiple_of(cc * CHUNK, CHUNK), CHUNK)
            for h0 in range(0, N_HEADS, 2):
                ops, vs = [], []
                for idx in range(2):
                    lo, hi = (h0 + idx) * HEAD_D, (h0 + idx + 1) * HEAD_D
                    qc = qkv_ref[rows, lo:hi].astype(F32) * (HEAD_D ** -0.5)
                    kc = qkv_ref[rows, D_GLA + lo:D_GLA + hi].astype(F32)
                    ops.append(_gla_head_operands(qc, kc, la_ref[rows, lo:hi], rev, idx * CHUNK))
                    vs.append(qkv_ref[rows, 2 * D_GLA + lo:2 * D_GLA + hi].astype(F32))
                oc, (st_a, st_b) = _gla_chunk_pair(ops[0], ops[1], vs[0], vs[1], st_ref[h0], st_ref[h0 + 1])
                st_ref[h0] = st_a
                st_ref[h0 + 1] = st_b
                cols = slice(h0 * HEAD_D, (h0 + 2) * HEAD_D)
                if rev:
                    o_ref[rows, cols] += oc
                else:
                    o_ref[rows, cols] = oc
            return carry
        lax.fori_loop(0, n_chunks, chunk_body, 0, unroll=min(n_chunks, 8))

        if emit_state:
            for h in range(N_HEADS):
                st_out_ref[0, d, h] = st_ref[h].T

    out_ref[...] = o_ref[...].astype(BF16)


def _gla_scan(proj, wdec, bdec, s0, *, seq_len, emit_state):
    t = proj.shape[0]
    nb = t // seq_len
    has_s0 = s0 is not None
    const = lambda shape: pl.BlockSpec(shape, lambda b: (0,) * len(shape), pipeline_mode=pl.Buffered(1))
    assert C_Q % (3 * D_GLA) == 0 and C_K == C_Q + D_GLA and C_V == C_K + D_GLA and C_LR % LR_PAD == 0
    in_specs = [pl.BlockSpec((seq_len, 3 * D_GLA), lambda b: (b, C_Q // (3 * D_GLA))),
                pl.BlockSpec((seq_len, LR_PAD), lambda b: (b, C_LR // LR_PAD)),
                const((2, LR_PAD, D_GLA)), const((2, D_GLA))]
    args = [proj, proj, wdec, bdec]
    if has_s0:
        in_specs.append(pl.BlockSpec((1, 2, N_HEADS, HEAD_D, HEAD_D), lambda b: (b, 0, 0, 0, 0)))
        args.append(s0)
    out_specs = [pl.BlockSpec((seq_len, D_GLA), lambda b: (b, 0))]
    out_shape = [jax.ShapeDtypeStruct((t, D_GLA), BF16)]
    if emit_state:
        out_specs.append(pl.BlockSpec((1, 2, N_HEADS, HEAD_D, HEAD_D), lambda b: (b, 0, 0, 0, 0)))
        out_shape.append(jax.ShapeDtypeStruct((nb, 2, N_HEADS, HEAD_D, HEAD_D), F32))
    kern = functools.partial(_scan_kernel, seq_len=seq_len, has_s0=has_s0, emit_state=emit_state)
    return pl.pallas_call(
        kern,
        grid=(nb,),
        in_specs=in_specs,
        out_specs=out_specs,
        out_shape=out_shape,
        scratch_shapes=[pltpu.VMEM((seq_len, D_GLA), F32), pltpu.VMEM((seq_len, D_GLA), F32),
                        pltpu.VMEM((seq_len, D_GLA), F32), pltpu.VMEM((N_HEADS, HEAD_D, HEAD_D), F32)],
        compiler_params=pltpu.CompilerParams(dimension_semantics=("arbitrary",),
                                             vmem_limit_bytes=VMEM_LIMIT),
        name="gla_scan",
    )(*args)


MIX_TM = 1024
MIX_SUB = 256


def _mix_dense_kernel(conv_ref, go_ref, gc0_ref, gc1_ref, gg0_ref, gg1_ref, o_ref, x_ref, mod_ref, cw_ref, cb_ref,
                      gnw_ref, wbc_ref, wbg_ref, wout_ref, out_ref, yc_ref, yg_ref, *, row_len, tiles_per_mod):
    i = pl.program_id(0)
    mod_row = i // tiles_per_mod if mod_ref.shape[0] > 1 else 0
    g1 = mod_ref[pl.ds(mod_row, 1), 2 * D_MODEL:3 * D_MODEL]
    hsel_r = lax.broadcasted_iota(jnp.int32, (D_GLA, D_GLA), 0) // HEAD_D
    hsel_c = lax.broadcasted_iota(jnp.int32, (D_GLA, D_GLA), 1) // HEAD_D
    head_avg = jnp.where(hsel_r == hsel_c, 1.0 / HEAD_D, 0.0).astype(BF16)
    head_avg2 = jnp.concatenate([head_avg, head_avg], axis=0)
    pos = lax.broadcasted_iota(jnp.int32, (MIX_SUB, 1), 0)
    in_row = pos % row_len

    def branch_body(t, carry):
        rows = pl.ds(pl.multiple_of(t * MIX_SUB, MIX_SUB), MIX_SUB)
        cu = conv_ref[rows, 2 * D_CONV:3 * D_CONV].astype(F32) * conv_ref[rows, 0:D_CONV].astype(F32)
        left = jnp.where(in_row == 0, 0.0, pltpu.roll(cu, 1, axis=0))
        right = jnp.where(in_row == row_len - 1, 0.0, pltpu.roll(cu, MIX_SUB - 1, axis=0))
        conv = cw_ref[0:1] * left + cw_ref[1:2] * cu + cw_ref[2:3] * right + cb_ref[...]
        yc_ref[rows, :] = (conv_ref[rows, D_CONV:2 * D_CONV].astype(F32) * conv).astype(BF16)
        o = o_ref[rows, :].astype(F32)
        ms = _dot(jnp.concatenate(_split_bf16(o * o), axis=1), head_avg2)
        g_out = go_ref[rows, :].astype(F32)
        yg_ref[rows, :] = (o * lax.rsqrt(ms + EPS) * gnw_ref[...] * (g_out * _sigmoid(g_out))).astype(BF16)
        return carry
    lax.fori_loop(0, MIX_TM // MIX_SUB, branch_body, 0)

    half = D_MODEL // 2
    pc = _dot(yc_ref[...], wbc_ref[...])
    pg = _dot(yg_ref[...], wbg_ref[...])
    merged = jnp.concatenate(
        [_sigmoid(gc0_ref[...].astype(F32)) * pc[:, :half] + _sigmoid(gg0_ref[...].astype(F32)) * pg[:, :half],
         _sigmoid(gc1_ref[...].astype(F32)) * pc[:, half:] + _sigmoid(gg1_ref[...].astype(F32)) * pg[:, half:]],
        axis=1)
    out_ref[...] = x_ref[...] + g1 * _dot(merged.astype(BF16), wout_ref[...])


def _mix_dense(proj, o, x2d, mod, conv_w, conv_b, gnw, wbc, wbg, wout, *, row_len, tokens_per_mod):
    t = x2d.shape[0]
    assert MIX_SUB % row_len == 0 and t % MIX_TM == 0 and tokens_per_mod % MIX_TM == 0
    const = lambda shape: pl.BlockSpec(shape, lambda i: (0,) * len(shape), pipeline_mode=pl.Buffered(1))
    cols = lambda width, start: pl.BlockSpec((MIX_TM, width), lambda i: (i, start // width))
    half = D_MODEL // 2
    assert C_U == 0 and C_GB == D_CONV and C_GC == 2 * D_CONV and C_GO % D_GLA == 0 and C_BRC % half == 0
    kern = functools.partial(_mix_dense_kernel, row_len=row_len, tiles_per_mod=tokens_per_mod // MIX_TM)
    return pl.pallas_call(
        kern,
        grid=(t // MIX_TM,),
        in_specs=[cols(3 * D_CONV, C_U), cols(D_GLA, C_GO),
                  cols(half, C_BRC), cols(half, C_BRC + half), cols(half, C_BRG), cols(half, C_BRG + half),
                  pl.BlockSpec((MIX_TM, D_GLA), lambda i: (i, 0)), pl.BlockSpec((MIX_TM, D_MODEL), lambda i: (i, 0)),
                  const(mod.shape), const((3, D_CONV)), const((1, D_CONV)), const((1, D_GLA)),
                  const((D_CONV, D_MODEL)), const((D_GLA, D_MODEL)), const((D_MODEL, D_MODEL))],
        out_specs=pl.BlockSpec((MIX_TM, D_MODEL), lambda i: (i, 0)),
        out_shape=jax.ShapeDtypeStruct((t, D_MODEL), F32),
        scratch_shapes=[pltpu.VMEM((MIX_TM, D_CONV), BF16), pltpu.VMEM((MIX_TM, D_GLA), BF16)],
        compiler_params=pltpu.CompilerParams(dimension_semantics=("arbitrary",), vmem_limit_bytes=VMEM_LIMIT),
        name="mix_dense",
    )(proj, proj, proj, proj, proj, proj, o, x2d, mod, conv_w, conv_b, gnw, wbc, wbg, wout)


I32 = jnp.int32
TB = 256
ROW_ALIGN = 16
G_ALIGN = 256
E_CHUNK = 2048
R_LOC = 3072
H2W = 1152


def _select_x(i, n_p_tiles, xp_ref, xs_ref):
    return jnp.where(i < n_p_tiles, xp_ref[...], xs_ref[...])


def _mod_row(i, n_p_tiles, tiles_per_mod):
    return jnp.where(i < n_p_tiles, 0, 1 + (i - n_p_tiles) // tiles_per_mod)


def _route_kernel(xp_ref, xs_ref, mod_ref, n2w_ref, wrt_ref, brb_ref, h2_ref, lpos_ref, cnt_ref, *,
                  n_p_tiles, tiles_per_mod):
    i = pl.program_id(0)
    row = _mod_row(i, n_p_tiles, tiles_per_mod)
    x = _select_x(i, n_p_tiles, xp_ref, xs_ref)
    sh = mod_ref[pl.ds(row, 1), 3 * D_MODEL:4 * D_MODEL]
    sc = mod_ref[pl.ds(row, 1), 4 * D_MODEL:5 * D_MODEL]
    h2 = _rms(x) * n2w_ref[...] * (1.0 + sc) + sh

    h_hi, h_lo = _split_bf16(h2)
    w_hi, w_lo = _split_bf16(wrt_ref[...])
    scores = jax.nn.sigmoid(_dot_nt(jnp.concatenate([w_hi, w_hi, w_lo], axis=1),
                                    jnp.concatenate([h_hi, h_lo, h_hi], axis=1)))
    biased = scores + brb_ref[...]
    eidx = lax.broadcasted_iota(I32, scores.shape, 0)
    picks = []
    for _k in range(TOP_K):
        m = jnp.max(biased, axis=0, keepdims=True)
        first = jnp.min(jnp.where(biased == m, eidx, N_EXPERTS), axis=0, keepdims=True)
        pick = eidx == first
        picks.append(pick)
        biased = jnp.where(pick, -jnp.inf, biased)
    sel = jnp.zeros(scores.shape, F32)
    for pick in picks:
        sel = jnp.where(pick, 1.0, sel)
    selsc = sel * scores
    comb = selsc / jnp.sum(selsc, axis=0, keepdims=True) * ROUTED_SCALE

    selb = sel.astype(BF16)
    tr_ = lax.broadcasted_iota(I32, (TB, TB), 0)
    tc_ = lax.broadcasted_iota(I32, (TB, TB), 1)
    rank_n = _dot(selb, jnp.concatenate([jnp.where(tr_ < tc_, 1.0, 0.0).astype(BF16), jnp.ones((TB, 128), BF16)],
                                        axis=1))
    rank, n_b = rank_n[:, :TB], rank_n[:, TB:]
    m_b = jnp.maximum(jnp.floor((n_b + (ROW_ALIGN - 1)) * (1.0 / ROW_ALIGN)), 1.0) * ROW_ALIGN
    er_ = lax.broadcasted_iota(I32, (N_EXPERTS, N_EXPERTS), 0)
    ec_ = lax.broadcasted_iota(I32, (N_EXPERTS, N_EXPERTS), 1)
    loff_b = _dot(jnp.where(ec_ < er_, 1.0, 0.0).astype(BF16), m_b.astype(BF16))
    lposf = jnp.concatenate([loff_b] * (TB // 128), axis=1) + rank
    rows = [jnp.sum(jnp.where(pick, lposf, 0.0), axis=0, keepdims=True) for pick in picks]
    lpos_ref[0] = jnp.concatenate(rows, axis=0).astype(I32)
    cnt_ref[0] = m_b

    combt = comb.T
    chi = combt.astype(BF16).astype(F32)
    h2_ref[:, 0:D_MODEL] = h_hi
    h2_ref[:, D_MODEL:H2W] = jnp.concatenate([chi, combt - chi], axis=1).astype(BF16)


def _route(x1p, x1s, mod, n2w, w_router_t, b_router_b, *, tiles_per_mod):
    n_p, n_s = x1p.shape[0] // TB, x1s.shape[0] // TB
    nt = n_p + n_s
    kern = functools.partial(_route_kernel, n_p_tiles=n_p, tiles_per_mod=tiles_per_mod)
    const = lambda shape: pl.BlockSpec(shape, lambda i: (0,) * len(shape))
    return pl.pallas_call(
        kern,
        grid=(nt,),
        in_specs=[pl.BlockSpec((TB, D_MODEL), lambda i: (jnp.minimum(i, n_p - 1), 0)),
                  pl.BlockSpec((TB, D_MODEL), lambda i: (jnp.maximum(i - n_p, 0), 0)),
                  const(mod.shape), const((1, D_MODEL)), const((N_EXPERTS, D_MODEL)), const((N_EXPERTS, TB))],
        out_specs=[pl.BlockSpec((TB, H2W), lambda i: (i, 0)),
                   pl.BlockSpec((1, TOP_K, TB), lambda i: (i, 0, 0)),
                   pl.BlockSpec((1, N_EXPERTS, 128), lambda i: (i, 0, 0))],
        out_shape=[jax.ShapeDtypeStruct((nt * TB, H2W), BF16),
                   jax.ShapeDtypeStruct((nt, TOP_K, TB), I32),
                   jax.ShapeDtypeStruct((nt, N_EXPERTS, 128), F32)],
        compiler_params=pltpu.CompilerParams(dimension_semantics=("arbitrary",), vmem_limit_bytes=VMEM_LIMIT),
        name="moe_route",
    )(x1p, x1s, mod, n2w, w_router_t, b_router_b)


def _plan_kernel(cnt_ref, off_ref, loff_ref, msz_ref, grp_ref, *, nt):
    lane = lax.broadcasted_iota(I32, (N_EXPERTS, 128), 1)
    m = jnp.zeros((N_EXPERTS, 128), F32)
    for i in range(nt):
        m = jnp.where(lane == i, cnt_ref[i], m)
    total = jnp.broadcast_to(jnp.sum(m, axis=1, keepdims=True), (N_EXPERTS, 128))
    gsz = jnp.floor((total + (G_ALIGN - 1)) * (1.0 / G_ALIGN)) * G_ALIGN
    er_ = lax.broadcasted_iota(I32, (N_EXPERTS, N_EXPERTS), 0)
    ec_ = lax.broadcasted_iota(I32, (N_EXPERTS, N_EXPERTS), 1)
    lstrict = jnp.where(ec_ < er_, 1.0, 0.0)
    ir_ = lax.broadcasted_iota(I32, (128, 128), 0)
    ic_ = lax.broadcasted_iota(I32, (128, 128), 1)
    ustrict = jnp.where(ir_ < ic_, 1.0, 0.0)
    gstart = _dot_hi(lstrict, gsz)
    off_ref[...] = (gstart + _dot_hi(m, ustrict)).astype(I32)
    loff_ref[...] = _dot_hi(lstrict, m).astype(I32)
    msz_ref[...] = m.astype(I32)
    grp = jnp.where(lane == 0, gstart + total, jnp.where(lane == 1, gsz - total, jnp.where(lane == 2, gstart, gsz)))
    grp_ref[...] = grp.astype(I32)


def _plan(cnt):
    nt = cnt.shape[0]
    assert nt <= 128
    tab = jax.ShapeDtypeStruct((N_EXPERTS, 128), I32)
    return pl.pallas_call(
        functools.partial(_plan_kernel, nt=nt),
        out_shape=[tab, tab, tab, tab],
        compiler_params=pltpu.CompilerParams(vmem_limit_bytes=VMEM_LIMIT),
        name="moe_plan",
    )(cnt)


def _start_copies(msz_ref, tile, make_copy):
    for e in range(N_EXPERTS):
        make_copy(e, pl.multiple_of(msz_ref[e, tile], ROW_ALIGN)).start()


def _tile_rows(loff_ref, msz_ref, tile):
    return pl.multiple_of(loff_ref[N_EXPERTS - 1, tile] + msz_ref[N_EXPERTS - 1, tile], ROW_ALIGN)


def _dispatch_kernel(off_ref, loff_ref, msz_ref, tail_ref, h2_ref, lpos_ref, xs_hbm, xloc_ref, zero_ref, sem,
                     tail_sem, *, nt):
    i = pl.program_id(0)
    slot = i % 2

    def copy_for(tile, slot_):
        def make(e, m):
            lo = pl.multiple_of(loff_ref[e, tile], ROW_ALIGN)
            of = pl.multiple_of(off_ref[e, tile], ROW_ALIGN)
            return pltpu.make_async_copy(xloc_ref.at[slot_, pl.ds(lo, m)], xs_hbm.at[pl.ds(of, m)], sem.at[slot_])
        return make

    def wait_tile(tile, slot_):
        n = _tile_rows(loff_ref, msz_ref, tile)
        pltpu.make_async_copy(xloc_ref.at[slot_, pl.ds(0, n)], xs_hbm.at[pl.ds(0, n)], sem.at[slot_]).wait()

    lpos = lpos_ref[0].astype(jnp.int16)
    h2 = h2_ref[...]
    ck = 1024
    one, zero = jnp.ones((ck, TB), BF16), jnp.zeros((ck, TB), BF16)
    for c in range(R_LOC // ck):
        r = (lax.broadcasted_iota(I32, (ck, TB), 0) + c * ck).astype(jnp.int16)
        d = zero
        for k in range(TOP_K):
            d = jnp.where(r == lpos[k:k + 1, :], one, d)
        res = _dot(d, h2)
        xloc_ref[slot, c * ck:(c + 1) * ck, :] = res.astype(BF16)

    _start_copies(msz_ref, i, copy_for(i, slot))

    @pl.when(i > 0)
    def _():
        wait_tile(i - 1, 1 - slot)

    @pl.when(i == nt - 1)
    def _():
        zero_ref[...] = jnp.zeros(zero_ref.shape, BF16)

        def tail_copies(start):
            def body(e, carry):
                n = tail_ref[e, 1]

                @pl.when(n > 0)
                def _():
                    st = pl.multiple_of(tail_ref[e, 0], ROW_ALIGN)
                    nn = pl.multiple_of(n, ROW_ALIGN)
                    cp = pltpu.make_async_copy(zero_ref.at[pl.ds(0, nn)], xs_hbm.at[pl.ds(st, nn)], tail_sem)
                    if start:
                        cp.start()
                    else:
                        cp.wait()
                return carry
            lax.fori_loop(0, N_EXPERTS, body, 0)
        tail_copies(True)
        wait_tile(i, slot)
        tail_copies(False)


def _dispatch(off, loff, msz, tail, h2ext, lpos, n_rows):
    nt = lpos.shape[0]
    grid_spec = pltpu.PrefetchScalarGridSpec(
        num_scalar_prefetch=4,
        grid=(nt,),
        in_specs=[pl.BlockSpec((TB, H2W), lambda i, *_: (i, 0)),
                  pl.BlockSpec((1, TOP_K, TB), lambda i, *_: (i, 0, 0))],
        out_specs=pl.BlockSpec(memory_space=pl.ANY),
        scratch_shapes=[pltpu.VMEM((2, R_LOC, H2W), BF16), pltpu.VMEM((G_ALIGN, H2W), BF16),
                        pltpu.SemaphoreType.DMA((2,)), pltpu.SemaphoreType.DMA],
    )
    return pl.pallas_call(
        functools.partial(_dispatch_kernel, nt=nt),
        grid_spec=grid_spec,
        out_shape=jax.ShapeDtypeStruct((n_rows, H2W), BF16),
        compiler_params=pltpu.CompilerParams(dimension_semantics=("arbitrary",), vmem_limit_bytes=VMEM_LIMIT),
        name="moe_dispatch",
    )(off, loff, msz, tail, h2ext, lpos)


def _expert_kernel(grp_ref, wg_ref, wu_ref, wd_ref, xs_hbm, ys_hbm, xbuf, ybuf, wgu_ref, wdb_ref, st_ref,
                   in_sem, out_sem):
    e = pl.program_id(0)
    n_exp = pl.num_programs(0)

    def in_copy(row0, n, slot):
        return pltpu.make_async_copy(xs_hbm.at[pl.ds(row0, n)], xbuf.at[slot, pl.ds(0, n)], in_sem.at[slot])

    def out_copy(row0, n, slot):
        return pltpu.make_async_copy(ybuf.at[slot, pl.ds(0, n)], ys_hbm.at[pl.ds(row0, n)], out_sem.at[slot])

    def rows_of(ex):
        return grp_ref[jnp.minimum(ex, n_exp - 1), 3]

    def next_nonempty(ex):
        return lax.while_loop(lambda c: (c < n_exp) & (rows_of(c) == 0), lambda c: c + 1, ex + 1)

    def start_first_chunk(ex, slot):
        @pl.when(ex < n_exp)
        def _():
            exc = jnp.minimum(ex, n_exp - 1)
            n = pl.multiple_of(jnp.minimum(grp_ref[exc, 3], E_CHUNK), G_ALIGN)
            in_copy(pl.multiple_of(grp_ref[exc, 2], G_ALIGN), n, slot).start(priority=1)

    def drain_out(slot):
        pend = st_ref[1 + slot]

        @pl.when(pend > 0)
        def _():
            out_copy(0, pl.multiple_of(pend, G_ALIGN), slot).wait()
            st_ref[1 + slot] = 0

    @pl.when(e == 0)
    def _():
        st_ref[0] = 0
        st_ref[1] = 0
        st_ref[2] = 0
        start_first_chunk(next_nonempty(-1), 0)

    g0 = grp_ref[e, 2]
    gn = grp_ref[e, 3]

    @pl.when(gn > 0)
    def _():
        wgu_ref[:, :D_EXPERT] = wg_ref[0].astype(BF16)
        wgu_ref[:, D_EXPERT:] = wu_ref[0].astype(BF16)
        wdb_ref[...] = wd_ref[0].astype(BF16)
        n_chunks = (gn + (E_CHUNK - 1)) // E_CHUNK

        def compute(n, slot):
            x = xbuf[slot, 0:n, 0:D_MODEL]
            ext = xbuf[slot, 0:n, D_MODEL:H2W].astype(F32)
            wts = ext[:, :N_EXPERTS] + ext[:, N_EXPERTS:]
            lane = lax.broadcasted_iota(I32, wts.shape, 1)
            w = jnp.sum(jnp.where(lane == e, wts, 0.0), axis=-1, keepdims=True)
            h = _dot(x, wgu_ref[...])
            hg, hu = h[:, :D_EXPERT], h[:, D_EXPERT:]
            act = hg * _sigmoid(hg) * hu * w
            ybuf[slot, 0:n, :] = _dot(act.astype(BF16), wdb_ref[...]).astype(BF16)

        def chunk_body(c, slot):
            row0 = pl.multiple_of(g0 + c * E_CHUNK, G_ALIGN)
            n = pl.multiple_of(jnp.minimum(gn - c * E_CHUNK, E_CHUNK), G_ALIGN)
            in_copy(row0, n, slot).wait()

            @pl.when(c + 1 < n_chunks)
            def _():
                n1 = pl.multiple_of(jnp.minimum(gn - (c + 1) * E_CHUNK, E_CHUNK), G_ALIGN)
                in_copy(pl.multiple_of(row0 + E_CHUNK, G_ALIGN), n1, 1 - slot).start(priority=1)

            @pl.when(c + 1 == n_chunks)
            def _():
                start_first_chunk(next_nonempty(e), 1 - slot)

            drain_out(slot)
            for v in range(G_ALIGN, E_CHUNK + 1, G_ALIGN):
                @pl.when(n == v)
                def _(v=v):
                    compute(v, slot)
            out_copy(row0, n, slot).start()
            st_ref[1 + slot] = n
            return 1 - slot

        st_ref[0] = lax.fori_loop(0, n_chunks, chunk_body, st_ref[0])

    @pl.when(e == n_exp - 1)
    def _():
        drain_out(0)
        drain_out(1)


def _experts(grp, xs, wg, wu, wd):
    w_in = pl.BlockSpec((1, D_MODEL, D_EXPERT), lambda e, grp_ref: (e, 0, 0))
    grid_spec = pltpu.PrefetchScalarGridSpec(
        num_scalar_prefetch=1,
        grid=(N_EXPERTS,),
        in_specs=[w_in, w_in, pl.BlockSpec((1, D_EXPERT, D_MODEL), lambda e, grp_ref: (e, 0, 0)),
                  pl.BlockSpec(memory_space=pl.ANY)],
        out_specs=pl.BlockSpec(memory_space=pl.ANY),
        scratch_shapes=[pltpu.VMEM((2, E_CHUNK, H2W), BF16), pltpu.VMEM((2, E_CHUNK, D_MODEL), BF16),
                        pltpu.VMEM((D_MODEL, 2 * D_EXPERT), BF16), pltpu.VMEM((D_EXPERT, D_MODEL), BF16),
                        pltpu.SMEM((4,), I32), pltpu.SemaphoreType.DMA((2,)), pltpu.SemaphoreType.DMA((2,))],
    )
    return pl.pallas_call(
        _expert_kernel,
        grid_spec=grid_spec,
        out_shape=jax.ShapeDtypeStruct((xs.shape[0], D_MODEL), BF16),
        compiler_params=pltpu.CompilerParams(dimension_semantics=("arbitrary",), vmem_limit_bytes=VMEM_LIMIT),
        name="moe_experts",
    )(grp, wg, wu, wd, xs)


def _combine_kernel(off_ref, loff_ref, msz_ref, lpos_ref, h2_ref, xp_ref, xs_ref, mod_ref, wgs_ref, wus_ref, wds_ref,
                    fnw_ref, ysrt_hbm, yp_ref, ys_ref, yloc_ref, acc_ref, wgus_ref, wdsb_ref, sem, *, nt, n_p_tiles,
                    tiles_per_mod, final):
    i = pl.program_id(0)
    slot = i % 2

    def copy_for(tile, slot_):
        def make(e, m):
            lo = pl.multiple_of(loff_ref[e, tile], ROW_ALIGN)
            of = pl.multiple_of(off_ref[e, tile], ROW_ALIGN)
            return pltpu.make_async_copy(ysrt_hbm.at[pl.ds(of, m)], yloc_ref.at[slot_, pl.ds(lo, m)], sem.at[slot_])
        return make

    @pl.when(i == 0)
    def _():
        yloc_ref[...] = jnp.zeros(yloc_ref.shape, BF16)
        _start_copies(msz_ref, 0, copy_for(0, 0))
        wgus_ref[:, :D_EXPERT] = wgs_ref[...].astype(BF16)
        wgus_ref[:, D_EXPERT:] = wus_ref[...].astype(BF16)
        wdsb_ref[...] = wds_ref[...].astype(BF16)

    nxt = jnp.minimum(i + 1, nt - 1)
    _start_copies(msz_ref, nxt, copy_for(nxt, 1 - slot))

    hgu = _dot(h2_ref[...], wgus_ref[...])
    hg, hu = hgu[:, :D_EXPERT], hgu[:, D_EXPERT:]
    acc_ref[...] = _dot((hg * _sigmoid(hg) * hu).astype(BF16), wdsb_ref[...])

    def wait_tile(tile, slot_):
        n = _tile_rows(loff_ref, msz_ref, tile)
        pltpu.make_async_copy(ysrt_hbm.at[pl.ds(0, n)], yloc_ref.at[slot_, pl.ds(0, n)], sem.at[slot_]).wait()

    wait_tile(i, slot)

    @pl.when(i == nt - 1)
    def _():
        wait_tile(i, 1 - slot)

    lpos_pad = jnp.concatenate([lpos_ref[0].astype(F32), jnp.zeros((128 - TOP_K, TB), F32)], axis=0)
    lposc = lpos_pad.T.astype(I32)
    ck = 512
    cols = [jnp.broadcast_to(lposc[:, k:k + 1], (TB, ck)).astype(jnp.int16) for k in range(TOP_K)]
    one, zero = jnp.ones((TB, ck), BF16), jnp.zeros((TB, ck), BF16)
    for c in range(R_LOC // ck):
        r = (lax.broadcasted_iota(I32, (TB, ck), 1) + c * ck).astype(jnp.int16)
        cm = zero
        for k in range(TOP_K):
            cm = jnp.where(r == cols[k], one, cm)
        acc_ref[...] += _dot(cm, yloc_ref[slot, c * ck:(c + 1) * ck, :])

    row = _mod_row(i, n_p_tiles, tiles_per_mod)
    g2 = mod_ref[pl.ds(row, 1), 5 * D_MODEL:6 * D_MODEL]
    x2 = _select_x(i, n_p_tiles, xp_ref, xs_ref) + g2 * acc_ref[...]
    y = _rms(x2) * fnw_ref[...] if final else x2

    @pl.when(i < n_p_tiles)
    def _():
        yp_ref[...] = y

    @pl.when(i >= n_p_tiles)
    def _():
        ys_ref[...] = y


def _combine(off, loff, msz, lpos, h2ext, x1p, x1s, mod, wgs, wus, wds, fnw, ysorted, *, tiles_per_mod, final):
    n_p, n_s = x1p.shape[0] // TB, x1s.shape[0] // TB
    nt = n_p + n_s
    const = lambda shape: pl.BlockSpec(shape, lambda i, *_: (0,) * len(shape), pipeline_mode=pl.Buffered(1))
    p_idx = lambda i, *_: (jnp.minimum(i, n_p - 1), 0)
    s_idx = lambda i, *_: (jnp.maximum(i - n_p, 0), 0)
    grid_spec = pltpu.PrefetchScalarGridSpec(
        num_scalar_prefetch=3,
        grid=(nt,),
        in_specs=[pl.BlockSpec((1, TOP_K, TB), lambda i, *_: (i, 0, 0)),
                  pl.BlockSpec((TB, D_MODEL), lambda i, *_: (i, 0)),
                  pl.BlockSpec((TB, D_MODEL), p_idx), pl.BlockSpec((TB, D_MODEL), s_idx),
                  const(mod.shape), const((D_MODEL, D_EXPERT)), const((D_MODEL, D_EXPERT)), const((D_EXPERT, D_MODEL)),
                  const((1, D_MODEL)), pl.BlockSpec(memory_space=pl.ANY)],
        out_specs=[pl.BlockSpec((TB, D_MODEL), p_idx), pl.BlockSpec((TB, D_MODEL), s_idx)],
        scratch_shapes=[pltpu.VMEM((2, R_LOC, D_MODEL), BF16), pltpu.VMEM((TB, D_MODEL), F32),
                        pltpu.VMEM((D_MODEL, 2 * D_EXPERT), BF16), pltpu.VMEM((D_EXPERT, D_MODEL), BF16),
                        pltpu.SemaphoreType.DMA((2,))],
    )
    kern = functools.partial(_combine_kernel, nt=nt, n_p_tiles=n_p, tiles_per_mod=tiles_per_mod, final=final)
    return pl.pallas_call(
        kern,
        grid_spec=grid_spec,
        out_shape=[jax.ShapeDtypeStruct(x1p.shape, F32), jax.ShapeDtypeStruct(x1s.shape, F32)],
        compiler_params=pltpu.CompilerParams(dimension_semantics=("arbitrary",), vmem_limit_bytes=VMEM_LIMIT),
        name="moe_combine",
    )(off, loff, msz, lpos, h2ext, x1p, x1s, mod, wgs, wus, wds, fnw, ysorted)


def _moe(x1p, x1s, mod, n2w, w_router, b_router, wg, wu, wd, wgs, wus, wds, fnw, *, tokens_per_mod, final):
    assert R_LOC >= TB * TOP_K + N_EXPERTS * ROW_ALIGN and tokens_per_mod % TB == 0
    nt = (x1p.shape[0] + x1s.shape[0]) // TB
    n_rows_max = nt * TB * TOP_K + nt * N_EXPERTS * ROW_ALIGN + N_EXPERTS * (G_ALIGN - ROW_ALIGN)
    tiles_per_mod = tokens_per_mod // TB
    brb = jnp.broadcast_to(b_router.reshape(N_EXPERTS, 1), (N_EXPERTS, TB))
    h2ext, lpos, cnt = _route(x1p, x1s, mod, n2w, w_router.T, brb, tiles_per_mod=tiles_per_mod)
    off, loff, msz, grp = _plan(cnt)
    xs = _dispatch(off, loff, msz, grp, h2ext, lpos, n_rows_max)
    ysorted = _experts(grp, xs, wg, wu, wd)
    return _combine(off, loff, msz, lpos, h2ext, x1p, x1s, mod, wgs, wus, wds, fnw, ysorted,
                    tiles_per_mod=tiles_per_mod, final=final)


def kernel(x_prompt, x_sample, state_gla, c, c_ctx, w_mod, b_mod, norm1_w, w_in, conv_w, conv_b, w_decay, b_decay,
           gla_norm_w, w_br_conv, w_br_gla, w_out, norm2_w, w_router, b_router, w_gate_e, w_up_e, w_down_e,
           w_gate_s, w_up_s, w_down_s, final_norm_w):
    depth = w_mod.shape[0]
    nb_p, len_p, _ = x_prompt.shape
    nb_s, len_s, _ = x_sample.shape
    yp = x_prompt.reshape(nb_p * len_p, D_MODEL)
    ys = x_sample.reshape(nb_s * len_s, D_MODEL)
    fnw = final_norm_w.reshape(1, D_MODEL)

    cond = jnp.concatenate([c_ctx[None, :], c, jnp.zeros((8 - 1 - nb_s, D_MODEL), F32)], axis=0)
    states = []
    for l in range(depth):
        mod = _modulation(cond, w_mod[l], b_mod[l].reshape(1, -1))
        mod_p, mod_s = mod[0:1], mod[1:1 + nb_s]

        w_in_r = _w_in_prep(w_in[l].T)
        wdec = jnp.zeros((2, LR_PAD, D_GLA), F32)
        wdec = wdec.at[0, 0:GLA_RANK].set(w_decay[l, 0]).at[1, GLA_RANK:2 * GLA_RANK].set(w_decay[l, 1])
        n1w = norm1_w[l].reshape(1, D_MODEL)
        mix_w = (conv_w[l], conv_b[l].reshape(1, D_CONV), gla_norm_w[l].reshape(1, D_GLA),
                 w_br_conv[l].astype(BF16), w_br_gla[l].astype(BF16), w_out[l].astype(BF16))
        moe_w = (norm2_w[l].reshape(1, D_MODEL), w_router[l], b_router[l],
                 w_gate_e[l], w_up_e[l], w_down_e[l], w_gate_s[l], w_up_s[l], w_down_s[l])

        proj_p = _in_proj(yp, mod_p, n1w, w_in_r, rows_per_mod=nb_p * len_p)
        o_p, st = _gla_scan(proj_p, wdec, b_decay[l], None, seq_len=len_p, emit_state=True)
        yp = _mix_dense(proj_p, o_p, yp, mod_p, *mix_w, row_len=len_p, tokens_per_mod=nb_p * len_p)
        states.append(st)
        proj_s = _in_proj(ys, mod_s, n1w, w_in_r, rows_per_mod=len_s)
        (o_s,) = _gla_scan(proj_s, wdec, b_decay[l], state_gla[:, l], seq_len=len_s, emit_state=False)
        ys = _mix_dense(proj_s, o_s, ys, mod_s, *mix_w, row_len=GRID_W, tokens_per_mod=len_s)

        yp, ys = _moe(yp, ys, mod, *moe_w, fnw, tokens_per_mod=len_s, final=l == depth - 1)
    new_state = jnp.stack(states, axis=1)
    return (yp.reshape(nb_p, len_p, D_MODEL), ys.reshape(nb_s, len_s, D_MODEL), new_state)
```
